```python
import math
import jax, jax.numpy as jnp
from jax import lax
import numpy as np

D_MODEL = 1024
BATCH = 8
SEQ = 4096
DEPTH = 2

GRID_W = 64
CTX_LEN = 256
EPS = 1e-6
ROPE_THETA = 10000.0
Q_BLOCK = 128
N_MOD = 6

MLA_HEADS = 8
MLA_NOPE = 64
MLA_ROPE = 32
MLA_V = 64
MLA_Q_RANK = 384
MLA_KV_RANK = 256
MLA_SCALE = (MLA_NOPE + MLA_ROPE) ** -0.5

DIFF_HEADS = 4
DIFF_HD = 64
DIFF_SCALE = DIFF_HD ** -0.5

AB_WIDTHS = (MLA_Q_RANK, MLA_KV_RANK, MLA_ROPE, 2 * DIFF_HEADS * DIFF_HD, 2 * DIFF_HEADS * DIFF_HD, DIFF_HEADS * 2 * DIFF_HD)
AB_IN = sum(AB_WIDTHS)
AB_SPLITS = [int(v) for v in np.cumsum(AB_WIDTHS)[:-1]]
AB_OUT = MLA_HEADS * MLA_V + DIFF_HEADS * 2 * DIFF_HD

NA_HEADS = 16
NA_HD = D_MODEL // NA_HEADS
NA_KH = 8
NA_KW = 16
NA_SCALE = NA_HD ** -0.5

D_FF = ((8 * D_MODEL // 3 + 127) // 128) * 128
N_EXPERTS = 8
TOP_K = 2

N_EVEN = (DEPTH + 1) // 2
N_ODD = DEPTH // 2

kernel_name = "hybrid_mla_diff_natten_moe_dit"


def rmsnorm(x, g):
    xf = x.astype(jnp.float32)
    y = xf * lax.rsqrt(jnp.mean(xf * xf, axis=-1, keepdims=True) + EPS)
    return (y * g.astype(jnp.float32)).astype(x.dtype)


def modulate(h, shift, scale):
    return h * (1 + scale) + shift


def softmax32(s):
    return jax.nn.softmax(s.astype(jnp.float32), axis=-1)


def axis_rope_tables(pos, dim):
    inv = ROPE_THETA ** (-jnp.arange(0, dim, 2, dtype=jnp.float32) / dim)
    ang = pos.astype(jnp.float32)[:, None] * inv[None, :]
    ang = jnp.concatenate([ang, ang], axis=-1)[:, None, :]
    return jnp.cos(ang), jnp.sin(ang)


def rotate_half(x):
    x1, x2 = jnp.split(x, 2, axis=-1)
    return jnp.concatenate([-x2, x1], axis=-1)


def axial_rope(x, row, col):
    d = x.shape[-1]
    xf = x.astype(jnp.float32)
    xr, xc = xf[..., : d // 2], xf[..., d // 2:]
    cr, sr = axis_rope_tables(row, d // 2)
    cc, scol = axis_rope_tables(col, d // 2)
    out = jnp.concatenate([xr * cr + rotate_half(xr) * sr, xc * cc + rotate_half(xc) * scol], axis=-1)
    return out.astype(x.dtype)


def sweep_query_blocks(fn, *qs):
    B, L = qs[0].shape[:2]
    nb = L // Q_BLOCK
    blocks = tuple(jnp.moveaxis(q.reshape((B, nb, Q_BLOCK) + q.shape[2:]), 1, 0) for q in qs)
    out = lax.map(lambda qb: fn(*qb), blocks)
    out = jnp.moveaxis(out, 0, 1)
    return out.reshape((B, L) + out.shape[3:])


def mla_attend(q_nope, q_rope, k_nope, k_rope, v):
    s = jnp.einsum('bqhd,bkhd->bhqk', q_nope, k_nope) + jnp.einsum('bqhd,bkd->bhqk', q_rope, k_rope)
    p = softmax32(s * MLA_SCALE).astype(v.dtype)
    return jnp.einsum('bhqk,bkhd->bqhd', p, v)


def diff_attend(q1, q2, k1, k2, v, lam, subln_g, lam_init):
    p1 = softmax32(jnp.einsum('bqhd,bkhd->bhqk', q1, k1) * DIFF_SCALE)
    p2 = softmax32(jnp.einsum('bqhd,bkhd->bhqk', q2, k2) * DIFF_SCALE)
    p = (p1 - lam * p2).astype(v.dtype)
    o = jnp.einsum('bhqk,bkhd->bqhd', p, v)
    return rmsnorm(o, subln_g) * (1 - lam_init)


def ab_project(h, w_in, q_norm, kv_norm, w_uq, w_ukv, row, col):
    B, n, _ = h.shape
    c_q, c_kv, k_rope, dq, dk, dv = jnp.split(h @ w_in, AB_SPLITS, axis=-1)
    q = (rmsnorm(c_q, q_norm) @ w_uq).reshape(B, n, MLA_HEADS, MLA_NOPE + MLA_ROPE)
    kv = (rmsnorm(c_kv, kv_norm) @ w_ukv).reshape(B, n, MLA_HEADS, MLA_NOPE + MLA_V)
    q_nope, q_rope = q[..., :MLA_NOPE], q[..., MLA_NOPE:]
    k_nope, v_a = kv[..., :MLA_NOPE], kv[..., MLA_NOPE:]
    k_rope = k_rope[:, :, None, :]
    dq = dq.reshape(B, n, 2 * DIFF_HEADS, DIFF_HD)
    dk = dk.reshape(B, n, 2 * DIFF_HEADS, DIFF_HD)
    if row is not None:
        q_rope, k_rope, dq, dk = [axial_rope(t, row, col) for t in (q_rope, k_rope, dq, dk)]
    dq = dq.reshape(B, n, DIFF_HEADS, 2, DIFF_HD)
    dk = dk.reshape(B, n, DIFF_HEADS, 2, DIFF_HD)
    dv = dv.reshape(B, n, DIFF_HEADS, 2 * DIFF_HD)
    queries = (q_nope, q_rope, dq[..., 0, :], dq[..., 1, :])
    keys = (k_nope, k_rope[:, :, 0, :], dk[..., 0, :], dk[..., 1, :], v_a, dv)
    return queries, keys


def mixer_ab(hx, hc, row, col, w_in, q_norm, kv_norm, w_uq, w_ukv, lam_vecs, subln, w_out, lam_init, ctx_out):
    B, L, _ = hx.shape
    qx, kx = ab_project(hx, w_in, q_norm, kv_norm, w_uq, w_ukv, row, col)
    qc, kc = ab_project(hc, w_in, q_norm, kv_norm, w_uq, w_ukv, None, None)
    lam = (jnp.exp(jnp.sum(lam_vecs[0].astype(jnp.float32) * lam_vecs[1].astype(jnp.float32)))
           - jnp.exp(jnp.sum(lam_vecs[2].astype(jnp.float32) * lam_vecs[3].astype(jnp.float32))) + lam_init)
    keys_all = tuple(jnp.concatenate([a, b], axis=1) for a, b in zip(kc, kx))

    def attend(q, k):
        qn, qr, q1, q2 = q
        kn, kr, k1, k2, va, vb = k
        oa = mla_attend(qn, qr, kn, kr, va)
        ob = diff_attend(q1, q2, k1, k2, vb, lam, subln, lam_init)
        return jnp.concatenate([oa.reshape(oa.shape[:2] + (-1,)), ob.reshape(ob.shape[:2] + (-1,))], axis=-1)

    ox = sweep_query_blocks(lambda *qb: attend(qb, keys_all), *qx)
    yx = ox @ w_out
    yc = attend(qc, kc) @ w_out if ctx_out else None
    return yx, yc


def mixer_na(hx, hc, w_qkv, b_qkv, rpb, w_out, b_out, ctx_out):
    B, L, _ = hx.shape
    rows = L // GRID_W
    kh = min(NA_KH, rows)

    def qkv(h):
        z = h @ w_qkv + b_qkv
        return [t.reshape(B, h.shape[1], NA_HEADS, NA_HD) for t in jnp.split(z, 3, axis=-1)]

    qc, kc, vc = qkv(hc)
    qx, kx, vx = qkv(hx)
    grid = lambda t: t.reshape(B, rows, GRID_W, NA_HEADS, NA_HD)
    kg, vg = grid(kx), grid(vx)
    cols = np.arange(GRID_W)
    col_start = np.clip(cols - NA_KW // 2, 0, GRID_W - NA_KW)
    col_idx = col_start[:, None] + np.arange(NA_KW)[None, :]
    dcol = col_idx - cols[:, None] + (NA_KW - 1)
    rpb_c = rpb[:, :, dcol]

    def row_block(args):
        r, q_r = args
        rs = jnp.clip(r - kh // 2, 0, rows - kh)
        k_rows = lax.dynamic_slice_in_dim(kg, rs, kh, axis=1)
        v_rows = lax.dynamic_slice_in_dim(vg, rs, kh, axis=1)
        k_win = k_rows[:, :, col_idx]
        v_win = v_rows[:, :, col_idx]
        drow = rs + jnp.arange(kh) - r + (NA_KH - 1)
        bias = jnp.take(rpb_c, drow, axis=1).transpose(0, 2, 1, 3)
        s_win = jnp.einsum('bqhd,biqjhd->bhqij', q_r, k_win) * NA_SCALE + bias[None].astype(q_r.dtype)
        s_win = s_win.reshape(B, NA_HEADS, GRID_W, kh * NA_KW)
        s_ctx = jnp.einsum('bqhd,bkhd->bhqk', q_r, kc) * NA_SCALE
        p = softmax32(jnp.concatenate([s_win, s_ctx], axis=-1)).astype(vx.dtype)
        p_win = p[..., : kh * NA_KW].reshape(B, NA_HEADS, GRID_W, kh, NA_KW)
        o = jnp.einsum('bhqij,biqjhd->bqhd', p_win, v_win) + jnp.einsum('bhqk,bkhd->bqhd', p[..., kh * NA_KW:], vc)
        return o

    q_rows = jnp.moveaxis(grid(qx), 1, 0)
    o = lax.map(row_block, (jnp.arange(rows, dtype=jnp.int32), q_rows))
    o = jnp.moveaxis(o, 0, 1).reshape(B, L, D_MODEL)
    yx = o @ w_out + b_out
    if ctx_out:
        pc = softmax32(jnp.einsum('bqhd,bkhd->bhqk', qc, kc) * NA_SCALE).astype(vc.dtype)
        yc = jnp.einsum('bhqk,bkhd->bqhd', pc, vc).reshape(B, hc.shape[1], D_MODEL) @ w_out + b_out
    else:
        yc = None
    return yx, yc


def swiglu(h, w1, w3, w2):
    return (jax.nn.silu(h @ w1) * (h @ w3)) @ w2


def moe_swiglu(h, router, w1, w3, w2):
    logits = (h @ router).astype(jnp.float32)
    top_v, top_i = lax.top_k(logits, TOP_K)
    gates = jax.nn.softmax(top_v, axis=-1)
    combine = jnp.sum(jax.nn.one_hot(top_i, N_EXPERTS, dtype=jnp.float32) * gates[..., None], axis=-2).astype(h.dtype)
    y = jnp.zeros_like(h)
    for e in range(N_EXPERTS):
        y = y + combine[..., e:e + 1] * swiglu(h, w1[e], w3[e], w2[e])
    return y


def setup_inputs(seed: int = 0) -> dict:
    key = jax.random.key(seed)
    ks = iter(jax.random.split(key, 40))
    D = D_MODEL

    def nrm(shape, s):
        return jax.random.normal(next(ks), shape, jnp.float32) * s

    def gain(shape):
        return 1.0 + nrm(shape, 0.02)

    return {
        "x": nrm((BATCH, SEQ, D), 1.0),
        "c": nrm((BATCH, D), 1.0),
        "ctx": nrm((BATCH, CTX_LEN, D), 1.0),
        "c_ctx": nrm((D,), 1.0),
        "w_mod": nrm((DEPTH, D, N_MOD * D), 0.5 * D ** -0.5),
        "b_mod": nrm((DEPTH, N_MOD * D), 0.02),
        "norm_g": gain((DEPTH, 4, D)),
        "a_w_in": nrm((N_EVEN, D, AB_IN), D ** -0.5),
        "a_q_norm": gain((N_EVEN, MLA_Q_RANK)),
        "a_kv_norm": gain((N_EVEN, MLA_KV_RANK)),
        "a_w_uq": nrm((N_EVEN, MLA_Q_RANK, MLA_HEADS * (MLA_NOPE + MLA_ROPE)), MLA_Q_RANK ** -0.5),
        "a_w_ukv": nrm((N_EVEN, MLA_KV_RANK, MLA_HEADS * (MLA_NOPE + MLA_V)), MLA_KV_RANK ** -0.5),
        "b_lambda": nrm((N_EVEN, 4, DIFF_HD), 0.1),
        "b_subln": gain((N_EVEN, 2 * DIFF_HD)),
        "ab_w_out": nrm((N_EVEN, AB_OUT, D), AB_OUT ** -0.5),
        "f_w1": nrm((N_EVEN, D, D_FF), D ** -0.5),
        "f_w3": nrm((N_EVEN, D, D_FF), D ** -0.5),
        "f_w2": nrm((N_EVEN, D_FF, D), D_FF ** -0.5),
        "c_w_qkv": nrm((N_ODD, D, 3 * D), D ** -0.5),
        "c_b_qkv": nrm((N_ODD, 3 * D), 0.02),
        "c_rpb": nrm((N_ODD, NA_HEADS, 2 * NA_KH - 1, 2 * NA_KW - 1), 0.05),
        "c_w_out": nrm((N_ODD, D, D), D ** -0.5),
        "c_b_out": nrm((N_ODD, D), 0.02),
        "m_router": nrm((N_ODD, D, N_EXPERTS), D ** -0.5),
        "m_w1": nrm((N_ODD, N_EXPERTS, D, D_FF), D ** -0.5),
        "m_w3": nrm((N_ODD, N_EXPERTS, D, D_FF), D ** -0.5),
        "m_w2": nrm((N_ODD, N_EXPERTS, D_FF, D), D_FF ** -0.5),
    }


def reference(x, c, ctx, c_ctx, w_mod, b_mod, norm_g, a_w_in, a_q_norm, a_kv_norm, a_w_uq, a_w_ukv,
              b_lambda, b_subln, ab_w_out, f_w1, f_w3, f_w2, c_w_qkv, c_b_qkv, c_rpb, c_w_out, c_b_out,
              m_router, m_w1, m_w3, m_w2):
    B, L, D = x.shape
    t = jnp.arange(L, dtype=jnp.int32)
    row, col = t // GRID_W, t % GRID_W
    sc = jax.nn.silu(c)
    scc = jax.nn.silu(c_ctx)
    cs = ctx
    for i in range(DEPTH):
        last = i == DEPTH - 1
        j = i // 2
        mod_x = (sc @ w_mod[i] + b_mod[i]).reshape(B, N_MOD, 1, D)
        mod_c = (scc @ w_mod[i] + b_mod[i]).reshape(1, N_MOD, 1, D)
        g = norm_g[i]
        hx = modulate(rmsnorm(x, g[0]), mod_x[:, 0], mod_x[:, 1])
        hc = modulate(rmsnorm(cs, g[0]), mod_c[:, 0], mod_c[:, 1])
        if i % 2 == 0:
            lam_init = 0.8 - 0.6 * math.exp(-0.3 * i)
            yx, yc = mixer_ab(hx, hc, row, col, a_w_in[j], a_q_norm[j], a_kv_norm[j], a_w_uq[j], a_w_ukv[j],
                              b_lambda[j], b_subln[j], ab_w_out[j], lam_init, not last)
        else:
            yx, yc = mixer_na(hx, hc, c_w_qkv[j], c_b_qkv[j], c_rpb[j], c_w_out[j], c_b_out[j], not last)
        x = x + mod_x[:, 2] * rmsnorm(yx, g[1])
        if not last:
            cs = cs + mod_c[:, 2] * rmsnorm(yc, g[1])
        hx = modulate(rmsnorm(x, g[2]), mod_x[:, 3], mod_x[:, 4])
        if i % 2 == 0:
            fx = swiglu(hx, f_w1[j], f_w3[j], f_w2[j])
        else:
            fx = moe_swiglu(hx, m_router[j], m_w1[j], m_w3[j], m_w2[j])
        x = x + mod_x[:, 5] * rmsnorm(fx, g[3])
        if not last:
            hc = modulate(rmsnorm(cs, g[2]), mod_c[:, 3], mod_c[:, 4])
            if i % 2 == 0:
                fc = swiglu(hc, f_w1[j], f_w3[j], f_w2[j])
            else:
                fc = moe_swiglu(hc, m_router[j], m_w1[j], m_w3[j], m_w2[j])
            cs = cs + mod_c[:, 5] * rmsnorm(fc, g[3])
    return x
```

```python
import functools
import math

import jax
import jax.numpy as jnp
import numpy as np
from jax import lax
from jax.experimental import pallas as pl
from jax.experimental.pallas import tpu as pltpu

BF = jnp.bfloat16
F32 = jnp.float32

LANES = 128
VMEM_LIMIT = 56 * 1024 * 1024

GRID_W = 64
EPS = 1e-6
ROPE_THETA = 10000.0
N_MOD = 6

MLA_HEADS = 8
MLA_NOPE = 64
MLA_ROPE = 32
MLA_V = 64
MLA_Q_RANK = 384
MLA_KV_RANK = 256
MLA_SCALE = (MLA_NOPE + MLA_ROPE) ** -0.5

DIFF_HEADS = 4
DIFF_HD = 64
DIFF_SCALE = DIFF_HD ** -0.5

NA_HEADS = 16
NA_HD = 64
NA_KH = 8
NA_KW = 16
NA_SCALE = NA_HD ** -0.5
NA_SLOTS = NA_HEADS * NA_HD // LANES
MASK_VALUE = -1e30

N_EXPERTS = 8

_NT = (((1,), (1,)), ((), ()))


def _cparams(*sem):
    return pltpu.CompilerParams(dimension_semantics=sem, vmem_limit_bytes=VMEM_LIMIT)


def _dot(a, b):
    return jnp.dot(a, b, preferred_element_type=F32)


def _rms(x, g):
    return x * lax.rsqrt(jnp.mean(x * x, axis=-1, keepdims=True) + EPS) * g


def _norm_mod(x, g, shift, scale):
    return _rms(x, g) * (1 + scale) + shift


def _const_spec(shape):
    return pl.BlockSpec(shape, lambda *_: (0,) * len(shape))


def _mod_spec(mod):
    if mod.shape[0] == 1:
        return pl.BlockSpec((1,) + mod.shape[1:], lambda b, *_: (0, 0, 0))
    return pl.BlockSpec((1,) + mod.shape[1:], lambda b, *_: (b, 0, 0))


def _mod_kernel(c_ref, w_ref, b_ref, o_ref):
    c = c_ref[...]
    sc = c * jax.nn.sigmoid(c)
    o_ref[0] = _dot(sc.astype(BF), w_ref[0].astype(BF)) + b_ref[0]


def _modulation(cvec, w_mod, b_mod):
    depth, d, n = w_mod.shape
    rows = cvec.shape[0]
    return pl.pallas_call(
        _mod_kernel,
        grid=(depth, n // d),
        in_specs=[
            pl.BlockSpec((rows, d), lambda i, j: (0, 0)),
            pl.BlockSpec((1, d, d), lambda i, j: (i, 0, j)),
            pl.BlockSpec((1, 1, d), lambda i, j: (i, 0, j)),
        ],
        out_specs=pl.BlockSpec((1, rows, d), lambda i, j: (i, 0, j)),
        out_shape=jax.ShapeDtypeStruct((depth, rows, n), F32),
        compiler_params=_cparams("arbitrary", "arbitrary"),
        name="modulation",
    )(cvec, w_mod, b_mod.reshape(depth, 1, n))


_Z_CQ = 0
_Z_CKV = MLA_Q_RANK
_Z_KR = _Z_CKV + MLA_KV_RANK
_Z_KR_ROT = _Z_KR + LANES
_Z_DQ = _Z_KR_ROT + LANES
_DW = 2 * DIFF_HEADS * DIFF_HD
_Z_DQ_ROT = _Z_DQ + _DW
_Z_DK = _Z_DQ_ROT + _DW
_Z_DK_ROT = _Z_DK + _DW
_Z_DV = _Z_DK_ROT + _DW
_Z_END = _Z_DV + _DW


def _proj_ab_kernel(x_ref, mod_ref, g_ref, win_ref, qn_ref, kvn_ref, wuq_ref, wuk_ref, wuv_ref,
                    place_ref, tab_ref, q_ref, k_ref, v_ref, dq_ref, dk_ref, dv_ref):
    h = _norm_mod(x_ref[0], g_ref[0:1, :], mod_ref[0, 0:1, :], mod_ref[0, 1:2, :])
    z = _dot(h.astype(BF), win_ref[...])
    cqn = _rms(z[:, _Z_CQ:_Z_CKV], qn_ref[...]).astype(BF)
    ckvn = _rms(z[:, _Z_CKV:_Z_KR], kvn_ref[...]).astype(BF)
    q2 = _dot(cqn, wuq_ref[...])
    nq = MLA_HEADS * LANES
    cos_q, sin_q = tab_ref[0], tab_ref[1]
    for hd in range(MLA_HEADS):
        lo = hd * LANES
        q_ref[0, hd] = (q2[:, lo:lo + LANES] * cos_q + q2[:, nq + lo:nq + lo + LANES] * sin_q).astype(BF)
    kr = (z[:, _Z_KR:_Z_KR_ROT] * tab_ref[2] + z[:, _Z_KR_ROT:_Z_DQ] * tab_ref[3]).astype(BF)
    kk = _dot(ckvn, wuk_ref[...]) + _dot(kr, place_ref[...])
    vv = _dot(ckvn, wuv_ref[...])
    for hd in range(MLA_HEADS):
        lo = hd * LANES
        k_ref[0, hd] = kk[:, lo:lo + LANES].astype(BF)
        v_ref[0, hd] = vv[:, lo:lo + LANES].astype(BF)
    cos_d, sin_d = tab_ref[4], tab_ref[5]
    for hd in range(DIFF_HEADS):
        lo = hd * LANES
        dq = z[:, _Z_DQ + lo:_Z_DQ + lo + LANES] * cos_d + z[:, _Z_DQ_ROT + lo:_Z_DQ_ROT + lo + LANES] * sin_d
        dq_ref[0, hd] = (dq * DIFF_SCALE).astype(BF)
        dk = z[:, _Z_DK + lo:_Z_DK + lo + LANES] * cos_d + z[:, _Z_DK_ROT + lo:_Z_DK_ROT + lo + LANES] * sin_d
        dk_ref[0, hd] = dk.astype(BF)
        dv_ref[0, hd] = z[:, _Z_DV + lo:_Z_DV + lo + LANES].astype(BF)


def _proj_ab(x, mod, g, wts, tabs, tm):
    b, t, d = x.shape
    win, qn, kvn, wuq, wuk, wuv, place = wts
    tok = lambda hh: pl.BlockSpec((1, hh, tm, LANES), lambda bi, i: (bi, 0, i, 0))
    shp = lambda hh: jax.ShapeDtypeStruct((b, hh, t, LANES), BF)
    return pl.pallas_call(
        _proj_ab_kernel,
        grid=(b, t // tm),
        in_specs=[
            pl.BlockSpec((1, tm, d), lambda bi, i: (bi, i, 0)),
            _mod_spec(mod),
            _const_spec(g.shape),
            _const_spec(win.shape), _const_spec(qn.shape), _const_spec(kvn.shape),
            _const_spec(wuq.shape), _const_spec(wuk.shape), _const_spec(wuv.shape),
            _const_spec(place.shape),
            pl.BlockSpec((6, tm, LANES), lambda bi, i: (0, i, 0)),
        ],
        out_specs=[tok(MLA_HEADS), tok(MLA_HEADS), tok(MLA_HEADS),
                   tok(DIFF_HEADS), tok(DIFF_HEADS), tok(DIFF_HEADS)],
        out_shape=[shp(MLA_HEADS), shp(MLA_HEADS), shp(MLA_HEADS),
                   shp(DIFF_HEADS), shp(DIFF_HEADS), shp(DIFF_HEADS)],
        compiler_params=_cparams("arbitrary", "arbitrary"),
        name="proj_ab",
    )(x, mod, g, win, qn, kvn, wuq, wuk, wuv, place, tabs)


def _softmax_pv(qm, kv_refs):
    ss = [lax.dot_general(qm, k_ref[0, 0], _NT, preferred_element_type=F32) for k_ref, _ in kv_refs]
    m = functools.reduce(jnp.maximum, [jnp.max(s, axis=-1, keepdims=True) for s in ss])
    ps = [jnp.exp(s - m) for s in ss]
    l = functools.reduce(jnp.add, [jnp.sum(p, axis=-1, keepdims=True) for p in ps])
    o = functools.reduce(jnp.add, [_dot(p.astype(BF), v_ref[0, 0]) for p, (_, v_ref) in zip(ps, kv_refs)])
    return o / l


def _attn_kernel(*refs, nseg, mode, lam_init):
    q_ref = refs[0]
    kv_refs = [(refs[1 + 2 * s], refs[2 + 2 * s]) for s in range(nseg)]
    o_ref = refs[-1]
    q = q_ref[0, 0]
    if mode == "single":
        o_ref[0, 0] = _softmax_pv(q, kv_refs).astype(BF)
        return
    lane = lax.broadcasted_iota(jnp.int32, q.shape, 1)
    zero = jnp.zeros_like(q)
    o1 = _softmax_pv(jnp.where(lane < DIFF_HD, q, zero), kv_refs)
    o2 = _softmax_pv(jnp.where(lane >= DIFF_HD, q, zero), kv_refs)
    if mode == "pair":
        o_ref[0, 0] = jnp.where(lane < DIFF_HD, o1, o2).astype(BF)
        return
    lam_ref, subln_ref = refs[1 + 2 * nseg], refs[2 + 2 * nseg]
    lv = lam_ref[...]
    lam = (jnp.exp(jnp.sum(lv[0:1] * lv[1:2], axis=-1, keepdims=True))
           - jnp.exp(jnp.sum(lv[2:3] * lv[3:4], axis=-1, keepdims=True)) + lam_init)
    o_ref[0, 0] = (_rms(o1 - lam * o2, subln_ref[...]) * (1 - lam_init)).astype(BF)


def _attention(q, kvs, tq, mode="single", lam_vecs=None, subln=None, lam_init=0.0):
    b, nh, lq, _ = q.shape
    diff = mode == "diff"
    in_specs = [pl.BlockSpec((1, 1, tq, LANES), lambda bi, h, i: (bi, h, i, 0))]
    args = [q]
    for k, v in kvs:
        lk = k.shape[2]
        spec = pl.BlockSpec((1, 1, lk, LANES), lambda bi, h, i: (bi, h, 0, 0))
        in_specs += [spec, spec]
        args += [k, v]
    if diff:
        in_specs += [_const_spec(lam_vecs.shape), _const_spec(subln.shape)]
        args += [lam_vecs, subln]
    return pl.pallas_call(
        functools.partial(_attn_kernel, nseg=len(kvs), mode=mode, lam_init=lam_init),
        grid=(b, nh, lq // tq),
        in_specs=in_specs,
        out_specs=pl.BlockSpec((1, 1, tq, LANES), lambda bi, h, i: (bi, h, i, 0)),
        out_shape=jax.ShapeDtypeStruct((b, nh, lq, LANES), BF),
        compiler_params=_cparams("arbitrary", "arbitrary", "arbitrary"),
        name="attn_diff" if diff else "attn_mla",
    )(*args)


def _outproj_kernel(*refs, n_in, has_bias):
    o_refs = refs[:n_in]
    w_ref = refs[n_in]
    pos = n_in + 1
    b_ref = None
    if has_bias:
        b_ref = refs[pos]
        pos += 1
    x_ref, mod_ref, g_ref, out_ref = refs[pos:pos + 4]
    o = jnp.concatenate([r[0, h] for r in o_refs for h in range(r.shape[1])], axis=-1)
    y = _dot(o, w_ref[...])
    if has_bias:
        y = y + b_ref[...]
    out_ref[0] = x_ref[0] + mod_ref[0, 2:3, :] * _rms(y, g_ref[1:2, :])


def _outproj(os_, w, bias, x, mod, g, tm):
    b, t, d = x.shape
    in_specs = [pl.BlockSpec((1, o.shape[1], tm, LANES), lambda bi, i: (bi, 0, i, 0)) for o in os_]
    in_specs.append(_const_spec(w.shape))
    args = list(os_) + [w]
    if bias is not None:
        in_specs.append(_const_spec(bias.shape))
        args.append(bias)
    in_specs += [pl.BlockSpec((1, tm, d), lambda bi, i: (bi, i, 0)), _mod_spec(mod), _const_spec(g.shape)]
    args += [x, mod, g]
    return pl.pallas_call(
        functools.partial(_outproj_kernel, n_in=len(os_), has_bias=bias is not None),
        grid=(b, t // tm),
        in_specs=in_specs,
        out_specs=pl.BlockSpec((1, tm, d), lambda bi, i: (bi, i, 0)),
        out_shape=jax.ShapeDtypeStruct((b, t, d), F32),
        compiler_params=_cparams("arbitrary", "arbitrary"),
        name="outproj",
    )(*args)


def _swiglu_chunk(hn, w1, w3, w2):
    a = _dot(hn, w1)
    act = (a * jax.nn.sigmoid(a) * _dot(hn, w3)).astype(BF)
    return _dot(act, w2)


def _ffn_kernel(x_ref, mod_ref, g_ref, w1_ref, w3_ref, w2_ref, out_ref, hn_ref, acc_ref):
    k = pl.program_id(2)

    @pl.when(k == 0)
    def _():
        h = _norm_mod(x_ref[0], g_ref[2:3, :], mod_ref[0, 3:4, :], mod_ref[0, 4:5, :])
        hn_ref[...] = h.astype(BF)
        acc_ref[...] = jnp.zeros_like(acc_ref)

    acc_ref[...] += _swiglu_chunk(hn_ref[...], w1_ref[...], w3_ref[...], w2_ref[...])

    @pl.when(k == pl.num_programs(2) - 1)
    def _():
        out_ref[0] = x_ref[0] + mod_ref[0, 5:6, :] * _rms(acc_ref[...], g_ref[3:4, :])


def _ff_chunk(f):
    half = f // 2
    return half if f % 2 == 0 and half % LANES == 0 else f


def _ffn(x, mod, g, w1, w3, w2, tm):
    b, t, d = x.shape
    f = w1.shape[1]
    tf = _ff_chunk(f)
    return pl.pallas_call(
        _ffn_kernel,
        grid=(b, t // tm, f // tf),
        in_specs=[
            pl.BlockSpec((1, tm, d), lambda bi, i, k: (bi, i, 0)),
            _mod_spec(mod),
            _const_spec(g.shape),
            pl.BlockSpec((d, tf), lambda bi, i, k: (0, k)),
            pl.BlockSpec((d, tf), lambda bi, i, k: (0, k)),
            pl.BlockSpec((tf, d), lambda bi, i, k: (k, 0)),
        ],
        out_specs=pl.BlockSpec((1, tm, d), lambda bi, i, k: (bi, i, 0)),
        out_shape=jax.ShapeDtypeStruct((b, t, d), F32),
        scratch_shapes=[pltpu.VMEM((tm, d), BF), pltpu.VMEM((tm, d), F32)],
        compiler_params=_cparams("arbitrary", "arbitrary", "arbitrary"),
        name="ffn_dense",
    )(x, mod, g, w1, w3, w2)


def _moe_kernel(x_ref, mod_ref, g_ref, router_ref, w1_ref, w3_ref, w2_ref, out_ref,
                hn_ref, acc_ref, comb_ref):
    e = pl.program_id(2)
    k = pl.program_id(3)

    @pl.when((e == 0) & (k == 0))
    def _():
        h = _norm_mod(x_ref[0], g_ref[2:3, :], mod_ref[0, 3:4, :], mod_ref[0, 4:5, :])
        hn_ref[...] = h.astype(BF)
        acc_ref[...] = jnp.zeros_like(acc_ref)
        logits = jnp.dot(h, router_ref[...], preferred_element_type=F32, precision=lax.Precision.HIGHEST)
        lane = lax.broadcasted_iota(jnp.int32, logits.shape, 1)
        logits = jnp.where(lane < N_EXPERTS, logits, -jnp.inf)
        m1 = jnp.max(logits, axis=-1, keepdims=True)
        i1 = jnp.min(jnp.where(logits == m1, lane, LANES), axis=-1, keepdims=True)
        rest = jnp.where(lane == i1, -jnp.inf, logits)
        m2 = jnp.max(rest, axis=-1, keepdims=True)
        i2 = jnp.min(jnp.where(rest == m2, lane, LANES), axis=-1, keepdims=True)
        e2 = jnp.exp(m2 - m1)
        denom = 1.0 + e2
        comb_ref[...] = jnp.where(lane == i1, 1.0 / denom, jnp.where(lane == i2, e2 / denom, 0.0))

    lane = lax.broadcasted_iota(jnp.int32, comb_ref.shape, 1)
    gate = jnp.sum(jnp.where(lane == e, comb_ref[...], 0.0), axis=-1, keepdims=True)
    acc_ref[...] += gate * _swiglu_chunk(hn_ref[...], w1_ref[0], w3_ref[0], w2_ref[0])

    @pl.when((e == pl.num_programs(2) - 1) & (k == pl.num_programs(3) - 1))
    def _():
        out_ref[0] = x_ref[0] + mod_ref[0, 5:6, :] * _rms(acc_ref[...], g_ref[3:4, :])


def _moe(x, mod, g, router, w1, w3, w2, tm):
    b, t, d = x.shape
    ne, _, f = w1.shape
    tf = _ff_chunk(f)
    return pl.pallas_call(
        _moe_kernel,
        grid=(b, t // tm, ne, f // tf),
        in_specs=[
            pl.BlockSpec((1, tm, d), lambda bi, i, e, k: (bi, i, 0)),
            _mod_spec(mod),
            _const_spec(g.shape),
            _const_spec(router.shape),
            pl.BlockSpec((1, d, tf), lambda bi, i, e, k: (e, 0, k)),
            pl.BlockSpec((1, d, tf), lambda bi, i, e, k: (e, 0, k)),
            pl.BlockSpec((1, tf, d), lambda bi, i, e, k: (e, k, 0)),
        ],
        out_specs=pl.BlockSpec((1, tm, d), lambda bi, i, e, k: (bi, i, 0)),
        out_shape=jax.ShapeDtypeStruct((b, t, d), F32),
        scratch_shapes=[pltpu.VMEM((tm, d), BF), pltpu.VMEM((tm, d), F32), pltpu.VMEM((tm, LANES), F32)],
        compiler_params=_cparams("arbitrary", "arbitrary", "arbitrary", "arbitrary"),
        name="moe_dense",
    )(x, mod, g, router, w1, w3, w2)


def _proj_na_kernel(x_ref, mod_ref, g_ref, w_ref, b_ref, q_ref, k_ref, v_ref):
    h = _norm_mod(x_ref[0], g_ref[0:1, :], mod_ref[0, 0:1, :], mod_ref[0, 1:2, :])
    z = _dot(h.astype(BF), w_ref[...]) + b_ref[...]
    n = NA_SLOTS * LANES
    for s in range(NA_SLOTS):
        lo = s * LANES
        q_ref[0, s] = (z[:, lo:lo + LANES] * NA_SCALE).astype(BF)
        k_ref[0, s] = z[:, n + lo:n + lo + LANES].astype(BF)
        v_ref[0, s] = z[:, 2 * n + lo:2 * n + lo + LANES].astype(BF)


def _proj_na(x, mod, g, w, bias, tm):
    b, t, d = x.shape
    tok = pl.BlockSpec((1, NA_SLOTS, tm, LANES), lambda bi, i: (bi, 0, i, 0))
    shp = jax.ShapeDtypeStruct((b, NA_SLOTS, t, LANES), BF)
    return pl.pallas_call(
        _proj_na_kernel,
        grid=(b, t // tm),
        in_specs=[
            pl.BlockSpec((1, tm, d), lambda bi, i: (bi, i, 0)),
            _mod_spec(mod),
            _const_spec(g.shape),
            _const_spec(w.shape),
            _const_spec(bias.shape),
        ],
        out_specs=[tok, tok, tok],
        out_shape=[shp, shp, shp],
        compiler_params=_cparams("arbitrary", "arbitrary"),
        name="proj_na",
    )(x, mod, g, w, bias)


def _na_kernel(q_ref, kx_ref, vx_ref, kc_ref, vc_ref, bias_ref, o_ref, *, rows):
    r = pl.program_id(1)
    rs = jnp.clip(r - NA_KH // 2, 0, rows - NA_KH)
    start = pl.multiple_of(rs * GRID_W, GRID_W)
    nwin = NA_KH * GRID_W
    lane = lax.broadcasted_iota(jnp.int32, (GRID_W, LANES), 1)
    for s in range(NA_SLOTS):
        q = q_ref[0, s]
        zero = jnp.zeros_like(q)
        kw = kx_ref[0, s, pl.ds(start, nwin), :]
        vw = vx_ref[0, s, pl.ds(start, nwin), :]
        kc = kc_ref[0, s]
        vc = vc_ref[0, s]
        halves = []
        for j in range(2):
            qm = jnp.where(lane < NA_HD, q, zero) if j == 0 else jnp.where(lane >= NA_HD, q, zero)
            sw = lax.dot_general(qm, kw, _NT, preferred_element_type=F32) + bias_ref[0, 2 * s + j]
            sc = lax.dot_general(qm, kc, _NT, preferred_element_type=F32)
            m = jnp.maximum(jnp.max(sw, axis=-1, keepdims=True), jnp.max(sc, axis=-1, keepdims=True))
            pw = jnp.exp(sw - m)
            pc = jnp.exp(sc - m)
            l = jnp.sum(pw, axis=-1, keepdims=True) + jnp.sum(pc, axis=-1, keepdims=True)
            halves.append((_dot(pw.astype(BF), vw) + _dot(pc.astype(BF), vc)) / l)
        o_ref[0, s] = jnp.where(lane < NA_HD, halves[0], halves[1]).astype(BF)


def _na_attention(q, kx, vx, kc, vc, bias):
    b, ns, l, _ = q.shape
    rows = l // GRID_W
    lc = kc.shape[2]
    ncase = bias.shape[0]

    def case_of(r):
        rs = jnp.clip(r - NA_KH // 2, 0, rows - NA_KH)
        return r - rs

    full = lambda n: pl.BlockSpec((1, ns, n, LANES), lambda bi, r: (bi, 0, 0, 0))
    row = pl.BlockSpec((1, ns, GRID_W, LANES), lambda bi, r: (bi, 0, r, 0))
    return pl.pallas_call(
        functools.partial(_na_kernel, rows=rows),
        grid=(b, rows),
        in_specs=[row, full(l), full(l), full(lc), full(lc),
                  pl.BlockSpec((1,) + bias.shape[1:], lambda bi, r: (case_of(r), 0, 0, 0))],
        out_specs=row,
        out_shape=jax.ShapeDtypeStruct((b, ns, l, LANES), BF),
        compiler_params=_cparams("arbitrary", "arbitrary"),
        name="na_attn",
    )(q, kx, vx, kc, vc, bias)


def _axis_tables(pos, dim):
    inv = ROPE_THETA ** (-jnp.arange(0, dim, 2, dtype=F32) / dim)
    ang = pos.astype(F32)[:, None] * inv[None, :]
    ang = jnp.concatenate([ang, ang], axis=-1)
    return jnp.cos(ang), jnp.sin(ang)


def _rope_tables(n, rope):
    if rope:
        t = jnp.arange(n, dtype=jnp.int32)
        row, col = t // GRID_W, t % GRID_W

        def cs(d):
            cr, sr = _axis_tables(row, d // 2)
            cc, sc = _axis_tables(col, d // 2)
            return jnp.concatenate([cr, cc], axis=-1), jnp.concatenate([sr, sc], axis=-1)

        c32, s32 = cs(MLA_ROPE)
        c64, s64 = cs(DIFF_HD)
    else:
        c32, s32 = jnp.ones((n, MLA_ROPE), F32), jnp.zeros((n, MLA_ROPE), F32)
        c64, s64 = jnp.ones((n, DIFF_HD), F32), jnp.zeros((n, DIFF_HD), F32)
    ones = jnp.ones((n, MLA_NOPE), F32)
    pad = lambda a: jnp.pad(a, ((0, 0), (0, LANES - a.shape[1])))
    cos_q = MLA_SCALE * pad(jnp.concatenate([ones, c32], axis=-1))
    sin_q = MLA_SCALE * pad(jnp.concatenate([jnp.zeros_like(ones), s32], axis=-1))
    return jnp.stack([cos_q, sin_q, pad(c32), pad(s32),
                      jnp.concatenate([c64, c64], axis=-1), jnp.concatenate([s64, s64], axis=-1)])


def _rot_cols(w, d):
    shp = w.shape
    w5 = w.reshape(shp[:-1] + (-1, 2, 2, d // 4))
    r = jnp.concatenate([-w5[..., 1:2, :], w5[..., 0:1, :]], axis=-2)
    return r.reshape(shp)


def _pad_cols(w, n):
    return jnp.pad(w, ((0, 0), (0, n - w.shape[1])))


def _prep_ab_weights(w_in, q_norm, kv_norm, w_uq, w_ukv, w_out):
    o_cq, o_ckv, o_kr = MLA_Q_RANK, MLA_Q_RANK + MLA_KV_RANK, MLA_Q_RANK + MLA_KV_RANK + MLA_ROPE
    o_dq, o_dk = o_kr + _DW, o_kr + 2 * _DW
    w_kr, w_dq, w_dk, w_dv = w_in[:, o_ckv:o_kr], w_in[:, o_kr:o_dq], w_in[:, o_dq:o_dk], w_in[:, o_dk:]
    win = jnp.concatenate([
        w_in[:, :o_ckv], _pad_cols(w_kr, LANES), _pad_cols(_rot_cols(w_kr, MLA_ROPE), LANES),
        w_dq, _rot_cols(w_dq, DIFF_HD), w_dk, _rot_cols(w_dk, DIFF_HD), w_dv], axis=1).astype(BF)
    rq = w_uq.shape[0]
    uq = w_uq.reshape(rq, MLA_HEADS, MLA_NOPE + MLA_ROPE)
    nope, rope = uq[..., :MLA_NOPE], uq[..., MLA_NOPE:]
    zpad = jnp.zeros((rq, MLA_HEADS, LANES - MLA_NOPE - MLA_ROPE), F32)
    main = jnp.concatenate([nope, rope, zpad], axis=-1).reshape(rq, MLA_HEADS * LANES)
    rot = jnp.concatenate([jnp.zeros_like(nope), _rot_cols(rope, MLA_ROPE), zpad], axis=-1)
    wuq = jnp.concatenate([main, rot.reshape(rq, MLA_HEADS * LANES)], axis=1).astype(BF)
    rkv = w_ukv.shape[0]
    ukv = w_ukv.reshape(rkv, MLA_HEADS, MLA_NOPE + MLA_V)
    slot = lambda a: jnp.pad(a, ((0, 0), (0, 0), (0, LANES - a.shape[-1]))).reshape(rkv, MLA_HEADS * LANES)
    wuk = slot(ukv[..., :MLA_NOPE]).astype(BF)
    wuv = slot(ukv[..., MLA_NOPE:]).astype(BF)
    place = np.zeros((LANES, MLA_HEADS * LANES), np.float32)
    for hd in range(MLA_HEADS):
        place[np.arange(MLA_ROPE), hd * LANES + MLA_NOPE + np.arange(MLA_ROPE)] = 1.0
    d = w_out.shape[1]
    wo_a = w_out[:MLA_HEADS * MLA_V].reshape(MLA_HEADS, MLA_V, d)
    wo_a = jnp.pad(wo_a, ((0, 0), (0, LANES - MLA_V), (0, 0))).reshape(MLA_HEADS * LANES, d)
    wo = jnp.concatenate([wo_a, w_out[MLA_HEADS * MLA_V:]], axis=0).astype(BF)
    proj = (win, q_norm.reshape(1, -1), kv_norm.reshape(1, -1), wuq, wuk, wuv, jnp.asarray(place, BF))
    return proj, wo


def _na_bias_table(rpb):
    case = np.arange(NA_KH)
    di = np.arange(NA_KH)[None, :] - case[:, None] + (NA_KH - 1)
    cols = np.arange(GRID_W)
    col_start = np.clip(cols - NA_KW // 2, 0, GRID_W - NA_KW)
    kcol = np.arange(GRID_W)
    valid = (kcol[None, :] >= col_start[:, None]) & (kcol[None, :] < col_start[:, None] + NA_KW)
    dc = np.clip(kcol[None, :] - cols[:, None] + (NA_KW - 1), 0, 2 * NA_KW - 2)
    tbl = rpb[:, di[:, :, None, None], dc[None, None, :, :]]
    tbl = jnp.where(valid[None, None, None], tbl, MASK_VALUE)
    tbl = jnp.transpose(tbl, (1, 0, 3, 2, 4))
    return tbl.reshape(NA_KH, rpb.shape[0], GRID_W, NA_KH * GRID_W)


def _tile(n, pref):
    return pref if n % pref == 0 else n


def kernel(x, c, ctx, c_ctx, w_mod, b_mod, norm_g, a_w_in, a_q_norm, a_kv_norm, a_w_uq, a_w_ukv, b_lambda, b_subln,
           ab_w_out, f_w1, f_w3, f_w2, c_w_qkv, c_b_qkv, c_rpb, c_w_out, c_b_out, m_router, m_w1, m_w3, m_w2):
    b, l, d = x.shape
    lc = ctx.shape[1]
    depth = w_mod.shape[0]
    assert l % GRID_W == 0 and l // GRID_W >= NA_KH

    mod_rows = 16
    cvec = jnp.zeros((mod_rows, d), F32).at[:b].set(c).at[b].set(c_ctx)
    mod = _modulation(cvec, w_mod, b_mod)

    tm_x, tm_c = _tile(l, 512), _tile(lc, 256)
    tq_x, tq_c = _tile(l, 256), _tile(lc, 256)
    cs = ctx
    for i in range(depth):
        last = i == depth - 1
        j = i // 2
        mx = mod[i, :b].reshape(b, N_MOD, d)
        mc = mod[i, b].reshape(1, N_MOD, d)
        g = norm_g[i]
        if i % 2 == 0:
            lam_init = 0.8 - 0.6 * math.exp(-0.3 * i)
            proj_w, wo = _prep_ab_weights(a_w_in[j], a_q_norm[j], a_kv_norm[j], a_w_uq[j], a_w_ukv[j], ab_w_out[j])
            qx, kx, vx, dqx, dkx, dvx = _proj_ab(x, mx, g, proj_w, _rope_tables(l, True), tm_x)
            qc, kc, vc, dqc, dkc, dvc = _proj_ab(cs, mc, g, proj_w, _rope_tables(lc, False), tm_c)
            subln = b_subln[j].reshape(1, -1)
            oa = _attention(qx, [(kx, vx), (kc, vc)], tq_x)
            ob = _attention(dqx, [(dkx, dvx), (dkc, dvc)], tq_x, "diff", b_lambda[j], subln, lam_init)
            x = _outproj([oa, ob], wo, None, x, mx, g, tm_x)
            if not last:
                oa = _attention(qc, [(kc, vc)], tq_c)
                ob = _attention(dqc, [(dkc, dvc)], tq_c, "diff", b_lambda[j], subln, lam_init)
                cs = _outproj([oa, ob], wo, None, cs, mc, g, tm_c)
            w1, w3, w2 = f_w1[j].astype(BF), f_w3[j].astype(BF), f_w2[j].astype(BF)
            x = _ffn(x, mx, g, w1, w3, w2, tm_x)
            if not last:
                cs = _ffn(cs, mc, g, w1, w3, w2, tm_c)
        else:
            wqkv = c_w_qkv[j].astype(BF)
            bqkv = c_b_qkv[j].reshape(1, -1)
            wo = c_w_out[j].astype(BF)
            bo = c_b_out[j].reshape(1, -1)
            qx, kx, vx = _proj_na(x, mx, g, wqkv, bqkv, tm_x)
            qc, kc, vc = _proj_na(cs, mc, g, wqkv, bqkv, tm_c)
            o = _na_attention(qx, kx, vx, kc, vc, _na_bias_table(c_rpb[j]))
            x = _outproj([o], wo, bo, x, mx, g, tm_x)
            router = _pad_cols(m_router[j], LANES)
            w1, w3, w2 = m_w1[j].astype(BF), m_w3[j].astype(BF), m_w2[j].astype(BF)
            if not last:
                oc = _attention(qc, [(kc, vc)], tq_c, "pair")
                cs = _outproj([oc], wo, bo, cs, mc, g, tm_c)
            x = _moe(x, mx, g, router, w1, w3, w2, tm_x)
            if not last:
                cs = _moe(cs, mc, g, router, w1, w3, w2, tm_c)
    return x
```

```python
import functools
import math

import jax
import jax.numpy as jnp
import numpy as np
from jax import lax
from jax.experimental import pallas as pl
from jax.experimental.pallas import tpu as pltpu

BF = jnp.bfloat16
F32 = jnp.float32

LANES = 128
VMEM_LIMIT = 56 * 1024 * 1024

GRID_W = 64
EPS = 1e-6
ROPE_THETA = 10000.0
N_MOD = 6

MLA_HEADS = 8
MLA_NOPE = 64
MLA_ROPE = 32
MLA_V = 64
MLA_Q_RANK = 384
MLA_KV_RANK = 256
LOG2E = math.log2(math.e)
MLA_SCALE = (MLA_NOPE + MLA_ROPE) ** -0.5 * LOG2E

DIFF_HEADS = 4
DIFF_HD = 64
DIFF_SCALE = DIFF_HD ** -0.5 * LOG2E

NA_HEADS = 16
NA_HD = 64
NA_KH = 8
NA_KW = 16
NA_SCALE = NA_HD ** -0.5 * LOG2E
NA_SLOTS = NA_HEADS * NA_HD // LANES
MASK_VALUE = -1e30

N_EXPERTS = 8

_NT = (((1,), (1,)), ((), ()))


def _cparams(*sem):
    return pltpu.CompilerParams(dimension_semantics=sem, vmem_limit_bytes=VMEM_LIMIT)


def _dot(a, b):
    return jnp.dot(a, b, preferred_element_type=F32)


def _rms(x, g):
    return x * lax.rsqrt(jnp.mean(x * x, axis=-1, keepdims=True) + EPS) * g


def _norm_mod(x, g, shift, scale):
    return _rms(x, g) * (1 + scale) + shift


def _const_spec(shape):
    return pl.BlockSpec(shape, lambda *_: (0,) * len(shape))


def _mod_spec(mod):
    if mod.shape[0] == 1:
        return pl.BlockSpec((1,) + mod.shape[1:], lambda b, *_: (0, 0, 0))
    return pl.BlockSpec((1,) + mod.shape[1:], lambda b, *_: (b, 0, 0))


def _mod_kernel(c_ref, w_ref, b_ref, o_ref):
    c = c_ref[...]
    sc = c * jax.nn.sigmoid(c)
    o_ref[0] = _dot(sc.astype(BF), w_ref[0].astype(BF)) + b_ref[0]


def _modulation(cvec, w_mod, b_mod):
    depth, d, n = w_mod.shape
    rows = cvec.shape[0]
    return pl.pallas_call(
        _mod_kernel,
        grid=(depth, n // d),
        in_specs=[
            pl.BlockSpec((rows, d), lambda i, j: (0, 0)),
            pl.BlockSpec((1, d, d), lambda i, j: (i, 0, j)),
            pl.BlockSpec((1, 1, d), lambda i, j: (i, 0, j)),
        ],
        out_specs=pl.BlockSpec((1, rows, d), lambda i, j: (i, 0, j)),
        out_shape=jax.ShapeDtypeStruct((depth, rows, n), F32),
        compiler_params=_cparams("arbitrary", "arbitrary"),
        name="modulation",
    )(cvec, w_mod, b_mod.reshape(depth, 1, n))


_Z_CQ = 0
_Z_CKV = MLA_Q_RANK
_Z_KR = _Z_CKV + MLA_KV_RANK
_Z_KR_ROT = _Z_KR + LANES
_Z_DQ = _Z_KR_ROT + LANES
_DW = 2 * DIFF_HEADS * DIFF_HD
_Z_DQ_ROT = _Z_DQ + _DW
_Z_DK = _Z_DQ_ROT + _DW
_Z_DK_ROT = _Z_DK + _DW
_Z_DV = _Z_DK_ROT + _DW
_Z_END = _Z_DV + _DW


def _proj_ab_kernel(x_ref, mod_ref, g_ref, win_ref, qn_ref, kvn_ref, wuq_ref, wuk_ref, wuv_ref,
                    place_ref, tab_ref, q_ref, k_ref, v_ref, dq_ref, dk_ref, dv_ref):
    h = _norm_mod(x_ref[0], g_ref[0:1, :], mod_ref[0, 0:1, :], mod_ref[0, 1:2, :])
    z = _dot(h.astype(BF), win_ref[...])
    cqn = _rms(z[:, _Z_CQ:_Z_CKV], qn_ref[...]).astype(BF)
    ckvn = _rms(z[:, _Z_CKV:_Z_KR], kvn_ref[...]).astype(BF)
    q2 = _dot(cqn, wuq_ref[...])
    nq = MLA_HEADS * LANES
    cos_q, sin_q = tab_ref[0], tab_ref[1]
    for hd in range(MLA_HEADS):
        lo = hd * LANES
        q_ref[0, hd] = (q2[:, lo:lo + LANES] * cos_q + q2[:, nq + lo:nq + lo + LANES] * sin_q).astype(BF)
    kr = (z[:, _Z_KR:_Z_KR_ROT] * tab_ref[2] + z[:, _Z_KR_ROT:_Z_DQ] * tab_ref[3]).astype(BF)
    kk = _dot(ckvn, wuk_ref[...]) + _dot(kr, place_ref[...])
    vv = _dot(ckvn, wuv_ref[...])
    for hd in range(MLA_HEADS):
        lo = hd * LANES
        k_ref[0, hd] = kk[:, lo:lo + LANES].astype(BF)
        v_ref[0, hd] = vv[:, lo:lo + LANES].astype(BF)
    cos_d, sin_d = tab_ref[4], tab_ref[5]
    for hd in range(DIFF_HEADS):
        lo = hd * LANES
        dq = z[:, _Z_DQ + lo:_Z_DQ + lo + LANES] * cos_d + z[:, _Z_DQ_ROT + lo:_Z_DQ_ROT + lo + LANES] * sin_d
        dq_ref[0, hd] = (dq * DIFF_SCALE).astype(BF)
        dk = z[:, _Z_DK + lo:_Z_DK + lo + LANES] * cos_d + z[:, _Z_DK_ROT + lo:_Z_DK_ROT + lo + LANES] * sin_d
        dk_ref[0, hd] = dk.astype(BF)
        dv_ref[0, hd] = z[:, _Z_DV + lo:_Z_DV + lo + LANES].astype(BF)


def _proj_ab(x, mod, g, wts, tabs, tm):
    b, t, d = x.shape
    win, qn, kvn, wuq, wuk, wuv, place = wts
    tok = lambda hh: pl.BlockSpec((1, hh, tm, LANES), lambda bi, i: (bi, 0, i, 0))
    shp = lambda hh: jax.ShapeDtypeStruct((b, hh, t, LANES), BF)
    return pl.pallas_call(
        _proj_ab_kernel,
        grid=(b, t // tm),
        in_specs=[
            pl.BlockSpec((1, tm, d), lambda bi, i: (bi, i, 0)),
            _mod_spec(mod),
            _const_spec(g.shape),
            _const_spec(win.shape), _const_spec(qn.shape), _const_spec(kvn.shape),
            _const_spec(wuq.shape), _const_spec(wuk.shape), _const_spec(wuv.shape),
            _const_spec(place.shape),
            pl.BlockSpec((6, tm, LANES), lambda bi, i: (0, i, 0)),
        ],
        out_specs=[tok(MLA_HEADS), tok(MLA_HEADS), tok(MLA_HEADS),
                   tok(DIFF_HEADS), tok(DIFF_HEADS), tok(DIFF_HEADS)],
        out_shape=[shp(MLA_HEADS), shp(MLA_HEADS), shp(MLA_HEADS),
                   shp(DIFF_HEADS), shp(DIFF_HEADS), shp(DIFF_HEADS)],
        compiler_params=_cparams("arbitrary", "arbitrary"),
        name="proj_ab",
    )(x, mod, g, win, qn, kvn, wuq, wuk, wuv, place, tabs)


def _softmax_pv(qm, kv_refs):
    ss = [lax.dot_general(qm, k_ref[0, 0], _NT, preferred_element_type=F32) for k_ref, _ in kv_refs]
    m = functools.reduce(jnp.maximum, [jnp.max(s, axis=-1, keepdims=True) for s in ss])
    ps = [jnp.exp2(s - m) for s in ss]
    l = functools.reduce(jnp.add, [jnp.sum(p, axis=-1, keepdims=True) for p in ps])
    o = functools.reduce(jnp.add, [_dot(p.astype(BF), v_ref[0, 0]) for p, (_, v_ref) in zip(ps, kv_refs)])
    return o / l


def _attn_kernel(*refs, nseg, mode, lam_init):
    q_ref = refs[0]
    kv_refs = [(refs[1 + 2 * s], refs[2 + 2 * s]) for s in range(nseg)]
    o_ref = refs[-1]
    q = q_ref[0, 0]
    if mode == "single":
        o_ref[0, 0] = _softmax_pv(q, kv_refs).astype(BF)
        return
    lane = lax.broadcasted_iota(jnp.int32, q.shape, 1)
    zero = jnp.zeros_like(q)
    o1 = _softmax_pv(jnp.where(lane < DIFF_HD, q, zero), kv_refs)
    o2 = _softmax_pv(jnp.where(lane >= DIFF_HD, q, zero), kv_refs)
    if mode == "pair":
        o_ref[0, 0] = jnp.where(lane < DIFF_HD, o1, o2).astype(BF)
        return
    lam_ref, subln_ref = refs[1 + 2 * nseg], refs[2 + 2 * nseg]
    lv = lam_ref[...]
    lam = (jnp.exp(jnp.sum(lv[0:1] * lv[1:2], axis=-1, keepdims=True))
           - jnp.exp(jnp.sum(lv[2:3] * lv[3:4], axis=-1, keepdims=True)) + lam_init)
    o_ref[0, 0] = (_rms(o1 - lam * o2, subln_ref[...]) * (1 - lam_init)).astype(BF)


def _attention(q, kvs, tq, mode="single", lam_vecs=None, subln=None, lam_init=0.0):
    b, nh, lq, _ = q.shape
    diff = mode == "diff"
    in_specs = [pl.BlockSpec((1, 1, tq, LANES), lambda bi, h, i: (bi, h, i, 0))]
    args = [q]
    for k, v in kvs:
        lk = k.shape[2]
        spec = pl.BlockSpec((1, 1, lk, LANES), lambda bi, h, i: (bi, h, 0, 0))
        in_specs += [spec, spec]
        args += [k, v]
    if diff:
        in_specs += [_const_spec(lam_vecs.shape), _const_spec(subln.shape)]
        args += [lam_vecs, subln]
    return pl.pallas_call(
        functools.partial(_attn_kernel, nseg=len(kvs), mode=mode, lam_init=lam_init),
        grid=(b, nh, lq // tq),
        in_specs=in_specs,
        out_specs=pl.BlockSpec((1, 1, tq, LANES), lambda bi, h, i: (bi, h, i, 0)),
        out_shape=jax.ShapeDtypeStruct((b, nh, lq, LANES), BF),
        compiler_params=_cparams("arbitrary", "arbitrary", "arbitrary"),
        name="attn_diff" if diff else "attn_mla",
    )(*args)


def _outproj_kernel(*refs, n_in, has_bias):
    o_refs = refs[:n_in]
    w_ref = refs[n_in]
    pos = n_in + 1
    b_ref = None
    if has_bias:
        b_ref = refs[pos]
        pos += 1
    x_ref, mod_ref, g_ref, out_ref = refs[pos:pos + 4]
    o = jnp.concatenate([r[0, h] for r in o_refs for h in range(r.shape[1])], axis=-1)
    y = _dot(o, w_ref[...])
    if has_bias:
        y = y + b_ref[...]
    out_ref[0] = x_ref[0] + mod_ref[0, 2:3, :] * _rms(y, g_ref[1:2, :])


def _outproj(os_, w, bias, x, mod, g, tm):
    b, t, d = x.shape
    in_specs = [pl.BlockSpec((1, o.shape[1], tm, LANES), lambda bi, i: (bi, 0, i, 0)) for o in os_]
    in_specs.append(_const_spec(w.shape))
    args = list(os_) + [w]
    if bias is not None:
        in_specs.append(_const_spec(bias.shape))
        args.append(bias)
    in_specs += [pl.BlockSpec((1, tm, d), lambda bi, i: (bi, i, 0)), _mod_spec(mod), _const_spec(g.shape)]
    args += [x, mod, g]
    return pl.pallas_call(
        functools.partial(_outproj_kernel, n_in=len(os_), has_bias=bias is not None),
        grid=(b, t // tm),
        in_specs=in_specs,
        out_specs=pl.BlockSpec((1, tm, d), lambda bi, i: (bi, i, 0)),
        out_shape=jax.ShapeDtypeStruct((b, t, d), F32),
        compiler_params=_cparams("arbitrary", "arbitrary"),
        name="outproj",
    )(*args)


def _swiglu_chunk(hn, w1, w3, w2):
    a = _dot(hn, w1)
    act = (a * jax.nn.sigmoid(a) * _dot(hn, w3)).astype(BF)
    return _dot(act, w2)


def _ffn_kernel(x_ref, mod_ref, g_ref, w1_ref, w3_ref, w2_ref, out_ref, hn_ref, acc_ref):
    k = pl.program_id(2)

    @pl.when(k == 0)
    def _():
        h = _norm_mod(x_ref[0], g_ref[2:3, :], mod_ref[0, 3:4, :], mod_ref[0, 4:5, :])
        hn_ref[...] = h.astype(BF)
        acc_ref[...] = jnp.zeros_like(acc_ref)

    acc_ref[...] += _swiglu_chunk(hn_ref[...], w1_ref[...], w3_ref[...], w2_ref[...])

    @pl.when(k == pl.num_programs(2) - 1)
    def _():
        out_ref[0] = x_ref[0] + mod_ref[0, 5:6, :] * _rms(acc_ref[...], g_ref[3:4, :])


def _ff_chunk(f):
    half = f // 2
    return half if f % 2 == 0 and half % LANES == 0 else f


def _ffn(x, mod, g, w1, w3, w2, tm):
    b, t, d = x.shape
    f = w1.shape[1]
    tf = _ff_chunk(f)
    return pl.pallas_call(
        _ffn_kernel,
        grid=(b, t // tm, f // tf),
        in_specs=[
            pl.BlockSpec((1, tm, d), lambda bi, i, k: (bi, i, 0)),
            _mod_spec(mod),
            _const_spec(g.shape),
            pl.BlockSpec((d, tf), lambda bi, i, k: (0, k)),
            pl.BlockSpec((d, tf), lambda bi, i, k: (0, k)),
            pl.BlockSpec((tf, d), lambda bi, i, k: (k, 0)),
        ],
        out_specs=pl.BlockSpec((1, tm, d), lambda bi, i, k: (bi, i, 0)),
        out_shape=jax.ShapeDtypeStruct((b, t, d), F32),
        scratch_shapes=[pltpu.VMEM((tm, d), BF), pltpu.VMEM((tm, d), F32)],
        compiler_params=_cparams("arbitrary", "arbitrary", "arbitrary"),
        name="ffn_dense",
    )(x, mod, g, w1, w3, w2)


def _moe_kernel(x_ref, mod_ref, g_ref, router_ref, w1_ref, w3_ref, w2_ref, out_ref,
                hn_ref, acc_ref, comb_ref):
    e = pl.program_id(2)
    k = pl.program_id(3)

    @pl.when((e == 0) & (k == 0))
    def _():
        h = _norm_mod(x_ref[0], g_ref[2:3, :], mod_ref[0, 3:4, :], mod_ref[0, 4:5, :])
        hn_ref[...] = h.astype(BF)
        acc_ref[...] = jnp.zeros_like(acc_ref)
        logits = jnp.dot(h, router_ref[...], preferred_element_type=F32, precision=lax.Precision.HIGHEST)
        lane = lax.broadcasted_iota(jnp.int32, logits.shape, 1)
        logits = jnp.where(lane < N_EXPERTS, logits, -jnp.inf)
        m1 = jnp.max(logits, axis=-1, keepdims=True)
        i1 = jnp.min(jnp.where(logits == m1, lane, LANES), axis=-1, keepdims=True)
        rest = jnp.where(lane == i1, -jnp.inf, logits)
        m2 = jnp.max(rest, axis=-1, keepdims=True)
        i2 = jnp.min(jnp.where(rest == m2, lane, LANES), axis=-1, keepdims=True)
        e2 = jnp.exp(m2 - m1)
        denom = 1.0 + e2
        comb_ref[...] = jnp.where(lane == i1, 1.0 / denom, jnp.where(lane == i2, e2 / denom, 0.0))

    lane = lax.broadcasted_iota(jnp.int32, comb_ref.shape, 1)
    gate = jnp.sum(jnp.where(lane == e, comb_ref[...], 0.0), axis=-1, keepdims=True)
    acc_ref[...] += gate * _swiglu_chunk(hn_ref[...], w1_ref[0], w3_ref[0], w2_ref[0])

    @pl.when((e == pl.num_programs(2) - 1) & (k == pl.num_programs(3) - 1))
    def _():
        out_ref[0] = x_ref[0] + mod_ref[0, 5:6, :] * _rms(acc_ref[...], g_ref[3:4, :])


def _moe(x, mod, g, router, w1, w3, w2, tm):
    b, t, d = x.shape
    ne, _, f = w1.shape
    tf = _ff_chunk(f)
    return pl.pallas_call(
        _moe_kernel,
        grid=(b, t // tm, ne, f // tf),
        in_specs=[
            pl.BlockSpec((1, tm, d), lambda bi, i, e, k: (bi, i, 0)),
            _mod_spec(mod),
            _const_spec(g.shape),
            _const_spec(router.shape),
            pl.BlockSpec((1, d, tf), lambda bi, i, e, k: (e, 0, k)),
            pl.BlockSpec((1, d, tf), lambda bi, i, e, k: (e, 0, k)),
            pl.BlockSpec((1, tf, d), lambda bi, i, e, k: (e, k, 0)),
        ],
        out_specs=pl.BlockSpec((1, tm, d), lambda bi, i, e, k: (bi, i, 0)),
        out_shape=jax.ShapeDtypeStruct((b, t, d), F32),
        scratch_shapes=[pltpu.VMEM((tm, d), BF), pltpu.VMEM((tm, d), F32), pltpu.VMEM((tm, LANES), F32)],
        compiler_params=_cparams("arbitrary", "arbitrary", "arbitrary", "arbitrary"),
        name="moe_dense",
    )(x, mod, g, router, w1, w3, w2)


def _proj_na_kernel(x_ref, mod_ref, g_ref, w_ref, b_ref, q_ref, k_ref, v_ref):
    h = _norm_mod(x_ref[0], g_ref[0:1, :], mod_ref[0, 0:1, :], mod_ref[0, 1:2, :])
    z = _dot(h.astype(BF), w_ref[...]) + b_ref[...]
    n = NA_SLOTS * LANES
    for s in range(NA_SLOTS):
        lo = s * LANES
        q_ref[0, s] = (z[:, lo:lo + LANES] * NA_SCALE).astype(BF)
        k_ref[0, s] = z[:, n + lo:n + lo + LANES].astype(BF)
        v_ref[0, s] = z[:, 2 * n + lo:2 * n + lo + LANES].astype(BF)


def _proj_na(x, mod, g, w, bias, tm):
    b, t, d = x.shape
    tok = pl.BlockSpec((1, NA_SLOTS, tm, LANES), lambda bi, i: (bi, 0, i, 0))
    shp = jax.ShapeDtypeStruct((b, NA_SLOTS, t, LANES), BF)
    return pl.pallas_call(
        _proj_na_kernel,
        grid=(b, t // tm),
        in_specs=[
            pl.BlockSpec((1, tm, d), lambda bi, i: (bi, i, 0)),
            _mod_spec(mod),
            _const_spec(g.shape),
            _const_spec(w.shape),
            _const_spec(bias.shape),
        ],
        out_specs=[tok, tok, tok],
        out_shape=[shp, shp, shp],
        compiler_params=_cparams("arbitrary", "arbitrary"),
        name="proj_na",
    )(x, mod, g, w, bias)


def _na_kernel(q_ref, kx_ref, vx_ref, kc_ref, vc_ref, bias_ref, o_ref, *, rows):
    r = pl.program_id(1)
    rs = jnp.clip(r - NA_KH // 2, 0, rows - NA_KH)
    start = pl.multiple_of(rs * GRID_W, GRID_W)
    nwin = NA_KH * GRID_W
    lane = lax.broadcasted_iota(jnp.int32, (GRID_W, LANES), 1)
    for s in range(NA_SLOTS):
        q = q_ref[0, s]
        zero = jnp.zeros_like(q)
        kw = kx_ref[0, s, pl.ds(start, nwin), :]
        vw = vx_ref[0, s, pl.ds(start, nwin), :]
        kc = kc_ref[0, s]
        vc = vc_ref[0, s]
        q2 = jnp.concatenate([jnp.where(lane < NA_HD, q, zero), jnp.where(lane >= NA_HD, q, zero)], axis=0)
        sw = lax.dot_general(q2, kw, _NT, preferred_element_type=F32) + bias_ref[0, s]
        sc = lax.dot_general(q2, kc, _NT, preferred_element_type=F32)
        m = jnp.maximum(jnp.max(sw, axis=-1, keepdims=True), jnp.max(sc, axis=-1, keepdims=True))
        pw = jnp.exp2(sw - m)
        pc = jnp.exp2(sc - m)
        l = jnp.sum(pw, axis=-1, keepdims=True) + jnp.sum(pc, axis=-1, keepdims=True)
        o2 = (_dot(pw.astype(BF), vw) + _dot(pc.astype(BF), vc)) / l
        o_ref[0, s] = jnp.where(lane < NA_HD, o2[:GRID_W], o2[GRID_W:]).astype(BF)


def _na_attention(q, kx, vx, kc, vc, bias):
    b, ns, l, _ = q.shape
    rows = l // GRID_W
    lc = kc.shape[2]
    ncase = bias.shape[0]

    def case_of(r):
        rs = jnp.clip(r - NA_KH // 2, 0, rows - NA_KH)
        return r - rs

    full = lambda n: pl.BlockSpec((1, ns, n, LANES), lambda bi, r: (bi, 0, 0, 0))
    row = pl.BlockSpec((1, ns, GRID_W, LANES), lambda bi, r: (bi, 0, r, 0))
    return pl.pallas_call(
        functools.partial(_na_kernel, rows=rows),
        grid=(b, rows),
        in_specs=[row, full(l), full(l), full(lc), full(lc),
                  pl.BlockSpec((1,) + bias.shape[1:], lambda bi, r: (case_of(r), 0, 0, 0))],
        out_specs=row,
        out_shape=jax.ShapeDtypeStruct((b, ns, l, LANES), BF),
        compiler_params=_cparams("arbitrary", "arbitrary"),
        name="na_attn",
    )(q, kx, vx, kc, vc, bias)


def _axis_tables(pos, dim):
    inv = ROPE_THETA ** (-jnp.arange(0, dim, 2, dtype=F32) / dim)
    ang = pos.astype(F32)[:, None] * inv[None, :]
    ang = jnp.concatenate([ang, ang], axis=-1)
    return jnp.cos(ang), jnp.sin(ang)


def _rope_tables(n, rope):
    if rope:
        t = jnp.arange(n, dtype=jnp.int32)
        row, col = t // GRID_W, t % GRID_W

        def cs(d):
            cr, sr = _axis_tables(row, d // 2)
            cc, sc = _axis_tables(col, d // 2)
            return jnp.concatenate([cr, cc], axis=-1), jnp.concatenate([sr, sc], axis=-1)

        c32, s32 = cs(MLA_ROPE)
        c64, s64 = cs(DIFF_HD)
    else:
        c32, s32 = jnp.ones((n, MLA_ROPE), F32), jnp.zeros((n, MLA_ROPE), F32)
        c64, s64 = jnp.ones((n, DIFF_HD), F32), jnp.zeros((n, DIFF_HD), F32)
    ones = jnp.ones((n, MLA_NOPE), F32)
    pad = lambda a: jnp.pad(a, ((0, 0), (0, LANES - a.shape[1])))
    cos_q = MLA_SCALE * pad(jnp.concatenate([ones, c32], axis=-1))
    sin_q = MLA_SCALE * pad(jnp.concatenate([jnp.zeros_like(ones), s32], axis=-1))
    return jnp.stack([cos_q, sin_q, pad(c32), pad(s32),
                      jnp.concatenate([c64, c64], axis=-1), jnp.concatenate([s64, s64], axis=-1)])


def _rot_cols(w, d):
    shp = w.shape
    w5 = w.reshape(shp[:-1] + (-1, 2, 2, d // 4))
    r = jnp.concatenate([-w5[..., 1:2, :], w5[..., 0:1, :]], axis=-2)
    return r.reshape(shp)


def _pad_cols(w, n):
    return jnp.pad(w, ((0, 0), (0, n - w.shape[1])))


def _prep_ab_weights(w_in, q_norm, kv_norm, w_uq, w_ukv, w_out):
    o_cq, o_ckv, o_kr = MLA_Q_RANK, MLA_Q_RANK + MLA_KV_RANK, MLA_Q_RANK + MLA_KV_RANK + MLA_ROPE
    o_dq, o_dk = o_kr + _DW, o_kr + 2 * _DW
    w_kr, w_dq, w_dk, w_dv = w_in[:, o_ckv:o_kr], w_in[:, o_kr:o_dq], w_in[:, o_dq:o_dk], w_in[:, o_dk:]
    win = jnp.concatenate([
        w_in[:, :o_ckv], _pad_cols(w_kr, LANES), _pad_cols(_rot_cols(w_kr, MLA_ROPE), LANES),
        w_dq, _rot_cols(w_dq, DIFF_HD), w_dk, _rot_cols(w_dk, DIFF_HD), w_dv], axis=1).astype(BF)
    rq = w_uq.shape[0]
    uq = w_uq.reshape(rq, MLA_HEADS, MLA_NOPE + MLA_ROPE)
    nope, rope = uq[..., :MLA_NOPE], uq[..., MLA_NOPE:]
    zpad = jnp.zeros((rq, MLA_HEADS, LANES - MLA_NOPE - MLA_ROPE), F32)
    main = jnp.concatenate([nope, rope, zpad], axis=-1).reshape(rq, MLA_HEADS * LANES)
    rot = jnp.concatenate([jnp.zeros_like(nope), _rot_cols(rope, MLA_ROPE), zpad], axis=-1)
    wuq = jnp.concatenate([main, rot.reshape(rq, MLA_HEADS * LANES)], axis=1).astype(BF)
    rkv = w_ukv.shape[0]
    ukv = w_ukv.reshape(rkv, MLA_HEADS, MLA_NOPE + MLA_V)
    slot = lambda a: jnp.pad(a, ((0, 0), (0, 0), (0, LANES - a.shape[-1]))).reshape(rkv, MLA_HEADS * LANES)
    wuk = slot(ukv[..., :MLA_NOPE]).astype(BF)
    wuv = slot(ukv[..., MLA_NOPE:]).astype(BF)
    place = np.zeros((LANES, MLA_HEADS * LANES), np.float32)
    for hd in range(MLA_HEADS):
        place[np.arange(MLA_ROPE), hd * LANES + MLA_NOPE + np.arange(MLA_ROPE)] = 1.0
    d = w_out.shape[1]
    wo_a = w_out[:MLA_HEADS * MLA_V].reshape(MLA_HEADS, MLA_V, d)
    wo_a = jnp.pad(wo_a, ((0, 0), (0, LANES - MLA_V), (0, 0))).reshape(MLA_HEADS * LANES, d)
    wo = jnp.concatenate([wo_a, w_out[MLA_HEADS * MLA_V:]], axis=0).astype(BF)
    proj = (win, q_norm.reshape(1, -1), kv_norm.reshape(1, -1), wuq, wuk, wuv, jnp.asarray(place, BF))
    return proj, wo


def _na_bias_table(rpb):
    nh = rpb.shape[0]
    cols = np.arange(GRID_W)
    col_start = np.clip(cols - NA_KW // 2, 0, GRID_W - NA_KW)
    kcol = np.arange(GRID_W)
    valid = (kcol[None, :] >= col_start[:, None]) & (kcol[None, :] < col_start[:, None] + NA_KW)
    dc = kcol[None, :] - cols[:, None] + (NA_KW - 1)
    onehot = (dc[None] == np.arange(2 * NA_KW - 1)[:, None, None]) & valid[None]
    tz = jnp.einsum('hrd,dck->hrck', rpb * LOG2E, jnp.asarray(onehot, F32), precision=lax.Precision.HIGHEST)
    tz = jnp.where(valid[None, None], tz, MASK_VALUE)
    cases = [tz[:, NA_KH - 1 - c:2 * NA_KH - 1 - c] for c in range(NA_KH)]
    tbl = jnp.transpose(jnp.stack(cases), (0, 1, 3, 2, 4))
    return tbl.reshape(NA_KH, nh // 2, 2 * GRID_W, NA_KH * GRID_W)


def _tile(n, pref):
    return pref if n % pref == 0 else n


def kernel(x, c, ctx, c_ctx, w_mod, b_mod, norm_g, a_w_in, a_q_norm, a_kv_norm, a_w_uq, a_w_ukv, b_lambda, b_subln,
           ab_w_out, f_w1, f_w3, f_w2, c_w_qkv, c_b_qkv, c_rpb, c_w_out, c_b_out, m_router, m_w1, m_w3, m_w2):
    b, l, d = x.shape
    lc = ctx.shape[1]
    depth = w_mod.shape[0]
    assert l % GRID_W == 0 and l // GRID_W >= NA_KH

    mod_rows = 16
    cvec = jnp.zeros((mod_rows, d), F32).at[:b].set(c).at[b].set(c_ctx)
    mod = _modulation(cvec, w_mod, b_mod)

    tm_x, tm_c = _tile(l, 512), _tile(lc, 256)
    tq_x, tq_c = _tile(l, 256), _tile(lc, 256)
    cs = ctx
    for i in range(depth):
        last = i == depth - 1
        j = i // 2
        mx = mod[i, :b].reshape(b, N_MOD, d)
        mc = mod[i, b].reshape(1, N_MOD, d)
        g = norm_g[i]
        if i % 2 == 0:
            lam_init = 0.8 - 0.6 * math.exp(-0.3 * i)
            proj_w, wo = _prep_ab_weights(a_w_in[j], a_q_norm[j], a_kv_norm[j], a_w_uq[j], a_w_ukv[j], ab_w_out[j])
            qx, kx, vx, dqx, dkx, dvx = _proj_ab(x, mx, g, proj_w, _rope_tables(l, True), tm_x)
            qc, kc, vc, dqc, dkc, dvc = _proj_ab(cs, mc, g, proj_w, _rope_tables(lc, False), tm_c)
            subln = b_subln[j].reshape(1, -1)
            oa = _attention(qx, [(kx, vx), (kc, vc)], tq_x)
            ob = _attention(dqx, [(dkx, dvx), (dkc, dvc)], tq_x, "diff", b_lambda[j], subln, lam_init)
            x = _outproj([oa, ob], wo, None, x, mx, g, tm_x)
            if not last:
                oa = _attention(qc, [(kc, vc)], tq_c)
                ob = _attention(dqc, [(dkc, dvc)], tq_c, "diff", b_lambda[j], subln, lam_init)
                cs = _outproj([oa, ob], wo, None, cs, mc, g, tm_c)
            w1, w3, w2 = f_w1[j].astype(BF), f_w3[j].astype(BF), f_w2[j].astype(BF)
            x = _ffn(x, mx, g, w1, w3, w2, tm_x)
            if not last:
                cs = _ffn(cs, mc, g, w1, w3, w2, tm_c)
        else:
            wqkv = c_w_qkv[j].astype(BF)
            bqkv = c_b_qkv[j].reshape(1, -1)
            wo = c_w_out[j].astype(BF)
            bo = c_b_out[j].reshape(1, -1)
            qx, kx, vx = _proj_na(x, mx, g, wqkv, bqkv, tm_x)
            qc, kc, vc = _proj_na(cs, mc, g, wqkv, bqkv, tm_c)
            o = _na_attention(qx, kx, vx, kc, vc, _na_bias_table(c_rpb[j]))
            x = _outproj([o], wo, bo, x, mx, g, tm_x)
            router = _pad_cols(m_router[j], LANES)
            w1, w3, w2 = m_w1[j].astype(BF), m_w3[j].astype(BF), m_w2[j].astype(BF)
            if not last:
                oc = _attention(qc, [(kc, vc)], tq_c, "pair")
                cs = _outproj([oc], wo, bo, cs, mc, g, tm_c)
            x = _moe(x, mx, g, router, w1, w3, w2, tm_x)
            if not last:
                cs = _moe(cs, mc, g, router, w1, w3, w2, tm_c)
    return x
```

```python
import functools
import math

import jax
import jax.numpy as jnp
import numpy as np
from jax import lax
from jax.experimental import pallas as pl
from jax.experimental.pallas import tpu as pltpu
from jax.experimental.pallas import tpu_sc as plsc

BF = jnp.bfloat16
F32 = jnp.float32

LANES = 128
VMEM_LIMIT = 56 * 1024 * 1024

GRID_W = 64
EPS = 1e-6
ROPE_THETA = 10000.0
N_MOD = 6

MLA_HEADS = 8
MLA_NOPE = 64
MLA_ROPE = 32
MLA_V = 64
MLA_Q_RANK = 384
MLA_KV_RANK = 256
LOG2E = math.log2(math.e)
MLA_SCALE = (MLA_NOPE + MLA_ROPE) ** -0.5 * LOG2E

DIFF_HEADS = 4
DIFF_HD = 64
DIFF_SCALE = DIFF_HD ** -0.5 * LOG2E

NA_HEADS = 16
NA_HD = 64
NA_KH = 8
NA_KW = 16
NA_SCALE = NA_HD ** -0.5 * LOG2E
NA_SLOTS = NA_HEADS * NA_HD // LANES
MASK_VALUE = -1e30

N_EXPERTS = 8

_NT = (((1,), (1,)), ((), ()))


def _cparams(*sem):
    return pltpu.CompilerParams(dimension_semantics=sem, vmem_limit_bytes=VMEM_LIMIT)


def _dot(a, b):
    return jnp.dot(a, b, preferred_element_type=F32)


def _rms(x, g):
    return x * lax.rsqrt(jnp.mean(x * x, axis=-1, keepdims=True) + EPS) * g


def _norm_mod(x, g, shift, scale):
    return _rms(x, g) * (1 + scale) + shift


def _const_spec(shape):
    return pl.BlockSpec(shape, lambda *_: (0,) * len(shape))


def _mod_spec(mod):
    if mod.shape[0] == 1:
        return pl.BlockSpec((1,) + mod.shape[1:], lambda b, *_: (0, 0, 0))
    return pl.BlockSpec((1,) + mod.shape[1:], lambda b, *_: (b, 0, 0))


def _mod_kernel(c_ref, w_ref, b_ref, o_ref):
    c = c_ref[...]
    sc = c * jax.nn.sigmoid(c)
    o_ref[0] = _dot(sc.astype(BF), w_ref[0].astype(BF)) + b_ref[0]


def _modulation(cvec, w_mod, b_mod):
    depth, d, n = w_mod.shape
    rows = cvec.shape[0]
    return pl.pallas_call(
        _mod_kernel,
        grid=(depth, n // d),
        in_specs=[
            pl.BlockSpec((rows, d), lambda i, j: (0, 0)),
            pl.BlockSpec((1, d, d), lambda i, j: (i, 0, j)),
            pl.BlockSpec((1, 1, d), lambda i, j: (i, 0, j)),
        ],
        out_specs=pl.BlockSpec((1, rows, d), lambda i, j: (i, 0, j)),
        out_shape=jax.ShapeDtypeStruct((depth, rows, n), F32),
        compiler_params=_cparams("arbitrary", "arbitrary"),
        name="modulation",
    )(cvec, w_mod, b_mod.reshape(depth, 1, n))


_Z_CQ = 0
_Z_CKV = MLA_Q_RANK
_Z_KR = _Z_CKV + MLA_KV_RANK
_Z_KR_ROT = _Z_KR + LANES
_Z_DQ = _Z_KR_ROT + LANES
_DW = 2 * DIFF_HEADS * DIFF_HD
_Z_DQ_ROT = _Z_DQ + _DW
_Z_DK = _Z_DQ_ROT + _DW
_Z_DK_ROT = _Z_DK + _DW
_Z_DV = _Z_DK_ROT + _DW
_Z_END = _Z_DV + _DW


def _proj_ab_kernel(x_ref, mod_ref, g_ref, win_ref, qn_ref, kvn_ref, wuq_ref, wuk_ref, wuv_ref,
                    place_ref, tab_ref, q_ref, k_ref, v_ref, dq_ref, dk_ref, dv_ref):
    h = _norm_mod(x_ref[0], g_ref[0:1, :], mod_ref[0, 0:1, :], mod_ref[0, 1:2, :])
    z = _dot(h.astype(BF), win_ref[...])
    cqn = _rms(z[:, _Z_CQ:_Z_CKV], qn_ref[...]).astype(BF)
    ckvn = _rms(z[:, _Z_CKV:_Z_KR], kvn_ref[...]).astype(BF)
    q2 = _dot(cqn, wuq_ref[...])
    nq = MLA_HEADS * LANES
    cos_q, sin_q = tab_ref[0], tab_ref[1]
    for hd in range(MLA_HEADS):
        lo = hd * LANES
        q_ref[0, hd] = (q2[:, lo:lo + LANES] * cos_q + q2[:, nq + lo:nq + lo + LANES] * sin_q).astype(BF)
    kr = (z[:, _Z_KR:_Z_KR_ROT] * tab_ref[2] + z[:, _Z_KR_ROT:_Z_DQ] * tab_ref[3]).astype(BF)
    kk = _dot(ckvn, wuk_ref[...]) + _dot(kr, place_ref[...])
    vv = _dot(ckvn, wuv_ref[...])
    for hd in range(MLA_HEADS):
        lo = hd * LANES
        k_ref[0, hd] = kk[:, lo:lo + LANES].astype(BF)
        v_ref[0, hd] = vv[:, lo:lo + LANES].astype(BF)
    cos_d, sin_d = tab_ref[4], tab_ref[5]
    for hd in range(DIFF_HEADS):
        lo = hd * LANES
        dq = z[:, _Z_DQ + lo:_Z_DQ + lo + LANES] * cos_d + z[:, _Z_DQ_ROT + lo:_Z_DQ_ROT + lo + LANES] * sin_d
        dq_ref[0, hd] = (dq * DIFF_SCALE).astype(BF)
        dk = z[:, _Z_DK + lo:_Z_DK + lo + LANES] * cos_d + z[:, _Z_DK_ROT + lo:_Z_DK_ROT + lo + LANES] * sin_d
        dk_ref[0, hd] = dk.astype(BF)
        dv_ref[0, hd] = z[:, _Z_DV + lo:_Z_DV + lo + LANES].astype(BF)


def _proj_ab(x, mod, g, wts, tabs, tm):
    b, t, d = x.shape
    win, qn, kvn, wuq, wuk, wuv, place = wts
    tok = lambda hh: pl.BlockSpec((1, hh, tm, LANES), lambda bi, i: (bi, 0, i, 0))
    shp = lambda hh: jax.ShapeDtypeStruct((b, hh, t, LANES), BF)
    return pl.pallas_call(
        _proj_ab_kernel,
        grid=(b, t // tm),
        in_specs=[
            pl.BlockSpec((1, tm, d), lambda bi, i: (bi, i, 0)),
            _mod_spec(mod),
            _const_spec(g.shape),
            _const_spec(win.shape), _const_spec(qn.shape), _const_spec(kvn.shape),
            _const_spec(wuq.shape), _const_spec(wuk.shape), _const_spec(wuv.shape),
            _const_spec(place.shape),
            pl.BlockSpec((6, tm, LANES), lambda bi, i: (0, i, 0)),
        ],
        out_specs=[tok(MLA_HEADS), tok(MLA_HEADS), tok(MLA_HEADS),
                   tok(DIFF_HEADS), tok(DIFF_HEADS), tok(DIFF_HEADS)],
        out_shape=[shp(MLA_HEADS), shp(MLA_HEADS), shp(MLA_HEADS),
                   shp(DIFF_HEADS), shp(DIFF_HEADS), shp(DIFF_HEADS)],
        compiler_params=_cparams("arbitrary", "arbitrary"),
        name="proj_ab",
    )(x, mod, g, win, qn, kvn, wuq, wuk, wuv, place, tabs)


def _softmax_pv(qm, kv_refs):
    ss = [lax.dot_general(qm, k_ref[0, 0], _NT, preferred_element_type=F32) for k_ref, _ in kv_refs]
    m = functools.reduce(jnp.maximum, [jnp.max(s, axis=-1, keepdims=True) for s in ss])
    ps = [jnp.exp2(s - m) for s in ss]
    l = functools.reduce(jnp.add, [jnp.sum(p, axis=-1, keepdims=True) for p in ps])
    o = functools.reduce(jnp.add, [_dot(p.astype(BF), v_ref[0, 0]) for p, (_, v_ref) in zip(ps, kv_refs)])
    return o / l


def _attn_kernel(*refs, nseg, mode, lam_init):
    q_ref = refs[0]
    kv_refs = [(refs[1 + 2 * s], refs[2 + 2 * s]) for s in range(nseg)]
    o_ref = refs[-1]
    q = q_ref[0, 0]
    if mode == "single":
        o_ref[0, 0] = _softmax_pv(q, kv_refs).astype(BF)
        return
    lane = lax.broadcasted_iota(jnp.int32, q.shape, 1)
    zero = jnp.zeros_like(q)
    o1 = _softmax_pv(jnp.where(lane < DIFF_HD, q, zero), kv_refs)
    o2 = _softmax_pv(jnp.where(lane >= DIFF_HD, q, zero), kv_refs)
    if mode == "pair":
        o_ref[0, 0] = jnp.where(lane < DIFF_HD, o1, o2).astype(BF)
        return
    lam_ref, subln_ref = refs[1 + 2 * nseg], refs[2 + 2 * nseg]
    lv = lam_ref[...]
    lam = (jnp.exp(jnp.sum(lv[0:1] * lv[1:2], axis=-1, keepdims=True))
           - jnp.exp(jnp.sum(lv[2:3] * lv[3:4], axis=-1, keepdims=True)) + lam_init)
    o_ref[0, 0] = (_rms(o1 - lam * o2, subln_ref[...]) * (1 - lam_init)).astype(BF)


def _attention(q, kvs, tq, mode="single", lam_vecs=None, subln=None, lam_init=0.0):
    b, nh, lq, _ = q.shape
    diff = mode == "diff"
    in_specs = [pl.BlockSpec((1, 1, tq, LANES), lambda bi, h, i: (bi, h, i, 0))]
    args = [q]
    for k, v in kvs:
        lk = k.shape[2]
        spec = pl.BlockSpec((1, 1, lk, LANES), lambda bi, h, i: (bi, h, 0, 0))
        in_specs += [spec, spec]
        args += [k, v]
    if diff:
        in_specs += [_const_spec(lam_vecs.shape), _const_spec(subln.shape)]
        args += [lam_vecs, subln]
    return pl.pallas_call(
        functools.partial(_attn_kernel, nseg=len(kvs), mode=mode, lam_init=lam_init),
        grid=(b, nh, lq // tq),
        in_specs=in_specs,
        out_specs=pl.BlockSpec((1, 1, tq, LANES), lambda bi, h, i: (bi, h, i, 0)),
        out_shape=jax.ShapeDtypeStruct((b, nh, lq, LANES), BF),
        compiler_params=_cparams("arbitrary", "arbitrary", "arbitrary"),
        name="attn_diff" if diff else "attn_mla",
    )(*args)


def _outproj_kernel(*refs, n_in, has_bias):
    o_refs = refs[:n_in]
    w_ref = refs[n_in]
    pos = n_in + 1
    b_ref = None
    if has_bias:
        b_ref = refs[pos]
        pos += 1
    x_ref, mod_ref, g_ref, out_ref = refs[pos:pos + 4]
    o = jnp.concatenate([r[0, h] for r in o_refs for h in range(r.shape[1])], axis=-1)
    y = _dot(o, w_ref[...])
    if has_bias:
        y = y + b_ref[...]
    out_ref[0] = x_ref[0] + mod_ref[0, 2:3, :] * _rms(y, g_ref[1:2, :])


def _outproj(os_, w, bias, x, mod, g, tm):
    b, t, d = x.shape
    in_specs = [pl.BlockSpec((1, o.shape[1], tm, LANES), lambda bi, i: (bi, 0, i, 0)) for o in os_]
    in_specs.append(_const_spec(w.shape))
    args = list(os_) + [w]
    if bias is not None:
        in_specs.append(_const_spec(bias.shape))
        args.append(bias)
    in_specs += [pl.BlockSpec((1, tm, d), lambda bi, i: (bi, i, 0)), _mod_spec(mod), _const_spec(g.shape)]
    args += [x, mod, g]
    return pl.pallas_call(
        functools.partial(_outproj_kernel, n_in=len(os_), has_bias=bias is not None),
        grid=(b, t // tm),
        in_specs=in_specs,
        out_specs=pl.BlockSpec((1, tm, d), lambda bi, i: (bi, i, 0)),
        out_shape=jax.ShapeDtypeStruct((b, t, d), F32),
        compiler_params=_cparams("arbitrary", "arbitrary"),
        name="outproj",
    )(*args)


def _swiglu_chunk(hn, w1, w3, w2):
    a = _dot(hn, w1)
    act = (a * jax.nn.sigmoid(a) * _dot(hn, w3)).astype(BF)
    return _dot(act, w2)


def _ffn_kernel(x_ref, mod_ref, g_ref, w1_ref, w3_ref, w2_ref, out_ref, hn_ref, acc_ref):
    k = pl.program_id(2)

    @pl.when(k == 0)
    def _():
        h = _norm_mod(x_ref[0], g_ref[2:3, :], mod_ref[0, 3:4, :], mod_ref[0, 4:5, :])
        hn_ref[...] = h.astype(BF)
        acc_ref[...] = jnp.zeros_like(acc_ref)

    acc_ref[...] += _swiglu_chunk(hn_ref[...], w1_ref[...], w3_ref[...], w2_ref[...])

    @pl.when(k == pl.num_programs(2) - 1)
    def _():
        out_ref[0] = x_ref[0] + mod_ref[0, 5:6, :] * _rms(acc_ref[...], g_ref[3:4, :])


def _ff_chunk(f):
    half = f // 2
    return half if f % 2 == 0 and half % LANES == 0 else f


def _ffn(x, mod, g, w1, w3, w2, tm):
    b, t, d = x.shape
    f = w1.shape[1]
    tf = _ff_chunk(f)
    return pl.pallas_call(
        _ffn_kernel,
        grid=(b, t // tm, f // tf),
        in_specs=[
            pl.BlockSpec((1, tm, d), lambda bi, i, k: (bi, i, 0)),
            _mod_spec(mod),
            _const_spec(g.shape),
            pl.BlockSpec((d, tf), lambda bi, i, k: (0, k)),
            pl.BlockSpec((d, tf), lambda bi, i, k: (0, k)),
            pl.BlockSpec((tf, d), lambda bi, i, k: (k, 0)),
        ],
        out_specs=pl.BlockSpec((1, tm, d), lambda bi, i, k: (bi, i, 0)),
        out_shape=jax.ShapeDtypeStruct((b, t, d), F32),
        scratch_shapes=[pltpu.VMEM((tm, d), BF), pltpu.VMEM((tm, d), F32)],
        compiler_params=_cparams("arbitrary", "arbitrary", "arbitrary"),
        name="ffn_dense",
    )(x, mod, g, w1, w3, w2)


MOE_ROW_TILE = 512
PACK_CHUNKS = 4
SC_WINDOW = 128
_HI_MASK = -65536
_LO_MASK = 65535


def _pack_rows(v, out_ref):
    half = v.shape[1] // 2
    vb = v.astype(BF).astype(F32)
    lo = (pltpu.bitcast(vb[:, :half], jnp.int32) >> 16) & _LO_MASK
    hi = pltpu.bitcast(vb[:, half:], jnp.int32) & _HI_MASK
    w = lo | hi
    for c in range(PACK_CHUNKS):
        out_ref[c] = w[:, c * LANES:(c + 1) * LANES]


def _unpack_rows(ref):
    w = jnp.concatenate([ref[c] for c in range(PACK_CHUNKS)], axis=-1)
    lo = pltpu.bitcast(w << 16, F32)
    hi = pltpu.bitcast(w & _HI_MASK, F32)
    return jnp.concatenate([lo, hi], axis=-1)


def _router_kernel(x_ref, mod_ref, g_ref, router_ref, tri_ref, hp_ref, meta_ref, gate_ref, cnt_ref, carry_ref):
    @pl.when((pl.program_id(0) == 0) & (pl.program_id(1) == 0))
    def _():
        carry_ref[...] = jnp.zeros_like(carry_ref)

    h = _norm_mod(x_ref[0], g_ref[2:3, :], mod_ref[0, 3:4, :], mod_ref[0, 4:5, :])
    _pack_rows(h, hp_ref)
    logits = jnp.dot(h, router_ref[...], preferred_element_type=F32, precision=lax.Precision.HIGHEST)
    lane = lax.broadcasted_iota(jnp.int32, logits.shape, 1)
    logits = jnp.where(lane < N_EXPERTS, logits, -jnp.inf)
    m1 = jnp.max(logits, axis=-1, keepdims=True)
    i1 = jnp.min(jnp.where(logits == m1, lane, LANES), axis=-1, keepdims=True)
    rest = jnp.where(lane == i1, -jnp.inf, logits)
    m2 = jnp.max(rest, axis=-1, keepdims=True)
    i2 = jnp.min(jnp.where(rest == m2, lane, LANES), axis=-1, keepdims=True)
    e2 = jnp.exp(m2 - m1)
    denom = 1.0 + e2
    assigned = jnp.where((lane == i1) | (lane == i2), 1.0, 0.0)
    ranks = _dot(tri_ref[...], assigned.astype(BF)) + carry_ref[...]
    r1 = jnp.sum(jnp.where(lane == i1, ranks, 0.0), axis=-1, keepdims=True).astype(jnp.int32)
    r2 = jnp.sum(jnp.where(lane == i2, ranks, 0.0), axis=-1, keepdims=True).astype(jnp.int32)
    carry_ref[...] += jnp.sum(assigned, axis=0, keepdims=True)
    cnt_ref[...] = carry_ref[...]
    col = lax.broadcasted_iota(jnp.int32, meta_ref.shape, 1)
    meta_ref[...] = jnp.where(col == 0, i1, jnp.where(col == 1, i2, jnp.where(col == 2, r1, r2)))
    gate_ref[...] = jnp.where(col == 0, 1.0 / denom, e2 / denom)


def _router(x, mod, g, router, tm):
    b, t, d = x.shape
    n = b * t
    nt = t // tm
    tri = jnp.asarray(np.tril(np.ones((tm, tm), np.float32), -1), BF)
    row_spec = lambda w, dt: (pl.BlockSpec((tm, w), lambda bi, i: (bi * nt + i, 0)), jax.ShapeDtypeStruct((n, w), dt))
    meta_spec, meta_shape = row_spec(8, jnp.int32)
    gate_spec, gate_shape = row_spec(8, F32)
    return pl.pallas_call(
        _router_kernel,
        grid=(b, nt),
        in_specs=[
            pl.BlockSpec((1, tm, d), lambda bi, i: (bi, i, 0)),
            _mod_spec(mod),
            _const_spec(g.shape),
            _const_spec(router.shape),
            _const_spec(tri.shape),
        ],
        out_specs=[
            pl.BlockSpec((PACK_CHUNKS, tm, LANES), lambda bi, i: (0, bi * nt + i, 0)),
            meta_spec, gate_spec,
            pl.BlockSpec((1, LANES), lambda bi, i: (0, 0)),
        ],
        out_shape=[jax.ShapeDtypeStruct((PACK_CHUNKS, n, LANES), jnp.int32), meta_shape, gate_shape,
                   jax.ShapeDtypeStruct((1, LANES), F32)],
        scratch_shapes=[pltpu.VMEM((1, LANES), F32)],
        compiler_params=_cparams("arbitrary", "arbitrary"),
        name="moe_router",
    )(x, mod, g, router, tri)


def _sc_mesh():
    return plsc.VectorSubcoreMesh(core_axis_name="core", subcore_axis_name="subcore")


def _sc_scatter_rows(rows, idx_a, idx_b, n_out):
    nrows = rows.shape[0]

    @pl.kernel(out_type=jax.ShapeDtypeStruct((n_out, LANES), rows.dtype), mesh=_sc_mesh(), scratch_types=[])
    def scatter_kernel(x_hbm, ia_hbm, ib_hbm, o_hbm):
        def body(x_vmem, ia_vmem, ib_vmem):
            pltpu.sync_copy(x_vmem, o_hbm.at[ia_vmem.at[0]])
            pltpu.sync_copy(x_vmem, o_hbm.at[ib_vmem.at[0]])

        pltpu.emit_pipeline(
            body,
            grid=(nrows // SC_WINDOW,),
            in_specs=[pl.BlockSpec((SC_WINDOW, LANES), lambda i: (i, 0)),
                      pl.BlockSpec((1, SC_WINDOW), lambda i: (0, i)),
                      pl.BlockSpec((1, SC_WINDOW), lambda i: (0, i))],
            out_specs=[],
            core_axis_name=("core", "subcore"),
            dimension_semantics=(pltpu.PARALLEL,),
        )(x_hbm, ia_hbm, ib_hbm)

    return scatter_kernel(rows, idx_a, idx_b)


def _sc_gather_rows(table, idx_a, idx_b):
    nrows = idx_a.shape[1]
    out = jax.ShapeDtypeStruct((nrows, LANES), table.dtype)

    @pl.kernel(out_type=(out, out), mesh=_sc_mesh(), scratch_types=[])
    def gather_kernel(t_hbm, ia_hbm, ib_hbm, oa_hbm, ob_hbm):
        def body(ia_vmem, ib_vmem, oa_vmem, ob_vmem):
            pltpu.sync_copy(t_hbm.at[ia_vmem.at[0]], oa_vmem)
            pltpu.sync_copy(t_hbm.at[ib_vmem.at[0]], ob_vmem)

        pltpu.emit_pipeline(
            body,
            grid=(nrows // SC_WINDOW,),
            in_specs=[pl.BlockSpec((1, SC_WINDOW), lambda i: (0, i)),
                      pl.BlockSpec((1, SC_WINDOW), lambda i: (0, i))],
            out_specs=[pl.BlockSpec((SC_WINDOW, LANES), lambda i: (i, 0)),
                       pl.BlockSpec((SC_WINDOW, LANES), lambda i: (i, 0))],
            core_axis_name=("core", "subcore"),
            dimension_semantics=(pltpu.PARALLEL,),
        )(ia_hbm, ib_hbm, oa_hbm, ob_hbm)

    return gather_kernel(table, idx_a, idx_b)


def _experts_kernel(te_ref, tv_ref, xs_ref, w1_ref, w3_ref, w2_ref, ys_ref, hn_ref, acc_ref):
    del te_ref
    k = pl.program_id(1)
    valid = tv_ref[pl.program_id(0)]

    @pl.when(k == 0)
    def _():
        row = lax.broadcasted_iota(jnp.int32, (hn_ref.shape[0], 1), 0)
        hn_ref[...] = jnp.where(row < valid, _unpack_rows(xs_ref), 0.0).astype(BF)
        acc_ref[...] = jnp.zeros_like(acc_ref)

    @pl.when(valid > 0)
    def _():
        acc_ref[...] += _swiglu_chunk(hn_ref[...], w1_ref[0], w3_ref[0], w2_ref[0])

    @pl.when(k == pl.num_programs(1) - 1)
    def _():
        _pack_rows(acc_ref[...], ys_ref)


def _experts(xs, tile_expert, tile_valid, w1, w3, w2):
    _, p, _ = xs.shape
    ne, d, f = w1.shape
    tf = _ff_chunk(f)
    nk = f // tf
    tr = MOE_ROW_TILE

    def kk(j, k, tv):
        return jnp.where(tv[j] > 0, k, nk - 1)

    rows = pl.BlockSpec((PACK_CHUNKS, tr, LANES), lambda j, k, te, tv: (0, j, 0))
    return pl.pallas_call(
        _experts_kernel,
        grid_spec=pltpu.PrefetchScalarGridSpec(
            num_scalar_prefetch=2,
            grid=(p // tr, nk),
            in_specs=[
                rows,
                pl.BlockSpec((1, d, tf), lambda j, k, te, tv: (te[j], 0, kk(j, k, tv))),
                pl.BlockSpec((1, d, tf), lambda j, k, te, tv: (te[j], 0, kk(j, k, tv))),
                pl.BlockSpec((1, tf, d), lambda j, k, te, tv: (te[j], kk(j, k, tv), 0)),
            ],
            out_specs=rows,
            scratch_shapes=[pltpu.VMEM((tr, d), BF), pltpu.VMEM((tr, d), F32)],
        ),
        out_shape=jax.ShapeDtypeStruct(xs.shape, jnp.int32),
        compiler_params=_cparams("arbitrary", "arbitrary"),
        name="moe_experts",
    )(tile_expert, tile_valid, xs, w1, w3, w2)


def _combine_kernel(x_ref, ya_ref, yb_ref, gate_ref, mod_ref, g_ref, out_ref):
    gates = gate_ref[...]
    fx = gates[:, 0:1] * _unpack_rows(ya_ref) + gates[:, 1:2] * _unpack_rows(yb_ref)
    out_ref[0] = x_ref[0] + mod_ref[0, 5:6, :] * _rms(fx, g_ref[3:4, :])


def _combine(x, ya, yb, gates, mod, g, tm):
    b, t, d = x.shape
    nt = t // tm
    packed = pl.BlockSpec((PACK_CHUNKS, tm, LANES), lambda bi, i: (0, bi * nt + i, 0))
    return pl.pallas_call(
        _combine_kernel,
        grid=(b, nt),
        in_specs=[
            pl.BlockSpec((1, tm, d), lambda bi, i: (bi, i, 0)),
            packed, packed,
            pl.BlockSpec((tm, gates.shape[1]), lambda bi, i: (bi * nt + i, 0)),
            _mod_spec(mod),
            _const_spec(g.shape),
        ],
        out_specs=pl.BlockSpec((1, tm, d), lambda bi, i: (bi, i, 0)),
        out_shape=jax.ShapeDtypeStruct((b, t, d), F32),
        compiler_params=_cparams("arbitrary", "arbitrary"),
        name="moe_combine",
    )(x, ya, yb, gates, mod, g)


def _moe(x, mod, g, router, w1, w3, w2, tm):
    b, t, d = x.shape
    n = b * t
    ne = w1.shape[0]
    tr = MOE_ROW_TILE
    assert d == 2 * PACK_CHUNKS * LANES and (PACK_CHUNKS * n) % (SC_WINDOW * 32) == 0
    hp, meta, gates, counts = _router(x, mod, g, router, tm)

    ntile = 2 * n // tr + ne
    p = ntile * tr
    cnt = counts[0, :ne].astype(jnp.int32)
    tiles = (cnt + tr - 1) // tr
    tile_end = jnp.cumsum(tiles)
    tile_start = tile_end - tiles
    base = tile_start * tr
    eids = jnp.arange(ne, dtype=jnp.int32)
    base_of = lambda e: jnp.sum(jnp.where(e[:, None] == eids[None, :], base[None, :], 0), axis=-1)
    pos_a = base_of(meta[:, 0]) + meta[:, 2]
    pos_b = base_of(meta[:, 1]) + meta[:, 3]
    chunk = jnp.arange(PACK_CHUNKS, dtype=jnp.int32)[:, None] * p
    idx_a = (chunk + pos_a[None, :]).reshape(1, PACK_CHUNKS * n)
    idx_b = (chunk + pos_b[None, :]).reshape(1, PACK_CHUNKS * n)
    tj = jnp.arange(ntile, dtype=jnp.int32)
    tile_expert = jnp.minimum(jnp.sum(tj[:, None] >= tile_end[None, :], axis=-1), ne - 1).astype(jnp.int32)
    done = jnp.sum(jnp.where(tile_expert[:, None] == eids[None, :], tile_start[None, :], 0), axis=-1)
    left = jnp.sum(jnp.where(tile_expert[:, None] == eids[None, :], cnt[None, :], 0), axis=-1) - (tj - done) * tr
    tile_valid = jnp.where(tj < tile_end[-1], jnp.clip(left, 0, tr), 0).astype(jnp.int32)

    xs = _sc_scatter_rows(hp.reshape(PACK_CHUNKS * n, LANES), idx_a, idx_b, PACK_CHUNKS * p)
    ys = _experts(xs.reshape(PACK_CHUNKS, p, LANES), tile_expert, tile_valid, w1, w3, w2)
    ya, yb = _sc_gather_rows(ys.reshape(PACK_CHUNKS * p, LANES), idx_a, idx_b)
    shp = (PACK_CHUNKS, n, LANES)
    return _combine(x, ya.reshape(shp), yb.reshape(shp), gates, mod, g, tm)


def _proj_na_kernel(x_ref, mod_ref, g_ref, w_ref, b_ref, q_ref, k_ref, v_ref):
    h = _norm_mod(x_ref[0], g_ref[0:1, :], mod_ref[0, 0:1, :], mod_ref[0, 1:2, :])
    z = _dot(h.astype(BF), w_ref[...]) + b_ref[...]
    n = NA_SLOTS * LANES
    for s in range(NA_SLOTS):
        lo = s * LANES
        q_ref[0, s] = (z[:, lo:lo + LANES] * NA_SCALE).astype(BF)
        k_ref[0, s] = z[:, n + lo:n + lo + LANES].astype(BF)
        v_ref[0, s] = z[:, 2 * n + lo:2 * n + lo + LANES].astype(BF)


def _proj_na(x, mod, g, w, bias, tm):
    b, t, d = x.shape
    tok = pl.BlockSpec((1, NA_SLOTS, tm, LANES), lambda bi, i: (bi, 0, i, 0))
    shp = jax.ShapeDtypeStruct((b, NA_SLOTS, t, LANES), BF)
    return pl.pallas_call(
        _proj_na_kernel,
        grid=(b, t // tm),
        in_specs=[
            pl.BlockSpec((1, tm, d), lambda bi, i: (bi, i, 0)),
            _mod_spec(mod),
            _const_spec(g.shape),
            _const_spec(w.shape),
            _const_spec(bias.shape),
        ],
        out_specs=[tok, tok, tok],
        out_shape=[shp, shp, shp],
        compiler_params=_cparams("arbitrary", "arbitrary"),
        name="proj_na",
    )(x, mod, g, w, bias)


def _na_kernel(q_ref, kx_ref, vx_ref, kc_ref, vc_ref, bias_ref, o_ref, *, rows):
    r = pl.program_id(1)
    rs = jnp.clip(r - NA_KH // 2, 0, rows - NA_KH)
    start = pl.multiple_of(rs * GRID_W, GRID_W)
    nwin = NA_KH * GRID_W
    lane = lax.broadcasted_iota(jnp.int32, (GRID_W, LANES), 1)
    for s in range(NA_SLOTS):
        q = q_ref[0, s]
        zero = jnp.zeros_like(q)
        kw = kx_ref[0, s, pl.ds(start, nwin), :]
        vw = vx_ref[0, s, pl.ds(start, nwin), :]
        kc = kc_ref[0, s]
        vc = vc_ref[0, s]
        q2 = jnp.concatenate([jnp.where(lane < NA_HD, q, zero), jnp.where(lane >= NA_HD, q, zero)], axis=0)
        sw = lax.dot_general(q2, kw, _NT, preferred_element_type=F32) + bias_ref[0, s]
        sc = lax.dot_general(q2, kc, _NT, preferred_element_type=F32)
        m = jnp.maximum(jnp.max(sw, axis=-1, keepdims=True), jnp.max(sc, axis=-1, keepdims=True))
        pw = jnp.exp2(sw - m)
        pc = jnp.exp2(sc - m)
        l = jnp.sum(pw, axis=-1, keepdims=True) + jnp.sum(pc, axis=-1, keepdims=True)
        o2 = (_dot(pw.astype(BF), vw) + _dot(pc.astype(BF), vc)) / l
        o_ref[0, s] = jnp.where(lane < NA_HD, o2[:GRID_W], o2[GRID_W:]).astype(BF)


def _na_attention(q, kx, vx, kc, vc, bias):
    b, ns, l, _ = q.shape
    rows = l // GRID_W
    lc = kc.shape[2]
    ncase = bias.shape[0]

    def case_of(r):
        rs = jnp.clip(r - NA_KH // 2, 0, rows - NA_KH)
        return r - rs

    full = lambda n: pl.BlockSpec((1, ns, n, LANES), lambda bi, r: (bi, 0, 0, 0))
    row = pl.BlockSpec((1, ns, GRID_W, LANES), lambda bi, r: (bi, 0, r, 0))
    return pl.pallas_call(
        functools.partial(_na_kernel, rows=rows),
        grid=(b, rows),
        in_specs=[row, full(l), full(l), full(lc), full(lc),
                  pl.BlockSpec((1,) + bias.shape[1:], lambda bi, r: (case_of(r), 0, 0, 0))],
        out_specs=row,
        out_shape=jax.ShapeDtypeStruct((b, ns, l, LANES), BF),
        compiler_params=_cparams("arbitrary", "arbitrary"),
        name="na_attn",
    )(q, kx, vx, kc, vc, bias)


def _axis_tables(pos, dim):
    inv = ROPE_THETA ** (-jnp.arange(0, dim, 2, dtype=F32) / dim)
    ang = pos.astype(F32)[:, None] * inv[None, :]
    ang = jnp.concatenate([ang, ang], axis=-1)
    return jnp.cos(ang), jnp.sin(ang)


def _rope_tables(n, rope):
    if rope:
        t = jnp.arange(n, dtype=jnp.int32)
        row, col = t // GRID_W, t % GRID_W

        def cs(d):
            cr, sr = _axis_tables(row, d // 2)
            cc, sc = _axis_tables(col, d // 2)
            return jnp.concatenate([cr, cc], axis=-1), jnp.concatenate([sr, sc], axis=-1)

        c32, s32 = cs(MLA_ROPE)
        c64, s64 = cs(DIFF_HD)
    else:
        c32, s32 = jnp.ones((n, MLA_ROPE), F32), jnp.zeros((n, MLA_ROPE), F32)
        c64, s64 = jnp.ones((n, DIFF_HD), F32), jnp.zeros((n, DIFF_HD), F32)
    ones = jnp.ones((n, MLA_NOPE), F32)
    pad = lambda a: jnp.pad(a, ((0, 0), (0, LANES - a.shape[1])))
    cos_q = MLA_SCALE * pad(jnp.concatenate([ones, c32], axis=-1))
    sin_q = MLA_SCALE * pad(jnp.concatenate([jnp.zeros_like(ones), s32], axis=-1))
    return jnp.stack([cos_q, sin_q, pad(c32), pad(s32),
                      jnp.concatenate([c64, c64], axis=-1), jnp.concatenate([s64, s64], axis=-1)])


def _rot_cols(w, d):
    shp = w.shape
    w5 = w.reshape(shp[:-1] + (-1, 2, 2, d // 4))
    r = jnp.concatenate([-w5[..., 1:2, :], w5[..., 0:1, :]], axis=-2)
    return r.reshape(shp)


def _pad_cols(w, n):
    return jnp.pad(w, ((0, 0), (0, n - w.shape[1])))


def _prep_ab_weights(w_in, q_norm, kv_norm, w_uq, w_ukv, w_out):
    o_cq, o_ckv, o_kr = MLA_Q_RANK, MLA_Q_RANK + MLA_KV_RANK, MLA_Q_RANK + MLA_KV_RANK + MLA_ROPE
    o_dq, o_dk = o_kr + _DW, o_kr + 2 * _DW
    w_kr, w_dq, w_dk, w_dv = w_in[:, o_ckv:o_kr], w_in[:, o_kr:o_dq], w_in[:, o_dq:o_dk], w_in[:, o_dk:]
    win = jnp.concatenate([
        w_in[:, :o_ckv], _pad_cols(w_kr, LANES), _pad_cols(_rot_cols(w_kr, MLA_ROPE), LANES),
        w_dq, _rot_cols(w_dq, DIFF_HD), w_dk, _rot_cols(w_dk, DIFF_HD), w_dv], axis=1).astype(BF)
    rq = w_uq.shape[0]
    uq = w_uq.reshape(rq, MLA_HEADS, MLA_NOPE + MLA_ROPE)
    nope, rope = uq[..., :MLA_NOPE], uq[..., MLA_NOPE:]
    zpad = jnp.zeros((rq, MLA_HEADS, LANES - MLA_NOPE - MLA_ROPE), F32)
    main = jnp.concatenate([nope, rope, zpad], axis=-1).reshape(rq, MLA_HEADS * LANES)
    rot = jnp.concatenate([jnp.zeros_like(nope), _rot_cols(rope, MLA_ROPE), zpad], axis=-1)
    wuq = jnp.concatenate([main, rot.reshape(rq, MLA_HEADS * LANES)], axis=1).astype(BF)
    rkv = w_ukv.shape[0]
    ukv = w_ukv.reshape(rkv, MLA_HEADS, MLA_NOPE + MLA_V)
    slot = lambda a: jnp.pad(a, ((0, 0), (0, 0), (0, LANES - a.shape[-1]))).reshape(rkv, MLA_HEADS * LANES)
    wuk = slot(ukv[..., :MLA_NOPE]).astype(BF)
    wuv = slot(ukv[..., MLA_NOPE:]).astype(BF)
    place = np.zeros((LANES, MLA_HEADS * LANES), np.float32)
    for hd in range(MLA_HEADS):
        place[np.arange(MLA_ROPE), hd * LANES + MLA_NOPE + np.arange(MLA_ROPE)] = 1.0
    d = w_out.shape[1]
    wo_a = w_out[:MLA_HEADS * MLA_V].reshape(MLA_HEADS, MLA_V, d)
    wo_a = jnp.pad(wo_a, ((0, 0), (0, LANES - MLA_V), (0, 0))).reshape(MLA_HEADS * LANES, d)
    wo = jnp.concatenate([wo_a, w_out[MLA_HEADS * MLA_V:]], axis=0).astype(BF)
    proj = (win, q_norm.reshape(1, -1), kv_norm.reshape(1, -1), wuq, wuk, wuv, jnp.asarray(place, BF))
    return proj, wo


def _na_bias_table(rpb):
    nh = rpb.shape[0]
    cols = np.arange(GRID_W)
    col_start = np.clip(cols - NA_KW // 2, 0, GRID_W - NA_KW)
    kcol = np.arange(GRID_W)
    valid = (kcol[None, :] >= col_start[:, None]) & (kcol[None, :] < col_start[:, None] + NA_KW)
    dc = kcol[None, :] - cols[:, None] + (NA_KW - 1)
    onehot = (dc[None] == np.arange(2 * NA_KW - 1)[:, None, None]) & valid[None]
    tz = jnp.einsum('hrd,dck->hrck', rpb * LOG2E, jnp.asarray(onehot, F32), precision=lax.Precision.HIGHEST)
    tz = jnp.where(valid[None, None], tz, MASK_VALUE)
    cases = [tz[:, NA_KH - 1 - c:2 * NA_KH - 1 - c] for c in range(NA_KH)]
    tbl = jnp.transpose(jnp.stack(cases), (0, 1, 3, 2, 4))
    return tbl.reshape(NA_KH, nh // 2, 2 * GRID_W, NA_KH * GRID_W)


def _tile(n, pref):
    return pref if n % pref == 0 else n


def kernel(x, c, ctx, c_ctx, w_mod, b_mod, norm_g, a_w_in, a_q_norm, a_kv_norm, a_w_uq, a_w_ukv, b_lambda, b_subln,
           ab_w_out, f_w1, f_w3, f_w2, c_w_qkv, c_b_qkv, c_rpb, c_w_out, c_b_out, m_router, m_w1, m_w3, m_w2):
    b, l, d = x.shape
    lc = ctx.shape[1]
    depth = w_mod.shape[0]
    assert l % GRID_W == 0 and l // GRID_W >= NA_KH

    mod_rows = 16
    cvec = jnp.zeros((mod_rows, d), F32).at[:b].set(c).at[b].set(c_ctx)
    mod = _modulation(cvec, w_mod, b_mod)

    tm_x, tm_c = _tile(l, 512), _tile(lc, 256)
    tq_x, tq_c = _tile(l, 256), _tile(lc, 256)
    cs = ctx
    for i in range(depth):
        last = i == depth - 1
        j = i // 2
        mx = mod[i, :b].reshape(b, N_MOD, d)
        mc = mod[i, b].reshape(1, N_MOD, d)
        g = norm_g[i]
        if i % 2 == 0:
            lam_init = 0.8 - 0.6 * math.exp(-0.3 * i)
            proj_w, wo = _prep_ab_weights(a_w_in[j], a_q_norm[j], a_kv_norm[j], a_w_uq[j], a_w_ukv[j], ab_w_out[j])
            qx, kx, vx, dqx, dkx, dvx = _proj_ab(x, mx, g, proj_w, _rope_tables(l, True), tm_x)
            qc, kc, vc, dqc, dkc, dvc = _proj_ab(cs, mc, g, proj_w, _rope_tables(lc, False), tm_c)
            subln = b_subln[j].reshape(1, -1)
            oa = _attention(qx, [(kx, vx), (kc, vc)], tq_x)
            ob = _attention(dqx, [(dkx, dvx), (dkc, dvc)], tq_x, "diff", b_lambda[j], subln, lam_init)
            x = _outproj([oa, ob], wo, None, x, mx, g, tm_x)
            if not last:
                oa = _attention(qc, [(kc, vc)], tq_c)
                ob = _attention(dqc, [(dkc, dvc)], tq_c, "diff", b_lambda[j], subln, lam_init)
                cs = _outproj([oa, ob], wo, None, cs, mc, g, tm_c)
            w1, w3, w2 = f_w1[j].astype(BF), f_w3[j].astype(BF), f_w2[j].astype(BF)
            x = _ffn(x, mx, g, w1, w3, w2, tm_x)
            if not last:
                cs = _ffn(cs, mc, g, w1, w3, w2, tm_c)
        else:
            wqkv = c_w_qkv[j].astype(BF)
            bqkv = c_b_qkv[j].reshape(1, -1)
            wo = c_w_out[j].astype(BF)
            bo = c_b_out[j].reshape(1, -1)
            qx, kx, vx = _proj_na(x, mx, g, wqkv, bqkv, tm_x)
            qc, kc, vc = _proj_na(cs, mc, g, wqkv, bqkv, tm_c)
            o = _na_attention(qx, kx, vx, kc, vc, _na_bias_table(c_rpb[j]))
            x = _outproj([o], wo, bo, x, mx, g, tm_x)
            router = _pad_cols(m_router[j], LANES)
            w1, w3, w2 = m_w1[j].astype(BF), m_w3[j].astype(BF), m_w2[j].astype(BF)
            if not last:
                oc = _attention(qc, [(kc, vc)], tq_c, "pair")
                cs = _outproj([oc], wo, bo, cs, mc, g, tm_c)
            x = _moe(x, mx, g, router, w1, w3, w2, tm_x)
            if not last:
                cs = _moe(cs, mc, g, router, w1, w3, w2, tm_c)
    return x
```

```python
import functools
import math

import jax
import jax.numpy as jnp
import numpy as np
from jax import lax
from jax.experimental import pallas as pl
from jax.experimental.pallas import tpu as pltpu
from jax.experimental.pallas import tpu_sc as plsc

BF = jnp.bfloat16
F32 = jnp.float32

LANES = 128
VMEM_LIMIT = 56 * 1024 * 1024

GRID_W = 64
EPS = 1e-6
ROPE_THETA = 10000.0
N_MOD = 6

MLA_HEADS = 8
MLA_NOPE = 64
MLA_ROPE = 32
MLA_V = 64
MLA_Q_RANK = 384
MLA_KV_RANK = 256
LOG2E = math.log2(math.e)
MLA_SCALE = (MLA_NOPE + MLA_ROPE) ** -0.5 * LOG2E

DIFF_HEADS = 4
DIFF_HD = 64
DIFF_SCALE = DIFF_HD ** -0.5 * LOG2E

NA_HEADS = 16
NA_HD = 64
NA_KH = 8
NA_KW = 16
NA_SCALE = NA_HD ** -0.5 * LOG2E
NA_SLOTS = NA_HEADS * NA_HD // LANES
MASK_VALUE = -1e30

N_EXPERTS = 8

_NT = (((1,), (1,)), ((), ()))


def _cparams(*sem):
    return pltpu.CompilerParams(dimension_semantics=sem, vmem_limit_bytes=VMEM_LIMIT)


def _dot(a, b):
    return jnp.dot(a, b, preferred_element_type=F32)


def _rms(x, g):
    return x * lax.rsqrt(jnp.mean(x * x, axis=-1, keepdims=True) + EPS) * g


def _norm_mod(x, g, shift, scale):
    return _rms(x, g) * (1 + scale) + shift


def _const_spec(shape):
    return pl.BlockSpec(shape, lambda *_: (0,) * len(shape))


def _mod_spec(mod):
    if mod.shape[0] == 1:
        return pl.BlockSpec((1,) + mod.shape[1:], lambda b, *_: (0, 0, 0))
    return pl.BlockSpec((1,) + mod.shape[1:], lambda b, *_: (b, 0, 0))


def _mod_kernel(c_ref, w_ref, b_ref, o_ref):
    c = c_ref[...]
    sc = c * jax.nn.sigmoid(c)
    o_ref[0] = _dot(sc.astype(BF), w_ref[0].astype(BF)) + b_ref[0]


def _modulation(cvec, w_mod, b_mod):
    depth, d, n = w_mod.shape
    rows = cvec.shape[0]
    return pl.pallas_call(
        _mod_kernel,
        grid=(depth, n // d),
        in_specs=[
            pl.BlockSpec((rows, d), lambda i, j: (0, 0)),
            pl.BlockSpec((1, d, d), lambda i, j: (i, 0, j)),
            pl.BlockSpec((1, 1, d), lambda i, j: (i, 0, j)),
        ],
        out_specs=pl.BlockSpec((1, rows, d), lambda i, j: (i, 0, j)),
        out_shape=jax.ShapeDtypeStruct((depth, rows, n), F32),
        compiler_params=_cparams("arbitrary", "arbitrary"),
        name="modulation",
    )(cvec, w_mod, b_mod.reshape(depth, 1, n))


_Z_CQ = 0
_Z_CKV = MLA_Q_RANK
_Z_KR = _Z_CKV + MLA_KV_RANK
_Z_KR_ROT = _Z_KR + LANES
_Z_DQ = _Z_KR_ROT + LANES
_DW = 2 * DIFF_HEADS * DIFF_HD
_Z_DQ_ROT = _Z_DQ + _DW
_Z_DK = _Z_DQ_ROT + _DW
_Z_DK_ROT = _Z_DK + _DW
_Z_DV = _Z_DK_ROT + _DW
_Z_END = _Z_DV + _DW


def _proj_ab_kernel(x_ref, mod_ref, g_ref, win_ref, qn_ref, kvn_ref, wuq_ref, wuk_ref, wuv_ref,
                    place_ref, tab_ref, q_ref, k_ref, v_ref, dq_ref, dk_ref, dv_ref):
    h = _norm_mod(x_ref[0], g_ref[0:1, :], mod_ref[0, 0:1, :], mod_ref[0, 1:2, :])
    z = _dot(h.astype(BF), win_ref[...])
    cqn = _rms(z[:, _Z_CQ:_Z_CKV], qn_ref[...]).astype(BF)
    ckvn = _rms(z[:, _Z_CKV:_Z_KR], kvn_ref[...]).astype(BF)
    q2 = _dot(cqn, wuq_ref[...])
    nq = MLA_HEADS * LANES
    cos_q, sin_q = tab_ref[0], tab_ref[1]
    for hd in range(MLA_HEADS):
        lo = hd * LANES
        q_ref[0, hd] = (q2[:, lo:lo + LANES] * cos_q + q2[:, nq + lo:nq + lo + LANES] * sin_q).astype(BF)
    kr = (z[:, _Z_KR:_Z_KR_ROT] * tab_ref[2] + z[:, _Z_KR_ROT:_Z_DQ] * tab_ref[3]).astype(BF)
    kk = _dot(ckvn, wuk_ref[...]) + _dot(kr, place_ref[...])
    vv = _dot(ckvn, wuv_ref[...])
    for hd in range(MLA_HEADS):
        lo = hd * LANES
        k_ref[0, hd] = kk[:, lo:lo + LANES].astype(BF)
        v_ref[0, hd] = vv[:, lo:lo + LANES].T.astype(BF)
    cos_d, sin_d = tab_ref[4], tab_ref[5]
    for hd in range(DIFF_HEADS):
        lo = hd * LANES
        dq = z[:, _Z_DQ + lo:_Z_DQ + lo + LANES] * cos_d + z[:, _Z_DQ_ROT + lo:_Z_DQ_ROT + lo + LANES] * sin_d
        dq_ref[0, hd] = (dq * DIFF_SCALE).astype(BF)
        dk = z[:, _Z_DK + lo:_Z_DK + lo + LANES] * cos_d + z[:, _Z_DK_ROT + lo:_Z_DK_ROT + lo + LANES] * sin_d
        dk_ref[0, hd] = dk.astype(BF)
        dv_ref[0, hd] = z[:, _Z_DV + lo:_Z_DV + lo + LANES].T.astype(BF)


def _proj_ab(x, mod, g, wts, tabs, tm):
    b, t, d = x.shape
    win, qn, kvn, wuq, wuk, wuv, place = wts
    tok = lambda hh: pl.BlockSpec((1, hh, tm, LANES), lambda bi, i: (bi, 0, i, 0))
    shp = lambda hh: jax.ShapeDtypeStruct((b, hh, t, LANES), BF)
    tok_t = lambda hh: pl.BlockSpec((1, hh, LANES, tm), lambda bi, i: (bi, 0, 0, i))
    shp_t = lambda hh: jax.ShapeDtypeStruct((b, hh, LANES, t), BF)
    return pl.pallas_call(
        _proj_ab_kernel,
        grid=(b, t // tm),
        in_specs=[
            pl.BlockSpec((1, tm, d), lambda bi, i: (bi, i, 0)),
            _mod_spec(mod),
            _const_spec(g.shape),
            _const_spec(win.shape), _const_spec(qn.shape), _const_spec(kvn.shape),
            _const_spec(wuq.shape), _const_spec(wuk.shape), _const_spec(wuv.shape),
            _const_spec(place.shape),
            pl.BlockSpec((6, tm, LANES), lambda bi, i: (0, i, 0)),
        ],
        out_specs=[tok(MLA_HEADS), tok(MLA_HEADS), tok_t(MLA_HEADS),
                   tok(DIFF_HEADS), tok(DIFF_HEADS), tok_t(DIFF_HEADS)],
        out_shape=[shp(MLA_HEADS), shp(MLA_HEADS), shp_t(MLA_HEADS),
                   shp(DIFF_HEADS), shp(DIFF_HEADS), shp_t(DIFF_HEADS)],
        compiler_params=_cparams("arbitrary", "arbitrary"),
        name="proj_ab",
    )(x, mod, g, win, qn, kvn, wuq, wuk, wuv, place, tabs)


KEY_CHUNK = 512


def _key_chunks(kv_refs):
    off = 0
    for k_ref, v_ref in kv_refs:
        lk = k_ref.shape[2]
        for c0 in range(0, lk, KEY_CHUNK):
            w = min(KEY_CHUNK, lk - c0)
            yield k_ref, v_ref, c0, w, off
            off += w


SUBLANES = 8


def _softmax_pv(qm, kv_refs, s_ref):
    tq = qm.shape[0]
    fold = lambda a: a.reshape(a.shape[0] // SUBLANES, SUBLANES, tq)
    mm = None
    for k_ref, _, c0, w, off in _key_chunks(kv_refs):
        s = lax.dot_general(k_ref[0, 0, c0:c0 + w, :], qm, _NT, preferred_element_type=F32)
        s_ref[off:off + w, :] = s
        cm = jnp.max(fold(s), axis=0)
        mm = cm if mm is None else jnp.maximum(mm, cm)
    m = jnp.max(mm, axis=0, keepdims=True)
    ll = jnp.zeros((SUBLANES, tq), F32)
    o_t = jnp.zeros((LANES, tq), F32)
    for _, v_ref, c0, w, off in _key_chunks(kv_refs):
        p = jnp.exp2(s_ref[off:off + w, :] - m)
        ll = ll + jnp.sum(fold(p), axis=0)
        o_t = o_t + _dot(v_ref[0, 0, :, c0:c0 + w], p.astype(BF))
    return (o_t / jnp.sum(ll, axis=0, keepdims=True)).T


def _attn_kernel(*refs, nseg, mode, lam_init):
    q_ref = refs[0]
    kv_refs = [(refs[1 + 2 * s], refs[2 + 2 * s]) for s in range(nseg)]
    n_scratch = 1 if mode == "single" else 2
    o_ref = refs[-1 - n_scratch]
    s_refs = refs[-n_scratch:]
    q = q_ref[0, 0]
    if mode == "single":
        o_ref[0, 0] = _softmax_pv(q, kv_refs, s_refs[0]).astype(BF)
        return
    lane = lax.broadcasted_iota(jnp.int32, q.shape, 1)
    zero = jnp.zeros_like(q)
    o1 = _softmax_pv(jnp.where(lane < DIFF_HD, q, zero), kv_refs, s_refs[0])
    o2 = _softmax_pv(jnp.where(lane >= DIFF_HD, q, zero), kv_refs, s_refs[1])
    if mode == "pair":
        o_ref[0, 0] = jnp.where(lane < DIFF_HD, o1, o2).astype(BF)
        return
    lam_ref, subln_ref = refs[1 + 2 * nseg], refs[2 + 2 * nseg]
    lv = lam_ref[...]
    lam = (jnp.exp(jnp.sum(lv[0:1] * lv[1:2], axis=-1, keepdims=True))
           - jnp.exp(jnp.sum(lv[2:3] * lv[3:4], axis=-1, keepdims=True)) + lam_init)
    o_ref[0, 0] = (_rms(o1 - lam * o2, subln_ref[...]) * (1 - lam_init)).astype(BF)


def _attention(q, kvs, tq, mode="single", lam_vecs=None, subln=None, lam_init=0.0):
    b, nh, lq, _ = q.shape
    diff = mode == "diff"
    in_specs = [pl.BlockSpec((1, 1, tq, LANES), lambda bi, h, i: (bi, h, i, 0))]
    args = [q]
    for k, v in kvs:
        lk = k.shape[2]
        in_specs += [pl.BlockSpec((1, 1, lk, LANES), lambda bi, h, i: (bi, h, 0, 0)),
                     pl.BlockSpec((1, 1, LANES, lk), lambda bi, h, i: (bi, h, 0, 0))]
        args += [k, v]
    if diff:
        in_specs += [_const_spec(lam_vecs.shape), _const_spec(subln.shape)]
        args += [lam_vecs, subln]
    nkeys = sum(k.shape[2] for k, _ in kvs)
    n_scratch = 1 if mode == "single" else 2
    return pl.pallas_call(
        functools.partial(_attn_kernel, nseg=len(kvs), mode=mode, lam_init=lam_init),
        grid=(b, nh, lq // tq),
        in_specs=in_specs,
        out_specs=pl.BlockSpec((1, 1, tq, LANES), lambda bi, h, i: (bi, h, i, 0)),
        out_shape=jax.ShapeDtypeStruct((b, nh, lq, LANES), BF),
        scratch_shapes=[pltpu.VMEM((nkeys, tq), F32)] * n_scratch,
        compiler_params=_cparams("arbitrary", "arbitrary", "arbitrary"),
        name="attn_diff" if diff else "attn_mla",
    )(*args)


def _outproj_kernel(*refs, n_in, has_bias):
    o_refs = refs[:n_in]
    w_ref = refs[n_in]
    pos = n_in + 1
    b_ref = None
    if has_bias:
        b_ref = refs[pos]
        pos += 1
    x_ref, mod_ref, g_ref, out_ref = refs[pos:pos + 4]
    o = jnp.concatenate([r[0, h] for r in o_refs for h in range(r.shape[1])], axis=-1)
    y = _dot(o, w_ref[...])
    if has_bias:
        y = y + b_ref[...]
    out_ref[0] = x_ref[0] + mod_ref[0, 2:3, :] * _rms(y, g_ref[1:2, :])


def _outproj(os_, w, bias, x, mod, g, tm):
    b, t, d = x.shape
    in_specs = [pl.BlockSpec((1, o.shape[1], tm, LANES), lambda bi, i: (bi, 0, i, 0)) for o in os_]
    in_specs.append(_const_spec(w.shape))
    args = list(os_) + [w]
    if bias is not None:
        in_specs.append(_const_spec(bias.shape))
        args.append(bias)
    in_specs += [pl.BlockSpec((1, tm, d), lambda bi, i: (bi, i, 0)), _mod_spec(mod), _const_spec(g.shape)]
    args += [x, mod, g]
    return pl.pallas_call(
        functools.partial(_outproj_kernel, n_in=len(os_), has_bias=bias is not None),
        grid=(b, t // tm),
        in_specs=in_specs,
        out_specs=pl.BlockSpec((1, tm, d), lambda bi, i: (bi, i, 0)),
        out_shape=jax.ShapeDtypeStruct((b, t, d), F32),
        compiler_params=_cparams("arbitrary", "arbitrary"),
        name="outproj",
    )(*args)


def _swiglu_chunk(hn, w1, w3, w2):
    a = _dot(hn, w1)
    act = (a * jax.nn.sigmoid(a) * _dot(hn, w3)).astype(BF)
    return _dot(act, w2)


def _ffn_kernel(x_ref, mod_ref, g_ref, w1_ref, w3_ref, w2_ref, out_ref, hn_ref, acc_ref):
    k = pl.program_id(2)

    @pl.when(k == 0)
    def _():
        h = _norm_mod(x_ref[0], g_ref[2:3, :], mod_ref[0, 3:4, :], mod_ref[0, 4:5, :])
        hn_ref[...] = h.astype(BF)
        acc_ref[...] = jnp.zeros_like(acc_ref)

    acc_ref[...] += _swiglu_chunk(hn_ref[...], w1_ref[...], w3_ref[...], w2_ref[...])

    @pl.when(k == pl.num_programs(2) - 1)
    def _():
        out_ref[0] = x_ref[0] + mod_ref[0, 5:6, :] * _rms(acc_ref[...], g_ref[3:4, :])


def _ff_chunk(f):
    half = f // 2
    return half if f % 2 == 0 and half % LANES == 0 else f


def _ffn(x, mod, g, w1, w3, w2, tm):
    b, t, d = x.shape
    f = w1.shape[1]
    tf = _ff_chunk(f)
    return pl.pallas_call(
        _ffn_kernel,
        grid=(b, t // tm, f // tf),
        in_specs=[
            pl.BlockSpec((1, tm, d), lambda bi, i, k: (bi, i, 0)),
            _mod_spec(mod),
            _const_spec(g.shape),
            pl.BlockSpec((d, tf), lambda bi, i, k: (0, k)),
            pl.BlockSpec((d, tf), lambda bi, i, k: (0, k)),
            pl.BlockSpec((tf, d), lambda bi, i, k: (k, 0)),
        ],
        out_specs=pl.BlockSpec((1, tm, d), lambda bi, i, k: (bi, i, 0)),
        out_shape=jax.ShapeDtypeStruct((b, t, d), F32),
        scratch_shapes=[pltpu.VMEM((tm, d), BF), pltpu.VMEM((tm, d), F32)],
        compiler_params=_cparams("arbitrary", "arbitrary", "arbitrary"),
        name="ffn_dense",
    )(x, mod, g, w1, w3, w2)


MOE_ROW_TILE = 512
PACK_CHUNKS = 4
SC_WINDOW = 128
_HI_MASK = -65536
_LO_MASK = 65535


def _pack_rows(v, out_ref):
    half = v.shape[1] // 2
    vb = v.astype(BF).astype(F32)
    lo = (pltpu.bitcast(vb[:, :half], jnp.int32) >> 16) & _LO_MASK
    hi = pltpu.bitcast(vb[:, half:], jnp.int32) & _HI_MASK
    w = lo | hi
    for c in range(PACK_CHUNKS):
        out_ref[c] = w[:, c * LANES:(c + 1) * LANES]


def _unpack_rows(ref):
    w = jnp.concatenate([ref[c] for c in range(PACK_CHUNKS)], axis=-1)
    lo = pltpu.bitcast(w << 16, F32)
    hi = pltpu.bitcast(w & _HI_MASK, F32)
    return jnp.concatenate([lo, hi], axis=-1)


def _router_kernel(x_ref, mod_ref, g_ref, router_ref, tri_ref, hp_ref, meta_ref, gate_ref, cnt_ref, carry_ref):
    @pl.when((pl.program_id(0) == 0) & (pl.program_id(1) == 0))
    def _():
        carry_ref[...] = jnp.zeros_like(carry_ref)

    h = _norm_mod(x_ref[0], g_ref[2:3, :], mod_ref[0, 3:4, :], mod_ref[0, 4:5, :])
    _pack_rows(h, hp_ref)
    logits = jnp.dot(h, router_ref[...], preferred_element_type=F32, precision=lax.Precision.HIGHEST)
    lane = lax.broadcasted_iota(jnp.int32, logits.shape, 1)
    logits = jnp.where(lane < N_EXPERTS, logits, -jnp.inf)
    m1 = jnp.max(logits, axis=-1, keepdims=True)
    i1 = jnp.min(jnp.where(logits == m1, lane, LANES), axis=-1, keepdims=True)
    rest = jnp.where(lane == i1, -jnp.inf, logits)
    m2 = jnp.max(rest, axis=-1, keepdims=True)
    i2 = jnp.min(jnp.where(rest == m2, lane, LANES), axis=-1, keepdims=True)
    e2 = jnp.exp(m2 - m1)
    denom = 1.0 + e2
    assigned = jnp.where((lane == i1) | (lane == i2), 1.0, 0.0)
    ranks = _dot(tri_ref[...], assigned.astype(BF)) + carry_ref[...]
    r1 = jnp.sum(jnp.where(lane == i1, ranks, 0.0), axis=-1, keepdims=True).astype(jnp.int32)
    r2 = jnp.sum(jnp.where(lane == i2, ranks, 0.0), axis=-1, keepdims=True).astype(jnp.int32)
    carry_ref[...] += jnp.sum(assigned, axis=0, keepdims=True)
    cnt_ref[...] = carry_ref[...]
    col = lax.broadcasted_iota(jnp.int32, meta_ref.shape, 1)
    meta_ref[...] = jnp.where(col == 0, i1, jnp.where(col == 1, i2, jnp.where(col == 2, r1, r2)))
    gate_ref[...] = jnp.where(col == 0, 1.0 / denom, e2 / denom)


def _router(x, mod, g, router, tm):
    b, t, d = x.shape
    n = b * t
    nt = t // tm
    tri = jnp.asarray(np.tril(np.ones((tm, tm), np.float32), -1), BF)
    row_spec = lambda w, dt: (pl.BlockSpec((tm, w), lambda bi, i: (bi * nt + i, 0)), jax.ShapeDtypeStruct((n, w), dt))
    meta_spec, meta_shape = row_spec(8, jnp.int32)
    gate_spec, gate_shape = row_spec(8, F32)
    return pl.pallas_call(
        _router_kernel,
        grid=(b, nt),
        in_specs=[
            pl.BlockSpec((1, tm, d), lambda bi, i: (bi, i, 0)),
            _mod_spec(mod),
            _const_spec(g.shape),
            _const_spec(router.shape),
            _const_spec(tri.shape),
        ],
        out_specs=[
            pl.BlockSpec((PACK_CHUNKS, tm, LANES), lambda bi, i: (0, bi * nt + i, 0)),
            meta_spec, gate_spec,
            pl.BlockSpec((1, LANES), lambda bi, i: (0, 0)),
        ],
        out_shape=[jax.ShapeDtypeStruct((PACK_CHUNKS, n, LANES), jnp.int32), meta_shape, gate_shape,
                   jax.ShapeDtypeStruct((1, LANES), F32)],
        scratch_shapes=[pltpu.VMEM((1, LANES), F32)],
        compiler_params=_cparams("arbitrary", "arbitrary"),
        name="moe_router",
    )(x, mod, g, router, tri)


def _sc_mesh():
    return plsc.VectorSubcoreMesh(core_axis_name="core", subcore_axis_name="subcore")


def _sc_scatter_rows(rows, idx_a, idx_b, n_out):
    nrows = rows.shape[0]

    @pl.kernel(out_type=jax.ShapeDtypeStruct((n_out, LANES), rows.dtype), mesh=_sc_mesh(), scratch_types=[])
    def scatter_kernel(x_hbm, ia_hbm, ib_hbm, o_hbm):
        def body(x_vmem, ia_vmem, ib_vmem):
            pltpu.sync_copy(x_vmem, o_hbm.at[ia_vmem.at[0]])
            pltpu.sync_copy(x_vmem, o_hbm.at[ib_vmem.at[0]])

        pltpu.emit_pipeline(
            body,
            grid=(nrows // SC_WINDOW,),
            in_specs=[pl.BlockSpec((SC_WINDOW, LANES), lambda i: (i, 0)),
                      pl.BlockSpec((1, SC_WINDOW), lambda i: (0, i)),
                      pl.BlockSpec((1, SC_WINDOW), lambda i: (0, i))],
            out_specs=[],
            core_axis_name=("core", "subcore"),
            dimension_semantics=(pltpu.PARALLEL,),
        )(x_hbm, ia_hbm, ib_hbm)

    return scatter_kernel(rows, idx_a, idx_b)


def _sc_gather_rows(table, idx_a, idx_b):
    nrows = idx_a.shape[1]
    out = jax.ShapeDtypeStruct((nrows, LANES), table.dtype)

    @pl.kernel(out_type=(out, out), mesh=_sc_mesh(), scratch_types=[])
    def gather_kernel(t_hbm, ia_hbm, ib_hbm, oa_hbm, ob_hbm):
        def body(ia_vmem, ib_vmem, oa_vmem, ob_vmem):
            pltpu.sync_copy(t_hbm.at[ia_vmem.at[0]], oa_vmem)
            pltpu.sync_copy(t_hbm.at[ib_vmem.at[0]], ob_vmem)

        pltpu.emit_pipeline(
            body,
            grid=(nrows // SC_WINDOW,),
            in_specs=[pl.BlockSpec((1, SC_WINDOW), lambda i: (0, i)),
                      pl.BlockSpec((1, SC_WINDOW), lambda i: (0, i))],
            out_specs=[pl.BlockSpec((SC_WINDOW, LANES), lambda i: (i, 0)),
                       pl.BlockSpec((SC_WINDOW, LANES), lambda i: (i, 0))],
            core_axis_name=("core", "subcore"),
            dimension_semantics=(pltpu.PARALLEL,),
        )(ia_hbm, ib_hbm, oa_hbm, ob_hbm)

    return gather_kernel(table, idx_a, idx_b)


def _experts_kernel(te_ref, tv_ref, xs_ref, w1_ref, w3_ref, w2_ref, ys_ref, hn_ref, acc_ref):
    del te_ref
    k = pl.program_id(1)
    valid = tv_ref[pl.program_id(0)]

    @pl.when(k == 0)
    def _():
        row = lax.broadcasted_iota(jnp.int32, (hn_ref.shape[0], 1), 0)
        hn_ref[...] = jnp.where(row < valid, _unpack_rows(xs_ref), 0.0).astype(BF)
        acc_ref[...] = jnp.zeros_like(acc_ref)

    @pl.when(valid > 0)
    def _():
        acc_ref[...] += _swiglu_chunk(hn_ref[...], w1_ref[0], w3_ref[0], w2_ref[0])

    @pl.when(k == pl.num_programs(1) - 1)
    def _():
        _pack_rows(acc_ref[...], ys_ref)


def _experts(xs, tile_expert, tile_valid, w1, w3, w2):
    _, p, _ = xs.shape
    ne, d, f = w1.shape
    tf = _ff_chunk(f)
    nk = f // tf
    tr = MOE_ROW_TILE

    def kk(j, k, tv):
        return jnp.where(tv[j] > 0, k, nk - 1)

    rows = pl.BlockSpec((PACK_CHUNKS, tr, LANES), lambda j, k, te, tv: (0, j, 0))
    return pl.pallas_call(
        _experts_kernel,
        grid_spec=pltpu.PrefetchScalarGridSpec(
            num_scalar_prefetch=2,
            grid=(p // tr, nk),
            in_specs=[
                rows,
                pl.BlockSpec((1, d, tf), lambda j, k, te, tv: (te[j], 0, kk(j, k, tv))),
                pl.BlockSpec((1, d, tf), lambda j, k, te, tv: (te[j], 0, kk(j, k, tv))),
                pl.BlockSpec((1, tf, d), lambda j, k, te, tv: (te[j], kk(j, k, tv), 0)),
            ],
            out_specs=rows,
            scratch_shapes=[pltpu.VMEM((tr, d), BF), pltpu.VMEM((tr, d), F32)],
        ),
        out_shape=jax.ShapeDtypeStruct(xs.shape, jnp.int32),
        compiler_params=_cparams("arbitrary", "arbitrary"),
        name="moe_experts",
    )(tile_expert, tile_valid, xs, w1, w3, w2)


def _combine_kernel(x_ref, ya_ref, yb_ref, gate_ref, mod_ref, g_ref, out_ref):
    gates = gate_ref[...]
    fx = gates[:, 0:1] * _unpack_rows(ya_ref) + gates[:, 1:2] * _unpack_rows(yb_ref)
    out_ref[0] = x_ref[0] + mod_ref[0, 5:6, :] * _rms(fx, g_ref[3:4, :])


def _combine(x, ya, yb, gates, mod, g, tm):
    b, t, d = x.shape
    nt = t // tm
    packed = pl.BlockSpec((PACK_CHUNKS, tm, LANES), lambda bi, i: (0, bi * nt + i, 0))
    return pl.pallas_call(
        _combine_kernel,
        grid=(b, nt),
        in_specs=[
            pl.BlockSpec((1, tm, d), lambda bi, i: (bi, i, 0)),
            packed, packed,
            pl.BlockSpec((tm, gates.shape[1]), lambda bi, i: (bi * nt + i, 0)),
            _mod_spec(mod),
            _const_spec(g.shape),
        ],
        out_specs=pl.BlockSpec((1, tm, d), lambda bi, i: (bi, i, 0)),
        out_shape=jax.ShapeDtypeStruct((b, t, d), F32),
        compiler_params=_cparams("arbitrary", "arbitrary"),
        name="moe_combine",
    )(x, ya, yb, gates, mod, g)


def _moe(x, mod, g, router, w1, w3, w2, tm):
    b, t, d = x.shape
    n = b * t
    ne = w1.shape[0]
    tr = MOE_ROW_TILE
    assert d == 2 * PACK_CHUNKS * LANES and (PACK_CHUNKS * n) % (SC_WINDOW * 32) == 0
    hp, meta, gates, counts = _router(x, mod, g, router, tm)

    ntile = 2 * n // tr + ne
    p = ntile * tr
    cnt = counts[0, :ne].astype(jnp.int32)
    tiles = (cnt + tr - 1) // tr
    tile_end = jnp.cumsum(tiles)
    tile_start = tile_end - tiles
    base = tile_start * tr
    eids = jnp.arange(ne, dtype=jnp.int32)
    base_of = lambda e: jnp.sum(jnp.where(e[:, None] == eids[None, :], base[None, :], 0), axis=-1)
    pos_a = base_of(meta[:, 0]) + meta[:, 2]
    pos_b = base_of(meta[:, 1]) + meta[:, 3]
    chunk = jnp.arange(PACK_CHUNKS, dtype=jnp.int32)[:, None] * p
    idx_a = (chunk + pos_a[None, :]).reshape(1, PACK_CHUNKS * n)
    idx_b = (chunk + pos_b[None, :]).reshape(1, PACK_CHUNKS * n)
    tj = jnp.arange(ntile, dtype=jnp.int32)
    tile_expert = jnp.minimum(jnp.sum(tj[:, None] >= tile_end[None, :], axis=-1), ne - 1).astype(jnp.int32)
    done = jnp.sum(jnp.where(tile_expert[:, None] == eids[None, :], tile_start[None, :], 0), axis=-1)
    left = jnp.sum(jnp.where(tile_expert[:, None] == eids[None, :], cnt[None, :], 0), axis=-1) - (tj - done) * tr
    tile_valid = jnp.where(tj < tile_end[-1], jnp.clip(left, 0, tr), 0).astype(jnp.int32)

    xs = _sc_scatter_rows(hp.reshape(PACK_CHUNKS * n, LANES), idx_a, idx_b, PACK_CHUNKS * p)
    ys = _experts(xs.reshape(PACK_CHUNKS, p, LANES), tile_expert, tile_valid, w1, w3, w2)
    ya, yb = _sc_gather_rows(ys.reshape(PACK_CHUNKS * p, LANES), idx_a, idx_b)
    shp = (PACK_CHUNKS, n, LANES)
    return _combine(x, ya.reshape(shp), yb.reshape(shp), gates, mod, g, tm)


def _proj_na_kernel(x_ref, mod_ref, g_ref, w_ref, b_ref, q_ref, k_ref, v_ref):
    h = _norm_mod(x_ref[0], g_ref[0:1, :], mod_ref[0, 0:1, :], mod_ref[0, 1:2, :])
    z = _dot(h.astype(BF), w_ref[...]) + b_ref[...]
    n = NA_SLOTS * LANES
    for s in range(NA_SLOTS):
        lo = s * LANES
        q_ref[0, s] = (z[:, lo:lo + LANES] * NA_SCALE).astype(BF)
        k_ref[0, s] = z[:, n + lo:n + lo + LANES].astype(BF)
        v_ref[0, s] = z[:, 2 * n + lo:2 * n + lo + LANES].astype(BF)


def _proj_na(x, mod, g, w, bias, tm):
    b, t, d = x.shape
    tok = pl.BlockSpec((1, NA_SLOTS, tm, LANES), lambda bi, i: (bi, 0, i, 0))
    shp = jax.ShapeDtypeStruct((b, NA_SLOTS, t, LANES), BF)
    return pl.pallas_call(
        _proj_na_kernel,
        grid=(b, t // tm),
        in_specs=[
            pl.BlockSpec((1, tm, d), lambda bi, i: (bi, i, 0)),
            _mod_spec(mod),
            _const_spec(g.shape),
            _const_spec(w.shape),
            _const_spec(bias.shape),
        ],
        out_specs=[tok, tok, tok],
        out_shape=[shp, shp, shp],
        compiler_params=_cparams("arbitrary", "arbitrary"),
        name="proj_na",
    )(x, mod, g, w, bias)


def _na_kernel(q_ref, kx_ref, vx_ref, kc_ref, vc_ref, bias_ref, o_ref, *, rows):
    r = pl.program_id(1)
    rs = jnp.clip(r - NA_KH // 2, 0, rows - NA_KH)
    start = pl.multiple_of(rs * GRID_W, GRID_W)
    nwin = NA_KH * GRID_W
    lane = lax.broadcasted_iota(jnp.int32, (GRID_W, LANES), 1)
    for s in range(NA_SLOTS):
        q = q_ref[0, s]
        zero = jnp.zeros_like(q)
        kw = kx_ref[0, s, pl.ds(start, nwin), :]
        vw = vx_ref[0, s, pl.ds(start, nwin), :]
        kc = kc_ref[0, s]
        vc = vc_ref[0, s]
        q2 = jnp.concatenate([jnp.where(lane < NA_HD, q, zero), jnp.where(lane >= NA_HD, q, zero)], axis=0)
        sw = lax.dot_general(q2, kw, _NT, preferred_element_type=F32) + bias_ref[0, s]
        sc = lax.dot_general(q2, kc, _NT, preferred_element_type=F32)
        m = jnp.maximum(jnp.max(sw, axis=-1, keepdims=True), jnp.max(sc, axis=-1, keepdims=True))
        pw = jnp.exp2(sw - m)
        pc = jnp.exp2(sc - m)
        l = jnp.sum(pw, axis=-1, keepdims=True) + jnp.sum(pc, axis=-1, keepdims=True)
        o2 = (_dot(pw.astype(BF), vw) + _dot(pc.astype(BF), vc)) / l
        o_ref[0, s] = jnp.where(lane < NA_HD, o2[:GRID_W], o2[GRID_W:]).astype(BF)


def _na_attention(q, kx, vx, kc, vc, bias):
    b, ns, l, _ = q.shape
    rows = l // GRID_W
    lc = kc.shape[2]
    ncase = bias.shape[0]

    def case_of(r):
        rs = jnp.clip(r - NA_KH // 2, 0, rows - NA_KH)
        return r - rs

    full = lambda n: pl.BlockSpec((1, ns, n, LANES), lambda bi, r: (bi, 0, 0, 0))
    row = pl.BlockSpec((1, ns, GRID_W, LANES), lambda bi, r: (bi, 0, r, 0))
    return pl.pallas_call(
        functools.partial(_na_kernel, rows=rows),
        grid=(b, rows),
        in_specs=[row, full(l), full(l), full(lc), full(lc),
                  pl.BlockSpec((1,) + bias.shape[1:], lambda bi, r: (case_of(r), 0, 0, 0))],
        out_specs=row,
        out_shape=jax.ShapeDtypeStruct((b, ns, l, LANES), BF),
        compiler_params=_cparams("arbitrary", "arbitrary"),
        name="na_attn",
    )(q, kx, vx, kc, vc, bias)


def _axis_tables(pos, dim):
    inv = ROPE_THETA ** (-jnp.arange(0, dim, 2, dtype=F32) / dim)
    ang = pos.astype(F32)[:, None] * inv[None, :]
    ang = jnp.concatenate([ang, ang], axis=-1)
    return jnp.cos(ang), jnp.sin(ang)


def _rope_tables(n, rope):
    if rope:
        t = jnp.arange(n, dtype=jnp.int32)
        row, col = t // GRID_W, t % GRID_W

        def cs(d):
            cr, sr = _axis_tables(row, d // 2)
            cc, sc = _axis_tables(col, d // 2)
            return jnp.concatenate([cr, cc], axis=-1), jnp.concatenate([sr, sc], axis=-1)

        c32, s32 = cs(MLA_ROPE)
        c64, s64 = cs(DIFF_HD)
    else:
        c32, s32 = jnp.ones((n, MLA_ROPE), F32), jnp.zeros((n, MLA_ROPE), F32)
        c64, s64 = jnp.ones((n, DIFF_HD), F32), jnp.zeros((n, DIFF_HD), F32)
    ones = jnp.ones((n, MLA_NOPE), F32)
    pad = lambda a: jnp.pad(a, ((0, 0), (0, LANES - a.shape[1])))
    cos_q = MLA_SCALE * pad(jnp.concatenate([ones, c32], axis=-1))
    sin_q = MLA_SCALE * pad(jnp.concatenate([jnp.zeros_like(ones), s32], axis=-1))
    return jnp.stack([cos_q, sin_q, pad(c32), pad(s32),
                      jnp.concatenate([c64, c64], axis=-1), jnp.concatenate([s64, s64], axis=-1)])


def _rot_cols(w, d):
    shp = w.shape
    w5 = w.reshape(shp[:-1] + (-1, 2, 2, d // 4))
    r = jnp.concatenate([-w5[..., 1:2, :], w5[..., 0:1, :]], axis=-2)
    return r.reshape(shp)


def _pad_cols(w, n):
    return jnp.pad(w, ((0, 0), (0, n - w.shape[1])))


def _prep_ab_weights(w_in, q_norm, kv_norm, w_uq, w_ukv, w_out):
    o_cq, o_ckv, o_kr = MLA_Q_RANK, MLA_Q_RANK + MLA_KV_RANK, MLA_Q_RANK + MLA_KV_RANK + MLA_ROPE
    o_dq, o_dk = o_kr + _DW, o_kr + 2 * _DW
    w_kr, w_dq, w_dk, w_dv = w_in[:, o_ckv:o_kr], w_in[:, o_kr:o_dq], w_in[:, o_dq:o_dk], w_in[:, o_dk:]
    win = jnp.concatenate([
        w_in[:, :o_ckv], _pad_cols(w_kr, LANES), _pad_cols(_rot_cols(w_kr, MLA_ROPE), LANES),
        w_dq, _rot_cols(w_dq, DIFF_HD), w_dk, _rot_cols(w_dk, DIFF_HD), w_dv], axis=1).astype(BF)
    rq = w_uq.shape[0]
    uq = w_uq.reshape(rq, MLA_HEADS, MLA_NOPE + MLA_ROPE)
    nope, rope = uq[..., :MLA_NOPE], uq[..., MLA_NOPE:]
    zpad = jnp.zeros((rq, MLA_HEADS, LANES - MLA_NOPE - MLA_ROPE), F32)
    main = jnp.concatenate([nope, rope, zpad], axis=-1).reshape(rq, MLA_HEADS * LANES)
    rot = jnp.concatenate([jnp.zeros_like(nope), _rot_cols(rope, MLA_ROPE), zpad], axis=-1)
    wuq = jnp.concatenate([main, rot.reshape(rq, MLA_HEADS * LANES)], axis=1).astype(BF)
    rkv = w_ukv.shape[0]
    ukv = w_ukv.reshape(rkv, MLA_HEADS, MLA_NOPE + MLA_V)
    slot = lambda a: jnp.pad(a, ((0, 0), (0, 0), (0, LANES - a.shape[-1]))).reshape(rkv, MLA_HEADS * LANES)
    wuk = slot(ukv[..., :MLA_NOPE]).astype(BF)
    wuv = slot(ukv[..., MLA_NOPE:]).astype(BF)
    place = np.zeros((LANES, MLA_HEADS * LANES), np.float32)
    for hd in range(MLA_HEADS):
        place[np.arange(MLA_ROPE), hd * LANES + MLA_NOPE + np.arange(MLA_ROPE)] = 1.0
    d = w_out.shape[1]
    wo_a = w_out[:MLA_HEADS * MLA_V].reshape(MLA_HEADS, MLA_V, d)
    wo_a = jnp.pad(wo_a, ((0, 0), (0, LANES - MLA_V), (0, 0))).reshape(MLA_HEADS * LANES, d)
    wo = jnp.concatenate([wo_a, w_out[MLA_HEADS * MLA_V:]], axis=0).astype(BF)
    proj = (win, q_norm.reshape(1, -1), kv_norm.reshape(1, -1), wuq, wuk, wuv, jnp.asarray(place, BF))
    return proj, wo


def _na_bias_table(rpb):
    nh = rpb.shape[0]
    cols = np.arange(GRID_W)
    col_start = np.clip(cols - NA_KW // 2, 0, GRID_W - NA_KW)
    kcol = np.arange(GRID_W)
    valid = (kcol[None, :] >= col_start[:, None]) & (kcol[None, :] < col_start[:, None] + NA_KW)
    dc = kcol[None, :] - cols[:, None] + (NA_KW - 1)
    onehot = (dc[None] == np.arange(2 * NA_KW - 1)[:, None, None]) & valid[None]
    tz = jnp.einsum('hrd,dck->hrck', rpb * LOG2E, jnp.asarray(onehot, F32), precision=lax.Precision.HIGHEST)
    tz = jnp.where(valid[None, None], tz, MASK_VALUE)
    cases = [tz[:, NA_KH - 1 - c:2 * NA_KH - 1 - c] for c in range(NA_KH)]
    tbl = jnp.transpose(jnp.stack(cases), (0, 1, 3, 2, 4))
    return tbl.reshape(NA_KH, nh // 2, 2 * GRID_W, NA_KH * GRID_W)


def _tile(n, pref):
    return pref if n % pref == 0 else n


def kernel(x, c, ctx, c_ctx, w_mod, b_mod, norm_g, a_w_in, a_q_norm, a_kv_norm, a_w_uq, a_w_ukv, b_lambda, b_subln,
           ab_w_out, f_w1, f_w3, f_w2, c_w_qkv, c_b_qkv, c_rpb, c_w_out, c_b_out, m_router, m_w1, m_w3, m_w2):
    b, l, d = x.shape
    lc = ctx.shape[1]
    depth = w_mod.shape[0]
    assert l % GRID_W == 0 and l // GRID_W >= NA_KH

    mod_rows = 16
    cvec = jnp.zeros((mod_rows, d), F32).at[:b].set(c).at[b].set(c_ctx)
    mod = _modulation(cvec, w_mod, b_mod)

    tm_x, tm_c = _tile(l, 512), _tile(lc, 256)
    tq_x, tq_c = _tile(l, 512), _tile(lc, 256)
    cs = ctx
    for i in range(depth):
        last = i == depth - 1
        j = i // 2
        mx = mod[i, :b].reshape(b, N_MOD, d)
        mc = mod[i, b].reshape(1, N_MOD, d)
        g = norm_g[i]
        if i % 2 == 0:
            lam_init = 0.8 - 0.6 * math.exp(-0.3 * i)
            proj_w, wo = _prep_ab_weights(a_w_in[j], a_q_norm[j], a_kv_norm[j], a_w_uq[j], a_w_ukv[j], ab_w_out[j])
            qx, kx, vx, dqx, dkx, dvx = _proj_ab(x, mx, g, proj_w, _rope_tables(l, True), tm_x)
            qc, kc, vc, dqc, dkc, dvc = _proj_ab(cs, mc, g, proj_w, _rope_tables(lc, False), tm_c)
            subln = b_subln[j].reshape(1, -1)
            oa = _attention(qx, [(kx, vx), (kc, vc)], tq_x)
            ob = _attention(dqx, [(dkx, dvx), (dkc, dvc)], tq_x, "diff", b_lambda[j], subln, lam_init)
            x = _outproj([oa, ob], wo, None, x, mx, g, tm_x)
            if not last:
                oa = _attention(qc, [(kc, vc)], tq_c)
                ob = _attention(dqc, [(dkc, dvc)], tq_c, "diff", b_lambda[j], subln, lam_init)
                cs = _outproj([oa, ob], wo, None, cs, mc, g, tm_c)
            w1, w3, w2 = f_w1[j].astype(BF), f_w3[j].astype(BF), f_w2[j].astype(BF)
            x = _ffn(x, mx, g, w1, w3, w2, tm_x)
            if not last:
                cs = _ffn(cs, mc, g, w1, w3, w2, tm_c)
        else:
            wqkv = c_w_qkv[j].astype(BF)
            bqkv = c_b_qkv[j].reshape(1, -1)
            wo = c_w_out[j].astype(BF)
            bo = c_b_out[j].reshape(1, -1)
            qx, kx, vx = _proj_na(x, mx, g, wqkv, bqkv, tm_x)
            qc, kc, vc = _proj_na(cs, mc, g, wqkv, bqkv, tm_c)
            o = _na_attention(qx, kx, vx, kc, vc, _na_bias_table(c_rpb[j]))
            x = _outproj([o], wo, bo, x, mx, g, tm_x)
            router = _pad_cols(m_router[j], LANES)
            w1, w3, w2 = m_w1[j].astype(BF), m_w3[j].astype(BF), m_w2[j].astype(BF)
            if not last:
                oc = _attention(qc, [(kc, jnp.swapaxes(vc, 2, 3))], tq_c, "pair")
                cs = _outproj([oc], wo, bo, cs, mc, g, tm_c)
            x = _moe(x, mx, g, router, w1, w3, w2, tm_x)
            if not last:
                cs = _moe(cs, mc, g, router, w1, w3, w2, tm_c)
    return x
```

```python
import functools
import math

import jax
import jax.numpy as jnp
import numpy as np
from jax import lax
from jax.experimental import pallas as pl
from jax.experimental.pallas import tpu as pltpu
from jax.experimental.pallas import tpu_sc as plsc

BF = jnp.bfloat16
F32 = jnp.float32

LANES = 128
VMEM_LIMIT = 56 * 1024 * 1024

GRID_W = 64
EPS = 1e-6
ROPE_THETA = 10000.0
N_MOD = 6

MLA_HEADS = 8
MLA_NOPE = 64
MLA_ROPE = 32
MLA_V = 64
MLA_Q_RANK = 384
MLA_KV_RANK = 256
LOG2E = math.log2(math.e)
MLA_SCALE = (MLA_NOPE + MLA_ROPE) ** -0.5 * LOG2E

DIFF_HEADS = 4
DIFF_HD = 64
DIFF_SCALE = DIFF_HD ** -0.5 * LOG2E

NA_HEADS = 16
NA_HD = 64
NA_KH = 8
NA_KW = 16
NA_SCALE = NA_HD ** -0.5 * LOG2E
NA_SLOTS = NA_HEADS * NA_HD // LANES
MASK_VALUE = -1e30

N_EXPERTS = 8

_NT = (((1,), (1,)), ((), ()))


def _cparams(*sem):
    return pltpu.CompilerParams(dimension_semantics=sem, vmem_limit_bytes=VMEM_LIMIT)


def _dot(a, b):
    return jnp.dot(a, b, preferred_element_type=F32)


def _rms(x, g):
    return x * lax.rsqrt(jnp.mean(x * x, axis=-1, keepdims=True) + EPS) * g


def _norm_mod(x, g, shift, scale):
    return _rms(x, g) * (1 + scale) + shift


def _const_spec(shape):
    return pl.BlockSpec(shape, lambda *_: (0,) * len(shape))


def _mod_spec(mod):
    if mod.shape[0] == 1:
        return pl.BlockSpec((1,) + mod.shape[1:], lambda b, *_: (0, 0, 0))
    return pl.BlockSpec((1,) + mod.shape[1:], lambda b, *_: (b, 0, 0))


def _mod_kernel(c_ref, w_ref, b_ref, o_ref):
    c = c_ref[...]
    sc = c * jax.nn.sigmoid(c)
    o_ref[0] = _dot(sc.astype(BF), w_ref[0].astype(BF)) + b_ref[0]


def _modulation(cvec, w_mod, b_mod):
    depth, d, n = w_mod.shape
    rows = cvec.shape[0]
    return pl.pallas_call(
        _mod_kernel,
        grid=(depth, n // d),
        in_specs=[
            pl.BlockSpec((rows, d), lambda i, j: (0, 0)),
            pl.BlockSpec((1, d, d), lambda i, j: (i, 0, j)),
            pl.BlockSpec((1, 1, d), lambda i, j: (i, 0, j)),
        ],
        out_specs=pl.BlockSpec((1, rows, d), lambda i, j: (i, 0, j)),
        out_shape=jax.ShapeDtypeStruct((depth, rows, n), F32),
        compiler_params=_cparams("arbitrary", "arbitrary"),
        name="modulation",
    )(cvec, w_mod, b_mod.reshape(depth, 1, n))


_Z_CQ = 0
_Z_CKV = MLA_Q_RANK
_Z_KR = _Z_CKV + MLA_KV_RANK
_Z_KR_ROT = _Z_KR + LANES
_Z_DQ = _Z_KR_ROT + LANES
_DW = 2 * DIFF_HEADS * DIFF_HD
_Z_DQ_ROT = _Z_DQ + _DW
_Z_DK = _Z_DQ_ROT + _DW
_Z_DK_ROT = _Z_DK + _DW
_Z_DV = _Z_DK_ROT + _DW
_Z_END = _Z_DV + _DW


def _proj_ab_kernel(x_ref, mod_ref, g_ref, win_ref, qn_ref, kvn_ref, wuq_ref, wuk_ref, wuv_ref,
                    place_ref, tab_ref, q_ref, k_ref, v_ref, dq_ref, dk_ref, dv_ref):
    h = _norm_mod(x_ref[0], g_ref[0:1, :], mod_ref[0, 0:1, :], mod_ref[0, 1:2, :])
    z = _dot(h.astype(BF), win_ref[...])
    cqn = _rms(z[:, _Z_CQ:_Z_CKV], qn_ref[...]).astype(BF)
    ckvn = _rms(z[:, _Z_CKV:_Z_KR], kvn_ref[...]).astype(BF)
    q2 = _dot(cqn, wuq_ref[...])
    nq = MLA_HEADS * LANES
    cos_q, sin_q = tab_ref[0], tab_ref[1]
    for hd in range(MLA_HEADS):
        lo = hd * LANES
        q_ref[0, hd] = (q2[:, lo:lo + LANES] * cos_q + q2[:, nq + lo:nq + lo + LANES] * sin_q).astype(BF)
    kr = (z[:, _Z_KR:_Z_KR_ROT] * tab_ref[2] + z[:, _Z_KR_ROT:_Z_DQ] * tab_ref[3]).astype(BF)
    kk = _dot(ckvn, wuk_ref[...]) + _dot(kr, place_ref[...])
    vv = _dot(ckvn, wuv_ref[...])
    for hd in range(MLA_HEADS):
        lo = hd * LANES
        k_ref[0, hd] = kk[:, lo:lo + LANES].astype(BF)
        v_ref[0, hd] = vv[:, lo:lo + LANES].T.astype(BF)
    cos_d, sin_d = tab_ref[4], tab_ref[5]
    for hd in range(DIFF_HEADS):
        lo = hd * LANES
        dq = z[:, _Z_DQ + lo:_Z_DQ + lo + LANES] * cos_d + z[:, _Z_DQ_ROT + lo:_Z_DQ_ROT + lo + LANES] * sin_d
        dq_ref[0, hd] = (dq * DIFF_SCALE).astype(BF)
        dk = z[:, _Z_DK + lo:_Z_DK + lo + LANES] * cos_d + z[:, _Z_DK_ROT + lo:_Z_DK_ROT + lo + LANES] * sin_d
        dk_ref[0, hd] = dk.astype(BF)
        dv_ref[0, hd] = z[:, _Z_DV + lo:_Z_DV + lo + LANES].T.astype(BF)


def _proj_ab(x, mod, g, wts, tabs, tm):
    b, t, d = x.shape
    win, qn, kvn, wuq, wuk, wuv, place = wts
    tok = lambda hh: pl.BlockSpec((1, hh, tm, LANES), lambda bi, i: (bi, 0, i, 0))
    shp = lambda hh: jax.ShapeDtypeStruct((b, hh, t, LANES), BF)
    tok_t = lambda hh: pl.BlockSpec((1, hh, LANES, tm), lambda bi, i: (bi, 0, 0, i))
    shp_t = lambda hh: jax.ShapeDtypeStruct((b, hh, LANES, t), BF)
    return pl.pallas_call(
        _proj_ab_kernel,
        grid=(b, t // tm),
        in_specs=[
            pl.BlockSpec((1, tm, d), lambda bi, i: (bi, i, 0)),
            _mod_spec(mod),
            _const_spec(g.shape),
            _const_spec(win.shape), _const_spec(qn.shape), _const_spec(kvn.shape),
            _const_spec(wuq.shape), _const_spec(wuk.shape), _const_spec(wuv.shape),
            _const_spec(place.shape),
            pl.BlockSpec((6, tm, LANES), lambda bi, i: (0, i, 0)),
        ],
        out_specs=[tok(MLA_HEADS), tok(MLA_HEADS), tok_t(MLA_HEADS),
                   tok(DIFF_HEADS), tok(DIFF_HEADS), tok_t(DIFF_HEADS)],
        out_shape=[shp(MLA_HEADS), shp(MLA_HEADS), shp_t(MLA_HEADS),
                   shp(DIFF_HEADS), shp(DIFF_HEADS), shp_t(DIFF_HEADS)],
        compiler_params=_cparams("arbitrary", "arbitrary"),
        name="proj_ab",
    )(x, mod, g, win, qn, kvn, wuq, wuk, wuv, place, tabs)


KEY_CHUNK = 512


def _key_chunks(kv_refs):
    off = 0
    for k_ref, v_ref in kv_refs:
        lk = k_ref.shape[2]
        for c0 in range(0, lk, KEY_CHUNK):
            w = min(KEY_CHUNK, lk - c0)
            yield k_ref, v_ref, c0, w, off
            off += w


SUBLANES = 8
ATTN_VARIANTS = 2


def _softmax_pv(qms, heads, kv_refs, s_refs, p_refs):
    n = len(qms)
    tq = qms[0].shape[0]
    fold = lambda a: a.reshape(a.shape[0] // SUBLANES, SUBLANES, tq)
    mms = [None] * n
    ms = [None] * n
    lls = [jnp.zeros((SUBLANES, tq), F32)] * n
    o_ts = [jnp.zeros((LANES, tq), F32)] * n

    def scores(i, k_ref, c0, w, off):
        s = lax.dot_general(k_ref[0, heads[i], c0:c0 + w, :], qms[i], _NT, preferred_element_type=F32)
        s_refs[i][off:off + w, :] = s
        cm = jnp.max(fold(s), axis=0)
        mms[i] = cm if mms[i] is None else jnp.maximum(mms[i], cm)

    def probs(i, w, off):
        p = jnp.exp2(s_refs[i][off:off + w, :] - ms[i])
        lls[i] = lls[i] + jnp.sum(fold(p), axis=0)
        p_refs[i][off:off + w, :] = p.astype(BF)

    def values(i, v_ref, c0, w, off):
        o_ts[i] = o_ts[i] + _dot(v_ref[0, heads[i], :, c0:c0 + w], p_refs[i][off:off + w, :])

    for stage in range(n + 2):
        for k_ref, v_ref, c0, w, off in _key_chunks(kv_refs):
            if stage < n:
                scores(stage, k_ref, c0, w, off)
            if 1 <= stage <= n:
                probs(stage - 1, w, off)
            if stage >= 2:
                values(stage - 2, v_ref, c0, w, off)
        if stage < n:
            ms[stage] = jnp.max(mms[stage], axis=0, keepdims=True)
    return [(o_t / jnp.sum(ll, axis=0, keepdims=True)).T for o_t, ll in zip(o_ts, lls)]


def _attn_kernel(*refs, nseg, mode, lam_init):
    q_ref = refs[0]
    kv_refs = [(refs[1 + 2 * s], refs[2 + 2 * s]) for s in range(nseg)]
    o_ref = refs[-1 - 2 * ATTN_VARIANTS]
    s_refs = refs[-2 * ATTN_VARIANTS:-ATTN_VARIANTS]
    p_refs = refs[-ATTN_VARIANTS:]
    if mode == "single":
        heads = list(range(ATTN_VARIANTS))
        outs = _softmax_pv([q_ref[0, h] for h in heads], heads, kv_refs, s_refs, p_refs)
        for h in heads:
            o_ref[0, h] = outs[h].astype(BF)
        return
    q = q_ref[0, 0]
    lane = lax.broadcasted_iota(jnp.int32, q.shape, 1)
    zero = jnp.zeros_like(q)
    o1, o2 = _softmax_pv([jnp.where(lane < DIFF_HD, q, zero), jnp.where(lane >= DIFF_HD, q, zero)], [0, 0],
                         kv_refs, s_refs, p_refs)
    if mode == "pair":
        o_ref[0, 0] = jnp.where(lane < DIFF_HD, o1, o2).astype(BF)
        return
    lam_ref, subln_ref = refs[1 + 2 * nseg], refs[2 + 2 * nseg]
    lv = lam_ref[...]
    lam = (jnp.exp(jnp.sum(lv[0:1] * lv[1:2], axis=-1, keepdims=True))
           - jnp.exp(jnp.sum(lv[2:3] * lv[3:4], axis=-1, keepdims=True)) + lam_init)
    o_ref[0, 0] = (_rms(o1 - lam * o2, subln_ref[...]) * (1 - lam_init)).astype(BF)


def _attention(q, kvs, tq, mode="single", lam_vecs=None, subln=None, lam_init=0.0):
    b, nh, lq, _ = q.shape
    diff = mode == "diff"
    hb = ATTN_VARIANTS if mode == "single" else 1
    assert nh % hb == 0
    q_spec = pl.BlockSpec((1, hb, tq, LANES), lambda bi, h, i: (bi, h, i, 0))
    in_specs = [q_spec]
    args = [q]
    for k, v in kvs:
        lk = k.shape[2]
        in_specs += [pl.BlockSpec((1, hb, lk, LANES), lambda bi, h, i: (bi, h, 0, 0)),
                     pl.BlockSpec((1, hb, LANES, lk), lambda bi, h, i: (bi, h, 0, 0))]
        args += [k, v]
    if diff:
        in_specs += [_const_spec(lam_vecs.shape), _const_spec(subln.shape)]
        args += [lam_vecs, subln]
    nkeys = sum(k.shape[2] for k, _ in kvs)
    return pl.pallas_call(
        functools.partial(_attn_kernel, nseg=len(kvs), mode=mode, lam_init=lam_init),
        grid=(b, nh // hb, lq // tq),
        in_specs=in_specs,
        out_specs=q_spec,
        out_shape=jax.ShapeDtypeStruct((b, nh, lq, LANES), BF),
        scratch_shapes=([pltpu.VMEM((nkeys, tq), F32)] * ATTN_VARIANTS
                        + [pltpu.VMEM((nkeys, tq), BF)] * ATTN_VARIANTS),
        compiler_params=_cparams("arbitrary", "arbitrary", "arbitrary"),
        name="attn_diff" if diff else "attn_mla",
    )(*args)


def _outproj_kernel(*refs, n_in, has_bias):
    o_refs = refs[:n_in]
    w_ref = refs[n_in]
    pos = n_in + 1
    b_ref = None
    if has_bias:
        b_ref = refs[pos]
        pos += 1
    x_ref, mod_ref, g_ref, out_ref = refs[pos:pos + 4]
    o = jnp.concatenate([r[0, h] for r in o_refs for h in range(r.shape[1])], axis=-1)
    y = _dot(o, w_ref[...])
    if has_bias:
        y = y + b_ref[...]
    out_ref[0] = x_ref[0] + mod_ref[0, 2:3, :] * _rms(y, g_ref[1:2, :])


def _outproj(os_, w, bias, x, mod, g, tm):
    b, t, d = x.shape
    in_specs = [pl.BlockSpec((1, o.shape[1], tm, LANES), lambda bi, i: (bi, 0, i, 0)) for o in os_]
    in_specs.append(_const_spec(w.shape))
    args = list(os_) + [w]
    if bias is not None:
        in_specs.append(_const_spec(bias.shape))
        args.append(bias)
    in_specs += [pl.BlockSpec((1, tm, d), lambda bi, i: (bi, i, 0)), _mod_spec(mod), _const_spec(g.shape)]
    args += [x, mod, g]
    return pl.pallas_call(
        functools.partial(_outproj_kernel, n_in=len(os_), has_bias=bias is not None),
        grid=(b, t // tm),
        in_specs=in_specs,
        out_specs=pl.BlockSpec((1, tm, d), lambda bi, i: (bi, i, 0)),
        out_shape=jax.ShapeDtypeStruct((b, t, d), F32),
        compiler_params=_cparams("arbitrary", "arbitrary"),
        name="outproj",
    )(*args)


def _swiglu_chunk(hn, w1, w3, w2):
    a = _dot(hn, w1)
    act = (a * jax.nn.sigmoid(a) * _dot(hn, w3)).astype(BF)
    return _dot(act, w2)


def _ffn_kernel(x_ref, mod_ref, g_ref, w1_ref, w3_ref, w2_ref, out_ref, hn_ref, acc_ref):
    k = pl.program_id(2)

    @pl.when(k == 0)
    def _():
        h = _norm_mod(x_ref[0], g_ref[2:3, :], mod_ref[0, 3:4, :], mod_ref[0, 4:5, :])
        hn_ref[...] = h.astype(BF)
        acc_ref[...] = jnp.zeros_like(acc_ref)

    acc_ref[...] += _swiglu_chunk(hn_ref[...], w1_ref[...], w3_ref[...], w2_ref[...])

    @pl.when(k == pl.num_programs(2) - 1)
    def _():
        out_ref[0] = x_ref[0] + mod_ref[0, 5:6, :] * _rms(acc_ref[...], g_ref[3:4, :])


def _ff_chunk(f):
    half = f // 2
    return half if f % 2 == 0 and half % LANES == 0 else f


def _ffn(x, mod, g, w1, w3, w2, tm):
    b, t, d = x.shape
    f = w1.shape[1]
    tf = _ff_chunk(f)
    return pl.pallas_call(
        _ffn_kernel,
        grid=(b, t // tm, f // tf),
        in_specs=[
            pl.BlockSpec((1, tm, d), lambda bi, i, k: (bi, i, 0)),
            _mod_spec(mod),
            _const_spec(g.shape),
            pl.BlockSpec((d, tf), lambda bi, i, k: (0, k)),
            pl.BlockSpec((d, tf), lambda bi, i, k: (0, k)),
            pl.BlockSpec((tf, d), lambda bi, i, k: (k, 0)),
        ],
        out_specs=pl.BlockSpec((1, tm, d), lambda bi, i, k: (bi, i, 0)),
        out_shape=jax.ShapeDtypeStruct((b, t, d), F32),
        scratch_shapes=[pltpu.VMEM((tm, d), BF), pltpu.VMEM((tm, d), F32)],
        compiler_params=_cparams("arbitrary", "arbitrary", "arbitrary"),
        name="ffn_dense",
    )(x, mod, g, w1, w3, w2)


MOE_ROW_TILE = 512
PACK_CHUNKS = 4
SC_WINDOW = 128
_HI_MASK = -65536
_LO_MASK = 65535


def _pack_rows(v, out_ref):
    half = v.shape[1] // 2
    vb = v.astype(BF).astype(F32)
    lo = (pltpu.bitcast(vb[:, :half], jnp.int32) >> 16) & _LO_MASK
    hi = pltpu.bitcast(vb[:, half:], jnp.int32) & _HI_MASK
    w = lo | hi
    for c in range(PACK_CHUNKS):
        out_ref[c] = w[:, c * LANES:(c + 1) * LANES]


def _unpack_rows(ref):
    w = jnp.concatenate([ref[c] for c in range(PACK_CHUNKS)], axis=-1)
    lo = pltpu.bitcast(w << 16, F32)
    hi = pltpu.bitcast(w & _HI_MASK, F32)
    return jnp.concatenate([lo, hi], axis=-1)


def _router_kernel(x_ref, mod_ref, g_ref, router_ref, tri_ref, hp_ref, meta_ref, gate_ref, cnt_ref, carry_ref):
    @pl.when((pl.program_id(0) == 0) & (pl.program_id(1) == 0))
    def _():
        carry_ref[...] = jnp.zeros_like(carry_ref)

    h = _norm_mod(x_ref[0], g_ref[2:3, :], mod_ref[0, 3:4, :], mod_ref[0, 4:5, :])
    _pack_rows(h, hp_ref)
    logits = jnp.dot(h, router_ref[...], preferred_element_type=F32, precision=lax.Precision.HIGHEST)
    lane = lax.broadcasted_iota(jnp.int32, logits.shape, 1)
    logits = jnp.where(lane < N_EXPERTS, logits, -jnp.inf)
    m1 = jnp.max(logits, axis=-1, keepdims=True)
    i1 = jnp.min(jnp.where(logits == m1, lane, LANES), axis=-1, keepdims=True)
    rest = jnp.where(lane == i1, -jnp.inf, logits)
    m2 = jnp.max(rest, axis=-1, keepdims=True)
    i2 = jnp.min(jnp.where(rest == m2, lane, LANES), axis=-1, keepdims=True)
    e2 = jnp.exp(m2 - m1)
    denom = 1.0 + e2
    assigned = jnp.where((lane == i1) | (lane == i2), 1.0, 0.0)
    ranks = _dot(tri_ref[...], assigned.astype(BF)) + carry_ref[...]
    r1 = jnp.sum(jnp.where(lane == i1, ranks, 0.0), axis=-1, keepdims=True).astype(jnp.int32)
    r2 = jnp.sum(jnp.where(lane == i2, ranks, 0.0), axis=-1, keepdims=True).astype(jnp.int32)
    carry_ref[...] += jnp.sum(assigned, axis=0, keepdims=True)
    cnt_ref[...] = carry_ref[...]
    col = lax.broadcasted_iota(jnp.int32, meta_ref.shape, 1)
    meta_ref[...] = jnp.where(col == 0, i1, jnp.where(col == 1, i2, jnp.where(col == 2, r1, r2)))
    gate_ref[...] = jnp.where(col == 0, 1.0 / denom, e2 / denom)


def _router(x, mod, g, router, tm):
    b, t, d = x.shape
    n = b * t
    nt = t // tm
    tri = jnp.asarray(np.tril(np.ones((tm, tm), np.float32), -1), BF)
    row_spec = lambda w, dt: (pl.BlockSpec((tm, w), lambda bi, i: (bi * nt + i, 0)), jax.ShapeDtypeStruct((n, w), dt))
    meta_spec, meta_shape = row_spec(8, jnp.int32)
    gate_spec, gate_shape = row_spec(8, F32)
    return pl.pallas_call(
        _router_kernel,
        grid=(b, nt),
        in_specs=[
            pl.BlockSpec((1, tm, d), lambda bi, i: (bi, i, 0)),
            _mod_spec(mod),
            _const_spec(g.shape),
            _const_spec(router.shape),
            _const_spec(tri.shape),
        ],
        out_specs=[
            pl.BlockSpec((PACK_CHUNKS, tm, LANES), lambda bi, i: (0, bi * nt + i, 0)),
            meta_spec, gate_spec,
            pl.BlockSpec((1, LANES), lambda bi, i: (0, 0)),
        ],
        out_shape=[jax.ShapeDtypeStruct((PACK_CHUNKS, n, LANES), jnp.int32), meta_shape, gate_shape,
                   jax.ShapeDtypeStruct((1, LANES), F32)],
        scratch_shapes=[pltpu.VMEM((1, LANES), F32)],
        compiler_params=_cparams("arbitrary", "arbitrary"),
        name="moe_router",
    )(x, mod, g, router, tri)


def _sc_mesh():
    return plsc.VectorSubcoreMesh(core_axis_name="core", subcore_axis_name="subcore")


def _sc_scatter_rows(rows, idx_a, idx_b, n_out):
    nrows = rows.shape[0]

    @pl.kernel(out_type=jax.ShapeDtypeStruct((n_out, LANES), rows.dtype), mesh=_sc_mesh(), scratch_types=[])
    def scatter_kernel(x_hbm, ia_hbm, ib_hbm, o_hbm):
        def body(x_vmem, ia_vmem, ib_vmem):
            pltpu.sync_copy(x_vmem, o_hbm.at[ia_vmem.at[0]])
            pltpu.sync_copy(x_vmem, o_hbm.at[ib_vmem.at[0]])

        pltpu.emit_pipeline(
            body,
            grid=(nrows // SC_WINDOW,),
            in_specs=[pl.BlockSpec((SC_WINDOW, LANES), lambda i: (i, 0)),
                      pl.BlockSpec((1, SC_WINDOW), lambda i: (0, i)),
                      pl.BlockSpec((1, SC_WINDOW), lambda i: (0, i))],
            out_specs=[],
            core_axis_name=("core", "subcore"),
            dimension_semantics=(pltpu.PARALLEL,),
        )(x_hbm, ia_hbm, ib_hbm)

    return scatter_kernel(rows, idx_a, idx_b)


def _sc_gather_rows(table, idx_a, idx_b):
    nrows = idx_a.shape[1]
    out = jax.ShapeDtypeStruct((nrows, LANES), table.dtype)

    @pl.kernel(out_type=(out, out), mesh=_sc_mesh(), scratch_types=[])
    def gather_kernel(t_hbm, ia_hbm, ib_hbm, oa_hbm, ob_hbm):
        def body(ia_vmem, ib_vmem, oa_vmem, ob_vmem):
            pltpu.sync_copy(t_hbm.at[ia_vmem.at[0]], oa_vmem)
            pltpu.sync_copy(t_hbm.at[ib_vmem.at[0]], ob_vmem)

        pltpu.emit_pipeline(
            body,
            grid=(nrows // SC_WINDOW,),
            in_specs=[pl.BlockSpec((1, SC_WINDOW), lambda i: (0, i)),
                      pl.BlockSpec((1, SC_WINDOW), lambda i: (0, i))],
            out_specs=[pl.BlockSpec((SC_WINDOW, LANES), lambda i: (i, 0)),
                       pl.BlockSpec((SC_WINDOW, LANES), lambda i: (i, 0))],
            core_axis_name=("core", "subcore"),
            dimension_semantics=(pltpu.PARALLEL,),
        )(ia_hbm, ib_hbm, oa_hbm, ob_hbm)

    return gather_kernel(table, idx_a, idx_b)


def _experts_kernel(te_ref, tv_ref, xs_ref, w1_ref, w3_ref, w2_ref, ys_ref, hn_ref, acc_ref):
    del te_ref
    k = pl.program_id(1)
    valid = tv_ref[pl.program_id(0)]

    @pl.when(k == 0)
    def _():
        row = lax.broadcasted_iota(jnp.int32, (hn_ref.shape[0], 1), 0)
        hn_ref[...] = jnp.where(row < valid, _unpack_rows(xs_ref), 0.0).astype(BF)
        acc_ref[...] = jnp.zeros_like(acc_ref)

    @pl.when(valid > 0)
    def _():
        acc_ref[...] += _swiglu_chunk(hn_ref[...], w1_ref[0], w3_ref[0], w2_ref[0])

    @pl.when(k == pl.num_programs(1) - 1)
    def _():
        _pack_rows(acc_ref[...], ys_ref)


def _experts(xs, tile_expert, tile_valid, w1, w3, w2):
    _, p, _ = xs.shape
    ne, d, f = w1.shape
    tf = _ff_chunk(f)
    nk = f // tf
    tr = MOE_ROW_TILE

    def kk(j, k, tv):
        return jnp.where(tv[j] > 0, k, nk - 1)

    rows = pl.BlockSpec((PACK_CHUNKS, tr, LANES), lambda j, k, te, tv: (0, j, 0))
    return pl.pallas_call(
        _experts_kernel,
        grid_spec=pltpu.PrefetchScalarGridSpec(
            num_scalar_prefetch=2,
            grid=(p // tr, nk),
            in_specs=[
                rows,
                pl.BlockSpec((1, d, tf), lambda j, k, te, tv: (te[j], 0, kk(j, k, tv))),
                pl.BlockSpec((1, d, tf), lambda j, k, te, tv: (te[j], 0, kk(j, k, tv))),
                pl.BlockSpec((1, tf, d), lambda j, k, te, tv: (te[j], kk(j, k, tv), 0)),
            ],
            out_specs=rows,
            scratch_shapes=[pltpu.VMEM((tr, d), BF), pltpu.VMEM((tr, d), F32)],
        ),
        out_shape=jax.ShapeDtypeStruct(xs.shape, jnp.int32),
        compiler_params=_cparams("arbitrary", "arbitrary"),
        name="moe_experts",
    )(tile_expert, tile_valid, xs, w1, w3, w2)


def _combine_kernel(x_ref, ya_ref, yb_ref, gate_ref, mod_ref, g_ref, out_ref):
    gates = gate_ref[...]
    fx = gates[:, 0:1] * _unpack_rows(ya_ref) + gates[:, 1:2] * _unpack_rows(yb_ref)
    out_ref[0] = x_ref[0] + mod_ref[0, 5:6, :] * _rms(fx, g_ref[3:4, :])


def _combine(x, ya, yb, gates, mod, g, tm):
    b, t, d = x.shape
    nt = t // tm
    packed = pl.BlockSpec((PACK_CHUNKS, tm, LANES), lambda bi, i: (0, bi * nt + i, 0))
    return pl.pallas_call(
        _combine_kernel,
        grid=(b, nt),
        in_specs=[
            pl.BlockSpec((1, tm, d), lambda bi, i: (bi, i, 0)),
            packed, packed,
            pl.BlockSpec((tm, gates.shape[1]), lambda bi, i: (bi * nt + i, 0)),
            _mod_spec(mod),
            _const_spec(g.shape),
        ],
        out_specs=pl.BlockSpec((1, tm, d), lambda bi, i: (bi, i, 0)),
        out_shape=jax.ShapeDtypeStruct((b, t, d), F32),
        compiler_params=_cparams("arbitrary", "arbitrary"),
        name="moe_combine",
    )(x, ya, yb, gates, mod, g)


def _moe(x, mod, g, router, w1, w3, w2, tm):
    b, t, d = x.shape
    n = b * t
    ne = w1.shape[0]
    tr = MOE_ROW_TILE
    assert d == 2 * PACK_CHUNKS * LANES and (PACK_CHUNKS * n) % (SC_WINDOW * 32) == 0
    hp, meta, gates, counts = _router(x, mod, g, router, tm)

    ntile = 2 * n // tr + ne
    p = ntile * tr
    cnt = counts[0, :ne].astype(jnp.int32)
    tiles = (cnt + tr - 1) // tr
    tile_end = jnp.cumsum(tiles)
    tile_start = tile_end - tiles
    base = tile_start * tr
    eids = jnp.arange(ne, dtype=jnp.int32)
    base_of = lambda e: jnp.sum(jnp.where(e[:, None] == eids[None, :], base[None, :], 0), axis=-1)
    pos_a = base_of(meta[:, 0]) + meta[:, 2]
    pos_b = base_of(meta[:, 1]) + meta[:, 3]
    chunk = jnp.arange(PACK_CHUNKS, dtype=jnp.int32)[:, None] * p
    idx_a = (chunk + pos_a[None, :]).reshape(1, PACK_CHUNKS * n)
    idx_b = (chunk + pos_b[None, :]).reshape(1, PACK_CHUNKS * n)
    tj = jnp.arange(ntile, dtype=jnp.int32)
    tile_expert = jnp.minimum(jnp.sum(tj[:, None] >= tile_end[None, :], axis=-1), ne - 1).astype(jnp.int32)
    done = jnp.sum(jnp.where(tile_expert[:, None] == eids[None, :], tile_start[None, :], 0), axis=-1)
    left = jnp.sum(jnp.where(tile_expert[:, None] == eids[None, :], cnt[None, :], 0), axis=-1) - (tj - done) * tr
    tile_valid = jnp.where(tj < tile_end[-1], jnp.clip(left, 0, tr), 0).astype(jnp.int32)

    xs = _sc_scatter_rows(hp.reshape(PACK_CHUNKS * n, LANES), idx_a, idx_b, PACK_CHUNKS * p)
    ys = _experts(xs.reshape(PACK_CHUNKS, p, LANES), tile_expert, tile_valid, w1, w3, w2)
    ya, yb = _sc_gather_rows(ys.reshape(PACK_CHUNKS * p, LANES), idx_a, idx_b)
    shp = (PACK_CHUNKS, n, LANES)
    return _combine(x, ya.reshape(shp), yb.reshape(shp), gates, mod, g, tm)


def _proj_na_kernel(x_ref, mod_ref, g_ref, w_ref, b_ref, q_ref, k_ref, v_ref):
    h = _norm_mod(x_ref[0], g_ref[0:1, :], mod_ref[0, 0:1, :], mod_ref[0, 1:2, :])
    z = _dot(h.astype(BF), w_ref[...]) + b_ref[...]
    n = NA_SLOTS * LANES
    for s in range(NA_SLOTS):
        lo = s * LANES
        q_ref[0, s] = (z[:, lo:lo + LANES] * NA_SCALE).astype(BF)
        k_ref[0, s] = z[:, n + lo:n + lo + LANES].astype(BF)
        v_ref[0, s] = z[:, 2 * n + lo:2 * n + lo + LANES].astype(BF)


def _proj_na(x, mod, g, w, bias, tm):
    b, t, d = x.shape
    tok = pl.BlockSpec((1, NA_SLOTS, tm, LANES), lambda bi, i: (bi, 0, i, 0))
    shp = jax.ShapeDtypeStruct((b, NA_SLOTS, t, LANES), BF)
    return pl.pallas_call(
        _proj_na_kernel,
        grid=(b, t // tm),
        in_specs=[
            pl.BlockSpec((1, tm, d), lambda bi, i: (bi, i, 0)),
            _mod_spec(mod),
            _const_spec(g.shape),
            _const_spec(w.shape),
            _const_spec(bias.shape),
        ],
        out_specs=[tok, tok, tok],
        out_shape=[shp, shp, shp],
        compiler_params=_cparams("arbitrary", "arbitrary"),
        name="proj_na",
    )(x, mod, g, w, bias)


def _na_kernel(q_ref, kx_ref, vx_ref, kc_ref, vc_ref, bias_ref, o_ref, *, rows):
    r = pl.program_id(1)
    rs = jnp.clip(r - NA_KH // 2, 0, rows - NA_KH)
    start = pl.multiple_of(rs * GRID_W, GRID_W)
    nwin = NA_KH * GRID_W
    lane = lax.broadcasted_iota(jnp.int32, (GRID_W, LANES), 1)
    slots = range(NA_SLOTS)
    sws, scs = [], []
    for s in slots:
        q = q_ref[0, s]
        zero = jnp.zeros_like(q)
        q2 = jnp.concatenate([jnp.where(lane < NA_HD, q, zero), jnp.where(lane >= NA_HD, q, zero)], axis=0)
        kw = kx_ref[0, s, pl.ds(start, nwin), :]
        sws.append(lax.dot_general(q2, kw, _NT, preferred_element_type=F32) + bias_ref[0, s])
        scs.append(lax.dot_general(q2, kc_ref[0, s], _NT, preferred_element_type=F32))
    ms = [jnp.maximum(jnp.max(sws[s], axis=-1, keepdims=True), jnp.max(scs[s], axis=-1, keepdims=True))
          for s in slots]
    pws = [jnp.exp2(sws[s] - ms[s]) for s in slots]
    pcs = [jnp.exp2(scs[s] - ms[s]) for s in slots]
    ls = [jnp.sum(pws[s], axis=-1, keepdims=True) + jnp.sum(pcs[s], axis=-1, keepdims=True) for s in slots]
    for s in slots:
        vw = vx_ref[0, s, pl.ds(start, nwin), :]
        o2 = (_dot(pws[s].astype(BF), vw) + _dot(pcs[s].astype(BF), vc_ref[0, s])) / ls[s]
        o_ref[0, s] = jnp.where(lane < NA_HD, o2[:GRID_W], o2[GRID_W:]).astype(BF)


def _na_attention(q, kx, vx, kc, vc, bias):
    b, ns, l, _ = q.shape
    rows = l // GRID_W
    lc = kc.shape[2]
    ncase = bias.shape[0]

    def case_of(r):
        rs = jnp.clip(r - NA_KH // 2, 0, rows - NA_KH)
        return r - rs

    full = lambda n: pl.BlockSpec((1, ns, n, LANES), lambda bi, r: (bi, 0, 0, 0))
    row = pl.BlockSpec((1, ns, GRID_W, LANES), lambda bi, r: (bi, 0, r, 0))
    return pl.pallas_call(
        functools.partial(_na_kernel, rows=rows),
        grid=(b, rows),
        in_specs=[row, full(l), full(l), full(lc), full(lc),
                  pl.BlockSpec((1,) + bias.shape[1:], lambda bi, r: (case_of(r), 0, 0, 0))],
        out_specs=row,
        out_shape=jax.ShapeDtypeStruct((b, ns, l, LANES), BF),
        compiler_params=_cparams("arbitrary", "arbitrary"),
        name="na_attn",
    )(q, kx, vx, kc, vc, bias)


def _axis_tables(pos, dim):
    inv = ROPE_THETA ** (-jnp.arange(0, dim, 2, dtype=F32) / dim)
    ang = pos.astype(F32)[:, None] * inv[None, :]
    ang = jnp.concatenate([ang, ang], axis=-1)
    return jnp.cos(ang), jnp.sin(ang)


def _rope_tables(n, rope):
    if rope:
        t = jnp.arange(n, dtype=jnp.int32)
        row, col = t // GRID_W, t % GRID_W

        def cs(d):
            cr, sr = _axis_tables(row, d // 2)
            cc, sc = _axis_tables(col, d // 2)
            return jnp.concatenate([cr, cc], axis=-1), jnp.concatenate([sr, sc], axis=-1)

        c32, s32 = cs(MLA_ROPE)
        c64, s64 = cs(DIFF_HD)
    else:
        c32, s32 = jnp.ones((n, MLA_ROPE), F32), jnp.zeros((n, MLA_ROPE), F32)
        c64, s64 = jnp.ones((n, DIFF_HD), F32), jnp.zeros((n, DIFF_HD), F32)
    ones = jnp.ones((n, MLA_NOPE), F32)
    pad = lambda a: jnp.pad(a, ((0, 0), (0, LANES - a.shape[1])))
    cos_q = MLA_SCALE * pad(jnp.concatenate([ones, c32], axis=-1))
    sin_q = MLA_SCALE * pad(jnp.concatenate([jnp.zeros_like(ones), s32], axis=-1))
    return jnp.stack([cos_q, sin_q, pad(c32), pad(s32),
                      jnp.concatenate([c64, c64], axis=-1), jnp.concatenate([s64, s64], axis=-1)])


def _rot_cols(w, d):
    shp = w.shape
    w5 = w.reshape(shp[:-1] + (-1, 2, 2, d // 4))
    r = jnp.concatenate([-w5[..., 1:2, :], w5[..., 0:1, :]], axis=-2)
    return r.reshape(shp)


def _pad_cols(w, n):
    return jnp.pad(w, ((0, 0), (0, n - w.shape[1])))


def _prep_ab_weights(w_in, q_norm, kv_norm, w_uq, w_ukv, w_out):
    o_cq, o_ckv, o_kr = MLA_Q_RANK, MLA_Q_RANK + MLA_KV_RANK, MLA_Q_RANK + MLA_KV_RANK + MLA_ROPE
    o_dq, o_dk = o_kr + _DW, o_kr + 2 * _DW
    w_kr, w_dq, w_dk, w_dv = w_in[:, o_ckv:o_kr], w_in[:, o_kr:o_dq], w_in[:, o_dq:o_dk], w_in[:, o_dk:]
    win = jnp.concatenate([
        w_in[:, :o_ckv], _pad_cols(w_kr, LANES), _pad_cols(_rot_cols(w_kr, MLA_ROPE), LANES),
        w_dq, _rot_cols(w_dq, DIFF_HD), w_dk, _rot_cols(w_dk, DIFF_HD), w_dv], axis=1).astype(BF)
    rq = w_uq.shape[0]
    uq = w_uq.reshape(rq, MLA_HEADS, MLA_NOPE + MLA_ROPE)
    nope, rope = uq[..., :MLA_NOPE], uq[..., MLA_NOPE:]
    zpad = jnp.zeros((rq, MLA_HEADS, LANES - MLA_NOPE - MLA_ROPE), F32)
    main = jnp.concatenate([nope, rope, zpad], axis=-1).reshape(rq, MLA_HEADS * LANES)
    rot = jnp.concatenate([jnp.zeros_like(nope), _rot_cols(rope, MLA_ROPE), zpad], axis=-1)
    wuq = jnp.concatenate([main, rot.reshape(rq, MLA_HEADS * LANES)], axis=1).astype(BF)
    rkv = w_ukv.shape[0]
    ukv = w_ukv.reshape(rkv, MLA_HEADS, MLA_NOPE + MLA_V)
    slot = lambda a: jnp.pad(a, ((0, 0), (0, 0), (0, LANES - a.shape[-1]))).reshape(rkv, MLA_HEADS * LANES)
    wuk = slot(ukv[..., :MLA_NOPE]).astype(BF)
    wuv = slot(ukv[..., MLA_NOPE:]).astype(BF)
    place = np.zeros((LANES, MLA_HEADS * LANES), np.float32)
    for hd in range(MLA_HEADS):
        place[np.arange(MLA_ROPE), hd * LANES + MLA_NOPE + np.arange(MLA_ROPE)] = 1.0
    d = w_out.shape[1]
    wo_a = w_out[:MLA_HEADS * MLA_V].reshape(MLA_HEADS, MLA_V, d)
    wo_a = jnp.pad(wo_a, ((0, 0), (0, LANES - MLA_V), (0, 0))).reshape(MLA_HEADS * LANES, d)
    wo = jnp.concatenate([wo_a, w_out[MLA_HEADS * MLA_V:]], axis=0).astype(BF)
    proj = (win, q_norm.reshape(1, -1), kv_norm.reshape(1, -1), wuq, wuk, wuv, jnp.asarray(place, BF))
    return proj, wo


def _na_bias_table(rpb):
    nh = rpb.shape[0]
    cols = np.arange(GRID_W)
    col_start = np.clip(cols - NA_KW // 2, 0, GRID_W - NA_KW)
    kcol = np.arange(GRID_W)
    valid = (kcol[None, :] >= col_start[:, None]) & (kcol[None, :] < col_start[:, None] + NA_KW)
    dc = kcol[None, :] - cols[:, None] + (NA_KW - 1)
    onehot = (dc[None] == np.arange(2 * NA_KW - 1)[:, None, None]) & valid[None]
    tz = jnp.einsum('hrd,dck->hrck', rpb * LOG2E, jnp.asarray(onehot, F32), precision=lax.Precision.HIGHEST)
    tz = jnp.where(valid[None, None], tz, MASK_VALUE)
    cases = [tz[:, NA_KH - 1 - c:2 * NA_KH - 1 - c] for c in range(NA_KH)]
    tbl = jnp.transpose(jnp.stack(cases), (0, 1, 3, 2, 4))
    return tbl.reshape(NA_KH, nh // 2, 2 * GRID_W, NA_KH * GRID_W)


def _tile(n, pref):
    return pref if n % pref == 0 else n


def kernel(x, c, ctx, c_ctx, w_mod, b_mod, norm_g, a_w_in, a_q_norm, a_kv_norm, a_w_uq, a_w_ukv, b_lambda, b_subln,
           ab_w_out, f_w1, f_w3, f_w2, c_w_qkv, c_b_qkv, c_rpb, c_w_out, c_b_out, m_router, m_w1, m_w3, m_w2):
    b, l, d = x.shape
    lc = ctx.shape[1]
    depth = w_mod.shape[0]
    assert l % GRID_W == 0 and l // GRID_W >= NA_KH

    mod_rows = 16
    cvec = jnp.zeros((mod_rows, d), F32).at[:b].set(c).at[b].set(c_ctx)
    mod = _modulation(cvec, w_mod, b_mod)

    tm_x, tm_c = _tile(l, 512), _tile(lc, 256)
    tq_x, tq_c = _tile(l, 512), _tile(lc, 256)
    cs = ctx
    for i in range(depth):
        last = i == depth - 1
        j = i // 2
        mx = mod[i, :b].reshape(b, N_MOD, d)
        mc = mod[i, b].reshape(1, N_MOD, d)
        g = norm_g[i]
        if i % 2 == 0:
            lam_init = 0.8 - 0.6 * math.exp(-0.3 * i)
            proj_w, wo = _prep_ab_weights(a_w_in[j], a_q_norm[j], a_kv_norm[j], a_w_uq[j], a_w_ukv[j], ab_w_out[j])
            qx, kx, vx, dqx, dkx, dvx = _proj_ab(x, mx, g, proj_w, _rope_tables(l, True), tm_x)
            qc, kc, vc, dqc, dkc, dvc = _proj_ab(cs, mc, g, proj_w, _rope_tables(lc, False), tm_c)
            subln = b_subln[j].reshape(1, -1)
            oa = _attention(qx, [(kx, vx), (kc, vc)], tq_x)
            ob = _attention(dqx, [(dkx, dvx), (dkc, dvc)], tq_x, "diff", b_lambda[j], subln, lam_init)
            x = _outproj([oa, ob], wo, None, x, mx, g, tm_x)
            if not last:
                oa = _attention(qc, [(kc, vc)], tq_c)
                ob = _attention(dqc, [(dkc, dvc)], tq_c, "diff", b_lambda[j], subln, lam_init)
                cs = _outproj([oa, ob], wo, None, cs, mc, g, tm_c)
            w1, w3, w2 = f_w1[j].astype(BF), f_w3[j].astype(BF), f_w2[j].astype(BF)
            x = _ffn(x, mx, g, w1, w3, w2, tm_x)
            if not last:
                cs = _ffn(cs, mc, g, w1, w3, w2, tm_c)
        else:
            wqkv = c_w_qkv[j].astype(BF)
            bqkv = c_b_qkv[j].reshape(1, -1)
            wo = c_w_out[j].astype(BF)
            bo = c_b_out[j].reshape(1, -1)
            qx, kx, vx = _proj_na(x, mx, g, wqkv, bqkv, tm_x)
            qc, kc, vc = _proj_na(cs, mc, g, wqkv, bqkv, tm_c)
            o = _na_attention(qx, kx, vx, kc, vc, _na_bias_table(c_rpb[j]))
            x = _outproj([o], wo, bo, x, mx, g, tm_x)
            router = _pad_cols(m_router[j], LANES)
            w1, w3, w2 = m_w1[j].astype(BF), m_w3[j].astype(BF), m_w2[j].astype(BF)
            if not last:
                oc = _attention(qc, [(kc, jnp.swapaxes(vc, 2, 3))], tq_c, "pair")
                cs = _outproj([oc], wo, bo, cs, mc, g, tm_c)
            x = _moe(x, mx, g, router, w1, w3, w2, tm_x)
            if not last:
                cs = _moe(cs, mc, g, router, w1, w3, w2, tm_c)
    return x
```

```python
import functools
import math

import jax
import jax.numpy as jnp
import numpy as np
from jax import lax
from jax.experimental import pallas as pl
from jax.experimental.pallas import tpu as pltpu
from jax.experimental.pallas import tpu_sc as plsc

BF = jnp.bfloat16
F32 = jnp.float32

LANES = 128
VMEM_LIMIT = 56 * 1024 * 1024

GRID_W = 64
EPS = 1e-6
ROPE_THETA = 10000.0
N_MOD = 6

MLA_HEADS = 8
MLA_NOPE = 64
MLA_ROPE = 32
MLA_V = 64
MLA_Q_RANK = 384
MLA_KV_RANK = 256
LOG2E = math.log2(math.e)
MLA_SCALE = (MLA_NOPE + MLA_ROPE) ** -0.5 * LOG2E

DIFF_HEADS = 4
DIFF_HD = 64
DIFF_SCALE = DIFF_HD ** -0.5 * LOG2E

NA_HEADS = 16
NA_HD = 64
NA_KH = 8
NA_KW = 16
NA_SCALE = NA_HD ** -0.5 * LOG2E
NA_SLOTS = NA_HEADS * NA_HD // LANES
MASK_VALUE = -1e30

N_EXPERTS = 8

_NT = (((1,), (1,)), ((), ()))


def _cparams(*sem):
    return pltpu.CompilerParams(dimension_semantics=sem, vmem_limit_bytes=VMEM_LIMIT)


def _dot(a, b):
    return jnp.dot(a, b, preferred_element_type=F32)


def _rms(x, g):
    return x * lax.rsqrt(jnp.mean(x * x, axis=-1, keepdims=True) + EPS) * g


def _norm_mod(x, g, shift, scale):
    return _rms(x, g) * (1 + scale) + shift


def _const_spec(shape):
    return pl.BlockSpec(shape, lambda *_: (0,) * len(shape))


def _mod_spec(mod):
    if mod.shape[0] == 1:
        return pl.BlockSpec((1,) + mod.shape[1:], lambda b, *_: (0, 0, 0))
    return pl.BlockSpec((1,) + mod.shape[1:], lambda b, *_: (b, 0, 0))


def _mod_kernel(c_ref, w_ref, b_ref, o_ref):
    c = c_ref[...]
    sc = c * jax.nn.sigmoid(c)
    o_ref[0] = _dot(sc.astype(BF), w_ref[0].astype(BF)) + b_ref[0]


def _modulation(cvec, w_mod, b_mod):
    depth, d, n = w_mod.shape
    rows = cvec.shape[0]
    return pl.pallas_call(
        _mod_kernel,
        grid=(depth, n // d),
        in_specs=[
            pl.BlockSpec((rows, d), lambda i, j: (0, 0)),
            pl.BlockSpec((1, d, d), lambda i, j: (i, 0, j)),
            pl.BlockSpec((1, 1, d), lambda i, j: (i, 0, j)),
        ],
        out_specs=pl.BlockSpec((1, rows, d), lambda i, j: (i, 0, j)),
        out_shape=jax.ShapeDtypeStruct((depth, rows, n), F32),
        compiler_params=_cparams("arbitrary", "arbitrary"),
        name="modulation",
    )(cvec, w_mod, b_mod.reshape(depth, 1, n))


_Z_CQ = 0
_Z_CKV = MLA_Q_RANK
_Z_KR = _Z_CKV + MLA_KV_RANK
_Z_KR_ROT = _Z_KR + LANES
_Z_DQ = _Z_KR_ROT + LANES
_DW = 2 * DIFF_HEADS * DIFF_HD
_Z_DQ_ROT = _Z_DQ + _DW
_Z_DK = _Z_DQ_ROT + _DW
_Z_DK_ROT = _Z_DK + _DW
_Z_DV = _Z_DK_ROT + _DW
_Z_END = _Z_DV + _DW


def _proj_ab_kernel(x_ref, mod_ref, g_ref, win_ref, qn_ref, kvn_ref, wuq_ref, wuk_ref, wuv_ref,
                    place_ref, tab_ref, q_ref, k_ref, v_ref, dq_ref, dk_ref, dv_ref):
    h = _norm_mod(x_ref[0], g_ref[0:1, :], mod_ref[0, 0:1, :], mod_ref[0, 1:2, :])
    z = _dot(h.astype(BF), win_ref[...])
    cqn = _rms(z[:, _Z_CQ:_Z_CKV], qn_ref[...]).astype(BF)
    ckvn = _rms(z[:, _Z_CKV:_Z_KR], kvn_ref[...]).astype(BF)
    q2 = _dot(cqn, wuq_ref[...])
    nq = MLA_HEADS * LANES
    cos_q, sin_q = tab_ref[0], tab_ref[1]
    for hd in range(MLA_HEADS):
        lo = hd * LANES
        q_ref[0, hd] = (q2[:, lo:lo + LANES] * cos_q + q2[:, nq + lo:nq + lo + LANES] * sin_q).astype(BF)
    kr = (z[:, _Z_KR:_Z_KR_ROT] * tab_ref[2] + z[:, _Z_KR_ROT:_Z_DQ] * tab_ref[3]).astype(BF)
    kk = _dot(ckvn, wuk_ref[...]) + _dot(kr, place_ref[...])
    vv = _dot(ckvn, wuv_ref[...])
    for hd in range(MLA_HEADS):
        lo = hd * LANES
        k_ref[0, hd] = kk[:, lo:lo + LANES].astype(BF)
        v_ref[0, hd] = vv[:, lo:lo + LANES].T.astype(BF)
    cos_d, sin_d = tab_ref[4], tab_ref[5]
    for hd in range(DIFF_HEADS):
        lo = hd * LANES
        dq = z[:, _Z_DQ + lo:_Z_DQ + lo + LANES] * cos_d + z[:, _Z_DQ_ROT + lo:_Z_DQ_ROT + lo + LANES] * sin_d
        dq_ref[0, hd] = (dq * DIFF_SCALE).astype(BF)
        dk = z[:, _Z_DK + lo:_Z_DK + lo + LANES] * cos_d + z[:, _Z_DK_ROT + lo:_Z_DK_ROT + lo + LANES] * sin_d
        dk_ref[0, hd] = dk.astype(BF)
        dv_ref[0, hd] = z[:, _Z_DV + lo:_Z_DV + lo + LANES].T.astype(BF)


def _proj_ab(x, mod, g, wts, tabs, tm):
    b, t, d = x.shape
    win, qn, kvn, wuq, wuk, wuv, place = wts
    tok = lambda hh: pl.BlockSpec((1, hh, tm, LANES), lambda bi, i: (bi, 0, i, 0))
    shp = lambda hh: jax.ShapeDtypeStruct((b, hh, t, LANES), BF)
    tok_t = lambda hh: pl.BlockSpec((1, hh, LANES, tm), lambda bi, i: (bi, 0, 0, i))
    shp_t = lambda hh: jax.ShapeDtypeStruct((b, hh, LANES, t), BF)
    return pl.pallas_call(
        _proj_ab_kernel,
        grid=(b, t // tm),
        in_specs=[
            pl.BlockSpec((1, tm, d), lambda bi, i: (bi, i, 0)),
            _mod_spec(mod),
            _const_spec(g.shape),
            _const_spec(win.shape), _const_spec(qn.shape), _const_spec(kvn.shape),
            _const_spec(wuq.shape), _const_spec(wuk.shape), _const_spec(wuv.shape),
            _const_spec(place.shape),
            pl.BlockSpec((6, tm, LANES), lambda bi, i: (0, i, 0)),
        ],
        out_specs=[tok(MLA_HEADS), tok(MLA_HEADS), tok_t(MLA_HEADS),
                   tok(DIFF_HEADS), tok(DIFF_HEADS), tok_t(DIFF_HEADS)],
        out_shape=[shp(MLA_HEADS), shp(MLA_HEADS), shp_t(MLA_HEADS),
                   shp(DIFF_HEADS), shp(DIFF_HEADS), shp_t(DIFF_HEADS)],
        compiler_params=_cparams("arbitrary", "arbitrary"),
        name="proj_ab",
    )(x, mod, g, win, qn, kvn, wuq, wuk, wuv, place, tabs)


KEY_CHUNK = 512


SUBLANES = 8
ATTN_VARIANTS = 2


def _key_chunks(kv_refs):
    off = 0
    for k_ref, v_ref in kv_refs:
        lk = k_ref.shape[2]
        for c0 in range(0, lk, KEY_CHUNK):
            w = min(KEY_CHUNK, lk - c0)
            yield k_ref, v_ref, c0, w, off
            off += w


def _softmax_pv(qms, heads, kv_refs, s_refs, p_refs):
    n = len(qms)
    tq = qms[0].shape[0]
    fold = lambda a: a.reshape(a.shape[0] // SUBLANES, SUBLANES, tq)
    mms = [None] * n
    ms = [None] * n
    lls = [jnp.zeros((SUBLANES, tq), F32)] * n
    o_ts = [jnp.zeros((LANES, tq), F32)] * n

    def scores(i, k_ref, c0, w, off):
        s = lax.dot_general(k_ref[0, heads[i], c0:c0 + w, :], qms[i], _NT, preferred_element_type=F32)
        s_refs[i][off:off + w, :] = s
        cm = jnp.max(fold(s), axis=0)
        mms[i] = cm if mms[i] is None else jnp.maximum(mms[i], cm)

    def probs(i, w, off):
        p = jnp.exp2(s_refs[i][off:off + w, :] - ms[i])
        lls[i] = lls[i] + jnp.sum(fold(p), axis=0)
        p_refs[i][off:off + w, :] = p.astype(BF)

    def values(i, v_ref, c0, w, off):
        o_ts[i] = o_ts[i] + _dot(v_ref[0, heads[i], :, c0:c0 + w], p_refs[i][off:off + w, :])

    for stage in range(n + 2):
        for k_ref, v_ref, c0, w, off in _key_chunks(kv_refs):
            if stage < n:
                scores(stage, k_ref, c0, w, off)
            if 1 <= stage <= n:
                probs(stage - 1, w, off)
            if stage >= 2:
                values(stage - 2, v_ref, c0, w, off)
        if stage < n:
            ms[stage] = jnp.max(mms[stage], axis=0, keepdims=True)
    return [(o_t / jnp.sum(ll, axis=0, keepdims=True)).T for o_t, ll in zip(o_ts, lls)]


def _attn_kernel(*refs, nseg, mode, lam_init):
    q_ref = refs[0]
    kv_refs = [(refs[1 + 2 * s], refs[2 + 2 * s]) for s in range(nseg)]
    nv = ATTN_VARIANTS
    o_ref = refs[-1 - 2 * nv]
    s_refs, p_refs = refs[-2 * nv:-nv], refs[-nv:]
    if mode == "single":
        heads = list(range(nv))
        outs = _softmax_pv([q_ref[0, h] for h in heads], heads, kv_refs, s_refs, p_refs)
        for h in heads:
            o_ref[0, h] = outs[h].astype(BF)
        return
    q = q_ref[0, 0]
    lane = lax.broadcasted_iota(jnp.int32, q.shape, 1)
    zero = jnp.zeros_like(q)
    o1, o2 = _softmax_pv([jnp.where(lane < DIFF_HD, q, zero), jnp.where(lane >= DIFF_HD, q, zero)], [0, 0],
                         kv_refs, s_refs, p_refs)
    if mode == "pair":
        o_ref[0, 0] = jnp.where(lane < DIFF_HD, o1, o2).astype(BF)
        return
    lam_ref, subln_ref = refs[1 + 2 * nseg], refs[2 + 2 * nseg]
    lv = lam_ref[...]
    lam = (jnp.exp(jnp.sum(lv[0:1] * lv[1:2], axis=-1, keepdims=True))
           - jnp.exp(jnp.sum(lv[2:3] * lv[3:4], axis=-1, keepdims=True)) + lam_init)
    o_ref[0, 0] = (_rms(o1 - lam * o2, subln_ref[...]) * (1 - lam_init)).astype(BF)


def _attention(q, kvs, tq, mode="single", lam_vecs=None, subln=None, lam_init=0.0):
    b, nh, lq, _ = q.shape
    diff = mode == "diff"
    hb = ATTN_VARIANTS if mode == "single" else 1
    assert nh % hb == 0
    q_spec = pl.BlockSpec((1, hb, tq, LANES), lambda bi, h, i: (bi, h, i, 0))
    in_specs = [q_spec]
    args = [q]
    for k, v in kvs:
        lk = k.shape[2]
        in_specs += [pl.BlockSpec((1, hb, lk, LANES), lambda bi, h, i: (bi, h, 0, 0)),
                     pl.BlockSpec((1, hb, LANES, lk), lambda bi, h, i: (bi, h, 0, 0))]
        args += [k, v]
    if diff:
        in_specs += [_const_spec(lam_vecs.shape), _const_spec(subln.shape)]
        args += [lam_vecs, subln]
    nkeys = sum(k.shape[2] for k, _ in kvs)
    return pl.pallas_call(
        functools.partial(_attn_kernel, nseg=len(kvs), mode=mode, lam_init=lam_init),
        grid=(b, nh // hb, lq // tq),
        in_specs=in_specs,
        out_specs=q_spec,
        out_shape=jax.ShapeDtypeStruct((b, nh, lq, LANES), BF),
        scratch_shapes=([pltpu.VMEM((nkeys, tq), F32)] * ATTN_VARIANTS
                        + [pltpu.VMEM((nkeys, tq), BF)] * ATTN_VARIANTS),
        compiler_params=_cparams("arbitrary", "arbitrary", "arbitrary"),
        name="attn_diff" if diff else "attn_mla",
    )(*args)


def _outproj_kernel(*refs, n_in, has_bias):
    o_refs = refs[:n_in]
    w_ref = refs[n_in]
    pos = n_in + 1
    b_ref = None
    if has_bias:
        b_ref = refs[pos]
        pos += 1
    x_ref, mod_ref, g_ref, out_ref = refs[pos:pos + 4]
    o = jnp.concatenate([r[0, h] for r in o_refs for h in range(r.shape[1])], axis=-1)
    y = _dot(o, w_ref[...])
    if has_bias:
        y = y + b_ref[...]
    out_ref[0] = x_ref[0] + mod_ref[0, 2:3, :] * _rms(y, g_ref[1:2, :])


def _outproj(os_, w, bias, x, mod, g, tm):
    b, t, d = x.shape
    in_specs = [pl.BlockSpec((1, o.shape[1], tm, LANES), lambda bi, i: (bi, 0, i, 0)) for o in os_]
    in_specs.append(_const_spec(w.shape))
    args = list(os_) + [w]
    if bias is not None:
        in_specs.append(_const_spec(bias.shape))
        args.append(bias)
    in_specs += [pl.BlockSpec((1, tm, d), lambda bi, i: (bi, i, 0)), _mod_spec(mod), _const_spec(g.shape)]
    args += [x, mod, g]
    return pl.pallas_call(
        functools.partial(_outproj_kernel, n_in=len(os_), has_bias=bias is not None),
        grid=(b, t // tm),
        in_specs=in_specs,
        out_specs=pl.BlockSpec((1, tm, d), lambda bi, i: (bi, i, 0)),
        out_shape=jax.ShapeDtypeStruct((b, t, d), F32),
        compiler_params=_cparams("arbitrary", "arbitrary"),
        name="outproj",
    )(*args)


MXU_TILE = 256


def _ff_chunks(f):
    tiles = f // MXU_TILE
    if f % MXU_TILE or tiles < 2:
        return [(0, f)]
    first = (tiles + 1) // 2 * MXU_TILE
    return [(0, first), (first, f - first)]


def _swiglu(hn, w1_ref, w3_ref, w2_ref):
    y = None
    for lo, n in _ff_chunks(w1_ref.shape[1]):
        a = _dot(hn, w1_ref[:, lo:lo + n])
        act = (a * jax.nn.sigmoid(a) * _dot(hn, w3_ref[:, lo:lo + n])).astype(BF)
        part = _dot(act, w2_ref[lo:lo + n, :])
        y = part if y is None else y + part
    return y


def _ffn_kernel(x_ref, mod_ref, g_ref, w1_ref, w3_ref, w2_ref, out_ref):
    h = _norm_mod(x_ref[0], g_ref[2:3, :], mod_ref[0, 3:4, :], mod_ref[0, 4:5, :])
    y = _swiglu(h.astype(BF), w1_ref, w3_ref, w2_ref)
    out_ref[0] = x_ref[0] + mod_ref[0, 5:6, :] * _rms(y, g_ref[3:4, :])


def _resident_spec(shape, index_map):
    return pl.BlockSpec(shape, index_map, pipeline_mode=pl.Buffered(1))


def _ffn(x, mod, g, w1, w3, w2, tm):
    b, t, d = x.shape
    return pl.pallas_call(
        _ffn_kernel,
        grid=(b, t // tm),
        in_specs=[
            pl.BlockSpec((1, tm, d), lambda bi, i: (bi, i, 0)),
            _mod_spec(mod),
            _const_spec(g.shape),
            _resident_spec(w1.shape, lambda bi, i: (0, 0)),
            _resident_spec(w3.shape, lambda bi, i: (0, 0)),
            _resident_spec(w2.shape, lambda bi, i: (0, 0)),
        ],
        out_specs=pl.BlockSpec((1, tm, d), lambda bi, i: (bi, i, 0)),
        out_shape=jax.ShapeDtypeStruct((b, t, d), F32),
        compiler_params=_cparams("arbitrary", "arbitrary"),
        name="ffn_dense",
    )(x, mod, g, w1, w3, w2)


MOE_ROW_TILE = 512
PACK_CHUNKS = 4
SC_WINDOW = 128
_HI_MASK = -65536
_LO_MASK = 65535


def _pack_rows(v, out_ref):
    half = v.shape[1] // 2
    vb = v.astype(BF).astype(F32)
    lo = (pltpu.bitcast(vb[:, :half], jnp.int32) >> 16) & _LO_MASK
    hi = pltpu.bitcast(vb[:, half:], jnp.int32) & _HI_MASK
    w = lo | hi
    for c in range(PACK_CHUNKS):
        out_ref[c] = w[:, c * LANES:(c + 1) * LANES]


def _unpack_rows(ref):
    w = jnp.concatenate([ref[c] for c in range(PACK_CHUNKS)], axis=-1)
    lo = pltpu.bitcast(w << 16, F32)
    hi = pltpu.bitcast(w & _HI_MASK, F32)
    return jnp.concatenate([lo, hi], axis=-1)


def _router_kernel(x_ref, mod_ref, g_ref, router_ref, tri_ref, hp_ref, meta_ref, gate_ref, cnt_ref, carry_ref):
    @pl.when((pl.program_id(0) == 0) & (pl.program_id(1) == 0))
    def _():
        carry_ref[...] = jnp.zeros_like(carry_ref)

    h = _norm_mod(x_ref[0], g_ref[2:3, :], mod_ref[0, 3:4, :], mod_ref[0, 4:5, :])
    _pack_rows(h, hp_ref)
    logits = jnp.dot(h, router_ref[...], preferred_element_type=F32, precision=lax.Precision.HIGHEST)
    lane = lax.broadcasted_iota(jnp.int32, logits.shape, 1)
    logits = jnp.where(lane < N_EXPERTS, logits, -jnp.inf)
    m1 = jnp.max(logits, axis=-1, keepdims=True)
    i1 = jnp.min(jnp.where(logits == m1, lane, LANES), axis=-1, keepdims=True)
    rest = jnp.where(lane == i1, -jnp.inf, logits)
    m2 = jnp.max(rest, axis=-1, keepdims=True)
    i2 = jnp.min(jnp.where(rest == m2, lane, LANES), axis=-1, keepdims=True)
    e2 = jnp.exp(m2 - m1)
    denom = 1.0 + e2
    assigned = jnp.where((lane == i1) | (lane == i2), 1.0, 0.0)
    ranks = _dot(tri_ref[...], assigned.astype(BF)) + carry_ref[...]
    r1 = jnp.sum(jnp.where(lane == i1, ranks, 0.0), axis=-1, keepdims=True).astype(jnp.int32)
    r2 = jnp.sum(jnp.where(lane == i2, ranks, 0.0), axis=-1, keepdims=True).astype(jnp.int32)
    carry_ref[...] += jnp.sum(assigned, axis=0, keepdims=True)
    cnt_ref[...] = carry_ref[...]
    col = lax.broadcasted_iota(jnp.int32, meta_ref.shape, 1)
    meta_ref[...] = jnp.where(col == 0, i1, jnp.where(col == 1, i2, jnp.where(col == 2, r1, r2)))
    gate_ref[...] = jnp.where(col == 0, 1.0 / denom, e2 / denom)


def _router(x, mod, g, router, tm):
    b, t, d = x.shape
    n = b * t
    nt = t // tm
    tri = jnp.asarray(np.tril(np.ones((tm, tm), np.float32), -1), BF)
    row_spec = lambda w, dt: (pl.BlockSpec((tm, w), lambda bi, i: (bi * nt + i, 0)), jax.ShapeDtypeStruct((n, w), dt))
    meta_spec, meta_shape = row_spec(8, jnp.int32)
    gate_spec, gate_shape = row_spec(8, F32)
    return pl.pallas_call(
        _router_kernel,
        grid=(b, nt),
        in_specs=[
            pl.BlockSpec((1, tm, d), lambda bi, i: (bi, i, 0)),
            _mod_spec(mod),
            _const_spec(g.shape),
            _const_spec(router.shape),
            _const_spec(tri.shape),
        ],
        out_specs=[
            pl.BlockSpec((PACK_CHUNKS, tm, LANES), lambda bi, i: (0, bi * nt + i, 0)),
            meta_spec, gate_spec,
            pl.BlockSpec((1, LANES), lambda bi, i: (0, 0)),
        ],
        out_shape=[jax.ShapeDtypeStruct((PACK_CHUNKS, n, LANES), jnp.int32), meta_shape, gate_shape,
                   jax.ShapeDtypeStruct((1, LANES), F32)],
        scratch_shapes=[pltpu.VMEM((1, LANES), F32)],
        compiler_params=_cparams("arbitrary", "arbitrary"),
        name="moe_router",
    )(x, mod, g, router, tri)


def _sc_mesh():
    return plsc.VectorSubcoreMesh(core_axis_name="core", subcore_axis_name="subcore")


def _sc_scatter_rows(rows, idx_a, idx_b, n_out):
    nrows = rows.shape[0]

    @pl.kernel(out_type=jax.ShapeDtypeStruct((n_out, LANES), rows.dtype), mesh=_sc_mesh(), scratch_types=[])
    def scatter_kernel(x_hbm, ia_hbm, ib_hbm, o_hbm):
        def body(x_vmem, ia_vmem, ib_vmem):
            pltpu.sync_copy(x_vmem, o_hbm.at[ia_vmem.at[0]])
            pltpu.sync_copy(x_vmem, o_hbm.at[ib_vmem.at[0]])

        pltpu.emit_pipeline(
            body,
            grid=(nrows // SC_WINDOW,),
            in_specs=[pl.BlockSpec((SC_WINDOW, LANES), lambda i: (i, 0)),
                      pl.BlockSpec((1, SC_WINDOW), lambda i: (0, i)),
                      pl.BlockSpec((1, SC_WINDOW), lambda i: (0, i))],
            out_specs=[],
            core_axis_name=("core", "subcore"),
            dimension_semantics=(pltpu.PARALLEL,),
        )(x_hbm, ia_hbm, ib_hbm)

    return scatter_kernel(rows, idx_a, idx_b)


def _sc_gather_rows(table, idx_a, idx_b):
    nrows = idx_a.shape[1]
    out = jax.ShapeDtypeStruct((nrows, LANES), table.dtype)

    @pl.kernel(out_type=(out, out), mesh=_sc_mesh(), scratch_types=[])
    def gather_kernel(t_hbm, ia_hbm, ib_hbm, oa_hbm, ob_hbm):
        def body(ia_vmem, ib_vmem, oa_vmem, ob_vmem):
            pltpu.sync_copy(t_hbm.at[ia_vmem.at[0]], oa_vmem)
            pltpu.sync_copy(t_hbm.at[ib_vmem.at[0]], ob_vmem)

        pltpu.emit_pipeline(
            body,
            grid=(nrows // SC_WINDOW,),
            in_specs=[pl.BlockSpec((1, SC_WINDOW), lambda i: (0, i)),
                      pl.BlockSpec((1, SC_WINDOW), lambda i: (0, i))],
            out_specs=[pl.BlockSpec((SC_WINDOW, LANES), lambda i: (i, 0)),
                       pl.BlockSpec((SC_WINDOW, LANES), lambda i: (i, 0))],
            core_axis_name=("core", "subcore"),
            dimension_semantics=(pltpu.PARALLEL,),
        )(ia_hbm, ib_hbm, oa_hbm, ob_hbm)

    return gather_kernel(table, idx_a, idx_b)


def _experts_kernel(te_ref, tv_ref, xs_ref, w1_ref, w3_ref, w2_ref, ys_ref):
    del te_ref
    valid = tv_ref[pl.program_id(0)]

    @pl.when(valid > 0)
    def _():
        row = lax.broadcasted_iota(jnp.int32, (xs_ref.shape[1], 1), 0)
        hn = jnp.where(row < valid, _unpack_rows(xs_ref), 0.0).astype(BF)
        _pack_rows(_swiglu(hn, w1_ref.at[0], w3_ref.at[0], w2_ref.at[0]), ys_ref)

    @pl.when(valid == 0)
    def _():
        ys_ref[...] = jnp.zeros_like(ys_ref)


def _experts(xs, tile_expert, tile_valid, w1, w3, w2):
    _, p, _ = xs.shape
    tr = MOE_ROW_TILE
    rows = pl.BlockSpec((PACK_CHUNKS, tr, LANES), lambda j, te, tv: (0, j, 0))
    weights = lambda w: _resident_spec((1,) + w.shape[1:], lambda j, te, tv: (te[j], 0, 0))
    return pl.pallas_call(
        _experts_kernel,
        grid_spec=pltpu.PrefetchScalarGridSpec(
            num_scalar_prefetch=2,
            grid=(p // tr,),
            in_specs=[rows, weights(w1), weights(w3), weights(w2)],
            out_specs=rows,
        ),
        out_shape=jax.ShapeDtypeStruct(xs.shape, jnp.int32),
        compiler_params=_cparams("arbitrary"),
        name="moe_experts",
    )(tile_expert, tile_valid, xs, w1, w3, w2)


def _combine_kernel(x_ref, ya_ref, yb_ref, gate_ref, mod_ref, g_ref, out_ref):
    gates = gate_ref[...]
    fx = gates[:, 0:1] * _unpack_rows(ya_ref) + gates[:, 1:2] * _unpack_rows(yb_ref)
    out_ref[0] = x_ref[0] + mod_ref[0, 5:6, :] * _rms(fx, g_ref[3:4, :])


def _combine(x, ya, yb, gates, mod, g, tm):
    b, t, d = x.shape
    nt = t // tm
    packed = pl.BlockSpec((PACK_CHUNKS, tm, LANES), lambda bi, i: (0, bi * nt + i, 0))
    return pl.pallas_call(
        _combine_kernel,
        grid=(b, nt),
        in_specs=[
            pl.BlockSpec((1, tm, d), lambda bi, i: (bi, i, 0)),
            packed, packed,
            pl.BlockSpec((tm, gates.shape[1]), lambda bi, i: (bi * nt + i, 0)),
            _mod_spec(mod),
            _const_spec(g.shape),
        ],
        out_specs=pl.BlockSpec((1, tm, d), lambda bi, i: (bi, i, 0)),
        out_shape=jax.ShapeDtypeStruct((b, t, d), F32),
        compiler_params=_cparams("arbitrary", "arbitrary"),
        name="moe_combine",
    )(x, ya, yb, gates, mod, g)


def _moe(x, mod, g, router, w1, w3, w2, tm):
    b, t, d = x.shape
    n = b * t
    ne = w1.shape[0]
    tr = MOE_ROW_TILE
    assert d == 2 * PACK_CHUNKS * LANES and (PACK_CHUNKS * n) % (SC_WINDOW * 32) == 0
    hp, meta, gates, counts = _router(x, mod, g, router, tm)

    ntile = 2 * n // tr + ne
    p = ntile * tr
    cnt = counts[0, :ne].astype(jnp.int32)
    tiles = (cnt + tr - 1) // tr
    tile_end = jnp.cumsum(tiles)
    tile_start = tile_end - tiles
    base = tile_start * tr
    eids = jnp.arange(ne, dtype=jnp.int32)
    base_of = lambda e: jnp.sum(jnp.where(e[:, None] == eids[None, :], base[None, :], 0), axis=-1)
    pos_a = base_of(meta[:, 0]) + meta[:, 2]
    pos_b = base_of(meta[:, 1]) + meta[:, 3]
    chunk = jnp.arange(PACK_CHUNKS, dtype=jnp.int32)[:, None] * p
    idx_a = (chunk + pos_a[None, :]).reshape(1, PACK_CHUNKS * n)
    idx_b = (chunk + pos_b[None, :]).reshape(1, PACK_CHUNKS * n)
    tj = jnp.arange(ntile, dtype=jnp.int32)
    tile_expert = jnp.minimum(jnp.sum(tj[:, None] >= tile_end[None, :], axis=-1), ne - 1).astype(jnp.int32)
    done = jnp.sum(jnp.where(tile_expert[:, None] == eids[None, :], tile_start[None, :], 0), axis=-1)
    left = jnp.sum(jnp.where(tile_expert[:, None] == eids[None, :], cnt[None, :], 0), axis=-1) - (tj - done) * tr
    tile_valid = jnp.where(tj < tile_end[-1], jnp.clip(left, 0, tr), 0).astype(jnp.int32)

    xs = _sc_scatter_rows(hp.reshape(PACK_CHUNKS * n, LANES), idx_a, idx_b, PACK_CHUNKS * p)
    ys = _experts(xs.reshape(PACK_CHUNKS, p, LANES), tile_expert, tile_valid, w1, w3, w2)
    ya, yb = _sc_gather_rows(ys.reshape(PACK_CHUNKS * p, LANES), idx_a, idx_b)
    shp = (PACK_CHUNKS, n, LANES)
    return _combine(x, ya.reshape(shp), yb.reshape(shp), gates, mod, g, tm)


def _proj_na_kernel(x_ref, mod_ref, g_ref, w_ref, b_ref, q_ref, k_ref, v_ref):
    h = _norm_mod(x_ref[0], g_ref[0:1, :], mod_ref[0, 0:1, :], mod_ref[0, 1:2, :])
    z = _dot(h.astype(BF), w_ref[...]) + b_ref[...]
    n = NA_SLOTS * LANES
    for s in range(NA_SLOTS):
        lo = s * LANES
        q_ref[0, s] = (z[:, lo:lo + LANES] * NA_SCALE).astype(BF)
        k_ref[0, s] = z[:, n + lo:n + lo + LANES].astype(BF)
        v_ref[0, s] = z[:, 2 * n + lo:2 * n + lo + LANES].astype(BF)


def _proj_na(x, mod, g, w, bias, tm):
    b, t, d = x.shape
    tok = pl.BlockSpec((1, NA_SLOTS, tm, LANES), lambda bi, i: (bi, 0, i, 0))
    shp = jax.ShapeDtypeStruct((b, NA_SLOTS, t, LANES), BF)
    return pl.pallas_call(
        _proj_na_kernel,
        grid=(b, t // tm),
        in_specs=[
            pl.BlockSpec((1, tm, d), lambda bi, i: (bi, i, 0)),
            _mod_spec(mod),
            _const_spec(g.shape),
            _const_spec(w.shape),
            _const_spec(bias.shape),
        ],
        out_specs=[tok, tok, tok],
        out_shape=[shp, shp, shp],
        compiler_params=_cparams("arbitrary", "arbitrary"),
        name="proj_na",
    )(x, mod, g, w, bias)


NA_ROWS_PER_STEP = 2


def _na_window_start(r, rows):
    return jnp.clip(r - NA_KH // 2, 0, rows - NA_KH)


def _na_kernel(q_ref, kx_ref, vx_ref, kc_ref, vc_ref, *rest, rows):
    bias_refs, o_ref = rest[:NA_ROWS_PER_STEP], rest[NA_ROWS_PER_STEP]
    nwin = NA_KH * GRID_W
    lane = lax.broadcasted_iota(jnp.int32, (GRID_W, LANES), 1)
    work = [(a, s) for a in range(NA_ROWS_PER_STEP) for s in range(NA_SLOTS)]
    idx = range(len(work))
    starts = [pl.multiple_of(_na_window_start(pl.program_id(1) * NA_ROWS_PER_STEP + a, rows) * GRID_W, GRID_W)
              for a in range(NA_ROWS_PER_STEP)]
    sws, scs = [], []
    for a, s in work:
        q = q_ref[0, s, a * GRID_W:(a + 1) * GRID_W, :]
        zero = jnp.zeros_like(q)
        q2 = jnp.concatenate([jnp.where(lane < NA_HD, q, zero), jnp.where(lane >= NA_HD, q, zero)], axis=0)
        kw = kx_ref[0, s, pl.ds(starts[a], nwin), :]
        sws.append(lax.dot_general(q2, kw, _NT, preferred_element_type=F32) + bias_refs[a][0, s])
        scs.append(lax.dot_general(q2, kc_ref[0, s], _NT, preferred_element_type=F32))
    ms = [jnp.maximum(jnp.max(sws[i], axis=-1, keepdims=True), jnp.max(scs[i], axis=-1, keepdims=True))
          for i in idx]
    pws = [jnp.exp2(sws[i] - ms[i]) for i in idx]
    pcs = [jnp.exp2(scs[i] - ms[i]) for i in idx]
    ls = [jnp.sum(pws[i], axis=-1, keepdims=True) + jnp.sum(pcs[i], axis=-1, keepdims=True) for i in idx]
    for i, (a, s) in enumerate(work):
        vw = vx_ref[0, s, pl.ds(starts[a], nwin), :]
        o2 = (_dot(pws[i].astype(BF), vw) + _dot(pcs[i].astype(BF), vc_ref[0, s])) / ls[i]
        o_ref[0, s, a * GRID_W:(a + 1) * GRID_W, :] = jnp.where(lane < NA_HD, o2[:GRID_W], o2[GRID_W:]).astype(BF)


def _na_attention(q, kx, vx, kc, vc, bias):
    b, ns, l, _ = q.shape
    rows = l // GRID_W
    lc = kc.shape[2]
    nr = NA_ROWS_PER_STEP
    assert rows % nr == 0

    def bias_spec(a):
        def index_map(bi, i):
            r = i * nr + a
            return (r - _na_window_start(r, rows), 0, 0, 0)
        return pl.BlockSpec((1,) + bias.shape[1:], index_map)

    full = lambda n: pl.BlockSpec((1, ns, n, LANES), lambda bi, i: (bi, 0, 0, 0))
    row = pl.BlockSpec((1, ns, nr * GRID_W, LANES), lambda bi, i: (bi, 0, i, 0))
    return pl.pallas_call(
        functools.partial(_na_kernel, rows=rows),
        grid=(b, rows // nr),
        in_specs=[row, full(l), full(l), full(lc), full(lc)] + [bias_spec(a) for a in range(nr)],
        out_specs=row,
        out_shape=jax.ShapeDtypeStruct((b, ns, l, LANES), BF),
        compiler_params=_cparams("arbitrary", "arbitrary"),
        name="na_attn",
    )(q, kx, vx, kc, vc, *([bias] * nr))


def _axis_tables(pos, dim):
    inv = ROPE_THETA ** (-jnp.arange(0, dim, 2, dtype=F32) / dim)
    ang = pos.astype(F32)[:, None] * inv[None, :]
    ang = jnp.concatenate([ang, ang], axis=-1)
    return jnp.cos(ang), jnp.sin(ang)


def _rope_tables(n, rope):
    if rope:
        t = jnp.arange(n, dtype=jnp.int32)
        row, col = t // GRID_W, t % GRID_W

        def cs(d):
            cr, sr = _axis_tables(row, d // 2)
            cc, sc = _axis_tables(col, d // 2)
            return jnp.concatenate([cr, cc], axis=-1), jnp.concatenate([sr, sc], axis=-1)

        c32, s32 = cs(MLA_ROPE)
        c64, s64 = cs(DIFF_HD)
    else:
        c32, s32 = jnp.ones((n, MLA_ROPE), F32), jnp.zeros((n, MLA_ROPE), F32)
        c64, s64 = jnp.ones((n, DIFF_HD), F32), jnp.zeros((n, DIFF_HD), F32)
    ones = jnp.ones((n, MLA_NOPE), F32)
    pad = lambda a: jnp.pad(a, ((0, 0), (0, LANES - a.shape[1])))
    cos_q = MLA_SCALE * pad(jnp.concatenate([ones, c32], axis=-1))
    sin_q = MLA_SCALE * pad(jnp.concatenate([jnp.zeros_like(ones), s32], axis=-1))
    return jnp.stack([cos_q, sin_q, pad(c32), pad(s32),
                      jnp.concatenate([c64, c64], axis=-1), jnp.concatenate([s64, s64], axis=-1)])


def _rot_cols(w, d):
    shp = w.shape
    w5 = w.reshape(shp[:-1] + (-1, 2, 2, d // 4))
    r = jnp.concatenate([-w5[..., 1:2, :], w5[..., 0:1, :]], axis=-2)
    return r.reshape(shp)


def _pad_cols(w, n):
    return jnp.pad(w, ((0, 0), (0, n - w.shape[1])))


def _prep_ab_weights(w_in, q_norm, kv_norm, w_uq, w_ukv, w_out):
    o_cq, o_ckv, o_kr = MLA_Q_RANK, MLA_Q_RANK + MLA_KV_RANK, MLA_Q_RANK + MLA_KV_RANK + MLA_ROPE
    o_dq, o_dk = o_kr + _DW, o_kr + 2 * _DW
    w_kr, w_dq, w_dk, w_dv = w_in[:, o_ckv:o_kr], w_in[:, o_kr:o_dq], w_in[:, o_dq:o_dk], w_in[:, o_dk:]
    win = jnp.concatenate([
        w_in[:, :o_ckv], _pad_cols(w_kr, LANES), _pad_cols(_rot_cols(w_kr, MLA_ROPE), LANES),
        w_dq, _rot_cols(w_dq, DIFF_HD), w_dk, _rot_cols(w_dk, DIFF_HD), w_dv], axis=1).astype(BF)
    rq = w_uq.shape[0]
    uq = w_uq.reshape(rq, MLA_HEADS, MLA_NOPE + MLA_ROPE)
    nope, rope = uq[..., :MLA_NOPE], uq[..., MLA_NOPE:]
    zpad = jnp.zeros((rq, MLA_HEADS, LANES - MLA_NOPE - MLA_ROPE), F32)
    main = jnp.concatenate([nope, rope, zpad], axis=-1).reshape(rq, MLA_HEADS * LANES)
    rot = jnp.concatenate([jnp.zeros_like(nope), _rot_cols(rope, MLA_ROPE), zpad], axis=-1)
    wuq = jnp.concatenate([main, rot.reshape(rq, MLA_HEADS * LANES)], axis=1).astype(BF)
    rkv = w_ukv.shape[0]
    ukv = w_ukv.reshape(rkv, MLA_HEADS, MLA_NOPE + MLA_V)
    slot = lambda a: jnp.pad(a, ((0, 0), (0, 0), (0, LANES - a.shape[-1]))).reshape(rkv, MLA_HEADS * LANES)
    wuk = slot(ukv[..., :MLA_NOPE]).astype(BF)
    wuv = slot(ukv[..., MLA_NOPE:]).astype(BF)
    place = np.zeros((LANES, MLA_HEADS * LANES), np.float32)
    for hd in range(MLA_HEADS):
        place[np.arange(MLA_ROPE), hd * LANES + MLA_NOPE + np.arange(MLA_ROPE)] = 1.0
    d = w_out.shape[1]
    wo_a = w_out[:MLA_HEADS * MLA_V].reshape(MLA_HEADS, MLA_V, d)
    wo_a = jnp.pad(wo_a, ((0, 0), (0, LANES - MLA_V), (0, 0))).reshape(MLA_HEADS * LANES, d)
    wo = jnp.concatenate([wo_a, w_out[MLA_HEADS * MLA_V:]], axis=0).astype(BF)
    proj = (win, q_norm.reshape(1, -1), kv_norm.reshape(1, -1), wuq, wuk, wuv, jnp.asarray(place, BF))
    return proj, wo


def _na_bias_table(rpb):
    nh = rpb.shape[0]
    cols = np.arange(GRID_W)
    col_start = np.clip(cols - NA_KW // 2, 0, GRID_W - NA_KW)
    kcol = np.arange(GRID_W)
    valid = (kcol[None, :] >= col_start[:, None]) & (kcol[None, :] < col_start[:, None] + NA_KW)
    dc = kcol[None, :] - cols[:, None] + (NA_KW - 1)
    onehot = (dc[None] == np.arange(2 * NA_KW - 1)[:, None, None]) & valid[None]
    tz = jnp.einsum('hrd,dck->hrck', rpb * LOG2E, jnp.asarray(onehot, F32), precision=lax.Precision.HIGHEST)
    tz = jnp.where(valid[None, None], tz, MASK_VALUE)
    cases = [tz[:, NA_KH - 1 - c:2 * NA_KH - 1 - c] for c in range(NA_KH)]
    tbl = jnp.transpose(jnp.stack(cases), (0, 1, 3, 2, 4))
    return tbl.reshape(NA_KH, nh // 2, 2 * GRID_W, NA_KH * GRID_W)


def _tile(n, pref):
    return pref if n % pref == 0 else n


def kernel(x, c, ctx, c_ctx, w_mod, b_mod, norm_g, a_w_in, a_q_norm, a_kv_norm, a_w_uq, a_w_ukv, b_lambda, b_subln,
           ab_w_out, f_w1, f_w3, f_w2, c_w_qkv, c_b_qkv, c_rpb, c_w_out, c_b_out, m_router, m_w1, m_w3, m_w2):
    b, l, d = x.shape
    lc = ctx.shape[1]
    depth = w_mod.shape[0]
    assert l % GRID_W == 0 and l // GRID_W >= NA_KH

    mod_rows = 16
    cvec = jnp.zeros((mod_rows, d), F32).at[:b].set(c).at[b].set(c_ctx)
    mod = _modulation(cvec, w_mod, b_mod)

    tm_x, tm_c = _tile(l, 512), _tile(lc, 256)
    tq_x, tq_c = _tile(l, 512), _tile(lc, 256)
    cs = ctx
    for i in range(depth):
        last = i == depth - 1
        j = i // 2
        mx = mod[i, :b].reshape(b, N_MOD, d)
        mc = mod[i, b].reshape(1, N_MOD, d)
        g = norm_g[i]
        if i % 2 == 0:
            lam_init = 0.8 - 0.6 * math.exp(-0.3 * i)
            proj_w, wo = _prep_ab_weights(a_w_in[j], a_q_norm[j], a_kv_norm[j], a_w_uq[j], a_w_ukv[j], ab_w_out[j])
            qx, kx, vx, dqx, dkx, dvx = _proj_ab(x, mx, g, proj_w, _rope_tables(l, True), tm_x)
            qc, kc, vc, dqc, dkc, dvc = _proj_ab(cs, mc, g, proj_w, _rope_tables(lc, False), tm_c)
            subln = b_subln[j].reshape(1, -1)
            oa = _attention(qx, [(kx, vx), (kc, vc)], tq_x)
            ob = _attention(dqx, [(dkx, dvx), (dkc, dvc)], tq_x, "diff", b_lambda[j], subln, lam_init)
            x = _outproj([oa, ob], wo, None, x, mx, g, tm_x)
            if not last:
                oa = _attention(qc, [(kc, vc)], tq_c)
                ob = _attention(dqc, [(dkc, dvc)], tq_c, "diff", b_lambda[j], subln, lam_init)
                cs = _outproj([oa, ob], wo, None, cs, mc, g, tm_c)
            w1, w3, w2 = f_w1[j].astype(BF), f_w3[j].astype(BF), f_w2[j].astype(BF)
            x = _ffn(x, mx, g, w1, w3, w2, tm_x)
            if not last:
                cs = _ffn(cs, mc, g, w1, w3, w2, tm_c)
        else:
            wqkv = c_w_qkv[j].astype(BF)
            bqkv = c_b_qkv[j].reshape(1, -1)
            wo = c_w_out[j].astype(BF)
            bo = c_b_out[j].reshape(1, -1)
            qx, kx, vx = _proj_na(x, mx, g, wqkv, bqkv, tm_x)
            qc, kc, vc = _proj_na(cs, mc, g, wqkv, bqkv, tm_c)
            o = _na_attention(qx, kx, vx, kc, vc, _na_bias_table(c_rpb[j]))
            x = _outproj([o], wo, bo, x, mx, g, tm_x)
            router = _pad_cols(m_router[j], LANES)
            w1, w3, w2 = m_w1[j].astype(BF), m_w3[j].astype(BF), m_w2[j].astype(BF)
            if not last:
                oc = _attention(qc, [(kc, jnp.swapaxes(vc, 2, 3))], tq_c, "pair")
                cs = _outproj([oc], wo, bo, cs, mc, g, tm_c)
            x = _moe(x, mx, g, router, w1, w3, w2, tm_x)
            if not last:
                cs = _moe(cs, mc, g, router, w1, w3, w2, tm_c)
    return x
```

```python
import functools
import math

import jax
import jax.numpy as jnp
import numpy as np
from jax import lax
from jax.experimental import pallas as pl
from jax.experimental.pallas import tpu as pltpu
from jax.experimental.pallas import tpu_sc as plsc

BF = jnp.bfloat16
F32 = jnp.float32

LANES = 128
VMEM_LIMIT = 56 * 1024 * 1024

GRID_W = 64
EPS = 1e-6
ROPE_THETA = 10000.0
N_MOD = 6

MLA_HEADS = 8
MLA_NOPE = 64
MLA_ROPE = 32
MLA_V = 64
MLA_Q_RANK = 384
MLA_KV_RANK = 256
LOG2E = math.log2(math.e)
MLA_SCALE = (MLA_NOPE + MLA_ROPE) ** -0.5 * LOG2E

DIFF_HEADS = 4
DIFF_HD = 64
DIFF_SCALE = DIFF_HD ** -0.5 * LOG2E

NA_HEADS = 16
NA_HD = 64
NA_KH = 8
NA_KW = 16
NA_SCALE = NA_HD ** -0.5 * LOG2E
NA_SLOTS = NA_HEADS * NA_HD // LANES
MASK_VALUE = -1e30

N_EXPERTS = 8

_NT = (((1,), (1,)), ((), ()))


def _cparams(*sem):
    return pltpu.CompilerParams(dimension_semantics=sem, vmem_limit_bytes=VMEM_LIMIT)


def _dot(a, b):
    return jnp.dot(a, b, preferred_element_type=F32)


def _rms(x, g):
    return x * lax.rsqrt(jnp.mean(x * x, axis=-1, keepdims=True) + EPS) * g


def _norm_mod(x, g, shift, scale):
    return _rms(x, g) * (1 + scale) + shift


def _const_spec(shape):
    return pl.BlockSpec(shape, lambda *_: (0,) * len(shape))


def _mod_spec(mod):
    if mod.shape[0] == 1:
        return pl.BlockSpec((1,) + mod.shape[1:], lambda b, *_: (0, 0, 0))
    return pl.BlockSpec((1,) + mod.shape[1:], lambda b, *_: (b, 0, 0))


def _mod_kernel(c_ref, w_ref, b_ref, o_ref):
    c = c_ref[...]
    sc = c * jax.nn.sigmoid(c)
    o_ref[0] = _dot(sc.astype(BF), w_ref[0].astype(BF)) + b_ref[0]


def _modulation(cvec, w_mod, b_mod):
    depth, d, n = w_mod.shape
    rows = cvec.shape[0]
    return pl.pallas_call(
        _mod_kernel,
        grid=(depth, n // d),
        in_specs=[
            pl.BlockSpec((rows, d), lambda i, j: (0, 0)),
            pl.BlockSpec((1, d, d), lambda i, j: (i, 0, j)),
            pl.BlockSpec((1, 1, d), lambda i, j: (i, 0, j)),
        ],
        out_specs=pl.BlockSpec((1, rows, d), lambda i, j: (i, 0, j)),
        out_shape=jax.ShapeDtypeStruct((depth, rows, n), F32),
        compiler_params=_cparams("arbitrary", "arbitrary"),
        name="modulation",
    )(cvec, w_mod, b_mod.reshape(depth, 1, n))


_Z_CQ = 0
_Z_CKV = MLA_Q_RANK
_Z_KR = _Z_CKV + MLA_KV_RANK
_Z_KR_ROT = _Z_KR + LANES
_Z_DQ = _Z_KR_ROT + LANES
_DW = 2 * DIFF_HEADS * DIFF_HD
_Z_DQ_ROT = _Z_DQ + _DW
_Z_DK = _Z_DQ_ROT + _DW
_Z_DK_ROT = _Z_DK + _DW
_Z_DV = _Z_DK_ROT + _DW
_Z_END = _Z_DV + _DW


def _proj_ab_kernel(x_ref, mod_ref, g_ref, win_ref, qn_ref, kvn_ref, wuq_ref, wuk_ref, wuv_ref,
                    place_ref, tab_ref, q_ref, k_ref, v_ref, dq_ref, dk_ref, dv_ref):
    h = _norm_mod(x_ref[0], g_ref[0:1, :], mod_ref[0, 0:1, :], mod_ref[0, 1:2, :])
    z = _dot(h.astype(BF), win_ref[...])
    cqn = _rms(z[:, _Z_CQ:_Z_CKV], qn_ref[...]).astype(BF)
    ckvn = _rms(z[:, _Z_CKV:_Z_KR], kvn_ref[...]).astype(BF)
    q2 = _dot(cqn, wuq_ref[...])
    nq = MLA_HEADS * LANES
    cos_q, sin_q = tab_ref[0], tab_ref[1]
    for hd in range(MLA_HEADS):
        lo = hd * LANES
        q_ref[0, hd] = (q2[:, lo:lo + LANES] * cos_q + q2[:, nq + lo:nq + lo + LANES] * sin_q).astype(BF)
    kr = (z[:, _Z_KR:_Z_KR_ROT] * tab_ref[2] + z[:, _Z_KR_ROT:_Z_DQ] * tab_ref[3]).astype(BF)
    kk = _dot(ckvn, wuk_ref[...]) + _dot(kr, place_ref[...])
    vv = _dot(ckvn, wuv_ref[...])
    for hd in range(MLA_HEADS):
        lo = hd * LANES
        k_ref[0, hd] = kk[:, lo:lo + LANES].astype(BF)
        v_ref[0, hd] = vv[:, lo:lo + LANES].T.astype(BF)
    cos_d, sin_d = tab_ref[4], tab_ref[5]
    for hd in range(DIFF_HEADS):
        lo = hd * LANES
        dq = z[:, _Z_DQ + lo:_Z_DQ + lo + LANES] * cos_d + z[:, _Z_DQ_ROT + lo:_Z_DQ_ROT + lo + LANES] * sin_d
        dq_ref[0, hd] = (dq * DIFF_SCALE).astype(BF)
        dk = z[:, _Z_DK + lo:_Z_DK + lo + LANES] * cos_d + z[:, _Z_DK_ROT + lo:_Z_DK_ROT + lo + LANES] * sin_d
        dk_ref[0, hd] = dk.astype(BF)
        dv_ref[0, hd] = z[:, _Z_DV + lo:_Z_DV + lo + LANES].T.astype(BF)


def _proj_ab(x, mod, g, wts, tabs, tm):
    b, t, d = x.shape
    win, qn, kvn, wuq, wuk, wuv, place = wts
    tok = lambda hh: pl.BlockSpec((1, hh, tm, LANES), lambda bi, i: (bi, 0, i, 0))
    shp = lambda hh: jax.ShapeDtypeStruct((b, hh, t, LANES), BF)
    tok_t = lambda hh: pl.BlockSpec((1, hh, LANES, tm), lambda bi, i: (bi, 0, 0, i))
    shp_t = lambda hh: jax.ShapeDtypeStruct((b, hh, LANES, t), BF)
    return pl.pallas_call(
        _proj_ab_kernel,
        grid=(b, t // tm),
        in_specs=[
            pl.BlockSpec((1, tm, d), lambda bi, i: (bi, i, 0)),
            _mod_spec(mod),
            _const_spec(g.shape),
            _const_spec(win.shape), _const_spec(qn.shape), _const_spec(kvn.shape),
            _const_spec(wuq.shape), _const_spec(wuk.shape), _const_spec(wuv.shape),
            _const_spec(place.shape),
            pl.BlockSpec((6, tm, LANES), lambda bi, i: (0, i, 0)),
        ],
        out_specs=[tok(MLA_HEADS), tok(MLA_HEADS), tok_t(MLA_HEADS),
                   tok(DIFF_HEADS), tok(DIFF_HEADS), tok_t(DIFF_HEADS)],
        out_shape=[shp(MLA_HEADS), shp(MLA_HEADS), shp_t(MLA_HEADS),
                   shp(DIFF_HEADS), shp(DIFF_HEADS), shp_t(DIFF_HEADS)],
        compiler_params=_cparams("arbitrary", "arbitrary"),
        name="proj_ab",
    )(x, mod, g, win, qn, kvn, wuq, wuk, wuv, place, tabs)


KEY_CHUNK = 512


SUBLANES = 8
ATTN_QUERY_LANES = 512


def _key_chunks(kv_refs):
    off = 0
    for k_ref, v_ref in kv_refs:
        lk = k_ref.shape[2]
        for c0 in range(0, lk, KEY_CHUNK):
            w = min(KEY_CHUNK, lk - c0)
            yield k_ref, v_ref, c0, w, off
            off += w


def _attn_kernel(*refs, nseg, mode, lam_init):
    q_ref = refs[0]
    kv_refs = [(refs[1 + 2 * s], refs[2 + 2 * s]) for s in range(nseg)]
    o_ref, s0_ref, s1_ref, mm0_ref, mm1_ref = refs[-5:]
    nmap, _, tq = s0_ref.shape
    maps = range(nmap)
    fold = lambda a: a.reshape(a.shape[0] // SUBLANES, SUBLANES, tq)
    t = pl.program_id(0)

    @pl.when(t == 0)
    def _():
        s1_ref[...] = jnp.zeros_like(s1_ref)
        mm1_ref[...] = jnp.zeros_like(mm1_ref)

    def step(s_cur, mm_cur, s_prv, mm_prv):
        q = q_ref[0, 0]
        if mode == "single":
            qms = [q]
        else:
            lane = lax.broadcasted_iota(jnp.int32, q.shape, 1)
            zero = jnp.zeros_like(q)
            qms = [jnp.where(lane < DIFF_HD, q, zero), jnp.where(lane >= DIFF_HD, q, zero)]
        m_prv = [jnp.max(mm_prv[j], axis=0, keepdims=True) for j in maps]
        mms = [jnp.full((SUBLANES, tq), -jnp.inf, F32)] * nmap
        lls = [jnp.zeros((SUBLANES, tq), F32)] * nmap
        o_ts = [jnp.zeros((LANES, tq), F32)] * nmap
        for k_ref, v_ref, c0, w, off in _key_chunks(kv_refs):
            kc = k_ref[0, 0, c0:c0 + w, :]
            vc = v_ref[0, 0, :, c0:c0 + w]
            for j in maps:
                s = lax.dot_general(kc, qms[j], _NT, preferred_element_type=F32)
                s_cur[j, off:off + w, :] = s
                mms[j] = jnp.maximum(mms[j], jnp.max(fold(s), axis=0))
            for j in maps:
                p = jnp.exp2(s_prv[j, off:off + w, :] - m_prv[j])
                lls[j] = lls[j] + jnp.sum(fold(p), axis=0)
                o_ts[j] = o_ts[j] + _dot(vc, p.astype(BF))
        for j in maps:
            mm_cur[j] = mms[j]
        outs = [(o_ts[j] / jnp.sum(lls[j], axis=0, keepdims=True)).T for j in maps]
        if mode == "single":
            o_ref[0, 0] = outs[0].astype(BF)
        elif mode == "pair":
            o_ref[0, 0] = jnp.where(lane < DIFF_HD, outs[0], outs[1]).astype(BF)
        else:
            lam_ref, subln_ref = refs[1 + 2 * nseg], refs[2 + 2 * nseg]
            lv = lam_ref[...]
            lam = (jnp.exp(jnp.sum(lv[0:1] * lv[1:2], axis=-1, keepdims=True))
                   - jnp.exp(jnp.sum(lv[2:3] * lv[3:4], axis=-1, keepdims=True)) + lam_init)
            o_ref[0, 0] = (_rms(outs[0] - lam * outs[1], subln_ref[...]) * (1 - lam_init)).astype(BF)

    pl.when(t % 2 == 0)(lambda: step(s0_ref, mm0_ref, s1_ref, mm1_ref))
    pl.when(t % 2 == 1)(lambda: step(s1_ref, mm1_ref, s0_ref, mm0_ref))


def _attention(q, kvs, tq, mode="single", lam_vecs=None, subln=None, lam_init=0.0):
    b, nh, lq, _ = q.shape
    diff = mode == "diff"
    nmap = 1 if mode == "single" else 2
    nq = lq // tq
    ntile = b * nh * nq

    def tile(t):
        t = jnp.clip(t, 0, ntile - 1)
        return t // (nh * nq), (t // nq) % nh, t % nq

    cur_head = lambda t: tile(t)[:2] + (0, 0)
    prv_head = lambda t: tile(t - 1)[:2] + (0, 0)
    in_specs = [pl.BlockSpec((1, 1, tq, LANES), lambda t: tile(t) + (0,))]
    args = [q]
    for k, v in kvs:
        lk = k.shape[2]
        in_specs += [pl.BlockSpec((1, 1, lk, LANES), cur_head), pl.BlockSpec((1, 1, LANES, lk), prv_head)]
        args += [k, v]
    if diff:
        in_specs += [_const_spec(lam_vecs.shape), _const_spec(subln.shape)]
        args += [lam_vecs, subln]
    nkeys = sum(k.shape[2] for k, _ in kvs)
    return pl.pallas_call(
        functools.partial(_attn_kernel, nseg=len(kvs), mode=mode, lam_init=lam_init),
        grid=(ntile + 1,),
        in_specs=in_specs,
        out_specs=pl.BlockSpec((1, 1, tq, LANES), lambda t: tile(t - 1) + (0,)),
        out_shape=jax.ShapeDtypeStruct((b, nh, lq, LANES), BF),
        scratch_shapes=([pltpu.VMEM((nmap, nkeys, tq), F32)] * 2 + [pltpu.VMEM((nmap, SUBLANES, tq), F32)] * 2),
        compiler_params=_cparams("arbitrary"),
        name="attn_diff" if diff else "attn_mla",
    )(*args)


def _outproj_kernel(*refs, n_in, has_bias):
    o_refs = refs[:n_in]
    w_ref = refs[n_in]
    pos = n_in + 1
    b_ref = None
    if has_bias:
        b_ref = refs[pos]
        pos += 1
    x_ref, mod_ref, g_ref, out_ref = refs[pos:pos + 4]
    o = jnp.concatenate([r[0, h] for r in o_refs for h in range(r.shape[1])], axis=-1)
    y = _dot(o, w_ref[...])
    if has_bias:
        y = y + b_ref[...]
    out_ref[0] = x_ref[0] + mod_ref[0, 2:3, :] * _rms(y, g_ref[1:2, :])


def _outproj(os_, w, bias, x, mod, g, tm):
    b, t, d = x.shape
    in_specs = [pl.BlockSpec((1, o.shape[1], tm, LANES), lambda bi, i: (bi, 0, i, 0)) for o in os_]
    in_specs.append(_const_spec(w.shape))
    args = list(os_) + [w]
    if bias is not None:
        in_specs.append(_const_spec(bias.shape))
        args.append(bias)
    in_specs += [pl.BlockSpec((1, tm, d), lambda bi, i: (bi, i, 0)), _mod_spec(mod), _const_spec(g.shape)]
    args += [x, mod, g]
    return pl.pallas_call(
        functools.partial(_outproj_kernel, n_in=len(os_), has_bias=bias is not None),
        grid=(b, t // tm),
        in_specs=in_specs,
        out_specs=pl.BlockSpec((1, tm, d), lambda bi, i: (bi, i, 0)),
        out_shape=jax.ShapeDtypeStruct((b, t, d), F32),
        compiler_params=_cparams("arbitrary", "arbitrary"),
        name="outproj",
    )(*args)


MXU_TILE = 256


def _ff_chunks(f):
    tiles = f // MXU_TILE
    if f % MXU_TILE or tiles < 2:
        return [(0, f)]
    first = (tiles + 1) // 2 * MXU_TILE
    return [(0, first), (first, f - first)]


def _swiglu(hn, w1_ref, w3_ref, w2_ref):
    y = None
    for lo, n in _ff_chunks(w1_ref.shape[1]):
        a = _dot(hn, w1_ref[:, lo:lo + n])
        act = (a * jax.nn.sigmoid(a) * _dot(hn, w3_ref[:, lo:lo + n])).astype(BF)
        part = _dot(act, w2_ref[lo:lo + n, :])
        y = part if y is None else y + part
    return y


def _ffn_kernel(x_ref, mod_ref, g_ref, w1_ref, w3_ref, w2_ref, out_ref):
    h = _norm_mod(x_ref[0], g_ref[2:3, :], mod_ref[0, 3:4, :], mod_ref[0, 4:5, :])
    y = _swiglu(h.astype(BF), w1_ref, w3_ref, w2_ref)
    out_ref[0] = x_ref[0] + mod_ref[0, 5:6, :] * _rms(y, g_ref[3:4, :])


def _resident_spec(shape, index_map):
    return pl.BlockSpec(shape, index_map, pipeline_mode=pl.Buffered(1))


def _ffn(x, mod, g, w1, w3, w2, tm):
    b, t, d = x.shape
    return pl.pallas_call(
        _ffn_kernel,
        grid=(b, t // tm),
        in_specs=[
            pl.BlockSpec((1, tm, d), lambda bi, i: (bi, i, 0)),
            _mod_spec(mod),
            _const_spec(g.shape),
            _resident_spec(w1.shape, lambda bi, i: (0, 0)),
            _resident_spec(w3.shape, lambda bi, i: (0, 0)),
            _resident_spec(w2.shape, lambda bi, i: (0, 0)),
        ],
        out_specs=pl.BlockSpec((1, tm, d), lambda bi, i: (bi, i, 0)),
        out_shape=jax.ShapeDtypeStruct((b, t, d), F32),
        compiler_params=_cparams("arbitrary", "arbitrary"),
        name="ffn_dense",
    )(x, mod, g, w1, w3, w2)


MOE_ROW_TILE = 512
PACK_CHUNKS = 4
SC_WINDOW = 128
_HI_MASK = -65536
_LO_MASK = 65535


def _pack_rows(v, out_ref):
    half = v.shape[1] // 2
    vb = v.astype(BF).astype(F32)
    lo = (pltpu.bitcast(vb[:, :half], jnp.int32) >> 16) & _LO_MASK
    hi = pltpu.bitcast(vb[:, half:], jnp.int32) & _HI_MASK
    w = lo | hi
    for c in range(PACK_CHUNKS):
        out_ref[c] = w[:, c * LANES:(c + 1) * LANES]


def _unpack_rows(ref):
    w = jnp.concatenate([ref[c] for c in range(PACK_CHUNKS)], axis=-1)
    lo = pltpu.bitcast(w << 16, F32)
    hi = pltpu.bitcast(w & _HI_MASK, F32)
    return jnp.concatenate([lo, hi], axis=-1)


def _router_kernel(x_ref, mod_ref, g_ref, router_ref, tri_ref, hp_ref, meta_ref, gate_ref, cnt_ref, carry_ref):
    @pl.when((pl.program_id(0) == 0) & (pl.program_id(1) == 0))
    def _():
        carry_ref[...] = jnp.zeros_like(carry_ref)

    h = _norm_mod(x_ref[0], g_ref[2:3, :], mod_ref[0, 3:4, :], mod_ref[0, 4:5, :])
    _pack_rows(h, hp_ref)
    logits = jnp.dot(h, router_ref[...], preferred_element_type=F32, precision=lax.Precision.HIGHEST)
    lane = lax.broadcasted_iota(jnp.int32, logits.shape, 1)
    logits = jnp.where(lane < N_EXPERTS, logits, -jnp.inf)
    m1 = jnp.max(logits, axis=-1, keepdims=True)
    i1 = jnp.min(jnp.where(logits == m1, lane, LANES), axis=-1, keepdims=True)
    rest = jnp.where(lane == i1, -jnp.inf, logits)
    m2 = jnp.max(rest, axis=-1, keepdims=True)
    i2 = jnp.min(jnp.where(rest == m2, lane, LANES), axis=-1, keepdims=True)
    e2 = jnp.exp(m2 - m1)
    denom = 1.0 + e2
    assigned = jnp.where((lane == i1) | (lane == i2), 1.0, 0.0)
    ranks = _dot(tri_ref[...], assigned.astype(BF)) + carry_ref[...]
    r1 = jnp.sum(jnp.where(lane == i1, ranks, 0.0), axis=-1, keepdims=True).astype(jnp.int32)
    r2 = jnp.sum(jnp.where(lane == i2, ranks, 0.0), axis=-1, keepdims=True).astype(jnp.int32)
    carry_ref[...] += jnp.sum(assigned, axis=0, keepdims=True)
    cnt_ref[...] = carry_ref[...]
    col = lax.broadcasted_iota(jnp.int32, meta_ref.shape, 1)
    meta_ref[...] = jnp.where(col == 0, i1, jnp.where(col == 1, i2, jnp.where(col == 2, r1, r2)))
    gate_ref[...] = jnp.where(col == 0, 1.0 / denom, e2 / denom)


def _router(x, mod, g, router, tm):
    b, t, d = x.shape
    n = b * t
    nt = t // tm
    tri = jnp.asarray(np.tril(np.ones((tm, tm), np.float32), -1), BF)
    row_spec = lambda w, dt: (pl.BlockSpec((tm, w), lambda bi, i: (bi * nt + i, 0)), jax.ShapeDtypeStruct((n, w), dt))
    meta_spec, meta_shape = row_spec(8, jnp.int32)
    gate_spec, gate_shape = row_spec(8, F32)
    return pl.pallas_call(
        _router_kernel,
        grid=(b, nt),
        in_specs=[
            pl.BlockSpec((1, tm, d), lambda bi, i: (bi, i, 0)),
            _mod_spec(mod),
            _const_spec(g.shape),
            _const_spec(router.shape),
            _const_spec(tri.shape),
        ],
        out_specs=[
            pl.BlockSpec((PACK_CHUNKS, tm, LANES), lambda bi, i: (0, bi * nt + i, 0)),
            meta_spec, gate_spec,
            pl.BlockSpec((1, LANES), lambda bi, i: (0, 0)),
        ],
        out_shape=[jax.ShapeDtypeStruct((PACK_CHUNKS, n, LANES), jnp.int32), meta_shape, gate_shape,
                   jax.ShapeDtypeStruct((1, LANES), F32)],
        scratch_shapes=[pltpu.VMEM((1, LANES), F32)],
        compiler_params=_cparams("arbitrary", "arbitrary"),
        name="moe_router",
    )(x, mod, g, router, tri)


def _sc_mesh():
    return plsc.VectorSubcoreMesh(core_axis_name="core", subcore_axis_name="subcore")


def _sc_scatter_rows(rows, idx_a, idx_b, n_out):
    nrows = rows.shape[0]

    @pl.kernel(out_type=jax.ShapeDtypeStruct((n_out, LANES), rows.dtype), mesh=_sc_mesh(), scratch_types=[])
    def scatter_kernel(x_hbm, ia_hbm, ib_hbm, o_hbm):
        def body(x_vmem, ia_vmem, ib_vmem):
            pltpu.sync_copy(x_vmem, o_hbm.at[ia_vmem.at[0]])
            pltpu.sync_copy(x_vmem, o_hbm.at[ib_vmem.at[0]])

        pltpu.emit_pipeline(
            body,
            grid=(nrows // SC_WINDOW,),
            in_specs=[pl.BlockSpec((SC_WINDOW, LANES), lambda i: (i, 0)),
                      pl.BlockSpec((1, SC_WINDOW), lambda i: (0, i)),
                      pl.BlockSpec((1, SC_WINDOW), lambda i: (0, i))],
            out_specs=[],
            core_axis_name=("core", "subcore"),
            dimension_semantics=(pltpu.PARALLEL,),
        )(x_hbm, ia_hbm, ib_hbm)

    return scatter_kernel(rows, idx_a, idx_b)


def _sc_gather_rows(table, idx_a, idx_b):
    nrows = idx_a.shape[1]
    out = jax.ShapeDtypeStruct((nrows, LANES), table.dtype)

    @pl.kernel(out_type=(out, out), mesh=_sc_mesh(), scratch_types=[])
    def gather_kernel(t_hbm, ia_hbm, ib_hbm, oa_hbm, ob_hbm):
        def body(ia_vmem, ib_vmem, oa_vmem, ob_vmem):
            pltpu.sync_copy(t_hbm.at[ia_vmem.at[0]], oa_vmem)
            pltpu.sync_copy(t_hbm.at[ib_vmem.at[0]], ob_vmem)

        pltpu.emit_pipeline(
            body,
            grid=(nrows // SC_WINDOW,),
            in_specs=[pl.BlockSpec((1, SC_WINDOW), lambda i: (0, i)),
                      pl.BlockSpec((1, SC_WINDOW), lambda i: (0, i))],
            out_specs=[pl.BlockSpec((SC_WINDOW, LANES), lambda i: (i, 0)),
                       pl.BlockSpec((SC_WINDOW, LANES), lambda i: (i, 0))],
            core_axis_name=("core", "subcore"),
            dimension_semantics=(pltpu.PARALLEL,),
        )(ia_hbm, ib_hbm, oa_hbm, ob_hbm)

    return gather_kernel(table, idx_a, idx_b)


def _experts_kernel(te_ref, tv_ref, xs_ref, w1_ref, w3_ref, w2_ref, ys_ref):
    del te_ref
    valid = tv_ref[pl.program_id(0)]

    @pl.when(valid > 0)
    def _():
        row = lax.broadcasted_iota(jnp.int32, (xs_ref.shape[1], 1), 0)
        hn = jnp.where(row < valid, _unpack_rows(xs_ref), 0.0).astype(BF)
        _pack_rows(_swiglu(hn, w1_ref.at[0], w3_ref.at[0], w2_ref.at[0]), ys_ref)

    @pl.when(valid == 0)
    def _():
        ys_ref[...] = jnp.zeros_like(ys_ref)


def _experts(xs, tile_expert, tile_valid, w1, w3, w2):
    _, p, _ = xs.shape
    tr = MOE_ROW_TILE
    rows = pl.BlockSpec((PACK_CHUNKS, tr, LANES), lambda j, te, tv: (0, j, 0))
    weights = lambda w: _resident_spec((1,) + w.shape[1:], lambda j, te, tv: (te[j], 0, 0))
    return pl.pallas_call(
        _experts_kernel,
        grid_spec=pltpu.PrefetchScalarGridSpec(
            num_scalar_prefetch=2,
            grid=(p // tr,),
            in_specs=[rows, weights(w1), weights(w3), weights(w2)],
            out_specs=rows,
        ),
        out_shape=jax.ShapeDtypeStruct(xs.shape, jnp.int32),
        compiler_params=_cparams("arbitrary"),
        name="moe_experts",
    )(tile_expert, tile_valid, xs, w1, w3, w2)


def _combine_kernel(x_ref, ya_ref, yb_ref, gate_ref, mod_ref, g_ref, out_ref):
    gates = gate_ref[...]
    fx = gates[:, 0:1] * _unpack_rows(ya_ref) + gates[:, 1:2] * _unpack_rows(yb_ref)
    out_ref[0] = x_ref[0] + mod_ref[0, 5:6, :] * _rms(fx, g_ref[3:4, :])


def _combine(x, ya, yb, gates, mod, g, tm):
    b, t, d = x.shape
    nt = t // tm
    packed = pl.BlockSpec((PACK_CHUNKS, tm, LANES), lambda bi, i: (0, bi * nt + i, 0))
    return pl.pallas_call(
        _combine_kernel,
        grid=(b, nt),
        in_specs=[
            pl.BlockSpec((1, tm, d), lambda bi, i: (bi, i, 0)),
            packed, packed,
            pl.BlockSpec((tm, gates.shape[1]), lambda bi, i: (bi * nt + i, 0)),
            _mod_spec(mod),
            _const_spec(g.shape),
        ],
        out_specs=pl.BlockSpec((1, tm, d), lambda bi, i: (bi, i, 0)),
        out_shape=jax.ShapeDtypeStruct((b, t, d), F32),
        compiler_params=_cparams("arbitrary", "arbitrary"),
        name="moe_combine",
    )(x, ya, yb, gates, mod, g)


def _moe(x, mod, g, router, w1, w3, w2, tm):
    b, t, d = x.shape
    n = b * t
    ne = w1.shape[0]
    tr = MOE_ROW_TILE
    assert d == 2 * PACK_CHUNKS * LANES and (PACK_CHUNKS * n) % (SC_WINDOW * 32) == 0
    hp, meta, gates, counts = _router(x, mod, g, router, tm)

    ntile = 2 * n // tr + ne
    p = ntile * tr
    cnt = counts[0, :ne].astype(jnp.int32)
    tiles = (cnt + tr - 1) // tr
    tile_end = jnp.cumsum(tiles)
    tile_start = tile_end - tiles
    base = tile_start * tr
    eids = jnp.arange(ne, dtype=jnp.int32)
    base_of = lambda e: jnp.sum(jnp.where(e[:, None] == eids[None, :], base[None, :], 0), axis=-1)
    pos_a = base_of(meta[:, 0]) + meta[:, 2]
    pos_b = base_of(meta[:, 1]) + meta[:, 3]
    chunk = jnp.arange(PACK_CHUNKS, dtype=jnp.int32)[:, None] * p
    idx_a = (chunk + pos_a[None, :]).reshape(1, PACK_CHUNKS * n)
    idx_b = (chunk + pos_b[None, :]).reshape(1, PACK_CHUNKS * n)
    tj = jnp.arange(ntile, dtype=jnp.int32)
    tile_expert = jnp.minimum(jnp.sum(tj[:, None] >= tile_end[None, :], axis=-1), ne - 1).astype(jnp.int32)
    done = jnp.sum(jnp.where(tile_expert[:, None] == eids[None, :], tile_start[None, :], 0), axis=-1)
    left = jnp.sum(jnp.where(tile_expert[:, None] == eids[None, :], cnt[None, :], 0), axis=-1) - (tj - done) * tr
    tile_valid = jnp.where(tj < tile_end[-1], jnp.clip(left, 0, tr), 0).astype(jnp.int32)

    xs = _sc_scatter_rows(hp.reshape(PACK_CHUNKS * n, LANES), idx_a, idx_b, PACK_CHUNKS * p)
    ys = _experts(xs.reshape(PACK_CHUNKS, p, LANES), tile_expert, tile_valid, w1, w3, w2)
    ya, yb = _sc_gather_rows(ys.reshape(PACK_CHUNKS * p, LANES), idx_a, idx_b)
    shp = (PACK_CHUNKS, n, LANES)
    return _combine(x, ya.reshape(shp), yb.reshape(shp), gates, mod, g, tm)


def _proj_na_kernel(x_ref, mod_ref, g_ref, w_ref, b_ref, q_ref, k_ref, v_ref):
    h = _norm_mod(x_ref[0], g_ref[0:1, :], mod_ref[0, 0:1, :], mod_ref[0, 1:2, :])
    z = _dot(h.astype(BF), w_ref[...]) + b_ref[...]
    n = NA_SLOTS * LANES
    for s in range(NA_SLOTS):
        lo = s * LANES
        q_ref[0, s] = (z[:, lo:lo + LANES] * NA_SCALE).astype(BF)
        k_ref[0, s] = z[:, n + lo:n + lo + LANES].astype(BF)
        v_ref[0, s] = z[:, 2 * n + lo:2 * n + lo + LANES].astype(BF)


def _proj_na(x, mod, g, w, bias, tm):
    b, t, d = x.shape
    tok = pl.BlockSpec((1, NA_SLOTS, tm, LANES), lambda bi, i: (bi, 0, i, 0))
    shp = jax.ShapeDtypeStruct((b, NA_SLOTS, t, LANES), BF)
    return pl.pallas_call(
        _proj_na_kernel,
        grid=(b, t // tm),
        in_specs=[
            pl.BlockSpec((1, tm, d), lambda bi, i: (bi, i, 0)),
            _mod_spec(mod),
            _const_spec(g.shape),
            _const_spec(w.shape),
            _const_spec(bias.shape),
        ],
        out_specs=[tok, tok, tok],
        out_shape=[shp, shp, shp],
        compiler_params=_cparams("arbitrary", "arbitrary"),
        name="proj_na",
    )(x, mod, g, w, bias)


NA_ROWS_PER_STEP = 2


def _na_window_start(r, rows):
    return jnp.clip(r - NA_KH // 2, 0, rows - NA_KH)


def _na_kernel(q_ref, kx_ref, vx_ref, kc_ref, vc_ref, *rest, rows):
    bias_refs, o_ref = rest[:NA_ROWS_PER_STEP], rest[NA_ROWS_PER_STEP]
    nwin = NA_KH * GRID_W
    lane = lax.broadcasted_iota(jnp.int32, (GRID_W, LANES), 1)
    work = [(a, s) for a in range(NA_ROWS_PER_STEP) for s in range(NA_SLOTS)]
    idx = range(len(work))
    starts = [pl.multiple_of(_na_window_start(pl.program_id(1) * NA_ROWS_PER_STEP + a, rows) * GRID_W, GRID_W)
              for a in range(NA_ROWS_PER_STEP)]
    sws, scs = [], []
    for a, s in work:
        q = q_ref[0, s, a * GRID_W:(a + 1) * GRID_W, :]
        zero = jnp.zeros_like(q)
        q2 = jnp.concatenate([jnp.where(lane < NA_HD, q, zero), jnp.where(lane >= NA_HD, q, zero)], axis=0)
        kw = kx_ref[0, s, pl.ds(starts[a], nwin), :]
        sws.append(lax.dot_general(q2, kw, _NT, preferred_element_type=F32) + bias_refs[a][0, s])
        scs.append(lax.dot_general(q2, kc_ref[0, s], _NT, preferred_element_type=F32))
    ms = [jnp.maximum(jnp.max(sws[i], axis=-1, keepdims=True), jnp.max(scs[i], axis=-1, keepdims=True))
          for i in idx]
    pws = [jnp.exp2(sws[i] - ms[i]) for i in idx]
    pcs = [jnp.exp2(scs[i] - ms[i]) for i in idx]
    ls = [jnp.sum(pws[i], axis=-1, keepdims=True) + jnp.sum(pcs[i], axis=-1, keepdims=True) for i in idx]
    for i, (a, s) in enumerate(work):
        vw = vx_ref[0, s, pl.ds(starts[a], nwin), :]
        o2 = (_dot(pws[i].astype(BF), vw) + _dot(pcs[i].astype(BF), vc_ref[0, s])) / ls[i]
        o_ref[0, s, a * GRID_W:(a + 1) * GRID_W, :] = jnp.where(lane < NA_HD, o2[:GRID_W], o2[GRID_W:]).astype(BF)


def _na_attention(q, kx, vx, kc, vc, bias):
    b, ns, l, _ = q.shape
    rows = l // GRID_W
    lc = kc.shape[2]
    nr = NA_ROWS_PER_STEP
    assert rows % nr == 0

    def bias_spec(a):
        def index_map(bi, i):
            r = i * nr + a
            return (r - _na_window_start(r, rows), 0, 0, 0)
        return pl.BlockSpec((1,) + bias.shape[1:], index_map)

    full = lambda n: pl.BlockSpec((1, ns, n, LANES), lambda bi, i: (bi, 0, 0, 0))
    row = pl.BlockSpec((1, ns, nr * GRID_W, LANES), lambda bi, i: (bi, 0, i, 0))
    return pl.pallas_call(
        functools.partial(_na_kernel, rows=rows),
        grid=(b, rows // nr),
        in_specs=[row, full(l), full(l), full(lc), full(lc)] + [bias_spec(a) for a in range(nr)],
        out_specs=row,
        out_shape=jax.ShapeDtypeStruct((b, ns, l, LANES), BF),
        compiler_params=_cparams("arbitrary", "arbitrary"),
        name="na_attn",
    )(q, kx, vx, kc, vc, *([bias] * nr))


def _axis_tables(pos, dim):
    inv = ROPE_THETA ** (-jnp.arange(0, dim, 2, dtype=F32) / dim)
    ang = pos.astype(F32)[:, None] * inv[None, :]
    ang = jnp.concatenate([ang, ang], axis=-1)
    return jnp.cos(ang), jnp.sin(ang)


def _rope_tables(n, rope):
    if rope:
        t = jnp.arange(n, dtype=jnp.int32)
        row, col = t // GRID_W, t % GRID_W

        def cs(d):
            cr, sr = _axis_tables(row, d // 2)
            cc, sc = _axis_tables(col, d // 2)
            return jnp.concatenate([cr, cc], axis=-1), jnp.concatenate([sr, sc], axis=-1)

        c32, s32 = cs(MLA_ROPE)
        c64, s64 = cs(DIFF_HD)
    else:
        c32, s32 = jnp.ones((n, MLA_ROPE), F32), jnp.zeros((n, MLA_ROPE), F32)
        c64, s64 = jnp.ones((n, DIFF_HD), F32), jnp.zeros((n, DIFF_HD), F32)
    ones = jnp.ones((n, MLA_NOPE), F32)
    pad = lambda a: jnp.pad(a, ((0, 0), (0, LANES - a.shape[1])))
    cos_q = MLA_SCALE * pad(jnp.concatenate([ones, c32], axis=-1))
    sin_q = MLA_SCALE * pad(jnp.concatenate([jnp.zeros_like(ones), s32], axis=-1))
    return jnp.stack([cos_q, sin_q, pad(c32), pad(s32),
                      jnp.concatenate([c64, c64], axis=-1), jnp.concatenate([s64, s64], axis=-1)])


def _rot_cols(w, d):
    shp = w.shape
    w5 = w.reshape(shp[:-1] + (-1, 2, 2, d // 4))
    r = jnp.concatenate([-w5[..., 1:2, :], w5[..., 0:1, :]], axis=-2)
    return r.reshape(shp)


def _pad_cols(w, n):
    return jnp.pad(w, ((0, 0), (0, n - w.shape[1])))


def _prep_ab_weights(w_in, q_norm, kv_norm, w_uq, w_ukv, w_out):
    o_cq, o_ckv, o_kr = MLA_Q_RANK, MLA_Q_RANK + MLA_KV_RANK, MLA_Q_RANK + MLA_KV_RANK + MLA_ROPE
    o_dq, o_dk = o_kr + _DW, o_kr + 2 * _DW
    w_kr, w_dq, w_dk, w_dv = w_in[:, o_ckv:o_kr], w_in[:, o_kr:o_dq], w_in[:, o_dq:o_dk], w_in[:, o_dk:]
    win = jnp.concatenate([
        w_in[:, :o_ckv], _pad_cols(w_kr, LANES), _pad_cols(_rot_cols(w_kr, MLA_ROPE), LANES),
        w_dq, _rot_cols(w_dq, DIFF_HD), w_dk, _rot_cols(w_dk, DIFF_HD), w_dv], axis=1).astype(BF)
    rq = w_uq.shape[0]
    uq = w_uq.reshape(rq, MLA_HEADS, MLA_NOPE + MLA_ROPE)
    nope, rope = uq[..., :MLA_NOPE], uq[..., MLA_NOPE:]
    zpad = jnp.zeros((rq, MLA_HEADS, LANES - MLA_NOPE - MLA_ROPE), F32)
    main = jnp.concatenate([nope, rope, zpad], axis=-1).reshape(rq, MLA_HEADS * LANES)
    rot = jnp.concatenate([jnp.zeros_like(nope), _rot_cols(rope, MLA_ROPE), zpad], axis=-1)
    wuq = jnp.concatenate([main, rot.reshape(rq, MLA_HEADS * LANES)], axis=1).astype(BF)
    rkv = w_ukv.shape[0]
    ukv = w_ukv.reshape(rkv, MLA_HEADS, MLA_NOPE + MLA_V)
    slot = lambda a: jnp.pad(a, ((0, 0), (0, 0), (0, LANES - a.shape[-1]))).reshape(rkv, MLA_HEADS * LANES)
    wuk = slot(ukv[..., :MLA_NOPE]).astype(BF)
    wuv = slot(ukv[..., MLA_NOPE:]).astype(BF)
    place = np.zeros((LANES, MLA_HEADS * LANES), np.float32)
    for hd in range(MLA_HEADS):
        place[np.arange(MLA_ROPE), hd * LANES + MLA_NOPE + np.arange(MLA_ROPE)] = 1.0
    d = w_out.shape[1]
    wo_a = w_out[:MLA_HEADS * MLA_V].reshape(MLA_HEADS, MLA_V, d)
    wo_a = jnp.pad(wo_a, ((0, 0), (0, LANES - MLA_V), (0, 0))).reshape(MLA_HEADS * LANES, d)
    wo = jnp.concatenate([wo_a, w_out[MLA_HEADS * MLA_V:]], axis=0).astype(BF)
    proj = (win, q_norm.reshape(1, -1), kv_norm.reshape(1, -1), wuq, wuk, wuv, jnp.asarray(place, BF))
    return proj, wo


def _na_bias_table(rpb):
    nh = rpb.shape[0]
    cols = np.arange(GRID_W)
    col_start = np.clip(cols - NA_KW // 2, 0, GRID_W - NA_KW)
    kcol = np.arange(GRID_W)
    valid = (kcol[None, :] >= col_start[:, None]) & (kcol[None, :] < col_start[:, None] + NA_KW)
    dc = kcol[None, :] - cols[:, None] + (NA_KW - 1)
    onehot = (dc[None] == np.arange(2 * NA_KW - 1)[:, None, None]) & valid[None]
    tz = jnp.einsum('hrd,dck->hrck', rpb * LOG2E, jnp.asarray(onehot, F32), precision=lax.Precision.HIGHEST)
    tz = jnp.where(valid[None, None], tz, MASK_VALUE)
    cases = [tz[:, NA_KH - 1 - c:2 * NA_KH - 1 - c] for c in range(NA_KH)]
    tbl = jnp.transpose(jnp.stack(cases), (0, 1, 3, 2, 4))
    return tbl.reshape(NA_KH, nh // 2, 2 * GRID_W, NA_KH * GRID_W)


def _tile(n, pref):
    return pref if n % pref == 0 else n


def kernel(x, c, ctx, c_ctx, w_mod, b_mod, norm_g, a_w_in, a_q_norm, a_kv_norm, a_w_uq, a_w_ukv, b_lambda, b_subln,
           ab_w_out, f_w1, f_w3, f_w2, c_w_qkv, c_b_qkv, c_rpb, c_w_out, c_b_out, m_router, m_w1, m_w3, m_w2):
    b, l, d = x.shape
    lc = ctx.shape[1]
    depth = w_mod.shape[0]
    assert l % GRID_W == 0 and l // GRID_W >= NA_KH

    mod_rows = 16
    cvec = jnp.zeros((mod_rows, d), F32).at[:b].set(c).at[b].set(c_ctx)
    mod = _modulation(cvec, w_mod, b_mod)

    tm_x, tm_c = _tile(l, 512), _tile(lc, 256)
    tq_x, tq_c = _tile(l, ATTN_QUERY_LANES), _tile(lc, ATTN_QUERY_LANES // 2)
    tq2_x = _tile(l, ATTN_QUERY_LANES // 2)
    cs = ctx
    for i in range(depth):
        last = i == depth - 1
        j = i // 2
        mx = mod[i, :b].reshape(b, N_MOD, d)
        mc = mod[i, b].reshape(1, N_MOD, d)
        g = norm_g[i]
        if i % 2 == 0:
            lam_init = 0.8 - 0.6 * math.exp(-0.3 * i)
            proj_w, wo = _prep_ab_weights(a_w_in[j], a_q_norm[j], a_kv_norm[j], a_w_uq[j], a_w_ukv[j], ab_w_out[j])
            qx, kx, vx, dqx, dkx, dvx = _proj_ab(x, mx, g, proj_w, _rope_tables(l, True), tm_x)
            qc, kc, vc, dqc, dkc, dvc = _proj_ab(cs, mc, g, proj_w, _rope_tables(lc, False), tm_c)
            subln = b_subln[j].reshape(1, -1)
            oa = _attention(qx, [(kx, vx), (kc, vc)], tq_x)
            ob = _attention(dqx, [(dkx, dvx), (dkc, dvc)], tq2_x, "diff", b_lambda[j], subln, lam_init)
            x = _outproj([oa, ob], wo, None, x, mx, g, tm_x)
            if not last:
                oa = _attention(qc, [(kc, vc)], tq_c)
                ob = _attention(dqc, [(dkc, dvc)], tq_c, "diff", b_lambda[j], subln, lam_init)
                cs = _outproj([oa, ob], wo, None, cs, mc, g, tm_c)
            w1, w3, w2 = f_w1[j].astype(BF), f_w3[j].astype(BF), f_w2[j].astype(BF)
            x = _ffn(x, mx, g, w1, w3, w2, tm_x)
            if not last:
                cs = _ffn(cs, mc, g, w1, w3, w2, tm_c)
        else:
            wqkv = c_w_qkv[j].astype(BF)
            bqkv = c_b_qkv[j].reshape(1, -1)
            wo = c_w_out[j].astype(BF)
            bo = c_b_out[j].reshape(1, -1)
            qx, kx, vx = _proj_na(x, mx, g, wqkv, bqkv, tm_x)
            qc, kc, vc = _proj_na(cs, mc, g, wqkv, bqkv, tm_c)
            o = _na_attention(qx, kx, vx, kc, vc, _na_bias_table(c_rpb[j]))
            x = _outproj([o], wo, bo, x, mx, g, tm_x)
            router = _pad_cols(m_router[j], LANES)
            w1, w3, w2 = m_w1[j].astype(BF), m_w3[j].astype(BF), m_w2[j].astype(BF)
            if not last:
                oc = _attention(qc, [(kc, jnp.swapaxes(vc, 2, 3))], tq_c, "pair")
                cs = _outproj([oc], wo, bo, cs, mc, g, tm_c)
            x = _moe(x, mx, g, router, w1, w3, w2, tm_x)
            if not last:
                cs = _moe(cs, mc, g, router, w1, w3, w2, tm_c)
    return x
```

```python
import functools
import math

import jax
import jax.numpy as jnp
import numpy as np
from jax import lax
from jax.experimental import pallas as pl
from jax.experimental.pallas import tpu as pltpu
from jax.experimental.pallas import tpu_sc as plsc

BF = jnp.bfloat16
F32 = jnp.float32

LANES = 128
VMEM_LIMIT = 56 * 1024 * 1024

GRID_W = 64
EPS = 1e-6
ROPE_THETA = 10000.0
N_MOD = 6

MLA_HEADS = 8
MLA_NOPE = 64
MLA_ROPE = 32
MLA_V = 64
MLA_Q_RANK = 384
MLA_KV_RANK = 256
LOG2E = math.log2(math.e)
MLA_SCALE = (MLA_NOPE + MLA_ROPE) ** -0.5 * LOG2E

DIFF_HEADS = 4
DIFF_HD = 64
DIFF_SCALE = DIFF_HD ** -0.5 * LOG2E

NA_HEADS = 16
NA_HD = 64
NA_KH = 8
NA_KW = 16
NA_SCALE = NA_HD ** -0.5 * LOG2E
NA_SLOTS = NA_HEADS * NA_HD // LANES
MASK_VALUE = -1e30

N_EXPERTS = 8

_NT = (((1,), (1,)), ((), ()))


def _cparams(*sem):
    return pltpu.CompilerParams(dimension_semantics=sem, vmem_limit_bytes=VMEM_LIMIT)


def _dot(a, b):
    return jnp.dot(a, b, preferred_element_type=F32)


def _rms(x, g):
    return x * lax.rsqrt(jnp.mean(x * x, axis=-1, keepdims=True) + EPS) * g


def _norm_mod(x, g, shift, scale):
    return _rms(x, g) * (1 + scale) + shift


def _const_spec(shape):
    return pl.BlockSpec(shape, lambda *_: (0,) * len(shape))


def _mod_spec(mod):
    if mod.shape[0] == 1:
        return pl.BlockSpec((1,) + mod.shape[1:], lambda b, *_: (0, 0, 0))
    return pl.BlockSpec((1,) + mod.shape[1:], lambda b, *_: (b, 0, 0))


def _mod_kernel(c_ref, w_ref, b_ref, o_ref):
    c = c_ref[...]
    sc = c * jax.nn.sigmoid(c)
    o_ref[0] = _dot(sc.astype(BF), w_ref[0].astype(BF)) + b_ref[0]


def _modulation(cvec, w_mod, b_mod):
    depth, d, n = w_mod.shape
    rows = cvec.shape[0]
    return pl.pallas_call(
        _mod_kernel,
        grid=(depth, n // d),
        in_specs=[
            pl.BlockSpec((rows, d), lambda i, j: (0, 0)),
            pl.BlockSpec((1, d, d), lambda i, j: (i, 0, j)),
            pl.BlockSpec((1, 1, d), lambda i, j: (i, 0, j)),
        ],
        out_specs=pl.BlockSpec((1, rows, d), lambda i, j: (i, 0, j)),
        out_shape=jax.ShapeDtypeStruct((depth, rows, n), F32),
        compiler_params=_cparams("arbitrary", "arbitrary"),
        name="modulation",
    )(cvec, w_mod, b_mod.reshape(depth, 1, n))


_Z_CQ = 0
_Z_CKV = MLA_Q_RANK
_Z_KR = _Z_CKV + MLA_KV_RANK
_Z_KR_ROT = _Z_KR + LANES
_Z_DQ = _Z_KR_ROT + LANES
_DW = 2 * DIFF_HEADS * DIFF_HD
_Z_DQ_ROT = _Z_DQ + _DW
_Z_DK = _Z_DQ_ROT + _DW
_Z_DK_ROT = _Z_DK + _DW
_Z_DV = _Z_DK_ROT + _DW
_Z_END = _Z_DV + _DW


def _proj_ab_kernel(x_ref, mod_ref, g_ref, win_ref, qn_ref, kvn_ref, wuq_ref, wuk_ref, wuv_ref,
                    place_ref, tab_ref, q_ref, k_ref, v_ref, dq_ref, dk_ref, dv_ref):
    h = _norm_mod(x_ref[0], g_ref[0:1, :], mod_ref[0, 0:1, :], mod_ref[0, 1:2, :])
    z = _dot(h.astype(BF), win_ref[...])
    cqn = _rms(z[:, _Z_CQ:_Z_CKV], qn_ref[...]).astype(BF)
    ckvn = _rms(z[:, _Z_CKV:_Z_KR], kvn_ref[...]).astype(BF)
    q2 = _dot(cqn, wuq_ref[...])
    nq = MLA_HEADS * LANES
    cos_q, sin_q = tab_ref[0], tab_ref[1]
    for hd in range(MLA_HEADS):
        lo = hd * LANES
        q_ref[0, hd] = (q2[:, lo:lo + LANES] * cos_q + q2[:, nq + lo:nq + lo + LANES] * sin_q).astype(BF)
    kr = (z[:, _Z_KR:_Z_KR_ROT] * tab_ref[2] + z[:, _Z_KR_ROT:_Z_DQ] * tab_ref[3]).astype(BF)
    kk = _dot(ckvn, wuk_ref[...]) + _dot(kr, place_ref[...])
    vv = _dot(ckvn, wuv_ref[...])
    for hd in range(MLA_HEADS):
        lo = hd * LANES
        k_ref[0, hd] = kk[:, lo:lo + LANES].astype(BF)
        v_ref[0, hd] = vv[:, lo:lo + LANES].T.astype(BF)
    cos_d, sin_d = tab_ref[4], tab_ref[5]
    for hd in range(DIFF_HEADS):
        lo = hd * LANES
        dq = z[:, _Z_DQ + lo:_Z_DQ + lo + LANES] * cos_d + z[:, _Z_DQ_ROT + lo:_Z_DQ_ROT + lo + LANES] * sin_d
        dq_ref[0, hd] = (dq * DIFF_SCALE).astype(BF)
        dk = z[:, _Z_DK + lo:_Z_DK + lo + LANES] * cos_d + z[:, _Z_DK_ROT + lo:_Z_DK_ROT + lo + LANES] * sin_d
        dk_ref[0, hd] = dk.astype(BF)
        dv_ref[0, hd] = z[:, _Z_DV + lo:_Z_DV + lo + LANES].T.astype(BF)


def _proj_ab(x, mod, g, wts, tabs, tm):
    b, t, d = x.shape
    win, qn, kvn, wuq, wuk, wuv, place = wts
    tok = lambda hh: pl.BlockSpec((1, hh, tm, LANES), lambda bi, i: (bi, 0, i, 0))
    shp = lambda hh: jax.ShapeDtypeStruct((b, hh, t, LANES), BF)
    tok_t = lambda hh: pl.BlockSpec((1, hh, LANES, tm), lambda bi, i: (bi, 0, 0, i))
    shp_t = lambda hh: jax.ShapeDtypeStruct((b, hh, LANES, t), BF)
    return pl.pallas_call(
        _proj_ab_kernel,
        grid=(b, t // tm),
        in_specs=[
            pl.BlockSpec((1, tm, d), lambda bi, i: (bi, i, 0)),
            _mod_spec(mod),
            _const_spec(g.shape),
            _const_spec(win.shape), _const_spec(qn.shape), _const_spec(kvn.shape),
            _const_spec(wuq.shape), _const_spec(wuk.shape), _const_spec(wuv.shape),
            _const_spec(place.shape),
            pl.BlockSpec((6, tm, LANES), lambda bi, i: (0, i, 0)),
        ],
        out_specs=[tok(MLA_HEADS), tok(MLA_HEADS), tok_t(MLA_HEADS),
                   tok(DIFF_HEADS), tok(DIFF_HEADS), tok_t(DIFF_HEADS)],
        out_shape=[shp(MLA_HEADS), shp(MLA_HEADS), shp_t(MLA_HEADS),
                   shp(DIFF_HEADS), shp(DIFF_HEADS), shp_t(DIFF_HEADS)],
        compiler_params=_cparams("arbitrary", "arbitrary"),
        name="proj_ab",
    )(x, mod, g, win, qn, kvn, wuq, wuk, wuv, place, tabs)


KEY_CHUNK = 512


SUBLANES = 8
ATTN_QUERY_LANES = 512


def _key_chunks(kv_refs):
    off = 0
    for k_ref, v_ref in kv_refs:
        lk = k_ref.shape[2]
        for c0 in range(0, lk, KEY_CHUNK):
            w = min(KEY_CHUNK, lk - c0)
            yield k_ref, v_ref, c0, w, off
            off += w


def _attn_kernel(*refs, nseg, mode, lam_init):
    q_ref = refs[0]
    kv_refs = [(refs[1 + 2 * s], refs[2 + 2 * s]) for s in range(nseg)]
    o_ref, s0_ref, s1_ref, mm0_ref, mm1_ref = refs[-5:]
    nmap, _, tq = s0_ref.shape
    maps = range(nmap)
    fold = lambda a: a.reshape(a.shape[0] // SUBLANES, SUBLANES, tq)
    t = pl.program_id(0)

    @pl.when(t == 0)
    def _():
        s1_ref[...] = jnp.zeros_like(s1_ref)
        mm1_ref[...] = jnp.zeros_like(mm1_ref)

    def step(s_cur, mm_cur, s_prv, mm_prv):
        q = q_ref[0, 0]
        if mode == "single":
            qms = [q]
        else:
            lane = lax.broadcasted_iota(jnp.int32, q.shape, 1)
            zero = jnp.zeros_like(q)
            qms = [jnp.where(lane < DIFF_HD, q, zero), jnp.where(lane >= DIFF_HD, q, zero)]
        m_prv = [jnp.max(mm_prv[j], axis=0, keepdims=True) for j in maps]
        mms = [jnp.full((SUBLANES, tq), -jnp.inf, F32)] * nmap
        lls = [jnp.zeros((SUBLANES, tq), F32)] * nmap
        o_ts = [jnp.zeros((LANES, tq), F32)] * nmap
        for k_ref, v_ref, c0, w, off in _key_chunks(kv_refs):
            kc = k_ref[0, 0, c0:c0 + w, :]
            vc = v_ref[0, 0, :, c0:c0 + w]
            for j in maps:
                s = lax.dot_general(kc, qms[j], _NT, preferred_element_type=F32)
                s_cur[j, off:off + w, :] = s
                mms[j] = jnp.maximum(mms[j], jnp.max(fold(s), axis=0))
            for j in maps:
                p = jnp.exp2(s_prv[j, off:off + w, :] - m_prv[j])
                lls[j] = lls[j] + jnp.sum(fold(p), axis=0)
                o_ts[j] = o_ts[j] + _dot(vc, p.astype(BF))
        for j in maps:
            mm_cur[j] = mms[j]
        outs = [(o_ts[j] / jnp.sum(lls[j], axis=0, keepdims=True)).T for j in maps]
        if mode == "single":
            o_ref[0, 0] = outs[0].astype(BF)
        elif mode == "pair":
            o_ref[0, 0] = jnp.where(lane < DIFF_HD, outs[0], outs[1]).astype(BF)
        else:
            lam_ref, subln_ref = refs[1 + 2 * nseg], refs[2 + 2 * nseg]
            lv = lam_ref[...]
            lam = (jnp.exp(jnp.sum(lv[0:1] * lv[1:2], axis=-1, keepdims=True))
                   - jnp.exp(jnp.sum(lv[2:3] * lv[3:4], axis=-1, keepdims=True)) + lam_init)
            o_ref[0, 0] = (_rms(outs[0] - lam * outs[1], subln_ref[...]) * (1 - lam_init)).astype(BF)

    pl.when(t % 2 == 0)(lambda: step(s0_ref, mm0_ref, s1_ref, mm1_ref))
    pl.when(t % 2 == 1)(lambda: step(s1_ref, mm1_ref, s0_ref, mm0_ref))


def _attention(q, kvs, tq, mode="single", lam_vecs=None, subln=None, lam_init=0.0):
    b, nh, lq, _ = q.shape
    diff = mode == "diff"
    nmap = 1 if mode == "single" else 2
    nq = lq // tq
    ntile = b * nh * nq

    def tile(t):
        t = jnp.clip(t, 0, ntile - 1)
        return t // (nh * nq), (t // nq) % nh, t % nq

    cur_head = lambda t: tile(t)[:2] + (0, 0)
    prv_head = lambda t: tile(t - 1)[:2] + (0, 0)
    in_specs = [pl.BlockSpec((1, 1, tq, LANES), lambda t: tile(t) + (0,))]
    args = [q]
    for k, v in kvs:
        lk = k.shape[2]
        in_specs += [pl.BlockSpec((1, 1, lk, LANES), cur_head), pl.BlockSpec((1, 1, LANES, lk), prv_head)]
        args += [k, v]
    if diff:
        in_specs += [_const_spec(lam_vecs.shape), _const_spec(subln.shape)]
        args += [lam_vecs, subln]
    nkeys = sum(k.shape[2] for k, _ in kvs)
    return pl.pallas_call(
        functools.partial(_attn_kernel, nseg=len(kvs), mode=mode, lam_init=lam_init),
        grid=(ntile + 1,),
        in_specs=in_specs,
        out_specs=pl.BlockSpec((1, 1, tq, LANES), lambda t: tile(t - 1) + (0,)),
        out_shape=jax.ShapeDtypeStruct((b, nh, lq, LANES), BF),
        scratch_shapes=([pltpu.VMEM((nmap, nkeys, tq), F32)] * 2 + [pltpu.VMEM((nmap, SUBLANES, tq), F32)] * 2),
        compiler_params=_cparams("arbitrary"),
        name="attn_diff" if diff else "attn_mla",
    )(*args)


def _outproj_kernel(*refs, n_in, has_bias, tail):
    o_refs = refs[:n_in]
    w_ref = refs[n_in]
    pos = n_in + 1
    b_ref = None
    if has_bias:
        b_ref = refs[pos]
        pos += 1
    x_ref, mod_ref, g_ref = refs[pos:pos + 3]
    rest = refs[pos + 3:]
    o = jnp.concatenate([r[0, h] for r in o_refs for h in range(r.shape[1])], axis=-1)
    y = _dot(o, w_ref[...])
    if has_bias:
        y = y + b_ref[...]
    x1 = x_ref[0] + mod_ref[0, 2:3, :] * _rms(y, g_ref[1:2, :])
    h = _norm_mod(x1, g_ref[2:3, :], mod_ref[0, 3:4, :], mod_ref[0, 4:5, :])
    if tail == "ffn":
        w1_ref, w3_ref, w2_ref, out_ref = rest
        out_ref[0] = x1 + mod_ref[0, 5:6, :] * _rms(_swiglu(h.astype(BF), w1_ref, w3_ref, w2_ref), g_ref[3:4, :])
    else:
        router_ref, tri_ref, out_ref = rest[:3]
        out_ref[0] = x1
        first = (pl.program_id(0) == 0) & (pl.program_id(1) == 0)
        _route(h, first, router_ref, tri_ref, *rest[3:])


def _outproj(os_, w, bias, x, mod, g, tm, tail, tail_args):
    b, t, d = x.shape
    n = b * t
    nt = t // tm
    in_specs = [pl.BlockSpec((1, o.shape[1], tm, LANES), lambda bi, i: (bi, 0, i, 0)) for o in os_]
    in_specs.append(_const_spec(w.shape))
    args = list(os_) + [w]
    if bias is not None:
        in_specs.append(_const_spec(bias.shape))
        args.append(bias)
    x_spec = pl.BlockSpec((1, tm, d), lambda bi, i: (bi, i, 0))
    in_specs += [x_spec, _mod_spec(mod), _const_spec(g.shape)]
    args += [x, mod, g]
    out_specs, out_shape, scratch = x_spec, jax.ShapeDtypeStruct((b, t, d), F32), []
    if tail == "ffn":
        in_specs += [_resident_spec(a.shape, lambda bi, i: (0, 0)) for a in tail_args]
        args += list(tail_args)
    else:
        router, = tail_args
        tri = jnp.asarray(np.tril(np.ones((tm, tm), np.float32), -1), BF)
        in_specs += [_const_spec(router.shape), _const_spec(tri.shape)]
        args += [router, tri]
        rows = lambda wd: pl.BlockSpec((tm, wd), lambda bi, i: (bi * nt + i, 0))
        out_specs = [x_spec, pl.BlockSpec((PACK_CHUNKS, tm, LANES), lambda bi, i: (0, bi * nt + i, 0)),
                     rows(8), rows(8), pl.BlockSpec((1, LANES), lambda bi, i: (0, 0))]
        out_shape = [out_shape, jax.ShapeDtypeStruct((PACK_CHUNKS, n, LANES), jnp.int32),
                     jax.ShapeDtypeStruct((n, 8), jnp.int32), jax.ShapeDtypeStruct((n, 8), F32),
                     jax.ShapeDtypeStruct((1, LANES), F32)]
        scratch = [pltpu.VMEM((1, LANES), F32)]
    return pl.pallas_call(
        functools.partial(_outproj_kernel, n_in=len(os_), has_bias=bias is not None, tail=tail),
        grid=(b, nt),
        in_specs=in_specs,
        out_specs=out_specs,
        out_shape=out_shape,
        scratch_shapes=scratch,
        compiler_params=_cparams("arbitrary", "arbitrary"),
        name="outproj_" + tail,
    )(*args)


MXU_TILE = 256


def _ff_chunks(f):
    tiles = f // MXU_TILE
    if f % MXU_TILE or tiles < 2:
        return [(0, f)]
    first = (tiles + 1) // 2 * MXU_TILE
    return [(0, first), (first, f - first)]


def _swiglu(hn, w1_ref, w3_ref, w2_ref):
    y = None
    for lo, n in _ff_chunks(w1_ref.shape[1]):
        a = _dot(hn, w1_ref[:, lo:lo + n])
        act = (a * jax.nn.sigmoid(a) * _dot(hn, w3_ref[:, lo:lo + n])).astype(BF)
        part = _dot(act, w2_ref[lo:lo + n, :])
        y = part if y is None else y + part
    return y


def _resident_spec(shape, index_map):
    return pl.BlockSpec(shape, index_map, pipeline_mode=pl.Buffered(1))


MOE_ROW_TILE = 512
PACK_CHUNKS = 4
SC_WINDOW = 128
_HI_MASK = -65536
_LO_MASK = 65535


def _pack_rows(v, out_ref):
    half = v.shape[1] // 2
    vb = v.astype(BF).astype(F32)
    lo = (pltpu.bitcast(vb[:, :half], jnp.int32) >> 16) & _LO_MASK
    hi = pltpu.bitcast(vb[:, half:], jnp.int32) & _HI_MASK
    w = lo | hi
    for c in range(PACK_CHUNKS):
        out_ref[c] = w[:, c * LANES:(c + 1) * LANES]


def _unpack_rows(ref):
    w = jnp.concatenate([ref[c] for c in range(PACK_CHUNKS)], axis=-1)
    lo = pltpu.bitcast(w << 16, F32)
    hi = pltpu.bitcast(w & _HI_MASK, F32)
    return jnp.concatenate([lo, hi], axis=-1)


def _route(h, first, router_ref, tri_ref, hp_ref, meta_ref, gate_ref, cnt_ref, carry_ref):
    @pl.when(first)
    def _():
        carry_ref[...] = jnp.zeros_like(carry_ref)

    _pack_rows(h, hp_ref)
    h_hi = h.astype(BF)
    h_lo = (h - h_hi.astype(F32)).astype(BF)
    logits = _dot(h_hi, router_ref[0]) + (_dot(h_hi, router_ref[1]) + _dot(h_lo, router_ref[0]))
    lane = lax.broadcasted_iota(jnp.int32, logits.shape, 1)
    logits = jnp.where(lane < N_EXPERTS, logits, -jnp.inf)
    m1 = jnp.max(logits, axis=-1, keepdims=True)
    i1 = jnp.min(jnp.where(logits == m1, lane, LANES), axis=-1, keepdims=True)
    rest = jnp.where(lane == i1, -jnp.inf, logits)
    m2 = jnp.max(rest, axis=-1, keepdims=True)
    i2 = jnp.min(jnp.where(rest == m2, lane, LANES), axis=-1, keepdims=True)
    e2 = jnp.exp(m2 - m1)
    denom = 1.0 + e2
    assigned = jnp.where((lane == i1) | (lane == i2), 1.0, 0.0)
    ranks = _dot(tri_ref[...], assigned.astype(BF)) + carry_ref[...]
    r1 = jnp.sum(jnp.where(lane == i1, ranks, 0.0), axis=-1, keepdims=True).astype(jnp.int32)
    r2 = jnp.sum(jnp.where(lane == i2, ranks, 0.0), axis=-1, keepdims=True).astype(jnp.int32)
    carry_ref[...] += jnp.sum(assigned, axis=0, keepdims=True)
    cnt_ref[...] = carry_ref[...]
    col = lax.broadcasted_iota(jnp.int32, meta_ref.shape, 1)
    meta_ref[...] = jnp.where(col == 0, i1, jnp.where(col == 1, i2, jnp.where(col == 2, r1, r2)))
    gate_ref[...] = jnp.where(col == 0, 1.0 / denom, e2 / denom)


def _sc_mesh():
    return plsc.VectorSubcoreMesh(core_axis_name="core", subcore_axis_name="subcore")


def _sc_scatter_rows(rows, idx_a, idx_b, n_out):
    nrows = rows.shape[0]

    @pl.kernel(out_type=jax.ShapeDtypeStruct((n_out, LANES), rows.dtype), mesh=_sc_mesh(), scratch_types=[])
    def scatter_kernel(x_hbm, ia_hbm, ib_hbm, o_hbm):
        def body(x_vmem, ia_vmem, ib_vmem):
            pltpu.sync_copy(x_vmem, o_hbm.at[ia_vmem.at[0]])
            pltpu.sync_copy(x_vmem, o_hbm.at[ib_vmem.at[0]])

        pltpu.emit_pipeline(
            body,
            grid=(nrows // SC_WINDOW,),
            in_specs=[pl.BlockSpec((SC_WINDOW, LANES), lambda i: (i, 0)),
                      pl.BlockSpec((1, SC_WINDOW), lambda i: (0, i)),
                      pl.BlockSpec((1, SC_WINDOW), lambda i: (0, i))],
            out_specs=[],
            core_axis_name=("core", "subcore"),
            dimension_semantics=(pltpu.PARALLEL,),
        )(x_hbm, ia_hbm, ib_hbm)

    return scatter_kernel(rows, idx_a, idx_b)


def _sc_gather_rows(table, idx_a, idx_b):
    nrows = idx_a.shape[1]
    out = jax.ShapeDtypeStruct((nrows, LANES), table.dtype)

    @pl.kernel(out_type=(out, out), mesh=_sc_mesh(), scratch_types=[])
    def gather_kernel(t_hbm, ia_hbm, ib_hbm, oa_hbm, ob_hbm):
        def body(ia_vmem, ib_vmem, oa_vmem, ob_vmem):
            pltpu.sync_copy(t_hbm.at[ia_vmem.at[0]], oa_vmem)
            pltpu.sync_copy(t_hbm.at[ib_vmem.at[0]], ob_vmem)

        pltpu.emit_pipeline(
            body,
            grid=(nrows // SC_WINDOW,),
            in_specs=[pl.BlockSpec((1, SC_WINDOW), lambda i: (0, i)),
                      pl.BlockSpec((1, SC_WINDOW), lambda i: (0, i))],
            out_specs=[pl.BlockSpec((SC_WINDOW, LANES), lambda i: (i, 0)),
                       pl.BlockSpec((SC_WINDOW, LANES), lambda i: (i, 0))],
            core_axis_name=("core", "subcore"),
            dimension_semantics=(pltpu.PARALLEL,),
        )(ia_hbm, ib_hbm, oa_hbm, ob_hbm)

    return gather_kernel(table, idx_a, idx_b)


def _experts_kernel(te_ref, tv_ref, xs_ref, w1_ref, w3_ref, w2_ref, ys_ref):
    del te_ref
    valid = tv_ref[pl.program_id(0)]

    @pl.when(valid > 0)
    def _():
        row = lax.broadcasted_iota(jnp.int32, (xs_ref.shape[1], 1), 0)
        hn = jnp.where(row < valid, _unpack_rows(xs_ref), 0.0).astype(BF)
        _pack_rows(_swiglu(hn, w1_ref.at[0], w3_ref.at[0], w2_ref.at[0]), ys_ref)

    @pl.when(valid == 0)
    def _():
        ys_ref[...] = jnp.zeros_like(ys_ref)


def _experts(xs, tile_expert, tile_valid, w1, w3, w2):
    _, p, _ = xs.shape
    tr = MOE_ROW_TILE
    rows = pl.BlockSpec((PACK_CHUNKS, tr, LANES), lambda j, te, tv: (0, j, 0))
    weights = lambda w: _resident_spec((1,) + w.shape[1:], lambda j, te, tv: (te[j], 0, 0))
    return pl.pallas_call(
        _experts_kernel,
        grid_spec=pltpu.PrefetchScalarGridSpec(
            num_scalar_prefetch=2,
            grid=(p // tr,),
            in_specs=[rows, weights(w1), weights(w3), weights(w2)],
            out_specs=rows,
        ),
        out_shape=jax.ShapeDtypeStruct(xs.shape, jnp.int32),
        compiler_params=_cparams("arbitrary"),
        name="moe_experts",
    )(tile_expert, tile_valid, xs, w1, w3, w2)


def _combine_kernel(x_ref, ya_ref, yb_ref, gate_ref, mod_ref, g_ref, out_ref):
    gates = gate_ref[...]
    fx = gates[:, 0:1] * _unpack_rows(ya_ref) + gates[:, 1:2] * _unpack_rows(yb_ref)
    out_ref[0] = x_ref[0] + mod_ref[0, 5:6, :] * _rms(fx, g_ref[3:4, :])


def _combine(x, ya, yb, gates, mod, g, tm):
    b, t, d = x.shape
    nt = t // tm
    packed = pl.BlockSpec((PACK_CHUNKS, tm, LANES), lambda bi, i: (0, bi * nt + i, 0))
    return pl.pallas_call(
        _combine_kernel,
        grid=(b, nt),
        in_specs=[
            pl.BlockSpec((1, tm, d), lambda bi, i: (bi, i, 0)),
            packed, packed,
            pl.BlockSpec((tm, gates.shape[1]), lambda bi, i: (bi * nt + i, 0)),
            _mod_spec(mod),
            _const_spec(g.shape),
        ],
        out_specs=pl.BlockSpec((1, tm, d), lambda bi, i: (bi, i, 0)),
        out_shape=jax.ShapeDtypeStruct((b, t, d), F32),
        compiler_params=_cparams("arbitrary", "arbitrary"),
        name="moe_combine",
    )(x, ya, yb, gates, mod, g)


def _moe(x, routed, mod, g, w1, w3, w2, tm):
    b, t, d = x.shape
    n = b * t
    ne = w1.shape[0]
    tr = MOE_ROW_TILE
    assert d == 2 * PACK_CHUNKS * LANES and (PACK_CHUNKS * n) % (SC_WINDOW * 32) == 0
    hp, meta, gates, counts = routed

    ntile = 2 * n // tr + ne
    p = ntile * tr
    cnt = counts[0, :ne].astype(jnp.int32)
    tiles = (cnt + tr - 1) // tr
    tile_end = jnp.cumsum(tiles)
    tile_start = tile_end - tiles
    base = tile_start * tr
    eids = jnp.arange(ne, dtype=jnp.int32)
    base_of = lambda e: jnp.sum(jnp.where(e[:, None] == eids[None, :], base[None, :], 0), axis=-1)
    pos_a = base_of(meta[:, 0]) + meta[:, 2]
    pos_b = base_of(meta[:, 1]) + meta[:, 3]
    chunk = jnp.arange(PACK_CHUNKS, dtype=jnp.int32)[:, None] * p
    idx_a = (chunk + pos_a[None, :]).reshape(1, PACK_CHUNKS * n)
    idx_b = (chunk + pos_b[None, :]).reshape(1, PACK_CHUNKS * n)
    tj = jnp.arange(ntile, dtype=jnp.int32)
    tile_expert = jnp.minimum(jnp.sum(tj[:, None] >= tile_end[None, :], axis=-1), ne - 1).astype(jnp.int32)
    done = jnp.sum(jnp.where(tile_expert[:, None] == eids[None, :], tile_start[None, :], 0), axis=-1)
    left = jnp.sum(jnp.where(tile_expert[:, None] == eids[None, :], cnt[None, :], 0), axis=-1) - (tj - done) * tr
    tile_valid = jnp.where(tj < tile_end[-1], jnp.clip(left, 0, tr), 0).astype(jnp.int32)

    xs = _sc_scatter_rows(hp.reshape(PACK_CHUNKS * n, LANES), idx_a, idx_b, PACK_CHUNKS * p)
    ys = _experts(xs.reshape(PACK_CHUNKS, p, LANES), tile_expert, tile_valid, w1, w3, w2)
    ya, yb = _sc_gather_rows(ys.reshape(PACK_CHUNKS * p, LANES), idx_a, idx_b)
    shp = (PACK_CHUNKS, n, LANES)
    return _combine(x, ya.reshape(shp), yb.reshape(shp), gates, mod, g, tm)


def _proj_na_kernel(x_ref, mod_ref, g_ref, w_ref, b_ref, q_ref, k_ref, v_ref):
    h = _norm_mod(x_ref[0], g_ref[0:1, :], mod_ref[0, 0:1, :], mod_ref[0, 1:2, :])
    z = _dot(h.astype(BF), w_ref[...]) + b_ref[...]
    n = NA_SLOTS * LANES
    for s in range(NA_SLOTS):
        lo = s * LANES
        q_ref[0, s] = (z[:, lo:lo + LANES] * NA_SCALE).astype(BF)
        k_ref[0, s] = z[:, n + lo:n + lo + LANES].astype(BF)
        v_ref[0, s] = z[:, 2 * n + lo:2 * n + lo + LANES].astype(BF)


def _proj_na(x, mod, g, w, bias, tm):
    b, t, d = x.shape
    tok = pl.BlockSpec((1, NA_SLOTS, tm, LANES), lambda bi, i: (bi, 0, i, 0))
    shp = jax.ShapeDtypeStruct((b, NA_SLOTS, t, LANES), BF)
    return pl.pallas_call(
        _proj_na_kernel,
        grid=(b, t // tm),
        in_specs=[
            pl.BlockSpec((1, tm, d), lambda bi, i: (bi, i, 0)),
            _mod_spec(mod),
            _const_spec(g.shape),
            _const_spec(w.shape),
            _const_spec(bias.shape),
        ],
        out_specs=[tok, tok, tok],
        out_shape=[shp, shp, shp],
        compiler_params=_cparams("arbitrary", "arbitrary"),
        name="proj_na",
    )(x, mod, g, w, bias)


NA_ROWS_PER_STEP = 2


def _na_window_start(r, rows):
    return jnp.clip(r - NA_KH // 2, 0, rows - NA_KH)


def _na_kernel(q_ref, kx_ref, vx_ref, kc_ref, vc_ref, *rest, rows):
    bias_refs, o_ref = rest[:NA_ROWS_PER_STEP], rest[NA_ROWS_PER_STEP]
    nwin = NA_KH * GRID_W
    lane = lax.broadcasted_iota(jnp.int32, (GRID_W, LANES), 1)
    work = [(a, s) for a in range(NA_ROWS_PER_STEP) for s in range(NA_SLOTS)]
    idx = range(len(work))
    starts = [pl.multiple_of(_na_window_start(pl.program_id(1) * NA_ROWS_PER_STEP + a, rows) * GRID_W, GRID_W)
              for a in range(NA_ROWS_PER_STEP)]
    sws, scs = [], []
    for a, s in work:
        q = q_ref[0, s, a * GRID_W:(a + 1) * GRID_W, :]
        zero = jnp.zeros_like(q)
        q2 = jnp.concatenate([jnp.where(lane < NA_HD, q, zero), jnp.where(lane >= NA_HD, q, zero)], axis=0)
        kw = kx_ref[0, s, pl.ds(starts[a], nwin), :]
        sws.append(lax.dot_general(q2, kw, _NT, preferred_element_type=F32) + bias_refs[a][0, s])
        scs.append(lax.dot_general(q2, kc_ref[0, s], _NT, preferred_element_type=F32))
    ms = [jnp.maximum(jnp.max(sws[i], axis=-1, keepdims=True), jnp.max(scs[i], axis=-1, keepdims=True))
          for i in idx]
    pws = [jnp.exp2(sws[i] - ms[i]) for i in idx]
    pcs = [jnp.exp2(scs[i] - ms[i]) for i in idx]
    ls = [jnp.sum(pws[i], axis=-1, keepdims=True) + jnp.sum(pcs[i], axis=-1, keepdims=True) for i in idx]
    for i, (a, s) in enumerate(work):
        vw = vx_ref[0, s, pl.ds(starts[a], nwin), :]
        o2 = (_dot(pws[i].astype(BF), vw) + _dot(pcs[i].astype(BF), vc_ref[0, s])) / ls[i]
        o_ref[0, s, a * GRID_W:(a + 1) * GRID_W, :] = jnp.where(lane < NA_HD, o2[:GRID_W], o2[GRID_W:]).astype(BF)


def _na_attention(q, kx, vx, kc, vc, bias):
    b, ns, l, _ = q.shape
    rows = l // GRID_W
    lc = kc.shape[2]
    nr = NA_ROWS_PER_STEP
    assert rows % nr == 0

    def bias_spec(a):
        def index_map(bi, i):
            r = i * nr + a
            return (r - _na_window_start(r, rows), 0, 0, 0)
        return pl.BlockSpec((1,) + bias.shape[1:], index_map)

    full = lambda n: pl.BlockSpec((1, ns, n, LANES), lambda bi, i: (bi, 0, 0, 0))
    row = pl.BlockSpec((1, ns, nr * GRID_W, LANES), lambda bi, i: (bi, 0, i, 0))
    return pl.pallas_call(
        functools.partial(_na_kernel, rows=rows),
        grid=(b, rows // nr),
        in_specs=[row, full(l), full(l), full(lc), full(lc)] + [bias_spec(a) for a in range(nr)],
        out_specs=row,
        out_shape=jax.ShapeDtypeStruct((b, ns, l, LANES), BF),
        compiler_params=_cparams("arbitrary", "arbitrary"),
        name="na_attn",
    )(q, kx, vx, kc, vc, *([bias] * nr))


def _axis_tables(pos, dim):
    inv = ROPE_THETA ** (-jnp.arange(0, dim, 2, dtype=F32) / dim)
    ang = pos.astype(F32)[:, None] * inv[None, :]
    ang = jnp.concatenate([ang, ang], axis=-1)
    return jnp.cos(ang), jnp.sin(ang)


def _rope_tables(n, rope):
    if rope:
        t = jnp.arange(n, dtype=jnp.int32)
        row, col = t // GRID_W, t % GRID_W

        def cs(d):
            cr, sr = _axis_tables(row, d // 2)
            cc, sc = _axis_tables(col, d // 2)
            return jnp.concatenate([cr, cc], axis=-1), jnp.concatenate([sr, sc], axis=-1)

        c32, s32 = cs(MLA_ROPE)
        c64, s64 = cs(DIFF_HD)
    else:
        c32, s32 = jnp.ones((n, MLA_ROPE), F32), jnp.zeros((n, MLA_ROPE), F32)
        c64, s64 = jnp.ones((n, DIFF_HD), F32), jnp.zeros((n, DIFF_HD), F32)
    ones = jnp.ones((n, MLA_NOPE), F32)
    pad = lambda a: jnp.pad(a, ((0, 0), (0, LANES - a.shape[1])))
    cos_q = MLA_SCALE * pad(jnp.concatenate([ones, c32], axis=-1))
    sin_q = MLA_SCALE * pad(jnp.concatenate([jnp.zeros_like(ones), s32], axis=-1))
    return jnp.stack([cos_q, sin_q, pad(c32), pad(s32),
                      jnp.concatenate([c64, c64], axis=-1), jnp.concatenate([s64, s64], axis=-1)])


def _rot_cols(w, d):
    shp = w.shape
    w5 = w.reshape(shp[:-1] + (-1, 2, 2, d // 4))
    r = jnp.concatenate([-w5[..., 1:2, :], w5[..., 0:1, :]], axis=-2)
    return r.reshape(shp)


def _pad_cols(w, n):
    return jnp.pad(w, ((0, 0), (0, n - w.shape[1])))


def _prep_ab_weights(w_in, q_norm, kv_norm, w_uq, w_ukv, w_out):
    o_cq, o_ckv, o_kr = MLA_Q_RANK, MLA_Q_RANK + MLA_KV_RANK, MLA_Q_RANK + MLA_KV_RANK + MLA_ROPE
    o_dq, o_dk = o_kr + _DW, o_kr + 2 * _DW
    w_kr, w_dq, w_dk, w_dv = w_in[:, o_ckv:o_kr], w_in[:, o_kr:o_dq], w_in[:, o_dq:o_dk], w_in[:, o_dk:]
    win = jnp.concatenate([
        w_in[:, :o_ckv], _pad_cols(w_kr, LANES), _pad_cols(_rot_cols(w_kr, MLA_ROPE), LANES),
        w_dq, _rot_cols(w_dq, DIFF_HD), w_dk, _rot_cols(w_dk, DIFF_HD), w_dv], axis=1).astype(BF)
    rq = w_uq.shape[0]
    uq = w_uq.reshape(rq, MLA_HEADS, MLA_NOPE + MLA_ROPE)
    nope, rope = uq[..., :MLA_NOPE], uq[..., MLA_NOPE:]
    zpad = jnp.zeros((rq, MLA_HEADS, LANES - MLA_NOPE - MLA_ROPE), F32)
    main = jnp.concatenate([nope, rope, zpad], axis=-1).reshape(rq, MLA_HEADS * LANES)
    rot = jnp.concatenate([jnp.zeros_like(nope), _rot_cols(rope, MLA_ROPE), zpad], axis=-1)
    wuq = jnp.concatenate([main, rot.reshape(rq, MLA_HEADS * LANES)], axis=1).astype(BF)
    rkv = w_ukv.shape[0]
    ukv = w_ukv.reshape(rkv, MLA_HEADS, MLA_NOPE + MLA_V)
    slot = lambda a: jnp.pad(a, ((0, 0), (0, 0), (0, LANES - a.shape[-1]))).reshape(rkv, MLA_HEADS * LANES)
    wuk = slot(ukv[..., :MLA_NOPE]).astype(BF)
    wuv = slot(ukv[..., MLA_NOPE:]).astype(BF)
    place = np.zeros((LANES, MLA_HEADS * LANES), np.float32)
    for hd in range(MLA_HEADS):
        place[np.arange(MLA_ROPE), hd * LANES + MLA_NOPE + np.arange(MLA_ROPE)] = 1.0
    d = w_out.shape[1]
    wo_a = w_out[:MLA_HEADS * MLA_V].reshape(MLA_HEADS, MLA_V, d)
    wo_a = jnp.pad(wo_a, ((0, 0), (0, LANES - MLA_V), (0, 0))).reshape(MLA_HEADS * LANES, d)
    wo = jnp.concatenate([wo_a, w_out[MLA_HEADS * MLA_V:]], axis=0).astype(BF)
    proj = (win, q_norm.reshape(1, -1), kv_norm.reshape(1, -1), wuq, wuk, wuv, jnp.asarray(place, BF))
    return proj, wo


def _na_bias_table(rpb):
    nh = rpb.shape[0]
    cols = np.arange(GRID_W)
    col_start = np.clip(cols - NA_KW // 2, 0, GRID_W - NA_KW)
    kcol = np.arange(GRID_W)
    valid = (kcol[None, :] >= col_start[:, None]) & (kcol[None, :] < col_start[:, None] + NA_KW)
    dc = kcol[None, :] - cols[:, None] + (NA_KW - 1)
    onehot = (dc[None] == np.arange(2 * NA_KW - 1)[:, None, None]) & valid[None]
    tz = jnp.einsum('hrd,dck->hrck', rpb * LOG2E, jnp.asarray(onehot, F32), precision=lax.Precision.HIGHEST)
    tz = jnp.where(valid[None, None], tz, MASK_VALUE)
    cases = [tz[:, NA_KH - 1 - c:2 * NA_KH - 1 - c] for c in range(NA_KH)]
    tbl = jnp.transpose(jnp.stack(cases), (0, 1, 3, 2, 4))
    return tbl.reshape(NA_KH, nh // 2, 2 * GRID_W, NA_KH * GRID_W)


def _tile(n, pref):
    return pref if n % pref == 0 else n


def kernel(x, c, ctx, c_ctx, w_mod, b_mod, norm_g, a_w_in, a_q_norm, a_kv_norm, a_w_uq, a_w_ukv, b_lambda, b_subln,
           ab_w_out, f_w1, f_w3, f_w2, c_w_qkv, c_b_qkv, c_rpb, c_w_out, c_b_out, m_router, m_w1, m_w3, m_w2):
    b, l, d = x.shape
    lc = ctx.shape[1]
    depth = w_mod.shape[0]
    assert l % GRID_W == 0 and l // GRID_W >= NA_KH

    mod_rows = 16
    cvec = jnp.zeros((mod_rows, d), F32).at[:b].set(c).at[b].set(c_ctx)
    mod = _modulation(cvec, w_mod, b_mod)

    tm_x, tm_c = _tile(l, 512), _tile(lc, 256)
    tq_x, tq_c = _tile(l, ATTN_QUERY_LANES), _tile(lc, ATTN_QUERY_LANES // 2)
    tq2_x = _tile(l, ATTN_QUERY_LANES // 2)
    cs = ctx
    for i in range(depth):
        last = i == depth - 1
        j = i // 2
        mx = mod[i, :b].reshape(b, N_MOD, d)
        mc = mod[i, b].reshape(1, N_MOD, d)
        g = norm_g[i]
        if i % 2 == 0:
            lam_init = 0.8 - 0.6 * math.exp(-0.3 * i)
            proj_w, wo = _prep_ab_weights(a_w_in[j], a_q_norm[j], a_kv_norm[j], a_w_uq[j], a_w_ukv[j], ab_w_out[j])
            qx, kx, vx, dqx, dkx, dvx = _proj_ab(x, mx, g, proj_w, _rope_tables(l, True), tm_x)
            qc, kc, vc, dqc, dkc, dvc = _proj_ab(cs, mc, g, proj_w, _rope_tables(lc, False), tm_c)
            subln = b_subln[j].reshape(1, -1)
            oa = _attention(qx, [(kx, vx), (kc, vc)], tq_x)
            ob = _attention(dqx, [(dkx, dvx), (dkc, dvc)], tq2_x, "diff", b_lambda[j], subln, lam_init)
            ffn_w = (f_w1[j].astype(BF), f_w3[j].astype(BF), f_w2[j].astype(BF))
            x = _outproj([oa, ob], wo, None, x, mx, g, tm_x, "ffn", ffn_w)
            if not last:
                oa = _attention(qc, [(kc, vc)], tq_c)
                ob = _attention(dqc, [(dkc, dvc)], tq_c, "diff", b_lambda[j], subln, lam_init)
                cs = _outproj([oa, ob], wo, None, cs, mc, g, tm_c, "ffn", ffn_w)
        else:
            wqkv = c_w_qkv[j].astype(BF)
            bqkv = c_b_qkv[j].reshape(1, -1)
            wo = c_w_out[j].astype(BF)
            bo = c_b_out[j].reshape(1, -1)
            qx, kx, vx = _proj_na(x, mx, g, wqkv, bqkv, tm_x)
            qc, kc, vc = _proj_na(cs, mc, g, wqkv, bqkv, tm_c)
            o = _na_attention(qx, kx, vx, kc, vc, _na_bias_table(c_rpb[j]))
            r_f32 = _pad_cols(m_router[j], LANES)
            r_hi = r_f32.astype(BF)
            router = (jnp.stack([r_hi, (r_f32 - r_hi.astype(F32)).astype(BF)]),)
            w1, w3, w2 = m_w1[j].astype(BF), m_w3[j].astype(BF), m_w2[j].astype(BF)
            x, *routed = _outproj([o], wo, bo, x, mx, g, tm_x, "router", router)
            x = _moe(x, routed, mx, g, w1, w3, w2, tm_x)
            if not last:
                oc = _attention(qc, [(kc, jnp.swapaxes(vc, 2, 3))], tq_c, "pair")
                cs, *routed = _outproj([oc], wo, bo, cs, mc, g, tm_c, "router", router)
                cs = _moe(cs, routed, mc, g, w1, w3, w2, tm_c)
    return x
```

```python
import functools
import math

import jax
import jax.numpy as jnp
import numpy as np
from jax import lax
from jax.experimental import pallas as pl
from jax.experimental.pallas import tpu as pltpu
from jax.experimental.pallas import tpu_sc as plsc

BF = jnp.bfloat16
F32 = jnp.float32

LANES = 128
VMEM_LIMIT = 56 * 1024 * 1024

GRID_W = 64
EPS = 1e-6
ROPE_THETA = 10000.0
N_MOD = 6

MLA_HEADS = 8
MLA_NOPE = 64
MLA_ROPE = 32
MLA_V = 64
MLA_Q_RANK = 384
MLA_KV_RANK = 256
LOG2E = math.log2(math.e)
MLA_SCALE = (MLA_NOPE + MLA_ROPE) ** -0.5 * LOG2E

DIFF_HEADS = 4
DIFF_HD = 64
DIFF_SCALE = DIFF_HD ** -0.5 * LOG2E

NA_HEADS = 16
NA_HD = 64
NA_KH = 8
NA_KW = 16
NA_SCALE = NA_HD ** -0.5 * LOG2E
NA_SLOTS = NA_HEADS * NA_HD // LANES
MASK_VALUE = -1e30

N_EXPERTS = 8

_NT = (((1,), (1,)), ((), ()))


def _cparams(*sem):
    return pltpu.CompilerParams(dimension_semantics=sem, vmem_limit_bytes=VMEM_LIMIT)


def _dot(a, b):
    return jnp.dot(a, b, preferred_element_type=F32)


def _rms(x, g):
    return x * lax.rsqrt(jnp.mean(x * x, axis=-1, keepdims=True) + EPS) * g


def _norm_mod(x, g, shift, scale):
    return _rms(x, g) * (1 + scale) + shift


def _const_spec(shape):
    return pl.BlockSpec(shape, lambda *_: (0,) * len(shape))


def _mod_spec(mod):
    if mod.shape[0] == 1:
        return pl.BlockSpec((1,) + mod.shape[1:], lambda b, *_: (0, 0, 0))
    return pl.BlockSpec((1,) + mod.shape[1:], lambda b, *_: (b, 0, 0))


def _mod_kernel(c_ref, w_ref, b_ref, o_ref):
    c = c_ref[...]
    sc = c * jax.nn.sigmoid(c)
    o_ref[0] = _dot(sc.astype(BF), w_ref[0].astype(BF)) + b_ref[0]


def _modulation(cvec, w_mod, b_mod):
    depth, d, n = w_mod.shape
    rows = cvec.shape[0]
    return pl.pallas_call(
        _mod_kernel,
        grid=(depth, n // d),
        in_specs=[
            pl.BlockSpec((rows, d), lambda i, j: (0, 0)),
            pl.BlockSpec((1, d, d), lambda i, j: (i, 0, j)),
            pl.BlockSpec((1, 1, d), lambda i, j: (i, 0, j)),
        ],
        out_specs=pl.BlockSpec((1, rows, d), lambda i, j: (i, 0, j)),
        out_shape=jax.ShapeDtypeStruct((depth, rows, n), F32),
        compiler_params=_cparams("arbitrary", "arbitrary"),
        name="modulation",
    )(cvec, w_mod, b_mod.reshape(depth, 1, n))


_Z_CQ = 0
_Z_CKV = MLA_Q_RANK
_Z_KR = _Z_CKV + MLA_KV_RANK
_Z_KR_ROT = _Z_KR + LANES
_Z_DQ = _Z_KR_ROT + LANES
_DW = 2 * DIFF_HEADS * DIFF_HD
_Z_DQ_ROT = _Z_DQ + _DW
_Z_DK = _Z_DQ_ROT + _DW
_Z_DK_ROT = _Z_DK + _DW
_Z_DV = _Z_DK_ROT + _DW
_Z_END = _Z_DV + _DW


def _proj_ab_kernel(x_ref, mod_ref, g_ref, win_ref, qn_ref, kvn_ref, wuq_ref, wuk_ref, wuv_ref,
                    place_ref, tab_ref, q_ref, k_ref, v_ref, dq_ref, dk_ref, dv_ref):
    h = _norm_mod(x_ref[0], g_ref[0:1, :], mod_ref[0, 0:1, :], mod_ref[0, 1:2, :])
    z = _dot(h.astype(BF), win_ref[...])
    cqn = _rms(z[:, _Z_CQ:_Z_CKV], qn_ref[...]).astype(BF)
    ckvn = _rms(z[:, _Z_CKV:_Z_KR], kvn_ref[...]).astype(BF)
    q2 = _dot(cqn, wuq_ref[...])
    nq = MLA_HEADS * LANES
    cos_q, sin_q = tab_ref[0], tab_ref[1]
    for hd in range(MLA_HEADS):
        lo = hd * LANES
        q_ref[0, hd] = (q2[:, lo:lo + LANES] * cos_q + q2[:, nq + lo:nq + lo + LANES] * sin_q).astype(BF)
    kr = (z[:, _Z_KR:_Z_KR_ROT] * tab_ref[2] + z[:, _Z_KR_ROT:_Z_DQ] * tab_ref[3]).astype(BF)
    kk = _dot(ckvn, wuk_ref[...]) + _dot(kr, place_ref[...])
    vv = _dot(ckvn, wuv_ref[...])
    for hd in range(MLA_HEADS):
        lo = hd * LANES
        k_ref[0, hd] = kk[:, lo:lo + LANES].astype(BF)
        v_ref[0, hd] = vv[:, lo:lo + LANES].T.astype(BF)
    cos_d, sin_d = tab_ref[4], tab_ref[5]
    for hd in range(DIFF_HEADS):
        lo = hd * LANES
        dq = z[:, _Z_DQ + lo:_Z_DQ + lo + LANES] * cos_d + z[:, _Z_DQ_ROT + lo:_Z_DQ_ROT + lo + LANES] * sin_d
        dq_ref[0, hd] = (dq * DIFF_SCALE).astype(BF)
        dk = z[:, _Z_DK + lo:_Z_DK + lo + LANES] * cos_d + z[:, _Z_DK_ROT + lo:_Z_DK_ROT + lo + LANES] * sin_d
        dk_ref[0, hd] = dk.astype(BF)
        dv_ref[0, hd] = z[:, _Z_DV + lo:_Z_DV + lo + LANES].T.astype(BF)


def _proj_ab(x, mod, g, wts, tabs, tm):
    b, t, d = x.shape
    win, qn, kvn, wuq, wuk, wuv, place = wts
    tok = lambda hh: pl.BlockSpec((1, hh, tm, LANES), lambda bi, i: (bi, 0, i, 0))
    shp = lambda hh: jax.ShapeDtypeStruct((b, hh, t, LANES), BF)
    tok_t = lambda hh: pl.BlockSpec((1, hh, LANES, tm), lambda bi, i: (bi, 0, 0, i))
    shp_t = lambda hh: jax.ShapeDtypeStruct((b, hh, LANES, t), BF)
    return pl.pallas_call(
        _proj_ab_kernel,
        grid=(b, t // tm),
        in_specs=[
            pl.BlockSpec((1, tm, d), lambda bi, i: (bi, i, 0)),
            _mod_spec(mod),
            _const_spec(g.shape),
            _const_spec(win.shape), _const_spec(qn.shape), _const_spec(kvn.shape),
            _const_spec(wuq.shape), _const_spec(wuk.shape), _const_spec(wuv.shape),
            _const_spec(place.shape),
            pl.BlockSpec((6, tm, LANES), lambda bi, i: (0, i, 0)),
        ],
        out_specs=[tok(MLA_HEADS), tok(MLA_HEADS), tok_t(MLA_HEADS),
                   tok(DIFF_HEADS), tok(DIFF_HEADS), tok_t(DIFF_HEADS)],
        out_shape=[shp(MLA_HEADS), shp(MLA_HEADS), shp_t(MLA_HEADS),
                   shp(DIFF_HEADS), shp(DIFF_HEADS), shp_t(DIFF_HEADS)],
        compiler_params=_cparams("arbitrary", "arbitrary"),
        name="proj_ab",
    )(x, mod, g, win, qn, kvn, wuq, wuk, wuv, place, tabs)


KEY_CHUNK = 256


SUBLANES = 8
ATTN_QUERY_LANES = 512


def _key_chunks(kv_refs):
    off = 0
    for k_ref, v_ref in kv_refs:
        lk = k_ref.shape[2]
        for c0 in range(0, lk, KEY_CHUNK):
            w = min(KEY_CHUNK, lk - c0)
            yield k_ref, v_ref, c0, w, off
            off += w


def _attn_kernel(*refs, nseg, mode, lam_init):
    q_ref = refs[0]
    kv_refs = [(refs[1 + 2 * s], refs[2 + 2 * s]) for s in range(nseg)]
    o_ref, s0_ref, s1_ref, mm0_ref, mm1_ref = refs[-5:]
    nmap, _, tq = s0_ref.shape
    maps = range(nmap)
    fold = lambda a: a.reshape(a.shape[0] // SUBLANES, SUBLANES, tq)
    t = pl.program_id(0)

    @pl.when(t == 0)
    def _():
        s1_ref[...] = jnp.zeros_like(s1_ref)
        mm1_ref[...] = jnp.zeros_like(mm1_ref)

    def step(s_cur, mm_cur, s_prv, mm_prv):
        q = q_ref[0, 0]
        if mode == "single":
            qms = [q]
        else:
            lane = lax.broadcasted_iota(jnp.int32, q.shape, 1)
            zero = jnp.zeros_like(q)
            qms = [jnp.where(lane < DIFF_HD, q, zero), jnp.where(lane >= DIFF_HD, q, zero)]
        m_prv = [jnp.max(mm_prv[j], axis=0, keepdims=True) for j in maps]
        mms = [jnp.full((SUBLANES, tq), -jnp.inf, F32)] * nmap
        lls = [jnp.zeros((SUBLANES, tq), F32)] * nmap
        o_ts = [jnp.zeros((LANES, tq), F32)] * nmap
        for k_ref, v_ref, c0, w, off in _key_chunks(kv_refs):
            kc = k_ref[0, 0, c0:c0 + w, :]
            vc = v_ref[0, 0, :, c0:c0 + w]
            for j in maps:
                s = lax.dot_general(kc, qms[j], _NT, preferred_element_type=F32)
                s_cur[j, off:off + w, :] = s
                mms[j] = jnp.maximum(mms[j], jnp.max(fold(s), axis=0))
            for j in maps:
                p = jnp.exp2(s_prv[j, off:off + w, :] - m_prv[j])
                lls[j] = lls[j] + jnp.sum(fold(p), axis=0)
                o_ts[j] = o_ts[j] + _dot(vc, p.astype(BF))
        for j in maps:
            mm_cur[j] = mms[j]
        outs = [(o_ts[j] / jnp.sum(lls[j], axis=0, keepdims=True)).T for j in maps]
        if mode == "single":
            o_ref[0, 0] = outs[0].astype(BF)
        elif mode == "pair":
            o_ref[0, 0] = jnp.where(lane < DIFF_HD, outs[0], outs[1]).astype(BF)
        else:
            lam_ref, subln_ref = refs[1 + 2 * nseg], refs[2 + 2 * nseg]
            lv = lam_ref[...]
            lam = (jnp.exp(jnp.sum(lv[0:1] * lv[1:2], axis=-1, keepdims=True))
                   - jnp.exp(jnp.sum(lv[2:3] * lv[3:4], axis=-1, keepdims=True)) + lam_init)
            o_ref[0, 0] = (_rms(outs[0] - lam * outs[1], subln_ref[...]) * (1 - lam_init)).astype(BF)

    pl.when(t % 2 == 0)(lambda: step(s0_ref, mm0_ref, s1_ref, mm1_ref))
    pl.when(t % 2 == 1)(lambda: step(s1_ref, mm1_ref, s0_ref, mm0_ref))


def _attention(q, kvs, tq, mode="single", lam_vecs=None, subln=None, lam_init=0.0):
    b, nh, lq, _ = q.shape
    diff = mode == "diff"
    nmap = 1 if mode == "single" else 2
    nq = lq // tq
    ntile = b * nh * nq

    def tile(t):
        t = jnp.clip(t, 0, ntile - 1)
        return t // (nh * nq), (t // nq) % nh, t % nq

    cur_head = lambda t: tile(t)[:2] + (0, 0)
    prv_head = lambda t: tile(t - 1)[:2] + (0, 0)
    in_specs = [pl.BlockSpec((1, 1, tq, LANES), lambda t: tile(t) + (0,))]
    args = [q]
    for k, v in kvs:
        lk = k.shape[2]
        in_specs += [pl.BlockSpec((1, 1, lk, LANES), cur_head), pl.BlockSpec((1, 1, LANES, lk), prv_head)]
        args += [k, v]
    if diff:
        in_specs += [_const_spec(lam_vecs.shape), _const_spec(subln.shape)]
        args += [lam_vecs, subln]
    nkeys = sum(k.shape[2] for k, _ in kvs)
    return pl.pallas_call(
        functools.partial(_attn_kernel, nseg=len(kvs), mode=mode, lam_init=lam_init),
        grid=(ntile + 1,),
        in_specs=in_specs,
        out_specs=pl.BlockSpec((1, 1, tq, LANES), lambda t: tile(t - 1) + (0,)),
        out_shape=jax.ShapeDtypeStruct((b, nh, lq, LANES), BF),
        scratch_shapes=([pltpu.VMEM((nmap, nkeys, tq), F32)] * 2 + [pltpu.VMEM((nmap, SUBLANES, tq), F32)] * 2),
        compiler_params=_cparams("arbitrary"),
        name="attn_diff" if diff else "attn_mla",
    )(*args)


def _outproj_kernel(*refs, n_in, has_bias, tail):
    o_refs = refs[:n_in]
    w_ref = refs[n_in]
    pos = n_in + 1
    b_ref = None
    if has_bias:
        b_ref = refs[pos]
        pos += 1
    x_ref, mod_ref, g_ref = refs[pos:pos + 3]
    rest = refs[pos + 3:]
    o = jnp.concatenate([r[0, h] for r in o_refs for h in range(r.shape[1])], axis=-1)
    y = _dot(o, w_ref[...])
    if has_bias:
        y = y + b_ref[...]
    x1 = x_ref[0] + mod_ref[0, 2:3, :] * _rms(y, g_ref[1:2, :])
    h = _norm_mod(x1, g_ref[2:3, :], mod_ref[0, 3:4, :], mod_ref[0, 4:5, :])
    if tail == "ffn":
        w1_ref, w3_ref, w2_ref, out_ref = rest
        out_ref[0] = x1 + mod_ref[0, 5:6, :] * _rms(_swiglu(h.astype(BF), w1_ref, w3_ref, w2_ref), g_ref[3:4, :])
    else:
        router_ref, tri_ref, out_ref = rest[:3]
        out_ref[0] = x1
        first = (pl.program_id(0) == 0) & (pl.program_id(1) == 0)
        _route(h, first, router_ref, tri_ref, *rest[3:])


def _outproj(os_, w, bias, x, mod, g, tm, tail, tail_args):
    b, t, d = x.shape
    n = b * t
    nt = t // tm
    in_specs = [pl.BlockSpec((1, o.shape[1], tm, LANES), lambda bi, i: (bi, 0, i, 0)) for o in os_]
    in_specs.append(_const_spec(w.shape))
    args = list(os_) + [w]
    if bias is not None:
        in_specs.append(_const_spec(bias.shape))
        args.append(bias)
    x_spec = pl.BlockSpec((1, tm, d), lambda bi, i: (bi, i, 0))
    in_specs += [x_spec, _mod_spec(mod), _const_spec(g.shape)]
    args += [x, mod, g]
    out_specs, out_shape, scratch = x_spec, jax.ShapeDtypeStruct((b, t, d), F32), []
    if tail == "ffn":
        in_specs += [_resident_spec(a.shape, lambda bi, i: (0, 0)) for a in tail_args]
        args += list(tail_args)
    else:
        router, = tail_args
        tri = jnp.asarray(np.tril(np.ones((tm, tm), np.float32), -1), BF)
        in_specs += [_const_spec(router.shape), _const_spec(tri.shape)]
        args += [router, tri]
        rows = lambda wd: pl.BlockSpec((tm, wd), lambda bi, i: (bi * nt + i, 0))
        out_specs = [x_spec, pl.BlockSpec((PACK_CHUNKS, tm, LANES), lambda bi, i: (0, bi * nt + i, 0)),
                     rows(8), rows(8), pl.BlockSpec((1, LANES), lambda bi, i: (0, 0))]
        out_shape = [out_shape, jax.ShapeDtypeStruct((PACK_CHUNKS, n, LANES), jnp.int32),
                     jax.ShapeDtypeStruct((n, 8), jnp.int32), jax.ShapeDtypeStruct((n, 8), F32),
                     jax.ShapeDtypeStruct((1, LANES), F32)]
        scratch = [pltpu.VMEM((1, LANES), F32)]
    return pl.pallas_call(
        functools.partial(_outproj_kernel, n_in=len(os_), has_bias=bias is not None, tail=tail),
        grid=(b, nt),
        in_specs=in_specs,
        out_specs=out_specs,
        out_shape=out_shape,
        scratch_shapes=scratch,
        compiler_params=_cparams("arbitrary", "arbitrary"),
        name="outproj_" + tail,
    )(*args)


MXU_TILE = 256


def _ff_chunks(f):
    tiles = f // MXU_TILE
    if f % MXU_TILE or tiles < 2:
        return [(0, f)]
    first = (tiles + 1) // 2 * MXU_TILE
    return [(0, first), (first, f - first)]


def _swiglu(hn, w1_ref, w3_ref, w2_ref):
    y = None
    for lo, n in _ff_chunks(w1_ref.shape[1]):
        a = _dot(hn, w1_ref[:, lo:lo + n])
        act = (a * jax.nn.sigmoid(a) * _dot(hn, w3_ref[:, lo:lo + n])).astype(BF)
        part = _dot(act, w2_ref[lo:lo + n, :])
        y = part if y is None else y + part
    return y


def _resident_spec(shape, index_map):
    return pl.BlockSpec(shape, index_map, pipeline_mode=pl.Buffered(1))


MOE_ROW_TILE = 512
PACK_CHUNKS = 4
SC_WINDOW = 128
_HI_MASK = -65536
_LO_MASK = 65535


def _pack_rows(v, out_ref):
    half = v.shape[1] // 2
    vb = v.astype(BF).astype(F32)
    lo = (pltpu.bitcast(vb[:, :half], jnp.int32) >> 16) & _LO_MASK
    hi = pltpu.bitcast(vb[:, half:], jnp.int32) & _HI_MASK
    w = lo | hi
    for c in range(PACK_CHUNKS):
        out_ref[c] = w[:, c * LANES:(c + 1) * LANES]


def _unpack_rows(ref):
    w = jnp.concatenate([ref[c] for c in range(PACK_CHUNKS)], axis=-1)
    lo = pltpu.bitcast(w << 16, F32)
    hi = pltpu.bitcast(w & _HI_MASK, F32)
    return jnp.concatenate([lo, hi], axis=-1)


def _route(h, first, router_ref, tri_ref, hp_ref, meta_ref, gate_ref, cnt_ref, carry_ref):
    @pl.when(first)
    def _():
        carry_ref[...] = jnp.zeros_like(carry_ref)

    _pack_rows(h, hp_ref)
    h_hi = h.astype(BF)
    h_lo = (h - h_hi.astype(F32)).astype(BF)
    logits = _dot(h_hi, router_ref[0]) + (_dot(h_hi, router_ref[1]) + _dot(h_lo, router_ref[0]))
    lane = lax.broadcasted_iota(jnp.int32, logits.shape, 1)
    logits = jnp.where(lane < N_EXPERTS, logits, -jnp.inf)
    m1 = jnp.max(logits, axis=-1, keepdims=True)
    i1 = jnp.min(jnp.where(logits == m1, lane, LANES), axis=-1, keepdims=True)
    rest = jnp.where(lane == i1, -jnp.inf, logits)
    m2 = jnp.max(rest, axis=-1, keepdims=True)
    i2 = jnp.min(jnp.where(rest == m2, lane, LANES), axis=-1, keepdims=True)
    e2 = jnp.exp(m2 - m1)
    denom = 1.0 + e2
    assigned = jnp.where((lane == i1) | (lane == i2), 1.0, 0.0)
    ranks = _dot(tri_ref[...], assigned.astype(BF)) + carry_ref[...]
    r1 = jnp.sum(jnp.where(lane == i1, ranks, 0.0), axis=-1, keepdims=True).astype(jnp.int32)
    r2 = jnp.sum(jnp.where(lane == i2, ranks, 0.0), axis=-1, keepdims=True).astype(jnp.int32)
    carry_ref[...] += jnp.sum(assigned, axis=0, keepdims=True)
    cnt_ref[...] = carry_ref[...]
    col = lax.broadcasted_iota(jnp.int32, meta_ref.shape, 1)
    meta_ref[...] = jnp.where(col == 0, i1, jnp.where(col == 1, i2, jnp.where(col == 2, r1, r2)))
    gate_ref[...] = jnp.where(col == 0, 1.0 / denom, e2 / denom)


def _sc_mesh():
    return plsc.VectorSubcoreMesh(core_axis_name="core", subcore_axis_name="subcore")


def _sc_scatter_rows(rows, idx_a, idx_b, n_out):
    nrows = rows.shape[0]

    @pl.kernel(out_type=jax.ShapeDtypeStruct((n_out, LANES), rows.dtype), mesh=_sc_mesh(), scratch_types=[])
    def scatter_kernel(x_hbm, ia_hbm, ib_hbm, o_hbm):
        def body(x_vmem, ia_vmem, ib_vmem):
            pltpu.sync_copy(x_vmem, o_hbm.at[ia_vmem.at[0]])
            pltpu.sync_copy(x_vmem, o_hbm.at[ib_vmem.at[0]])

        pltpu.emit_pipeline(
            body,
            grid=(nrows // SC_WINDOW,),
            in_specs=[pl.BlockSpec((SC_WINDOW, LANES), lambda i: (i, 0)),
                      pl.BlockSpec((1, SC_WINDOW), lambda i: (0, i)),
                      pl.BlockSpec((1, SC_WINDOW), lambda i: (0, i))],
            out_specs=[],
            core_axis_name=("core", "subcore"),
            dimension_semantics=(pltpu.PARALLEL,),
        )(x_hbm, ia_hbm, ib_hbm)

    return scatter_kernel(rows, idx_a, idx_b)


def _sc_gather_rows(table, idx_a, idx_b):
    nrows = idx_a.shape[1]
    out = jax.ShapeDtypeStruct((nrows, LANES), table.dtype)

    @pl.kernel(out_type=(out, out), mesh=_sc_mesh(), scratch_types=[])
    def gather_kernel(t_hbm, ia_hbm, ib_hbm, oa_hbm, ob_hbm):
        def body(ia_vmem, ib_vmem, oa_vmem, ob_vmem):
            pltpu.sync_copy(t_hbm.at[ia_vmem.at[0]], oa_vmem)
            pltpu.sync_copy(t_hbm.at[ib_vmem.at[0]], ob_vmem)

        pltpu.emit_pipeline(
            body,
            grid=(nrows // SC_WINDOW,),
            in_specs=[pl.BlockSpec((1, SC_WINDOW), lambda i: (0, i)),
                      pl.BlockSpec((1, SC_WINDOW), lambda i: (0, i))],
            out_specs=[pl.BlockSpec((SC_WINDOW, LANES), lambda i: (i, 0)),
                       pl.BlockSpec((SC_WINDOW, LANES), lambda i: (i, 0))],
            core_axis_name=("core", "subcore"),
            dimension_semantics=(pltpu.PARALLEL,),
        )(ia_hbm, ib_hbm, oa_hbm, ob_hbm)

    return gather_kernel(table, idx_a, idx_b)


def _experts_kernel(te_ref, tv_ref, xs_ref, w1_ref, w3_ref, w2_ref, ys_ref):
    del te_ref
    valid = tv_ref[pl.program_id(0)]

    @pl.when(valid > 0)
    def _():
        row = lax.broadcasted_iota(jnp.int32, (xs_ref.shape[1], 1), 0)
        hn = jnp.where(row < valid, _unpack_rows(xs_ref), 0.0).astype(BF)
        _pack_rows(_swiglu(hn, w1_ref.at[0], w3_ref.at[0], w2_ref.at[0]), ys_ref)

    @pl.when(valid == 0)
    def _():
        ys_ref[...] = jnp.zeros_like(ys_ref)


def _experts(xs, tile_expert, tile_valid, w1, w3, w2):
    _, p, _ = xs.shape
    tr = MOE_ROW_TILE
    rows = pl.BlockSpec((PACK_CHUNKS, tr, LANES), lambda j, te, tv: (0, j, 0))
    weights = lambda w: pl.BlockSpec((1,) + w.shape[1:], lambda j, te, tv: (te[j], 0, 0))
    return pl.pallas_call(
        _experts_kernel,
        grid_spec=pltpu.PrefetchScalarGridSpec(
            num_scalar_prefetch=2,
            grid=(p // tr,),
            in_specs=[rows, weights(w1), weights(w3), weights(w2)],
            out_specs=rows,
        ),
        out_shape=jax.ShapeDtypeStruct(xs.shape, jnp.int32),
        compiler_params=_cparams("arbitrary"),
        name="moe_experts",
    )(tile_expert, tile_valid, xs, w1, w3, w2)


def _combine_kernel(x_ref, ya_ref, yb_ref, gate_ref, mod_ref, g_ref, out_ref):
    gates = gate_ref[...]
    fx = gates[:, 0:1] * _unpack_rows(ya_ref) + gates[:, 1:2] * _unpack_rows(yb_ref)
    out_ref[0] = x_ref[0] + mod_ref[0, 5:6, :] * _rms(fx, g_ref[3:4, :])


def _combine(x, ya, yb, gates, mod, g, tm):
    b, t, d = x.shape
    nt = t // tm
    packed = pl.BlockSpec((PACK_CHUNKS, tm, LANES), lambda bi, i: (0, bi * nt + i, 0))
    return pl.pallas_call(
        _combine_kernel,
        grid=(b, nt),
        in_specs=[
            pl.BlockSpec((1, tm, d), lambda bi, i: (bi, i, 0)),
            packed, packed,
            pl.BlockSpec((tm, gates.shape[1]), lambda bi, i: (bi * nt + i, 0)),
            _mod_spec(mod),
            _const_spec(g.shape),
        ],
        out_specs=pl.BlockSpec((1, tm, d), lambda bi, i: (bi, i, 0)),
        out_shape=jax.ShapeDtypeStruct((b, t, d), F32),
        compiler_params=_cparams("arbitrary", "arbitrary"),
        name="moe_combine",
    )(x, ya, yb, gates, mod, g)


def _moe(x, routed, mod, g, w1, w3, w2, tm):
    b, t, d = x.shape
    n = b * t
    ne = w1.shape[0]
    tr = MOE_ROW_TILE
    assert d == 2 * PACK_CHUNKS * LANES and (PACK_CHUNKS * n) % (SC_WINDOW * 32) == 0
    hp, meta, gates, counts = routed

    ntile = 2 * n // tr + ne
    p = ntile * tr
    cnt = counts[0, :ne].astype(jnp.int32)
    tiles = (cnt + tr - 1) // tr
    tile_end = jnp.cumsum(tiles)
    tile_start = tile_end - tiles
    base = tile_start * tr
    eids = jnp.arange(ne, dtype=jnp.int32)
    base_of = lambda e: jnp.sum(jnp.where(e[:, None] == eids[None, :], base[None, :], 0), axis=-1)
    pos_a = base_of(meta[:, 0]) + meta[:, 2]
    pos_b = base_of(meta[:, 1]) + meta[:, 3]
    chunk = jnp.arange(PACK_CHUNKS, dtype=jnp.int32)[:, None] * p
    idx_a = (chunk + pos_a[None, :]).reshape(1, PACK_CHUNKS * n)
    idx_b = (chunk + pos_b[None, :]).reshape(1, PACK_CHUNKS * n)
    tj = jnp.arange(ntile, dtype=jnp.int32)
    tile_expert = jnp.minimum(jnp.sum(tj[:, None] >= tile_end[None, :], axis=-1), ne - 1).astype(jnp.int32)
    done = jnp.sum(jnp.where(tile_expert[:, None] == eids[None, :], tile_start[None, :], 0), axis=-1)
    left = jnp.sum(jnp.where(tile_expert[:, None] == eids[None, :], cnt[None, :], 0), axis=-1) - (tj - done) * tr
    tile_valid = jnp.where(tj < tile_end[-1], jnp.clip(left, 0, tr), 0).astype(jnp.int32)

    xs = _sc_scatter_rows(hp.reshape(PACK_CHUNKS * n, LANES), idx_a, idx_b, PACK_CHUNKS * p)
    ys = _experts(xs.reshape(PACK_CHUNKS, p, LANES), tile_expert, tile_valid, w1, w3, w2)
    ya, yb = _sc_gather_rows(ys.reshape(PACK_CHUNKS * p, LANES), idx_a, idx_b)
    shp = (PACK_CHUNKS, n, LANES)
    return _combine(x, ya.reshape(shp), yb.reshape(shp), gates, mod, g, tm)


def _proj_na_kernel(x_ref, mod_ref, g_ref, w_ref, b_ref, q_ref, k_ref, v_ref):
    h = _norm_mod(x_ref[0], g_ref[0:1, :], mod_ref[0, 0:1, :], mod_ref[0, 1:2, :])
    z = _dot(h.astype(BF), w_ref[...]) + b_ref[...]
    n = NA_SLOTS * LANES
    for s in range(NA_SLOTS):
        lo = s * LANES
        q_ref[0, s] = (z[:, lo:lo + LANES] * NA_SCALE).astype(BF)
        k_ref[0, s] = z[:, n + lo:n + lo + LANES].astype(BF)
        v_ref[0, s] = z[:, 2 * n + lo:2 * n + lo + LANES].astype(BF)


def _proj_na(x, mod, g, w, bias, tm):
    b, t, d = x.shape
    tok = pl.BlockSpec((1, NA_SLOTS, tm, LANES), lambda bi, i: (bi, 0, i, 0))
    shp = jax.ShapeDtypeStruct((b, NA_SLOTS, t, LANES), BF)
    return pl.pallas_call(
        _proj_na_kernel,
        grid=(b, t // tm),
        in_specs=[
            pl.BlockSpec((1, tm, d), lambda bi, i: (bi, i, 0)),
            _mod_spec(mod),
            _const_spec(g.shape),
            _const_spec(w.shape),
            _const_spec(bias.shape),
        ],
        out_specs=[tok, tok, tok],
        out_shape=[shp, shp, shp],
        compiler_params=_cparams("arbitrary", "arbitrary"),
        name="proj_na",
    )(x, mod, g, w, bias)


NA_ROWS_PER_STEP = 2


def _na_window_start(r, rows):
    return jnp.clip(r - NA_KH // 2, 0, rows - NA_KH)


def _na_kernel(q_ref, kx_ref, vx_ref, kc_ref, vc_ref, *rest, rows):
    bias_refs, o_ref = rest[:NA_ROWS_PER_STEP], rest[NA_ROWS_PER_STEP]
    nwin = NA_KH * GRID_W
    lane = lax.broadcasted_iota(jnp.int32, (GRID_W, LANES), 1)
    work = [(a, s) for a in range(NA_ROWS_PER_STEP) for s in range(NA_SLOTS)]
    idx = range(len(work))
    starts = [pl.multiple_of(_na_window_start(pl.program_id(1) * NA_ROWS_PER_STEP + a, rows) * GRID_W, GRID_W)
              for a in range(NA_ROWS_PER_STEP)]
    sws, scs = [], []
    for a, s in work:
        q = q_ref[0, s, a * GRID_W:(a + 1) * GRID_W, :]
        zero = jnp.zeros_like(q)
        q2 = jnp.concatenate([jnp.where(lane < NA_HD, q, zero), jnp.where(lane >= NA_HD, q, zero)], axis=0)
        kw = kx_ref[0, s, pl.ds(starts[a], nwin), :]
        sws.append(lax.dot_general(q2, kw, _NT, preferred_element_type=F32) + bias_refs[a][0, s])
        scs.append(lax.dot_general(q2, kc_ref[0, s], _NT, preferred_element_type=F32))
    ms = [jnp.maximum(jnp.max(sws[i], axis=-1, keepdims=True), jnp.max(scs[i], axis=-1, keepdims=True))
          for i in idx]
    pws = [jnp.exp2(sws[i] - ms[i]) for i in idx]
    pcs = [jnp.exp2(scs[i] - ms[i]) for i in idx]
    ls = [jnp.sum(pws[i], axis=-1, keepdims=True) + jnp.sum(pcs[i], axis=-1, keepdims=True) for i in idx]
    for i, (a, s) in enumerate(work):
        vw = vx_ref[0, s, pl.ds(starts[a], nwin), :]
        o2 = (_dot(pws[i].astype(BF), vw) + _dot(pcs[i].astype(BF), vc_ref[0, s])) / ls[i]
        o_ref[0, s, a * GRID_W:(a + 1) * GRID_W, :] = jnp.where(lane < NA_HD, o2[:GRID_W], o2[GRID_W:]).astype(BF)


def _na_attention(q, kx, vx, kc, vc, bias):
    b, ns, l, _ = q.shape
    rows = l // GRID_W
    lc = kc.shape[2]
    nr = NA_ROWS_PER_STEP
    assert rows % nr == 0

    def bias_spec(a):
        def index_map(bi, i):
            r = i * nr + a
            return (r - _na_window_start(r, rows), 0, 0, 0)
        return pl.BlockSpec((1,) + bias.shape[1:], index_map)

    full = lambda n: pl.BlockSpec((1, ns, n, LANES), lambda bi, i: (bi, 0, 0, 0))
    row = pl.BlockSpec((1, ns, nr * GRID_W, LANES), lambda bi, i: (bi, 0, i, 0))
    return pl.pallas_call(
        functools.partial(_na_kernel, rows=rows),
        grid=(b, rows // nr),
        in_specs=[row, full(l), full(l), full(lc), full(lc)] + [bias_spec(a) for a in range(nr)],
        out_specs=row,
        out_shape=jax.ShapeDtypeStruct((b, ns, l, LANES), BF),
        compiler_params=_cparams("arbitrary", "arbitrary"),
        name="na_attn",
    )(q, kx, vx, kc, vc, *([bias] * nr))


def _axis_tables(pos, dim):
    inv = ROPE_THETA ** (-jnp.arange(0, dim, 2, dtype=F32) / dim)
    ang = pos.astype(F32)[:, None] * inv[None, :]
    ang = jnp.concatenate([ang, ang], axis=-1)
    return jnp.cos(ang), jnp.sin(ang)


def _rope_tables(n, rope):
    if rope:
        t = jnp.arange(n, dtype=jnp.int32)
        row, col = t // GRID_W, t % GRID_W

        def cs(d):
            cr, sr = _axis_tables(row, d // 2)
            cc, sc = _axis_tables(col, d // 2)
            return jnp.concatenate([cr, cc], axis=-1), jnp.concatenate([sr, sc], axis=-1)

        c32, s32 = cs(MLA_ROPE)
        c64, s64 = cs(DIFF_HD)
    else:
        c32, s32 = jnp.ones((n, MLA_ROPE), F32), jnp.zeros((n, MLA_ROPE), F32)
        c64, s64 = jnp.ones((n, DIFF_HD), F32), jnp.zeros((n, DIFF_HD), F32)
    ones = jnp.ones((n, MLA_NOPE), F32)
    pad = lambda a: jnp.pad(a, ((0, 0), (0, LANES - a.shape[1])))
    cos_q = MLA_SCALE * pad(jnp.concatenate([ones, c32], axis=-1))
    sin_q = MLA_SCALE * pad(jnp.concatenate([jnp.zeros_like(ones), s32], axis=-1))
    return jnp.stack([cos_q, sin_q, pad(c32), pad(s32),
                      jnp.concatenate([c64, c64], axis=-1), jnp.concatenate([s64, s64], axis=-1)])


def _rot_cols(w, d):
    shp = w.shape
    w5 = w.reshape(shp[:-1] + (-1, 2, 2, d // 4))
    r = jnp.concatenate([-w5[..., 1:2, :], w5[..., 0:1, :]], axis=-2)
    return r.reshape(shp)


def _pad_cols(w, n):
    return jnp.pad(w, ((0, 0), (0, n - w.shape[1])))


def _prep_ab_weights(w_in, q_norm, kv_norm, w_uq, w_ukv, w_out):
    o_cq, o_ckv, o_kr = MLA_Q_RANK, MLA_Q_RANK + MLA_KV_RANK, MLA_Q_RANK + MLA_KV_RANK + MLA_ROPE
    o_dq, o_dk = o_kr + _DW, o_kr + 2 * _DW
    w_kr, w_dq, w_dk, w_dv = w_in[:, o_ckv:o_kr], w_in[:, o_kr:o_dq], w_in[:, o_dq:o_dk], w_in[:, o_dk:]
    win = jnp.concatenate([
        w_in[:, :o_ckv], _pad_cols(w_kr, LANES), _pad_cols(_rot_cols(w_kr, MLA_ROPE), LANES),
        w_dq, _rot_cols(w_dq, DIFF_HD), w_dk, _rot_cols(w_dk, DIFF_HD), w_dv], axis=1).astype(BF)
    rq = w_uq.shape[0]
    uq = w_uq.reshape(rq, MLA_HEADS, MLA_NOPE + MLA_ROPE)
    nope, rope = uq[..., :MLA_NOPE], uq[..., MLA_NOPE:]
    zpad = jnp.zeros((rq, MLA_HEADS, LANES - MLA_NOPE - MLA_ROPE), F32)
    main = jnp.concatenate([nope, rope, zpad], axis=-1).reshape(rq, MLA_HEADS * LANES)
    rot = jnp.concatenate([jnp.zeros_like(nope), _rot_cols(rope, MLA_ROPE), zpad], axis=-1)
    wuq = jnp.concatenate([main, rot.reshape(rq, MLA_HEADS * LANES)], axis=1).astype(BF)
    rkv = w_ukv.shape[0]
    ukv = w_ukv.reshape(rkv, MLA_HEADS, MLA_NOPE + MLA_V)
    slot = lambda a: jnp.pad(a, ((0, 0), (0, 0), (0, LANES - a.shape[-1]))).reshape(rkv, MLA_HEADS * LANES)
    wuk = slot(ukv[..., :MLA_NOPE]).astype(BF)
    wuv = slot(ukv[..., MLA_NOPE:]).astype(BF)
    place = np.zeros((LANES, MLA_HEADS * LANES), np.float32)
    for hd in range(MLA_HEADS):
        place[np.arange(MLA_ROPE), hd * LANES + MLA_NOPE + np.arange(MLA_ROPE)] = 1.0
    d = w_out.shape[1]
    wo_a = w_out[:MLA_HEADS * MLA_V].reshape(MLA_HEADS, MLA_V, d)
    wo_a = jnp.pad(wo_a, ((0, 0), (0, LANES - MLA_V), (0, 0))).reshape(MLA_HEADS * LANES, d)
    wo = jnp.concatenate([wo_a, w_out[MLA_HEADS * MLA_V:]], axis=0).astype(BF)
    proj = (win, q_norm.reshape(1, -1), kv_norm.reshape(1, -1), wuq, wuk, wuv, jnp.asarray(place, BF))
    return proj, wo


def _na_bias_table(rpb):
    nh = rpb.shape[0]
    cols = np.arange(GRID_W)
    col_start = np.clip(cols - NA_KW // 2, 0, GRID_W - NA_KW)
    kcol = np.arange(GRID_W)
    valid = (kcol[None, :] >= col_start[:, None]) & (kcol[None, :] < col_start[:, None] + NA_KW)
    dc = kcol[None, :] - cols[:, None] + (NA_KW - 1)
    onehot = (dc[None] == np.arange(2 * NA_KW - 1)[:, None, None]) & valid[None]
    tz = jnp.einsum('hrd,dck->hrck', rpb * LOG2E, jnp.asarray(onehot, F32), precision=lax.Precision.HIGHEST)
    tz = jnp.where(valid[None, None], tz, MASK_VALUE)
    cases = [tz[:, NA_KH - 1 - c:2 * NA_KH - 1 - c] for c in range(NA_KH)]
    tbl = jnp.transpose(jnp.stack(cases), (0, 1, 3, 2, 4))
    return tbl.reshape(NA_KH, nh // 2, 2 * GRID_W, NA_KH * GRID_W)


def _tile(n, pref):
    return pref if n % pref == 0 else n


def kernel(x, c, ctx, c_ctx, w_mod, b_mod, norm_g, a_w_in, a_q_norm, a_kv_norm, a_w_uq, a_w_ukv, b_lambda, b_subln,
           ab_w_out, f_w1, f_w3, f_w2, c_w_qkv, c_b_qkv, c_rpb, c_w_out, c_b_out, m_router, m_w1, m_w3, m_w2):
    b, l, d = x.shape
    lc = ctx.shape[1]
    depth = w_mod.shape[0]
    assert l % GRID_W == 0 and l // GRID_W >= NA_KH

    mod_rows = 16
    cvec = jnp.zeros((mod_rows, d), F32).at[:b].set(c).at[b].set(c_ctx)
    mod = _modulation(cvec, w_mod, b_mod)

    tm_x, tm_c = _tile(l, 512), _tile(lc, 256)
    tq_x, tq_c = _tile(l, ATTN_QUERY_LANES), _tile(lc, ATTN_QUERY_LANES // 2)
    tq2_x = _tile(l, ATTN_QUERY_LANES // 2)
    cs = ctx
    for i in range(depth):
        last = i == depth - 1
        j = i // 2
        mx = mod[i, :b].reshape(b, N_MOD, d)
        mc = mod[i, b].reshape(1, N_MOD, d)
        g = norm_g[i]
        if i % 2 == 0:
            lam_init = 0.8 - 0.6 * math.exp(-0.3 * i)
            proj_w, wo = _prep_ab_weights(a_w_in[j], a_q_norm[j], a_kv_norm[j], a_w_uq[j], a_w_ukv[j], ab_w_out[j])
            qx, kx, vx, dqx, dkx, dvx = _proj_ab(x, mx, g, proj_w, _rope_tables(l, True), tm_x)
            qc, kc, vc, dqc, dkc, dvc = _proj_ab(cs, mc, g, proj_w, _rope_tables(lc, False), tm_c)
            subln = b_subln[j].reshape(1, -1)
            oa = _attention(qx, [(kx, vx), (kc, vc)], tq_x)
            ob = _attention(dqx, [(dkx, dvx), (dkc, dvc)], tq2_x, "diff", b_lambda[j], subln, lam_init)
            ffn_w = (f_w1[j].astype(BF), f_w3[j].astype(BF), f_w2[j].astype(BF))
            x = _outproj([oa, ob], wo, None, x, mx, g, tm_x, "ffn", ffn_w)
            if not last:
                oa = _attention(qc, [(kc, vc)], tq_c)
                ob = _attention(dqc, [(dkc, dvc)], tq_c, "diff", b_lambda[j], subln, lam_init)
                cs = _outproj([oa, ob], wo, None, cs, mc, g, tm_c, "ffn", ffn_w)
        else:
            wqkv = c_w_qkv[j].astype(BF)
            bqkv = c_b_qkv[j].reshape(1, -1)
            wo = c_w_out[j].astype(BF)
            bo = c_b_out[j].reshape(1, -1)
            qx, kx, vx = _proj_na(x, mx, g, wqkv, bqkv, tm_x)
            qc, kc, vc = _proj_na(cs, mc, g, wqkv, bqkv, tm_c)
            o = _na_attention(qx, kx, vx, kc, vc, _na_bias_table(c_rpb[j]))
            r_f32 = _pad_cols(m_router[j], LANES)
            r_hi = r_f32.astype(BF)
            router = (jnp.stack([r_hi, (r_f32 - r_hi.astype(F32)).astype(BF)]),)
            w1, w3, w2 = m_w1[j].astype(BF), m_w3[j].astype(BF), m_w2[j].astype(BF)
            x, *routed = _outproj([o], wo, bo, x, mx, g, tm_x, "router", router)
            x = _moe(x, routed, mx, g, w1, w3, w2, tm_x)
            if not last:
                oc = _attention(qc, [(kc, jnp.swapaxes(vc, 2, 3))], tq_c, "pair")
                cs, *routed = _outproj([oc], wo, bo, cs, mc, g, tm_c, "router", router)
                cs = _moe(cs, routed, mc, g, w1, w3, w2, tm_c)
    return x
```

```python
import functools
import math

import jax
import jax.numpy as jnp
import numpy as np
from jax import lax
from jax.experimental import pallas as pl
from jax.experimental.pallas import tpu as pltpu
from jax.experimental.pallas import tpu_sc as plsc

BF = jnp.bfloat16
F32 = jnp.float32

LANES = 128
VMEM_LIMIT = 56 * 1024 * 1024

GRID_W = 64
EPS = 1e-6
ROPE_THETA = 10000.0
N_MOD = 6

MLA_HEADS = 8
MLA_NOPE = 64
MLA_ROPE = 32
MLA_V = 64
MLA_Q_RANK = 384
MLA_KV_RANK = 256
LOG2E = math.log2(math.e)
MLA_SCALE = (MLA_NOPE + MLA_ROPE) ** -0.5 * LOG2E

DIFF_HEADS = 4
DIFF_HD = 64
DIFF_SCALE = DIFF_HD ** -0.5 * LOG2E

NA_HEADS = 16
NA_HD = 64
NA_KH = 8
NA_KW = 16
NA_SCALE = NA_HD ** -0.5 * LOG2E
NA_SLOTS = NA_HEADS * NA_HD // LANES
MASK_VALUE = -1e30

N_EXPERTS = 8

_NT = (((1,), (1,)), ((), ()))


def _cparams(*sem):
    return pltpu.CompilerParams(dimension_semantics=sem, vmem_limit_bytes=VMEM_LIMIT)


def _dot(a, b):
    return jnp.dot(a, b, preferred_element_type=F32)


def _rms(x, g):
    return x * lax.rsqrt(jnp.mean(x * x, axis=-1, keepdims=True) + EPS) * g


def _norm_mod(x, g, shift, scale):
    return _rms(x, g) * (1 + scale) + shift


def _const_spec(shape):
    return pl.BlockSpec(shape, lambda *_: (0,) * len(shape))


def _mod_spec(mod):
    if mod.shape[0] == 1:
        return pl.BlockSpec((1,) + mod.shape[1:], lambda b, *_: (0, 0, 0))
    return pl.BlockSpec((1,) + mod.shape[1:], lambda b, *_: (b, 0, 0))


def _mod_kernel(c_ref, w_ref, b_ref, o_ref):
    c = c_ref[...]
    sc = c * jax.nn.sigmoid(c)
    o_ref[0] = _dot(sc.astype(BF), w_ref[0].astype(BF)) + b_ref[0]


def _modulation(cvec, w_mod, b_mod):
    depth, d, n = w_mod.shape
    rows = cvec.shape[0]
    return pl.pallas_call(
        _mod_kernel,
        grid=(depth, n // d),
        in_specs=[
            pl.BlockSpec((rows, d), lambda i, j: (0, 0)),
            pl.BlockSpec((1, d, d), lambda i, j: (i, 0, j)),
            pl.BlockSpec((1, 1, d), lambda i, j: (i, 0, j)),
        ],
        out_specs=pl.BlockSpec((1, rows, d), lambda i, j: (i, 0, j)),
        out_shape=jax.ShapeDtypeStruct((depth, rows, n), F32),
        compiler_params=_cparams("arbitrary", "arbitrary"),
        name="modulation",
    )(cvec, w_mod, b_mod.reshape(depth, 1, n))


_Z_CQ = 0
_Z_CKV = MLA_Q_RANK
_Z_KR = _Z_CKV + MLA_KV_RANK
_Z_KR_ROT = _Z_KR + LANES
_Z_DQ = _Z_KR_ROT + LANES
_DW = 2 * DIFF_HEADS * DIFF_HD
_Z_DQ_ROT = _Z_DQ + _DW
_Z_DK = _Z_DQ_ROT + _DW
_Z_DK_ROT = _Z_DK + _DW
_Z_DV = _Z_DK_ROT + _DW
_Z_END = _Z_DV + _DW


def _proj_ab_kernel(x_ref, mod_ref, g_ref, win_ref, qn_ref, kvn_ref, wuq_ref, wuk_ref, wuv_ref,
                    place_ref, tab_ref, q_ref, k_ref, v_ref, dq_ref, dk_ref, dv_ref):
    h = _norm_mod(x_ref[0], g_ref[0:1, :], mod_ref[0, 0:1, :], mod_ref[0, 1:2, :])
    z = _dot(h.astype(BF), win_ref[...])
    cqn = _rms(z[:, _Z_CQ:_Z_CKV], qn_ref[...]).astype(BF)
    ckvn = _rms(z[:, _Z_CKV:_Z_KR], kvn_ref[...]).astype(BF)
    q2 = _dot(cqn, wuq_ref[...])
    nq = MLA_HEADS * LANES
    cos_q, sin_q = tab_ref[0], tab_ref[1]
    for hd in range(MLA_HEADS):
        lo = hd * LANES
        q_ref[0, hd] = (q2[:, lo:lo + LANES] * cos_q + q2[:, nq + lo:nq + lo + LANES] * sin_q).astype(BF)
    kr = (z[:, _Z_KR:_Z_KR_ROT] * tab_ref[2] + z[:, _Z_KR_ROT:_Z_DQ] * tab_ref[3]).astype(BF)
    kk = _dot(ckvn, wuk_ref[...]) + _dot(kr, place_ref[...])
    vv = _dot(ckvn, wuv_ref[...])
    for hd in range(MLA_HEADS):
        lo = hd * LANES
        k_ref[0, hd] = kk[:, lo:lo + LANES].astype(BF)
        v_ref[0, hd] = vv[:, lo:lo + MLA_V].T.astype(BF)
    cos_d, sin_d = tab_ref[4], tab_ref[5]
    for hd in range(DIFF_HEADS):
        lo = hd * LANES
        dq = z[:, _Z_DQ + lo:_Z_DQ + lo + LANES] * cos_d + z[:, _Z_DQ_ROT + lo:_Z_DQ_ROT + lo + LANES] * sin_d
        dq_ref[0, hd] = (dq * DIFF_SCALE).astype(BF)
        dk = z[:, _Z_DK + lo:_Z_DK + lo + LANES] * cos_d + z[:, _Z_DK_ROT + lo:_Z_DK_ROT + lo + LANES] * sin_d
        dk_ref[0, hd] = dk.astype(BF)
        dv_ref[0, hd] = z[:, _Z_DV + lo:_Z_DV + lo + LANES].T.astype(BF)


def _proj_ab(x, mod, g, wts, tabs, tm):
    b, t, d = x.shape
    win, qn, kvn, wuq, wuk, wuv, place = wts
    tok = lambda hh: pl.BlockSpec((1, hh, tm, LANES), lambda bi, i: (bi, 0, i, 0))
    shp = lambda hh: jax.ShapeDtypeStruct((b, hh, t, LANES), BF)
    tok_t = lambda hh, dv: pl.BlockSpec((1, hh, dv, tm), lambda bi, i: (bi, 0, 0, i))
    shp_t = lambda hh, dv: jax.ShapeDtypeStruct((b, hh, dv, t), BF)
    return pl.pallas_call(
        _proj_ab_kernel,
        grid=(b, t // tm),
        in_specs=[
            pl.BlockSpec((1, tm, d), lambda bi, i: (bi, i, 0)),
            _mod_spec(mod),
            _const_spec(g.shape),
            _const_spec(win.shape), _const_spec(qn.shape), _const_spec(kvn.shape),
            _const_spec(wuq.shape), _const_spec(wuk.shape), _const_spec(wuv.shape),
            _const_spec(place.shape),
            pl.BlockSpec((6, tm, LANES), lambda bi, i: (0, i, 0)),
        ],
        out_specs=[tok(MLA_HEADS), tok(MLA_HEADS), tok_t(MLA_HEADS, MLA_V),
                   tok(DIFF_HEADS), tok(DIFF_HEADS), tok_t(DIFF_HEADS, 2 * DIFF_HD)],
        out_shape=[shp(MLA_HEADS), shp(MLA_HEADS), shp_t(MLA_HEADS, MLA_V),
                   shp(DIFF_HEADS), shp(DIFF_HEADS), shp_t(DIFF_HEADS, 2 * DIFF_HD)],
        compiler_params=_cparams("arbitrary", "arbitrary"),
        name="proj_ab",
    )(x, mod, g, win, qn, kvn, wuq, wuk, wuv, place, tabs)


KEY_CHUNK = 256


SUBLANES = 8
ATTN_QUERY_LANES = 512


def _key_chunks(kv_refs):
    off = 0
    for k_ref, v_ref in kv_refs:
        lk = k_ref.shape[2]
        for c0 in range(0, lk, KEY_CHUNK):
            w = min(KEY_CHUNK, lk - c0)
            yield k_ref, v_ref, c0, w, off
            off += w


def _attn_kernel(*refs, nseg, mode, lam_init):
    q_ref = refs[0]
    kv_refs = [(refs[1 + 2 * s], refs[2 + 2 * s]) for s in range(nseg)]
    o_ref, s0_ref, s1_ref, mm0_ref, mm1_ref = refs[-5:]
    nmap, _, tq = s0_ref.shape
    maps = range(nmap)
    fold = lambda a: a.reshape(a.shape[0] // SUBLANES, SUBLANES, tq)
    t = pl.program_id(0)

    @pl.when(t == 0)
    def _():
        s1_ref[...] = jnp.zeros_like(s1_ref)
        mm1_ref[...] = jnp.zeros_like(mm1_ref)

    def step(s_cur, mm_cur, s_prv, mm_prv):
        q = q_ref[0, 0]
        if mode == "single":
            qms = [q]
        else:
            lane = lax.broadcasted_iota(jnp.int32, q.shape, 1)
            zero = jnp.zeros_like(q)
            qms = [jnp.where(lane < DIFF_HD, q, zero), jnp.where(lane >= DIFF_HD, q, zero)]
        m_prv = [jnp.max(mm_prv[j], axis=0, keepdims=True) for j in maps]
        mms = [jnp.full((SUBLANES, tq), -jnp.inf, F32)] * nmap
        lls = [jnp.zeros((SUBLANES, tq), F32)] * nmap
        dv = kv_refs[0][1].shape[2]
        o_ts = [jnp.zeros((dv, tq), F32)] * nmap
        for k_ref, v_ref, c0, w, off in _key_chunks(kv_refs):
            kc = k_ref[0, 0, c0:c0 + w, :]
            vc = v_ref[0, 0, :, c0:c0 + w]
            for j in maps:
                s = lax.dot_general(kc, qms[j], _NT, preferred_element_type=F32)
                s_cur[j, off:off + w, :] = s
                mms[j] = jnp.maximum(mms[j], jnp.max(fold(s), axis=0))
            for j in maps:
                p = jnp.exp2(s_prv[j, off:off + w, :] - m_prv[j])
                lls[j] = lls[j] + jnp.sum(fold(p), axis=0)
                o_ts[j] = o_ts[j] + _dot(vc, p.astype(BF))
        for j in maps:
            mm_cur[j] = mms[j]
        outs = [(o_ts[j] / jnp.sum(lls[j], axis=0, keepdims=True)).T for j in maps]
        if mode == "single":
            o = outs[0].astype(BF)
            if dv < LANES:
                o = jnp.concatenate([o, jnp.zeros((tq, LANES - dv), BF)], axis=-1)
            o_ref[0, 0] = o
        elif mode == "pair":
            o_ref[0, 0] = jnp.where(lane < DIFF_HD, outs[0], outs[1]).astype(BF)
        else:
            lam_ref, subln_ref = refs[1 + 2 * nseg], refs[2 + 2 * nseg]
            lv = lam_ref[...]
            lam = (jnp.exp(jnp.sum(lv[0:1] * lv[1:2], axis=-1, keepdims=True))
                   - jnp.exp(jnp.sum(lv[2:3] * lv[3:4], axis=-1, keepdims=True)) + lam_init)
            o_ref[0, 0] = (_rms(outs[0] - lam * outs[1], subln_ref[...]) * (1 - lam_init)).astype(BF)

    pl.when(t % 2 == 0)(lambda: step(s0_ref, mm0_ref, s1_ref, mm1_ref))
    pl.when(t % 2 == 1)(lambda: step(s1_ref, mm1_ref, s0_ref, mm0_ref))


def _attention(q, kvs, tq, mode="single", lam_vecs=None, subln=None, lam_init=0.0):
    b, nh, lq, _ = q.shape
    diff = mode == "diff"
    nmap = 1 if mode == "single" else 2
    nq = lq // tq
    ntile = b * nh * nq

    def tile(t):
        t = jnp.clip(t, 0, ntile - 1)
        return t // (nh * nq), (t // nq) % nh, t % nq

    cur_head = lambda t: tile(t)[:2] + (0, 0)
    prv_head = lambda t: tile(t - 1)[:2] + (0, 0)
    in_specs = [pl.BlockSpec((1, 1, tq, LANES), lambda t: tile(t) + (0,))]
    args = [q]
    for k, v in kvs:
        lk = k.shape[2]
        in_specs += [pl.BlockSpec((1, 1, lk, LANES), cur_head), pl.BlockSpec((1, 1, v.shape[2], lk), prv_head)]
        args += [k, v]
    if diff:
        in_specs += [_const_spec(lam_vecs.shape), _const_spec(subln.shape)]
        args += [lam_vecs, subln]
    nkeys = sum(k.shape[2] for k, _ in kvs)
    return pl.pallas_call(
        functools.partial(_attn_kernel, nseg=len(kvs), mode=mode, lam_init=lam_init),
        grid=(ntile + 1,),
        in_specs=in_specs,
        out_specs=pl.BlockSpec((1, 1, tq, LANES), lambda t: tile(t - 1) + (0,)),
        out_shape=jax.ShapeDtypeStruct((b, nh, lq, LANES), BF),
        scratch_shapes=([pltpu.VMEM((nmap, nkeys, tq), F32)] * 2 + [pltpu.VMEM((nmap, SUBLANES, tq), F32)] * 2),
        compiler_params=_cparams("arbitrary"),
        name="attn_diff" if diff else "attn_mla",
    )(*args)


def _outproj_kernel(*refs, n_in, has_bias, tail):
    o_refs = refs[:n_in]
    w_ref = refs[n_in]
    pos = n_in + 1
    b_ref = None
    if has_bias:
        b_ref = refs[pos]
        pos += 1
    x_ref, mod_ref, g_ref = refs[pos:pos + 3]
    rest = refs[pos + 3:]
    o = jnp.concatenate([r[0, h] for r in o_refs for h in range(r.shape[1])], axis=-1)
    y = _dot(o, w_ref[...])
    if has_bias:
        y = y + b_ref[...]
    x1 = x_ref[0] + mod_ref[0, 2:3, :] * _rms(y, g_ref[1:2, :])
    h = _norm_mod(x1, g_ref[2:3, :], mod_ref[0, 3:4, :], mod_ref[0, 4:5, :])
    if tail == "ffn":
        w1_ref, w3_ref, w2_ref, out_ref = rest
        out_ref[0] = x1 + mod_ref[0, 5:6, :] * _rms(_swiglu(h.astype(BF), w1_ref, w3_ref, w2_ref), g_ref[3:4, :])
    else:
        router_ref, tri_ref, out_ref = rest[:3]
        out_ref[0] = x1
        first = (pl.program_id(0) == 0) & (pl.program_id(1) == 0)
        _route(h, first, router_ref, tri_ref, *rest[3:])


def _outproj(os_, w, bias, x, mod, g, tm, tail, tail_args):
    b, t, d = x.shape
    n = b * t
    nt = t // tm
    in_specs = [pl.BlockSpec((1, o.shape[1], tm, LANES), lambda bi, i: (bi, 0, i, 0)) for o in os_]
    in_specs.append(_const_spec(w.shape))
    args = list(os_) + [w]
    if bias is not None:
        in_specs.append(_const_spec(bias.shape))
        args.append(bias)
    x_spec = pl.BlockSpec((1, tm, d), lambda bi, i: (bi, i, 0))
    in_specs += [x_spec, _mod_spec(mod), _const_spec(g.shape)]
    args += [x, mod, g]
    out_specs, out_shape, scratch = x_spec, jax.ShapeDtypeStruct((b, t, d), F32), []
    if tail == "ffn":
        in_specs += [_resident_spec(a.shape, lambda bi, i: (0, 0)) for a in tail_args]
        args += list(tail_args)
    else:
        router, = tail_args
        tri = jnp.asarray(np.tril(np.ones((tm, tm), np.float32), -1), BF)
        in_specs += [_const_spec(router.shape), _const_spec(tri.shape)]
        args += [router, tri]
        rows = lambda wd: pl.BlockSpec((tm, wd), lambda bi, i: (bi * nt + i, 0))
        out_specs = [x_spec, pl.BlockSpec((PACK_CHUNKS, tm, LANES), lambda bi, i: (0, bi * nt + i, 0)),
                     rows(8), rows(8), pl.BlockSpec((1, LANES), lambda bi, i: (0, 0))]
        out_shape = [out_shape, jax.ShapeDtypeStruct((PACK_CHUNKS, n, LANES), jnp.int32),
                     jax.ShapeDtypeStruct((n, 8), jnp.int32), jax.ShapeDtypeStruct((n, 8), F32),
                     jax.ShapeDtypeStruct((1, LANES), F32)]
        scratch = [pltpu.VMEM((1, LANES), F32)]
    return pl.pallas_call(
        functools.partial(_outproj_kernel, n_in=len(os_), has_bias=bias is not None, tail=tail),
        grid=(b, nt),
        in_specs=in_specs,
        out_specs=out_specs,
        out_shape=out_shape,
        scratch_shapes=scratch,
        compiler_params=_cparams("arbitrary", "arbitrary"),
        name="outproj_" + tail,
    )(*args)


MXU_TILE = 256


def _ff_chunks(f):
    tiles = f // MXU_TILE
    if f % MXU_TILE or tiles < 2:
        return [(0, f)]
    first = (tiles + 1) // 2 * MXU_TILE
    return [(0, first), (first, f - first)]


def _swiglu(hn, w1_ref, w3_ref, w2_ref):
    y = None
    for lo, n in _ff_chunks(w1_ref.shape[1]):
        a = _dot(hn, w1_ref[:, lo:lo + n])
        act = (a * jax.nn.sigmoid(a) * _dot(hn, w3_ref[:, lo:lo + n])).astype(BF)
        part = _dot(act, w2_ref[lo:lo + n, :])
        y = part if y is None else y + part
    return y


def _resident_spec(shape, index_map):
    return pl.BlockSpec(shape, index_map, pipeline_mode=pl.Buffered(1))


MOE_ROW_TILE = 512
PACK_CHUNKS = 4
SC_WINDOW = 128
_HI_MASK = -65536
_LO_MASK = 65535


def _pack_rows(v, out_ref):
    half = v.shape[1] // 2
    vb = v.astype(BF).astype(F32)
    lo = (pltpu.bitcast(vb[:, :half], jnp.int32) >> 16) & _LO_MASK
    hi = pltpu.bitcast(vb[:, half:], jnp.int32) & _HI_MASK
    w = lo | hi
    for c in range(PACK_CHUNKS):
        out_ref[c] = w[:, c * LANES:(c + 1) * LANES]


def _unpack_rows(ref):
    w = jnp.concatenate([ref[c] for c in range(PACK_CHUNKS)], axis=-1)
    lo = pltpu.bitcast(w << 16, F32)
    hi = pltpu.bitcast(w & _HI_MASK, F32)
    return jnp.concatenate([lo, hi], axis=-1)


def _route(h, first, router_ref, tri_ref, hp_ref, meta_ref, gate_ref, cnt_ref, carry_ref):
    @pl.when(first)
    def _():
        carry_ref[...] = jnp.zeros_like(carry_ref)

    _pack_rows(h, hp_ref)
    h_hi = h.astype(BF)
    h_lo = (h - h_hi.astype(F32)).astype(BF)
    logits = _dot(h_hi, router_ref[0]) + (_dot(h_hi, router_ref[1]) + _dot(h_lo, router_ref[0]))
    lane = lax.broadcasted_iota(jnp.int32, logits.shape, 1)
    logits = jnp.where(lane < N_EXPERTS, logits, -jnp.inf)
    m1 = jnp.max(logits, axis=-1, keepdims=True)
    i1 = jnp.min(jnp.where(logits == m1, lane, LANES), axis=-1, keepdims=True)
    rest = jnp.where(lane == i1, -jnp.inf, logits)
    m2 = jnp.max(rest, axis=-1, keepdims=True)
    i2 = jnp.min(jnp.where(rest == m2, lane, LANES), axis=-1, keepdims=True)
    e2 = jnp.exp(m2 - m1)
    denom = 1.0 + e2
    assigned = jnp.where((lane == i1) | (lane == i2), 1.0, 0.0)
    ranks = _dot(tri_ref[...], assigned.astype(BF)) + carry_ref[...]
    r1 = jnp.sum(jnp.where(lane == i1, ranks, 0.0), axis=-1, keepdims=True).astype(jnp.int32)
    r2 = jnp.sum(jnp.where(lane == i2, ranks, 0.0), axis=-1, keepdims=True).astype(jnp.int32)
    carry_ref[...] += jnp.sum(assigned, axis=0, keepdims=True)
    cnt_ref[...] = carry_ref[...]
    col = lax.broadcasted_iota(jnp.int32, meta_ref.shape, 1)
    meta_ref[...] = jnp.where(col == 0, i1, jnp.where(col == 1, i2, jnp.where(col == 2, r1, r2)))
    gate_ref[...] = jnp.where(col == 0, 1.0 / denom, e2 / denom)


def _sc_mesh():
    return plsc.VectorSubcoreMesh(core_axis_name="core", subcore_axis_name="subcore")


def _sc_scatter_rows(rows, idx_a, idx_b, n_out):
    nrows = rows.shape[0]

    @pl.kernel(out_type=jax.ShapeDtypeStruct((n_out, LANES), rows.dtype), mesh=_sc_mesh(), scratch_types=[])
    def scatter_kernel(x_hbm, ia_hbm, ib_hbm, o_hbm):
        def body(x_vmem, ia_vmem, ib_vmem):
            pltpu.sync_copy(x_vmem, o_hbm.at[ia_vmem.at[0]])
            pltpu.sync_copy(x_vmem, o_hbm.at[ib_vmem.at[0]])

        pltpu.emit_pipeline(
            body,
            grid=(nrows // SC_WINDOW,),
            in_specs=[pl.BlockSpec((SC_WINDOW, LANES), lambda i: (i, 0)),
                      pl.BlockSpec((1, SC_WINDOW), lambda i: (0, i)),
                      pl.BlockSpec((1, SC_WINDOW), lambda i: (0, i))],
            out_specs=[],
            core_axis_name=("core", "subcore"),
            dimension_semantics=(pltpu.PARALLEL,),
        )(x_hbm, ia_hbm, ib_hbm)

    return scatter_kernel(rows, idx_a, idx_b)


def _sc_gather_rows(table, idx_a, idx_b):
    nrows = idx_a.shape[1]
    out = jax.ShapeDtypeStruct((nrows, LANES), table.dtype)

    @pl.kernel(out_type=(out, out), mesh=_sc_mesh(), scratch_types=[])
    def gather_kernel(t_hbm, ia_hbm, ib_hbm, oa_hbm, ob_hbm):
        def body(ia_vmem, ib_vmem, oa_vmem, ob_vmem):
            pltpu.sync_copy(t_hbm.at[ia_vmem.at[0]], oa_vmem)
            pltpu.sync_copy(t_hbm.at[ib_vmem.at[0]], ob_vmem)

        pltpu.emit_pipeline(
            body,
            grid=(nrows // SC_WINDOW,),
            in_specs=[pl.BlockSpec((1, SC_WINDOW), lambda i: (0, i)),
                      pl.BlockSpec((1, SC_WINDOW), lambda i: (0, i))],
            out_specs=[pl.BlockSpec((SC_WINDOW, LANES), lambda i: (i, 0)),
                       pl.BlockSpec((SC_WINDOW, LANES), lambda i: (i, 0))],
            core_axis_name=("core", "subcore"),
            dimension_semantics=(pltpu.PARALLEL,),
        )(ia_hbm, ib_hbm, oa_hbm, ob_hbm)

    return gather_kernel(table, idx_a, idx_b)


def _experts_kernel(te_ref, tv_ref, xs_ref, w1_ref, w3_ref, w2_ref, ys_ref):
    del te_ref
    valid = tv_ref[pl.program_id(0)]

    @pl.when(valid > 0)
    def _():
        row = lax.broadcasted_iota(jnp.int32, (xs_ref.shape[1], 1), 0)
        hn = jnp.where(row < valid, _unpack_rows(xs_ref), 0.0).astype(BF)
        _pack_rows(_swiglu(hn, w1_ref.at[0], w3_ref.at[0], w2_ref.at[0]), ys_ref)

    @pl.when(valid == 0)
    def _():
        ys_ref[...] = jnp.zeros_like(ys_ref)


def _experts(xs, tile_expert, tile_valid, w1, w3, w2):
    _, p, _ = xs.shape
    tr = MOE_ROW_TILE
    rows = pl.BlockSpec((PACK_CHUNKS, tr, LANES), lambda j, te, tv: (0, j, 0))
    weights = lambda w: pl.BlockSpec((1,) + w.shape[1:], lambda j, te, tv: (te[j], 0, 0))
    return pl.pallas_call(
        _experts_kernel,
        grid_spec=pltpu.PrefetchScalarGridSpec(
            num_scalar_prefetch=2,
            grid=(p // tr,),
            in_specs=[rows, weights(w1), weights(w3), weights(w2)],
            out_specs=rows,
        ),
        out_shape=jax.ShapeDtypeStruct(xs.shape, jnp.int32),
        compiler_params=_cparams("arbitrary"),
        name="moe_experts",
    )(tile_expert, tile_valid, xs, w1, w3, w2)


def _combine_kernel(x_ref, ya_ref, yb_ref, gate_ref, mod_ref, g_ref, out_ref):
    gates = gate_ref[...]
    fx = gates[:, 0:1] * _unpack_rows(ya_ref) + gates[:, 1:2] * _unpack_rows(yb_ref)
    out_ref[0] = x_ref[0] + mod_ref[0, 5:6, :] * _rms(fx, g_ref[3:4, :])


def _combine(x, ya, yb, gates, mod, g, tm):
    b, t, d = x.shape
    nt = t // tm
    packed = pl.BlockSpec((PACK_CHUNKS, tm, LANES), lambda bi, i: (0, bi * nt + i, 0))
    return pl.pallas_call(
        _combine_kernel,
        grid=(b, nt),
        in_specs=[
            pl.BlockSpec((1, tm, d), lambda bi, i: (bi, i, 0)),
            packed, packed,
            pl.BlockSpec((tm, gates.shape[1]), lambda bi, i: (bi * nt + i, 0)),
            _mod_spec(mod),
            _const_spec(g.shape),
        ],
        out_specs=pl.BlockSpec((1, tm, d), lambda bi, i: (bi, i, 0)),
        out_shape=jax.ShapeDtypeStruct((b, t, d), F32),
        compiler_params=_cparams("arbitrary", "arbitrary"),
        name="moe_combine",
    )(x, ya, yb, gates, mod, g)


def _moe(x, routed, mod, g, w1, w3, w2, tm):
    b, t, d = x.shape
    n = b * t
    ne = w1.shape[0]
    tr = MOE_ROW_TILE
    assert d == 2 * PACK_CHUNKS * LANES and (PACK_CHUNKS * n) % (SC_WINDOW * 32) == 0
    hp, meta, gates, counts = routed

    ntile = 2 * n // tr + ne
    p = ntile * tr
    cnt = counts[0, :ne].astype(jnp.int32)
    tiles = (cnt + tr - 1) // tr
    tile_end = jnp.cumsum(tiles)
    tile_start = tile_end - tiles
    base = tile_start * tr
    eids = jnp.arange(ne, dtype=jnp.int32)
    base_of = lambda e: jnp.sum(jnp.where(e[:, None] == eids[None, :], base[None, :], 0), axis=-1)
    pos_a = base_of(meta[:, 0]) + meta[:, 2]
    pos_b = base_of(meta[:, 1]) + meta[:, 3]
    chunk = jnp.arange(PACK_CHUNKS, dtype=jnp.int32)[:, None] * p
    idx_a = (chunk + pos_a[None, :]).reshape(1, PACK_CHUNKS * n)
    idx_b = (chunk + pos_b[None, :]).reshape(1, PACK_CHUNKS * n)
    tj = jnp.arange(ntile, dtype=jnp.int32)
    tile_expert = jnp.minimum(jnp.sum(tj[:, None] >= tile_end[None, :], axis=-1), ne - 1).astype(jnp.int32)
    done = jnp.sum(jnp.where(tile_expert[:, None] == eids[None, :], tile_start[None, :], 0), axis=-1)
    left = jnp.sum(jnp.where(tile_expert[:, None] == eids[None, :], cnt[None, :], 0), axis=-1) - (tj - done) * tr
    tile_valid = jnp.where(tj < tile_end[-1], jnp.clip(left, 0, tr), 0).astype(jnp.int32)

    xs = _sc_scatter_rows(hp.reshape(PACK_CHUNKS * n, LANES), idx_a, idx_b, PACK_CHUNKS * p)
    ys = _experts(xs.reshape(PACK_CHUNKS, p, LANES), tile_expert, tile_valid, w1, w3, w2)
    ya, yb = _sc_gather_rows(ys.reshape(PACK_CHUNKS * p, LANES), idx_a, idx_b)
    shp = (PACK_CHUNKS, n, LANES)
    return _combine(x, ya.reshape(shp), yb.reshape(shp), gates, mod, g, tm)


def _proj_na_kernel(x_ref, mod_ref, g_ref, w_ref, b_ref, q_ref, k_ref, v_ref):
    h = _norm_mod(x_ref[0], g_ref[0:1, :], mod_ref[0, 0:1, :], mod_ref[0, 1:2, :])
    z = _dot(h.astype(BF), w_ref[...]) + b_ref[...]
    n = NA_SLOTS * LANES
    for s in range(NA_SLOTS):
        lo = s * LANES
        q_ref[0, s] = (z[:, lo:lo + LANES] * NA_SCALE).astype(BF)
        k_ref[0, s] = z[:, n + lo:n + lo + LANES].astype(BF)
        v_ref[0, s] = z[:, 2 * n + lo:2 * n + lo + LANES].astype(BF)


def _proj_na(x, mod, g, w, bias, tm):
    b, t, d = x.shape
    tok = pl.BlockSpec((1, NA_SLOTS, tm, LANES), lambda bi, i: (bi, 0, i, 0))
    shp = jax.ShapeDtypeStruct((b, NA_SLOTS, t, LANES), BF)
    return pl.pallas_call(
        _proj_na_kernel,
        grid=(b, t // tm),
        in_specs=[
            pl.BlockSpec((1, tm, d), lambda bi, i: (bi, i, 0)),
            _mod_spec(mod),
            _const_spec(g.shape),
            _const_spec(w.shape),
            _const_spec(bias.shape),
        ],
        out_specs=[tok, tok, tok],
        out_shape=[shp, shp, shp],
        compiler_params=_cparams("arbitrary", "arbitrary"),
        name="proj_na",
    )(x, mod, g, w, bias)


NA_ROWS_PER_STEP = 2
NA_GROUP = 4


def _na_window_start(r, rows):
    return jnp.clip(r - NA_KH // 2, 0, rows - NA_KH)


def _na_kernel(q_ref, kx_ref, vx_ref, kc_ref, vc_ref, *rest, rows):
    bias_refs, o_ref = rest[:NA_ROWS_PER_STEP], rest[NA_ROWS_PER_STEP]
    nwin = NA_KH * GRID_W
    lane = lax.broadcasted_iota(jnp.int32, (GRID_W, LANES), 1)
    problems = [(a, s) for a in range(NA_ROWS_PER_STEP) for s in range(NA_SLOTS)]
    starts = [pl.multiple_of(_na_window_start(pl.program_id(1) * NA_ROWS_PER_STEP + a, rows) * GRID_W, GRID_W)
              for a in range(NA_ROWS_PER_STEP)]
    for g0 in range(0, len(problems), NA_GROUP):
        _na_group(problems[g0:g0 + NA_GROUP], starts, lane, q_ref, kx_ref, vx_ref, kc_ref, vc_ref, bias_refs, o_ref)


def _na_group(work, starts, lane, q_ref, kx_ref, vx_ref, kc_ref, vc_ref, bias_refs, o_ref):
    nwin = NA_KH * GRID_W
    idx = range(len(work))
    sws, scs = [], []
    for a, s in work:
        q = q_ref[0, s, a * GRID_W:(a + 1) * GRID_W, :]
        zero = jnp.zeros_like(q)
        q2 = jnp.concatenate([jnp.where(lane < NA_HD, q, zero), jnp.where(lane >= NA_HD, q, zero)], axis=0)
        kw = kx_ref[0, s, pl.ds(starts[a], nwin), :]
        sws.append(lax.dot_general(q2, kw, _NT, preferred_element_type=F32) + bias_refs[a][0, s])
        scs.append(lax.dot_general(q2, kc_ref[0, s], _NT, preferred_element_type=F32))
    ms = [jnp.maximum(jnp.max(sws[i], axis=-1, keepdims=True), jnp.max(scs[i], axis=-1, keepdims=True))
          for i in idx]
    pws = [jnp.exp2(sws[i] - ms[i]) for i in idx]
    pcs = [jnp.exp2(scs[i] - ms[i]) for i in idx]
    ls = [jnp.sum(pws[i], axis=-1, keepdims=True) + jnp.sum(pcs[i], axis=-1, keepdims=True) for i in idx]
    for i, (a, s) in enumerate(work):
        vw = vx_ref[0, s, pl.ds(starts[a], nwin), :]
        o2 = (_dot(pws[i].astype(BF), vw) + _dot(pcs[i].astype(BF), vc_ref[0, s])) / ls[i]
        o_ref[0, s, a * GRID_W:(a + 1) * GRID_W, :] = jnp.where(lane < NA_HD, o2[:GRID_W], o2[GRID_W:]).astype(BF)


def _na_attention(q, kx, vx, kc, vc, bias):
    b, ns, l, _ = q.shape
    rows = l // GRID_W
    lc = kc.shape[2]
    nr = NA_ROWS_PER_STEP
    assert rows % nr == 0

    def bias_spec(a):
        def index_map(bi, i):
            r = i * nr + a
            return (r - _na_window_start(r, rows), 0, 0, 0)
        return pl.BlockSpec((1,) + bias.shape[1:], index_map)

    full = lambda n: pl.BlockSpec((1, ns, n, LANES), lambda bi, i: (bi, 0, 0, 0))
    row = pl.BlockSpec((1, ns, nr * GRID_W, LANES), lambda bi, i: (bi, 0, i, 0))
    return pl.pallas_call(
        functools.partial(_na_kernel, rows=rows),
        grid=(b, rows // nr),
        in_specs=[row, full(l), full(l), full(lc), full(lc)] + [bias_spec(a) for a in range(nr)],
        out_specs=row,
        out_shape=jax.ShapeDtypeStruct((b, ns, l, LANES), BF),
        compiler_params=_cparams("arbitrary", "arbitrary"),
        name="na_attn",
    )(q, kx, vx, kc, vc, *([bias] * nr))


def _axis_tables(pos, dim):
    inv = ROPE_THETA ** (-jnp.arange(0, dim, 2, dtype=F32) / dim)
    ang = pos.astype(F32)[:, None] * inv[None, :]
    ang = jnp.concatenate([ang, ang], axis=-1)
    return jnp.cos(ang), jnp.sin(ang)


def _rope_tables(n, rope):
    if rope:
        t = jnp.arange(n, dtype=jnp.int32)
        row, col = t // GRID_W, t % GRID_W

        def cs(d):
            cr, sr = _axis_tables(row, d // 2)
            cc, sc = _axis_tables(col, d // 2)
            return jnp.concatenate([cr, cc], axis=-1), jnp.concatenate([sr, sc], axis=-1)

        c32, s32 = cs(MLA_ROPE)
        c64, s64 = cs(DIFF_HD)
    else:
        c32, s32 = jnp.ones((n, MLA_ROPE), F32), jnp.zeros((n, MLA_ROPE), F32)
        c64, s64 = jnp.ones((n, DIFF_HD), F32), jnp.zeros((n, DIFF_HD), F32)
    ones = jnp.ones((n, MLA_NOPE), F32)
    pad = lambda a: jnp.pad(a, ((0, 0), (0, LANES - a.shape[1])))
    cos_q = MLA_SCALE * pad(jnp.concatenate([ones, c32], axis=-1))
    sin_q = MLA_SCALE * pad(jnp.concatenate([jnp.zeros_like(ones), s32], axis=-1))
    return jnp.stack([cos_q, sin_q, pad(c32), pad(s32),
                      jnp.concatenate([c64, c64], axis=-1), jnp.concatenate([s64, s64], axis=-1)])


def _rot_cols(w, d):
    shp = w.shape
    w5 = w.reshape(shp[:-1] + (-1, 2, 2, d // 4))
    r = jnp.concatenate([-w5[..., 1:2, :], w5[..., 0:1, :]], axis=-2)
    return r.reshape(shp)


def _pad_cols(w, n):
    return jnp.pad(w, ((0, 0), (0, n - w.shape[1])))


def _prep_ab_weights(w_in, q_norm, kv_norm, w_uq, w_ukv, w_out):
    o_cq, o_ckv, o_kr = MLA_Q_RANK, MLA_Q_RANK + MLA_KV_RANK, MLA_Q_RANK + MLA_KV_RANK + MLA_ROPE
    o_dq, o_dk = o_kr + _DW, o_kr + 2 * _DW
    w_kr, w_dq, w_dk, w_dv = w_in[:, o_ckv:o_kr], w_in[:, o_kr:o_dq], w_in[:, o_dq:o_dk], w_in[:, o_dk:]
    win = jnp.concatenate([
        w_in[:, :o_ckv], _pad_cols(w_kr, LANES), _pad_cols(_rot_cols(w_kr, MLA_ROPE), LANES),
        w_dq, _rot_cols(w_dq, DIFF_HD), w_dk, _rot_cols(w_dk, DIFF_HD), w_dv], axis=1).astype(BF)
    rq = w_uq.shape[0]
    uq = w_uq.reshape(rq, MLA_HEADS, MLA_NOPE + MLA_ROPE)
    nope, rope = uq[..., :MLA_NOPE], uq[..., MLA_NOPE:]
    zpad = jnp.zeros((rq, MLA_HEADS, LANES - MLA_NOPE - MLA_ROPE), F32)
    main = jnp.concatenate([nope, rope, zpad], axis=-1).reshape(rq, MLA_HEADS * LANES)
    rot = jnp.concatenate([jnp.zeros_like(nope), _rot_cols(rope, MLA_ROPE), zpad], axis=-1)
    wuq = jnp.concatenate([main, rot.reshape(rq, MLA_HEADS * LANES)], axis=1).astype(BF)
    rkv = w_ukv.shape[0]
    ukv = w_ukv.reshape(rkv, MLA_HEADS, MLA_NOPE + MLA_V)
    slot = lambda a: jnp.pad(a, ((0, 0), (0, 0), (0, LANES - a.shape[-1]))).reshape(rkv, MLA_HEADS * LANES)
    wuk = slot(ukv[..., :MLA_NOPE]).astype(BF)
    wuv = slot(ukv[..., MLA_NOPE:]).astype(BF)
    place = np.zeros((LANES, MLA_HEADS * LANES), np.float32)
    for hd in range(MLA_HEADS):
        place[np.arange(MLA_ROPE), hd * LANES + MLA_NOPE + np.arange(MLA_ROPE)] = 1.0
    d = w_out.shape[1]
    wo_a = w_out[:MLA_HEADS * MLA_V].reshape(MLA_HEADS, MLA_V, d)
    wo_a = jnp.pad(wo_a, ((0, 0), (0, LANES - MLA_V), (0, 0))).reshape(MLA_HEADS * LANES, d)
    wo = jnp.concatenate([wo_a, w_out[MLA_HEADS * MLA_V:]], axis=0).astype(BF)
    proj = (win, q_norm.reshape(1, -1), kv_norm.reshape(1, -1), wuq, wuk, wuv, jnp.asarray(place, BF))
    return proj, wo


def _na_bias_table(rpb):
    nh = rpb.shape[0]
    cols = np.arange(GRID_W)
    col_start = np.clip(cols - NA_KW // 2, 0, GRID_W - NA_KW)
    kcol = np.arange(GRID_W)
    valid = (kcol[None, :] >= col_start[:, None]) & (kcol[None, :] < col_start[:, None] + NA_KW)
    dc = kcol[None, :] - cols[:, None] + (NA_KW - 1)
    onehot = (dc[None] == np.arange(2 * NA_KW - 1)[:, None, None]) & valid[None]
    tz = jnp.einsum('hrd,dck->hrck', rpb * LOG2E, jnp.asarray(onehot, F32), precision=lax.Precision.HIGHEST)
    tz = jnp.where(valid[None, None], tz, MASK_VALUE)
    cases = [tz[:, NA_KH - 1 - c:2 * NA_KH - 1 - c] for c in range(NA_KH)]
    tbl = jnp.transpose(jnp.stack(cases), (0, 1, 3, 2, 4))
    return tbl.reshape(NA_KH, nh // 2, 2 * GRID_W, NA_KH * GRID_W)


def _tile(n, pref):
    return pref if n % pref == 0 else n


def kernel(x, c, ctx, c_ctx, w_mod, b_mod, norm_g, a_w_in, a_q_norm, a_kv_norm, a_w_uq, a_w_ukv, b_lambda, b_subln,
           ab_w_out, f_w1, f_w3, f_w2, c_w_qkv, c_b_qkv, c_rpb, c_w_out, c_b_out, m_router, m_w1, m_w3, m_w2):
    b, l, d = x.shape
    lc = ctx.shape[1]
    depth = w_mod.shape[0]
    assert l % GRID_W == 0 and l // GRID_W >= NA_KH

    mod_rows = 16
    cvec = jnp.zeros((mod_rows, d), F32).at[:b].set(c).at[b].set(c_ctx)
    mod = _modulation(cvec, w_mod, b_mod)

    tm_x, tm_c = _tile(l, 512), _tile(lc, 256)
    tq_x, tq_c = _tile(l, ATTN_QUERY_LANES), _tile(lc, ATTN_QUERY_LANES // 2)
    tq2_x = _tile(l, ATTN_QUERY_LANES // 2)
    cs = ctx
    for i in range(depth):
        last = i == depth - 1
        j = i // 2
        mx = mod[i, :b].reshape(b, N_MOD, d)
        mc = mod[i, b].reshape(1, N_MOD, d)
        g = norm_g[i]
        if i % 2 == 0:
            lam_init = 0.8 - 0.6 * math.exp(-0.3 * i)
            proj_w, wo = _prep_ab_weights(a_w_in[j], a_q_norm[j], a_kv_norm[j], a_w_uq[j], a_w_ukv[j], ab_w_out[j])
            qx, kx, vx, dqx, dkx, dvx = _proj_ab(x, mx, g, proj_w, _rope_tables(l, True), tm_x)
            qc, kc, vc, dqc, dkc, dvc = _proj_ab(cs, mc, g, proj_w, _rope_tables(lc, False), tm_c)
            subln = b_subln[j].reshape(1, -1)
            oa = _attention(qx, [(kx, vx), (kc, vc)], tq_x)
            ob = _attention(dqx, [(dkx, dvx), (dkc, dvc)], tq2_x, "diff", b_lambda[j], subln, lam_init)
            ffn_w = (f_w1[j].astype(BF), f_w3[j].astype(BF), f_w2[j].astype(BF))
            x = _outproj([oa, ob], wo, None, x, mx, g, tm_x, "ffn", ffn_w)
            if not last:
                oa = _attention(qc, [(kc, vc)], tq_c)
                ob = _attention(dqc, [(dkc, dvc)], tq_c, "diff", b_lambda[j], subln, lam_init)
                cs = _outproj([oa, ob], wo, None, cs, mc, g, tm_c, "ffn", ffn_w)
        else:
            wqkv = c_w_qkv[j].astype(BF)
            bqkv = c_b_qkv[j].reshape(1, -1)
            wo = c_w_out[j].astype(BF)
            bo = c_b_out[j].reshape(1, -1)
            qx, kx, vx = _proj_na(x, mx, g, wqkv, bqkv, tm_x)
            qc, kc, vc = _proj_na(cs, mc, g, wqkv, bqkv, tm_c)
            o = _na_attention(qx, kx, vx, kc, vc, _na_bias_table(c_rpb[j]))
            r_f32 = _pad_cols(m_router[j], LANES)
            r_hi = r_f32.astype(BF)
            router = (jnp.stack([r_hi, (r_f32 - r_hi.astype(F32)).astype(BF)]),)
            w1, w3, w2 = m_w1[j].astype(BF), m_w3[j].astype(BF), m_w2[j].astype(BF)
            x, *routed = _outproj([o], wo, bo, x, mx, g, tm_x, "router", router)
            x = _moe(x, routed, mx, g, w1, w3, w2, tm_x)
            if not last:
                oc = _attention(qc, [(kc, jnp.swapaxes(vc, 2, 3))], tq_c, "pair")
                cs, *routed = _outproj([oc], wo, bo, cs, mc, g, tm_c, "router", router)
                cs = _moe(cs, routed, mc, g, w1, w3, w2, tm_c)
    return x
```

```python
import functools
import math

import jax
import jax.numpy as jnp
import numpy as np
from jax import lax
from jax.experimental import pallas as pl
from jax.experimental.pallas import tpu as pltpu
from jax.experimental.pallas import tpu_sc as plsc

BF = jnp.bfloat16
F32 = jnp.float32

LANES = 128
VMEM_LIMIT = 56 * 1024 * 1024

GRID_W = 64
EPS = 1e-6
ROPE_THETA = 10000.0
N_MOD = 6

MLA_HEADS = 8
MLA_NOPE = 64
MLA_ROPE = 32
MLA_V = 64
MLA_Q_RANK = 384
MLA_KV_RANK = 256
LOG2E = math.log2(math.e)
MLA_SCALE = (MLA_NOPE + MLA_ROPE) ** -0.5 * LOG2E

DIFF_HEADS = 4
DIFF_HD = 64
DIFF_SCALE = DIFF_HD ** -0.5 * LOG2E

NA_HEADS = 16
NA_HD = 64
NA_KH = 8
NA_KW = 16
NA_SCALE = NA_HD ** -0.5 * LOG2E
NA_SLOTS = NA_HEADS * NA_HD // LANES
MASK_VALUE = -1e30

N_EXPERTS = 8

_NT = (((1,), (1,)), ((), ()))


def _cparams(*sem):
    return pltpu.CompilerParams(dimension_semantics=sem, vmem_limit_bytes=VMEM_LIMIT)


def _dot(a, b):
    return jnp.dot(a, b, preferred_element_type=F32)


def _rms(x, g):
    return x * lax.rsqrt(jnp.mean(x * x, axis=-1, keepdims=True) + EPS) * g


def _norm_mod(x, g, shift, scale):
    return _rms(x, g) * (1 + scale) + shift


def _const_spec(shape):
    return pl.BlockSpec(shape, lambda *_: (0,) * len(shape))


def _mod_spec(mod):
    if mod.shape[0] == 1:
        return pl.BlockSpec((1,) + mod.shape[1:], lambda b, *_: (0, 0, 0))
    return pl.BlockSpec((1,) + mod.shape[1:], lambda b, *_: (b, 0, 0))


def _mod_kernel(c_ref, w_ref, b_ref, o_ref):
    c = c_ref[...]
    sc = c * jax.nn.sigmoid(c)
    o_ref[0] = _dot(sc.astype(BF), w_ref[0].astype(BF)) + b_ref[0]


def _modulation(cvec, w_mod, b_mod):
    depth, d, n = w_mod.shape
    rows = cvec.shape[0]
    return pl.pallas_call(
        _mod_kernel,
        grid=(depth, n // d),
        in_specs=[
            pl.BlockSpec((rows, d), lambda i, j: (0, 0)),
            pl.BlockSpec((1, d, d), lambda i, j: (i, 0, j)),
            pl.BlockSpec((1, 1, d), lambda i, j: (i, 0, j)),
        ],
        out_specs=pl.BlockSpec((1, rows, d), lambda i, j: (i, 0, j)),
        out_shape=jax.ShapeDtypeStruct((depth, rows, n), F32),
        compiler_params=_cparams("arbitrary", "arbitrary"),
        name="modulation",
    )(cvec, w_mod, b_mod.reshape(depth, 1, n))


_Z_CQ = 0
_Z_CKV = MLA_Q_RANK
_Z_KR = _Z_CKV + MLA_KV_RANK
_Z_KR_ROT = _Z_KR + LANES
_Z_DQ = _Z_KR_ROT + LANES
_DW = 2 * DIFF_HEADS * DIFF_HD
_Z_DQ_ROT = _Z_DQ + _DW
_Z_DK = _Z_DQ_ROT + _DW
_Z_DK_ROT = _Z_DK + _DW
_Z_DV = _Z_DK_ROT + _DW
_Z_END = _Z_DV + _DW


def _proj_ab_kernel(x_ref, mod_ref, g_ref, win_ref, qn_ref, kvn_ref, wuq_ref, wuk_ref, wuv_ref,
                    place_ref, tab_ref, q_ref, k_ref, v_ref, dq_ref, dk_ref, dv_ref):
    h = _norm_mod(x_ref[0], g_ref[0:1, :], mod_ref[0, 0:1, :], mod_ref[0, 1:2, :])
    z = _dot(h.astype(BF), win_ref[...])
    cqn = _rms(z[:, _Z_CQ:_Z_CKV], qn_ref[...]).astype(BF)
    ckvn = _rms(z[:, _Z_CKV:_Z_KR], kvn_ref[...]).astype(BF)
    q2 = _dot(cqn, wuq_ref[...])
    nq = MLA_HEADS * LANES
    cos_q, sin_q = tab_ref[0], tab_ref[1]
    for hd in range(MLA_HEADS):
        lo = hd * LANES
        q_ref[0, hd] = (q2[:, lo:lo + LANES] * cos_q + q2[:, nq + lo:nq + lo + LANES] * sin_q).astype(BF)
    kr = (z[:, _Z_KR:_Z_KR_ROT] * tab_ref[2] + z[:, _Z_KR_ROT:_Z_DQ] * tab_ref[3]).astype(BF)
    kk = _dot(ckvn, wuk_ref[...]) + _dot(kr, place_ref[...])
    vv = _dot(ckvn, wuv_ref[...])
    for hd in range(MLA_HEADS):
        lo = hd * LANES
        k_ref[0, hd] = kk[:, lo:lo + LANES].astype(BF)
        v_ref[0, hd] = vv[:, lo:lo + LANES].T.astype(BF)
    cos_d, sin_d = tab_ref[4], tab_ref[5]
    for hd in range(DIFF_HEADS):
        lo = hd * LANES
        dq = z[:, _Z_DQ + lo:_Z_DQ + lo + LANES] * cos_d + z[:, _Z_DQ_ROT + lo:_Z_DQ_ROT + lo + LANES] * sin_d
        dq_ref[0, hd] = (dq * DIFF_SCALE).astype(BF)
        dk = z[:, _Z_DK + lo:_Z_DK + lo + LANES] * cos_d + z[:, _Z_DK_ROT + lo:_Z_DK_ROT + lo + LANES] * sin_d
        dk_ref[0, hd] = dk.astype(BF)
        dv_ref[0, hd] = z[:, _Z_DV + lo:_Z_DV + lo + LANES].T.astype(BF)


def _proj_ab(x, mod, g, wts, tabs, tm):
    b, t, d = x.shape
    win, qn, kvn, wuq, wuk, wuv, place = wts
    tok = lambda hh: pl.BlockSpec((1, hh, tm, LANES), lambda bi, i: (bi, 0, i, 0))
    shp = lambda hh: jax.ShapeDtypeStruct((b, hh, t, LANES), BF)
    tok_t = lambda hh, dv: pl.BlockSpec((1, hh, dv, tm), lambda bi, i: (bi, 0, 0, i))
    shp_t = lambda hh, dv: jax.ShapeDtypeStruct((b, hh, dv, t), BF)
    return pl.pallas_call(
        _proj_ab_kernel,
        grid=(b, t // tm),
        in_specs=[
            pl.BlockSpec((1, tm, d), lambda bi, i: (bi, i, 0)),
            _mod_spec(mod),
            _const_spec(g.shape),
            _const_spec(win.shape), _const_spec(qn.shape), _const_spec(kvn.shape),
            _const_spec(wuq.shape), _const_spec(wuk.shape), _const_spec(wuv.shape),
            _const_spec(place.shape),
            pl.BlockSpec((6, tm, LANES), lambda bi, i: (0, i, 0)),
        ],
        out_specs=[tok(MLA_HEADS), tok(MLA_HEADS), tok_t(MLA_HEADS, LANES),
                   tok(DIFF_HEADS), tok(DIFF_HEADS), tok_t(DIFF_HEADS, LANES)],
        out_shape=[shp(MLA_HEADS), shp(MLA_HEADS), shp_t(MLA_HEADS, LANES),
                   shp(DIFF_HEADS), shp(DIFF_HEADS), shp_t(DIFF_HEADS, LANES)],
        compiler_params=_cparams("arbitrary", "arbitrary"),
        name="proj_ab",
    )(x, mod, g, win, qn, kvn, wuq, wuk, wuv, place, tabs)


KEY_CHUNK = 256


SUBLANES = 8
ATTN_QUERY_LANES = 1024


def _key_chunks(kv_refs):
    off = 0
    for k_ref, v_ref in kv_refs:
        lk = k_ref.shape[2]
        for c0 in range(0, lk, KEY_CHUNK):
            w = min(KEY_CHUNK, lk - c0)
            yield k_ref, v_ref, c0, w, off
            off += w


def _attn_kernel(*refs, nseg, mode, lam_init):
    q_ref = refs[0]
    kv_refs = [(refs[1 + 2 * s], refs[2 + 2 * s]) for s in range(nseg)]
    o_ref, s0_ref, s1_ref, mm0_ref, mm1_ref = refs[-5:]
    nmap, _, tq = s0_ref.shape
    maps = range(nmap)
    fold = lambda a: a.reshape(a.shape[0] // SUBLANES, SUBLANES, tq)
    t = pl.program_id(0)

    @pl.when(t == 0)
    def _():
        s1_ref[...] = jnp.zeros_like(s1_ref)
        mm1_ref[...] = jnp.zeros_like(mm1_ref)

    def step(s_cur, mm_cur, s_prv, mm_prv):
        q = q_ref[0, 0]
        if mode == "single":
            qms = [q]
        else:
            lane = lax.broadcasted_iota(jnp.int32, q.shape, 1)
            zero = jnp.zeros_like(q)
            qms = [jnp.where(lane < DIFF_HD, q, zero), jnp.where(lane >= DIFF_HD, q, zero)]
        m_prv = [jnp.max(mm_prv[j], axis=0, keepdims=True) for j in maps]
        mms = [jnp.full((SUBLANES, tq), -jnp.inf, F32)] * nmap
        lls = [jnp.zeros((SUBLANES, tq), F32)] * nmap
        dv = kv_refs[0][1].shape[2]
        o_ts = [jnp.zeros((dv, tq), F32)] * nmap
        for k_ref, v_ref, c0, w, off in _key_chunks(kv_refs):
            kc = k_ref[0, 0, c0:c0 + w, :]
            vc = v_ref[0, 0, :, c0:c0 + w]
            for j in maps:
                s = lax.dot_general(kc, qms[j], _NT, preferred_element_type=F32)
                s_cur[j, off:off + w, :] = s
                mms[j] = jnp.maximum(mms[j], jnp.max(fold(s), axis=0))
            for j in maps:
                p = jnp.exp2(s_prv[j, off:off + w, :] - m_prv[j])
                lls[j] = lls[j] + jnp.sum(fold(p), axis=0)
                o_ts[j] = o_ts[j] + _dot(vc, p.astype(BF))
        for j in maps:
            mm_cur[j] = mms[j]
        outs = [(o_ts[j] / jnp.sum(lls[j], axis=0, keepdims=True)).T for j in maps]
        if mode == "single":
            o = outs[0].astype(BF)
            if dv < LANES:
                o = jnp.concatenate([o, jnp.zeros((tq, LANES - dv), BF)], axis=-1)
            o_ref[0, 0] = o
        elif mode == "pair":
            o_ref[0, 0] = jnp.where(lane < DIFF_HD, outs[0], outs[1]).astype(BF)
        else:
            lam_ref, subln_ref = refs[1 + 2 * nseg], refs[2 + 2 * nseg]
            lv = lam_ref[...]
            lam = (jnp.exp(jnp.sum(lv[0:1] * lv[1:2], axis=-1, keepdims=True))
                   - jnp.exp(jnp.sum(lv[2:3] * lv[3:4], axis=-1, keepdims=True)) + lam_init)
            o_ref[0, 0] = (_rms(outs[0] - lam * outs[1], subln_ref[...]) * (1 - lam_init)).astype(BF)

    pl.when(t % 2 == 0)(lambda: step(s0_ref, mm0_ref, s1_ref, mm1_ref))
    pl.when(t % 2 == 1)(lambda: step(s1_ref, mm1_ref, s0_ref, mm0_ref))


def _attention(q, kvs, tq, mode="single", lam_vecs=None, subln=None, lam_init=0.0):
    b, nh, lq, _ = q.shape
    diff = mode == "diff"
    nmap = 1 if mode == "single" else 2
    nq = lq // tq
    ntile = b * nh * nq

    def tile(t):
        t = jnp.clip(t, 0, ntile - 1)
        return t // (nh * nq), (t // nq) % nh, t % nq

    cur_head = lambda t: tile(t)[:2] + (0, 0)
    prv_head = lambda t: tile(t - 1)[:2] + (0, 0)
    in_specs = [pl.BlockSpec((1, 1, tq, LANES), lambda t: tile(t) + (0,))]
    args = [q]
    for k, v in kvs:
        lk = k.shape[2]
        in_specs += [pl.BlockSpec((1, 1, lk, LANES), cur_head), pl.BlockSpec((1, 1, v.shape[2], lk), prv_head)]
        args += [k, v]
    if diff:
        in_specs += [_const_spec(lam_vecs.shape), _const_spec(subln.shape)]
        args += [lam_vecs, subln]
    nkeys = sum(k.shape[2] for k, _ in kvs)
    return pl.pallas_call(
        functools.partial(_attn_kernel, nseg=len(kvs), mode=mode, lam_init=lam_init),
        grid=(ntile + 1,),
        in_specs=in_specs,
        out_specs=pl.BlockSpec((1, 1, tq, LANES), lambda t: tile(t - 1) + (0,)),
        out_shape=jax.ShapeDtypeStruct((b, nh, lq, LANES), BF),
        scratch_shapes=([pltpu.VMEM((nmap, nkeys, tq), F32)] * 2 + [pltpu.VMEM((nmap, SUBLANES, tq), F32)] * 2),
        compiler_params=_cparams("arbitrary"),
        name="attn_diff" if diff else "attn_mla",
    )(*args)


def _outproj_kernel(*refs, n_in, has_bias, tail):
    o_refs = refs[:n_in]
    w_ref = refs[n_in]
    pos = n_in + 1
    b_ref = None
    if has_bias:
        b_ref = refs[pos]
        pos += 1
    x_ref, mod_ref, g_ref = refs[pos:pos + 3]
    rest = refs[pos + 3:]
    o = jnp.concatenate([r[0, h] for r in o_refs for h in range(r.shape[1])], axis=-1)
    y = _dot(o, w_ref[...])
    if has_bias:
        y = y + b_ref[...]
    x1 = x_ref[0] + mod_ref[0, 2:3, :] * _rms(y, g_ref[1:2, :])
    h = _norm_mod(x1, g_ref[2:3, :], mod_ref[0, 3:4, :], mod_ref[0, 4:5, :])
    if tail == "ffn":
        w1_ref, w3_ref, w2_ref, out_ref = rest
        out_ref[0] = x1 + mod_ref[0, 5:6, :] * _rms(_swiglu(h.astype(BF), w1_ref, w3_ref, w2_ref), g_ref[3:4, :])
    else:
        router_ref, tri_ref, out_ref = rest[:3]
        out_ref[0] = x1
        first = (pl.program_id(0) == 0) & (pl.program_id(1) == 0)
        _route(h, first, router_ref, tri_ref, *rest[3:])


def _outproj(os_, w, bias, x, mod, g, tm, tail, tail_args):
    b, t, d = x.shape
    n = b * t
    nt = t // tm
    in_specs = [pl.BlockSpec((1, o.shape[1], tm, LANES), lambda bi, i: (bi, 0, i, 0)) for o in os_]
    in_specs.append(_const_spec(w.shape))
    args = list(os_) + [w]
    if bias is not None:
        in_specs.append(_const_spec(bias.shape))
        args.append(bias)
    x_spec = pl.BlockSpec((1, tm, d), lambda bi, i: (bi, i, 0))
    in_specs += [x_spec, _mod_spec(mod), _const_spec(g.shape)]
    args += [x, mod, g]
    out_specs, out_shape, scratch = x_spec, jax.ShapeDtypeStruct((b, t, d), F32), []
    if tail == "ffn":
        in_specs += [_resident_spec(a.shape, lambda bi, i: (0, 0)) for a in tail_args]
        args += list(tail_args)
    else:
        router, = tail_args
        tri = jnp.asarray(np.tril(np.ones((tm, tm), np.float32), -1), BF)
        in_specs += [_const_spec(router.shape), _const_spec(tri.shape)]
        args += [router, tri]
        rows = lambda wd: pl.BlockSpec((tm, wd), lambda bi, i: (bi * nt + i, 0))
        out_specs = [x_spec, pl.BlockSpec((PACK_CHUNKS, tm, LANES), lambda bi, i: (0, bi * nt + i, 0)),
                     rows(8), rows(8), pl.BlockSpec((1, LANES), lambda bi, i: (0, 0))]
        out_shape = [out_shape, jax.ShapeDtypeStruct((PACK_CHUNKS, n, LANES), jnp.int32),
                     jax.ShapeDtypeStruct((n, 8), jnp.int32), jax.ShapeDtypeStruct((n, 8), F32),
                     jax.ShapeDtypeStruct((1, LANES), F32)]
        scratch = [pltpu.VMEM((1, LANES), F32)]
    return pl.pallas_call(
        functools.partial(_outproj_kernel, n_in=len(os_), has_bias=bias is not None, tail=tail),
        grid=(b, nt),
        in_specs=in_specs,
        out_specs=out_specs,
        out_shape=out_shape,
        scratch_shapes=scratch,
        compiler_params=_cparams("arbitrary", "arbitrary"),
        name="outproj_" + tail,
    )(*args)


MXU_TILE = 256


def _ff_chunks(f):
    tiles = f // MXU_TILE
    if f % MXU_TILE or tiles < 2:
        return [(0, f)]
    first = (tiles + 1) // 2 * MXU_TILE
    return [(0, first), (first, f - first)]


def _swiglu(hn, w1_ref, w3_ref, w2_ref):
    y = None
    for lo, n in _ff_chunks(w1_ref.shape[1]):
        a = _dot(hn, w1_ref[:, lo:lo + n])
        act = (a * jax.nn.sigmoid(a) * _dot(hn, w3_ref[:, lo:lo + n])).astype(BF)
        part = _dot(act, w2_ref[lo:lo + n, :])
        y = part if y is None else y + part
    return y


def _resident_spec(shape, index_map):
    return pl.BlockSpec(shape, index_map, pipeline_mode=pl.Buffered(1))


MOE_ROW_TILE = 512
PACK_CHUNKS = 4
SC_WINDOW = 128
_HI_MASK = -65536
_LO_MASK = 65535


def _pack_rows(v, out_ref):
    half = v.shape[1] // 2
    vb = v.astype(BF).astype(F32)
    lo = (pltpu.bitcast(vb[:, :half], jnp.int32) >> 16) & _LO_MASK
    hi = pltpu.bitcast(vb[:, half:], jnp.int32) & _HI_MASK
    w = lo | hi
    for c in range(PACK_CHUNKS):
        out_ref[c] = w[:, c * LANES:(c + 1) * LANES]


def _unpack_rows(ref):
    w = jnp.concatenate([ref[c] for c in range(PACK_CHUNKS)], axis=-1)
    lo = pltpu.bitcast(w << 16, F32)
    hi = pltpu.bitcast(w & _HI_MASK, F32)
    return jnp.concatenate([lo, hi], axis=-1)


def _route(h, first, router_ref, tri_ref, hp_ref, meta_ref, gate_ref, cnt_ref, carry_ref):
    @pl.when(first)
    def _():
        carry_ref[...] = jnp.zeros_like(carry_ref)

    _pack_rows(h, hp_ref)
    h_hi = h.astype(BF)
    h_lo = (h - h_hi.astype(F32)).astype(BF)
    logits = _dot(h_hi, router_ref[0]) + (_dot(h_hi, router_ref[1]) + _dot(h_lo, router_ref[0]))
    lane = lax.broadcasted_iota(jnp.int32, logits.shape, 1)
    logits = jnp.where(lane < N_EXPERTS, logits, -jnp.inf)
    m1 = jnp.max(logits, axis=-1, keepdims=True)
    i1 = jnp.min(jnp.where(logits == m1, lane, LANES), axis=-1, keepdims=True)
    rest = jnp.where(lane == i1, -jnp.inf, logits)
    m2 = jnp.max(rest, axis=-1, keepdims=True)
    i2 = jnp.min(jnp.where(rest == m2, lane, LANES), axis=-1, keepdims=True)
    e2 = jnp.exp(m2 - m1)
    denom = 1.0 + e2
    assigned = jnp.where((lane == i1) | (lane == i2), 1.0, 0.0)
    ranks = _dot(tri_ref[...], assigned.astype(BF)) + carry_ref[...]
    r1 = jnp.sum(jnp.where(lane == i1, ranks, 0.0), axis=-1, keepdims=True).astype(jnp.int32)
    r2 = jnp.sum(jnp.where(lane == i2, ranks, 0.0), axis=-1, keepdims=True).astype(jnp.int32)
    carry_ref[...] += jnp.sum(assigned, axis=0, keepdims=True)
    cnt_ref[...] = carry_ref[...]
    col = lax.broadcasted_iota(jnp.int32, meta_ref.shape, 1)
    meta_ref[...] = jnp.where(col == 0, i1, jnp.where(col == 1, i2, jnp.where(col == 2, r1, r2)))
    gate_ref[...] = jnp.where(col == 0, 1.0 / denom, e2 / denom)


def _sc_mesh():
    return plsc.VectorSubcoreMesh(core_axis_name="core", subcore_axis_name="subcore")


def _sc_scatter_rows(rows, idx_a, idx_b, n_out):
    nrows = rows.shape[0]

    @pl.kernel(out_type=jax.ShapeDtypeStruct((n_out, LANES), rows.dtype), mesh=_sc_mesh(), scratch_types=[])
    def scatter_kernel(x_hbm, ia_hbm, ib_hbm, o_hbm):
        def body(x_vmem, ia_vmem, ib_vmem):
            pltpu.sync_copy(x_vmem, o_hbm.at[ia_vmem.at[0]])
            pltpu.sync_copy(x_vmem, o_hbm.at[ib_vmem.at[0]])

        pltpu.emit_pipeline(
            body,
            grid=(nrows // SC_WINDOW,),
            in_specs=[pl.BlockSpec((SC_WINDOW, LANES), lambda i: (i, 0)),
                      pl.BlockSpec((1, SC_WINDOW), lambda i: (0, i)),
                      pl.BlockSpec((1, SC_WINDOW), lambda i: (0, i))],
            out_specs=[],
            core_axis_name=("core", "subcore"),
            dimension_semantics=(pltpu.PARALLEL,),
        )(x_hbm, ia_hbm, ib_hbm)

    return scatter_kernel(rows, idx_a, idx_b)


def _sc_gather_rows(table, idx_a, idx_b):
    nrows = idx_a.shape[1]
    out = jax.ShapeDtypeStruct((nrows, LANES), table.dtype)

    @pl.kernel(out_type=(out, out), mesh=_sc_mesh(), scratch_types=[])
    def gather_kernel(t_hbm, ia_hbm, ib_hbm, oa_hbm, ob_hbm):
        def body(ia_vmem, ib_vmem, oa_vmem, ob_vmem):
            pltpu.sync_copy(t_hbm.at[ia_vmem.at[0]], oa_vmem)
            pltpu.sync_copy(t_hbm.at[ib_vmem.at[0]], ob_vmem)

        pltpu.emit_pipeline(
            body,
            grid=(nrows // SC_WINDOW,),
            in_specs=[pl.BlockSpec((1, SC_WINDOW), lambda i: (0, i)),
                      pl.BlockSpec((1, SC_WINDOW), lambda i: (0, i))],
            out_specs=[pl.BlockSpec((SC_WINDOW, LANES), lambda i: (i, 0)),
                       pl.BlockSpec((SC_WINDOW, LANES), lambda i: (i, 0))],
            core_axis_name=("core", "subcore"),
            dimension_semantics=(pltpu.PARALLEL,),
        )(ia_hbm, ib_hbm, oa_hbm, ob_hbm)

    return gather_kernel(table, idx_a, idx_b)


def _experts_kernel(te_ref, tv_ref, xs_ref, w1_ref, w3_ref, w2_ref, ys_ref):
    del te_ref
    valid = tv_ref[pl.program_id(0)]

    @pl.when(valid > 0)
    def _():
        row = lax.broadcasted_iota(jnp.int32, (xs_ref.shape[1], 1), 0)
        hn = jnp.where(row < valid, _unpack_rows(xs_ref), 0.0).astype(BF)
        _pack_rows(_swiglu(hn, w1_ref.at[0], w3_ref.at[0], w2_ref.at[0]), ys_ref)

    @pl.when(valid == 0)
    def _():
        ys_ref[...] = jnp.zeros_like(ys_ref)


def _experts(xs, tile_expert, tile_valid, w1, w3, w2):
    _, p, _ = xs.shape
    tr = MOE_ROW_TILE
    rows = pl.BlockSpec((PACK_CHUNKS, tr, LANES), lambda j, te, tv: (0, j, 0))
    weights = lambda w: pl.BlockSpec((1,) + w.shape[1:], lambda j, te, tv: (te[j], 0, 0))
    return pl.pallas_call(
        _experts_kernel,
        grid_spec=pltpu.PrefetchScalarGridSpec(
            num_scalar_prefetch=2,
            grid=(p // tr,),
            in_specs=[rows, weights(w1), weights(w3), weights(w2)],
            out_specs=rows,
        ),
        out_shape=jax.ShapeDtypeStruct(xs.shape, jnp.int32),
        compiler_params=_cparams("arbitrary"),
        name="moe_experts",
    )(tile_expert, tile_valid, xs, w1, w3, w2)


def _combine_kernel(x_ref, ya_ref, yb_ref, gate_ref, mod_ref, g_ref, out_ref):
    gates = gate_ref[...]
    fx = gates[:, 0:1] * _unpack_rows(ya_ref) + gates[:, 1:2] * _unpack_rows(yb_ref)
    out_ref[0] = x_ref[0] + mod_ref[0, 5:6, :] * _rms(fx, g_ref[3:4, :])


def _combine(x, ya, yb, gates, mod, g, tm):
    b, t, d = x.shape
    nt = t // tm
    packed = pl.BlockSpec((PACK_CHUNKS, tm, LANES), lambda bi, i: (0, bi * nt + i, 0))
    return pl.pallas_call(
        _combine_kernel,
        grid=(b, nt),
        in_specs=[
            pl.BlockSpec((1, tm, d), lambda bi, i: (bi, i, 0)),
            packed, packed,
            pl.BlockSpec((tm, gates.shape[1]), lambda bi, i: (bi * nt + i, 0)),
            _mod_spec(mod),
            _const_spec(g.shape),
        ],
        out_specs=pl.BlockSpec((1, tm, d), lambda bi, i: (bi, i, 0)),
        out_shape=jax.ShapeDtypeStruct((b, t, d), F32),
        compiler_params=_cparams("arbitrary", "arbitrary"),
        name="moe_combine",
    )(x, ya, yb, gates, mod, g)


def _moe(x, routed, mod, g, w1, w3, w2, tm):
    b, t, d = x.shape
    n = b * t
    ne = w1.shape[0]
    tr = MOE_ROW_TILE
    assert d == 2 * PACK_CHUNKS * LANES and (PACK_CHUNKS * n) % (SC_WINDOW * 32) == 0
    hp, meta, gates, counts = routed

    ntile = 2 * n // tr + ne
    p = ntile * tr
    cnt = counts[0, :ne].astype(jnp.int32)
    tiles = (cnt + tr - 1) // tr
    tile_end = jnp.cumsum(tiles)
    tile_start = tile_end - tiles
    base = tile_start * tr
    eids = jnp.arange(ne, dtype=jnp.int32)
    base_of = lambda e: jnp.sum(jnp.where(e[:, None] == eids[None, :], base[None, :], 0), axis=-1)
    pos_a = base_of(meta[:, 0]) + meta[:, 2]
    pos_b = base_of(meta[:, 1]) + meta[:, 3]
    chunk = jnp.arange(PACK_CHUNKS, dtype=jnp.int32)[:, None] * p
    idx_a = (chunk + pos_a[None, :]).reshape(1, PACK_CHUNKS * n)
    idx_b = (chunk + pos_b[None, :]).reshape(1, PACK_CHUNKS * n)
    tj = jnp.arange(ntile, dtype=jnp.int32)
    tile_expert = jnp.minimum(jnp.sum(tj[:, None] >= tile_end[None, :], axis=-1), ne - 1).astype(jnp.int32)
    done = jnp.sum(jnp.where(tile_expert[:, None] == eids[None, :], tile_start[None, :], 0), axis=-1)
    left = jnp.sum(jnp.where(tile_expert[:, None] == eids[None, :], cnt[None, :], 0), axis=-1) - (tj - done) * tr
    tile_valid = jnp.where(tj < tile_end[-1], jnp.clip(left, 0, tr), 0).astype(jnp.int32)

    xs = _sc_scatter_rows(hp.reshape(PACK_CHUNKS * n, LANES), idx_a, idx_b, PACK_CHUNKS * p)
    ys = _experts(xs.reshape(PACK_CHUNKS, p, LANES), tile_expert, tile_valid, w1, w3, w2)
    ya, yb = _sc_gather_rows(ys.reshape(PACK_CHUNKS * p, LANES), idx_a, idx_b)
    shp = (PACK_CHUNKS, n, LANES)
    return _combine(x, ya.reshape(shp), yb.reshape(shp), gates, mod, g, tm)


def _proj_na_kernel(x_ref, mod_ref, g_ref, w_ref, b_ref, q_ref, k_ref, v_ref):
    h = _norm_mod(x_ref[0], g_ref[0:1, :], mod_ref[0, 0:1, :], mod_ref[0, 1:2, :])
    z = _dot(h.astype(BF), w_ref[...]) + b_ref[...]
    n = NA_SLOTS * LANES
    for s in range(NA_SLOTS):
        lo = s * LANES
        q_ref[0, s] = (z[:, lo:lo + LANES] * NA_SCALE).astype(BF)
        k_ref[0, s] = z[:, n + lo:n + lo + LANES].astype(BF)
        v_ref[0, s] = z[:, 2 * n + lo:2 * n + lo + LANES].astype(BF)


def _proj_na(x, mod, g, w, bias, tm):
    b, t, d = x.shape
    tok = pl.BlockSpec((1, NA_SLOTS, tm, LANES), lambda bi, i: (bi, 0, i, 0))
    shp = jax.ShapeDtypeStruct((b, NA_SLOTS, t, LANES), BF)
    return pl.pallas_call(
        _proj_na_kernel,
        grid=(b, t // tm),
        in_specs=[
            pl.BlockSpec((1, tm, d), lambda bi, i: (bi, i, 0)),
            _mod_spec(mod),
            _const_spec(g.shape),
            _const_spec(w.shape),
            _const_spec(bias.shape),
        ],
        out_specs=[tok, tok, tok],
        out_shape=[shp, shp, shp],
        compiler_params=_cparams("arbitrary", "arbitrary"),
        name="proj_na",
    )(x, mod, g, w, bias)


NA_ROWS_PER_STEP = 2
NA_GROUP = 4


def _na_window_start(r, rows):
    return jnp.clip(r - NA_KH // 2, 0, rows - NA_KH)


def _na_kernel(q_ref, kx_ref, vx_ref, kc_ref, vc_ref, *rest, rows):
    bias_refs, o_ref = rest[:NA_ROWS_PER_STEP], rest[NA_ROWS_PER_STEP]
    nwin = NA_KH * GRID_W
    lane = lax.broadcasted_iota(jnp.int32, (GRID_W, LANES), 1)
    problems = [(a, s) for a in range(NA_ROWS_PER_STEP) for s in range(NA_SLOTS)]
    starts = [pl.multiple_of(_na_window_start(pl.program_id(1) * NA_ROWS_PER_STEP + a, rows) * GRID_W, GRID_W)
              for a in range(NA_ROWS_PER_STEP)]
    for g0 in range(0, len(problems), NA_GROUP):
        _na_group(problems[g0:g0 + NA_GROUP], starts, lane, q_ref, kx_ref, vx_ref, kc_ref, vc_ref, bias_refs, o_ref)


def _na_group(work, starts, lane, q_ref, kx_ref, vx_ref, kc_ref, vc_ref, bias_refs, o_ref):
    nwin = NA_KH * GRID_W
    idx = range(len(work))
    sws, scs = [], []
    for a, s in work:
        q = q_ref[0, s, a * GRID_W:(a + 1) * GRID_W, :]
        zero = jnp.zeros_like(q)
        q2 = jnp.concatenate([jnp.where(lane < NA_HD, q, zero), jnp.where(lane >= NA_HD, q, zero)], axis=0)
        kw = kx_ref[0, s, pl.ds(starts[a], nwin), :]
        sws.append(lax.dot_general(q2, kw, _NT, preferred_element_type=F32) + bias_refs[a][0, s])
        scs.append(lax.dot_general(q2, kc_ref[0, s], _NT, preferred_element_type=F32))
    ms = [jnp.maximum(jnp.max(sws[i], axis=-1, keepdims=True), jnp.max(scs[i], axis=-1, keepdims=True))
          for i in idx]
    pws = [jnp.exp2(sws[i] - ms[i]) for i in idx]
    pcs = [jnp.exp2(scs[i] - ms[i]) for i in idx]
    ls = [jnp.sum(pws[i], axis=-1, keepdims=True) + jnp.sum(pcs[i], axis=-1, keepdims=True) for i in idx]
    for i, (a, s) in enumerate(work):
        vw = vx_ref[0, s, pl.ds(starts[a], nwin), :]
        o2 = (_dot(pws[i].astype(BF), vw) + _dot(pcs[i].astype(BF), vc_ref[0, s])) / ls[i]
        o_ref[0, s, a * GRID_W:(a + 1) * GRID_W, :] = jnp.where(lane < NA_HD, o2[:GRID_W], o2[GRID_W:]).astype(BF)


def _na_attention(q, kx, vx, kc, vc, bias):
    b, ns, l, _ = q.shape
    rows = l // GRID_W
    lc = kc.shape[2]
    nr = NA_ROWS_PER_STEP
    assert rows % nr == 0

    def bias_spec(a):
        def index_map(bi, i):
            r = i * nr + a
            return (r - _na_window_start(r, rows), 0, 0, 0)
        return pl.BlockSpec((1,) + bias.shape[1:], index_map)

    full = lambda n: pl.BlockSpec((1, ns, n, LANES), lambda bi, i: (bi, 0, 0, 0))
    row = pl.BlockSpec((1, ns, nr * GRID_W, LANES), lambda bi, i: (bi, 0, i, 0))
    return pl.pallas_call(
        functools.partial(_na_kernel, rows=rows),
        grid=(b, rows // nr),
        in_specs=[row, full(l), full(l), full(lc), full(lc)] + [bias_spec(a) for a in range(nr)],
        out_specs=row,
        out_shape=jax.ShapeDtypeStruct((b, ns, l, LANES), BF),
        compiler_params=_cparams("arbitrary", "arbitrary"),
        name="na_attn",
    )(q, kx, vx, kc, vc, *([bias] * nr))


def _axis_tables(pos, dim):
    inv = ROPE_THETA ** (-jnp.arange(0, dim, 2, dtype=F32) / dim)
    ang = pos.astype(F32)[:, None] * inv[None, :]
    ang = jnp.concatenate([ang, ang], axis=-1)
    return jnp.cos(ang), jnp.sin(ang)


def _rope_tables(n, rope):
    if rope:
        t = jnp.arange(n, dtype=jnp.int32)
        row, col = t // GRID_W, t % GRID_W

        def cs(d):
            cr, sr = _axis_tables(row, d // 2)
            cc, sc = _axis_tables(col, d // 2)
            return jnp.concatenate([cr, cc], axis=-1), jnp.concatenate([sr, sc], axis=-1)

        c32, s32 = cs(MLA_ROPE)
        c64, s64 = cs(DIFF_HD)
    else:
        c32, s32 = jnp.ones((n, MLA_ROPE), F32), jnp.zeros((n, MLA_ROPE), F32)
        c64, s64 = jnp.ones((n, DIFF_HD), F32), jnp.zeros((n, DIFF_HD), F32)
    ones = jnp.ones((n, MLA_NOPE), F32)
    pad = lambda a: jnp.pad(a, ((0, 0), (0, LANES - a.shape[1])))
    cos_q = MLA_SCALE * pad(jnp.concatenate([ones, c32], axis=-1))
    sin_q = MLA_SCALE * pad(jnp.concatenate([jnp.zeros_like(ones), s32], axis=-1))
    return jnp.stack([cos_q, sin_q, pad(c32), pad(s32),
                      jnp.concatenate([c64, c64], axis=-1), jnp.concatenate([s64, s64], axis=-1)])


def _rot_cols(w, d):
    shp = w.shape
    w5 = w.reshape(shp[:-1] + (-1, 2, 2, d // 4))
    r = jnp.concatenate([-w5[..., 1:2, :], w5[..., 0:1, :]], axis=-2)
    return r.reshape(shp)


def _pad_cols(w, n):
    return jnp.pad(w, ((0, 0), (0, n - w.shape[1])))


def _prep_ab_weights(w_in, q_norm, kv_norm, w_uq, w_ukv, w_out):
    o_cq, o_ckv, o_kr = MLA_Q_RANK, MLA_Q_RANK + MLA_KV_RANK, MLA_Q_RANK + MLA_KV_RANK + MLA_ROPE
    o_dq, o_dk = o_kr + _DW, o_kr + 2 * _DW
    w_kr, w_dq, w_dk, w_dv = w_in[:, o_ckv:o_kr], w_in[:, o_kr:o_dq], w_in[:, o_dq:o_dk], w_in[:, o_dk:]
    win = jnp.concatenate([
        w_in[:, :o_ckv], _pad_cols(w_kr, LANES), _pad_cols(_rot_cols(w_kr, MLA_ROPE), LANES),
        w_dq, _rot_cols(w_dq, DIFF_HD), w_dk, _rot_cols(w_dk, DIFF_HD), w_dv], axis=1).astype(BF)
    rq = w_uq.shape[0]
    uq = w_uq.reshape(rq, MLA_HEADS, MLA_NOPE + MLA_ROPE)
    nope, rope = uq[..., :MLA_NOPE], uq[..., MLA_NOPE:]
    zpad = jnp.zeros((rq, MLA_HEADS, LANES - MLA_NOPE - MLA_ROPE), F32)
    main = jnp.concatenate([nope, rope, zpad], axis=-1).reshape(rq, MLA_HEADS * LANES)
    rot = jnp.concatenate([jnp.zeros_like(nope), _rot_cols(rope, MLA_ROPE), zpad], axis=-1)
    wuq = jnp.concatenate([main, rot.reshape(rq, MLA_HEADS * LANES)], axis=1).astype(BF)
    rkv = w_ukv.shape[0]
    ukv = w_ukv.reshape(rkv, MLA_HEADS, MLA_NOPE + MLA_V)
    slot = lambda a: jnp.pad(a, ((0, 0), (0, 0), (0, LANES - a.shape[-1]))).reshape(rkv, MLA_HEADS * LANES)
    wuk = slot(ukv[..., :MLA_NOPE]).astype(BF)
    wuv = slot(ukv[..., MLA_NOPE:]).astype(BF)
    place = np.zeros((LANES, MLA_HEADS * LANES), np.float32)
    for hd in range(MLA_HEADS):
        place[np.arange(MLA_ROPE), hd * LANES + MLA_NOPE + np.arange(MLA_ROPE)] = 1.0
    d = w_out.shape[1]
    wo_a = w_out[:MLA_HEADS * MLA_V].reshape(MLA_HEADS, MLA_V, d)
    wo_a = jnp.pad(wo_a, ((0, 0), (0, LANES - MLA_V), (0, 0))).reshape(MLA_HEADS * LANES, d)
    wo = jnp.concatenate([wo_a, w_out[MLA_HEADS * MLA_V:]], axis=0).astype(BF)
    proj = (win, q_norm.reshape(1, -1), kv_norm.reshape(1, -1), wuq, wuk, wuv, jnp.asarray(place, BF))
    return proj, wo


def _na_bias_table(rpb):
    nh = rpb.shape[0]
    cols = np.arange(GRID_W)
    col_start = np.clip(cols - NA_KW // 2, 0, GRID_W - NA_KW)
    kcol = np.arange(GRID_W)
    valid = (kcol[None, :] >= col_start[:, None]) & (kcol[None, :] < col_start[:, None] + NA_KW)
    dc = kcol[None, :] - cols[:, None] + (NA_KW - 1)
    onehot = (dc[None] == np.arange(2 * NA_KW - 1)[:, None, None]) & valid[None]
    tz = jnp.einsum('hrd,dck->hrck', rpb * LOG2E, jnp.asarray(onehot, F32), precision=lax.Precision.HIGHEST)
    tz = jnp.where(valid[None, None], tz, MASK_VALUE)
    cases = [tz[:, NA_KH - 1 - c:2 * NA_KH - 1 - c] for c in range(NA_KH)]
    tbl = jnp.transpose(jnp.stack(cases), (0, 1, 3, 2, 4))
    return tbl.reshape(NA_KH, nh // 2, 2 * GRID_W, NA_KH * GRID_W)


def _tile(n, pref):
    return pref if n % pref == 0 else n


def kernel(x, c, ctx, c_ctx, w_mod, b_mod, norm_g, a_w_in, a_q_norm, a_kv_norm, a_w_uq, a_w_ukv, b_lambda, b_subln,
           ab_w_out, f_w1, f_w3, f_w2, c_w_qkv, c_b_qkv, c_rpb, c_w_out, c_b_out, m_router, m_w1, m_w3, m_w2):
    b, l, d = x.shape
    lc = ctx.shape[1]
    depth = w_mod.shape[0]
    assert l % GRID_W == 0 and l // GRID_W >= NA_KH

    mod_rows = 16
    cvec = jnp.zeros((mod_rows, d), F32).at[:b].set(c).at[b].set(c_ctx)
    mod = _modulation(cvec, w_mod, b_mod)

    tm_x, tm_c = _tile(l, 512), _tile(lc, 256)
    tq_x, tq_c = _tile(l, ATTN_QUERY_LANES), _tile(lc, ATTN_QUERY_LANES // 2)
    tq2_x = _tile(l, ATTN_QUERY_LANES // 2)
    cs = ctx
    for i in range(depth):
        last = i == depth - 1
        j = i // 2
        mx = mod[i, :b].reshape(b, N_MOD, d)
        mc = mod[i, b].reshape(1, N_MOD, d)
        g = norm_g[i]
        if i % 2 == 0:
            lam_init = 0.8 - 0.6 * math.exp(-0.3 * i)
            proj_w, wo = _prep_ab_weights(a_w_in[j], a_q_norm[j], a_kv_norm[j], a_w_uq[j], a_w_ukv[j], ab_w_out[j])
            qx, kx, vx, dqx, dkx, dvx = _proj_ab(x, mx, g, proj_w, _rope_tables(l, True), tm_x)
            qc, kc, vc, dqc, dkc, dvc = _proj_ab(cs, mc, g, proj_w, _rope_tables(lc, False), tm_c)
            subln = b_subln[j].reshape(1, -1)
            oa = _attention(qx, [(kx, vx), (kc, vc)], tq_x)
            ob = _attention(dqx, [(dkx, dvx), (dkc, dvc)], tq2_x, "diff", b_lambda[j], subln, lam_init)
            ffn_w = (f_w1[j].astype(BF), f_w3[j].astype(BF), f_w2[j].astype(BF))
            x = _outproj([oa, ob], wo, None, x, mx, g, tm_x, "ffn", ffn_w)
            if not last:
                oa = _attention(qc, [(kc, vc)], tq_c)
                ob = _attention(dqc, [(dkc, dvc)], tq_c, "diff", b_lambda[j], subln, lam_init)
                cs = _outproj([oa, ob], wo, None, cs, mc, g, tm_c, "ffn", ffn_w)
        else:
            wqkv = c_w_qkv[j].astype(BF)
            bqkv = c_b_qkv[j].reshape(1, -1)
            wo = c_w_out[j].astype(BF)
            bo = c_b_out[j].reshape(1, -1)
            qx, kx, vx = _proj_na(x, mx, g, wqkv, bqkv, tm_x)
            qc, kc, vc = _proj_na(cs, mc, g, wqkv, bqkv, tm_c)
            o = _na_attention(qx, kx, vx, kc, vc, _na_bias_table(c_rpb[j]))
            r_f32 = _pad_cols(m_router[j], LANES)
            r_hi = r_f32.astype(BF)
            router = (jnp.stack([r_hi, (r_f32 - r_hi.astype(F32)).astype(BF)]),)
            w1, w3, w2 = m_w1[j].astype(BF), m_w3[j].astype(BF), m_w2[j].astype(BF)
            x, *routed = _outproj([o], wo, bo, x, mx, g, tm_x, "router", router)
            x = _moe(x, routed, mx, g, w1, w3, w2, tm_x)
            if not last:
                oc = _attention(qc, [(kc, jnp.swapaxes(vc, 2, 3))], tq_c, "pair")
                cs, *routed = _outproj([oc], wo, bo, cs, mc, g, tm_c, "router", router)
                cs = _moe(cs, routed, mc, g, w1, w3, w2, tm_c)
    return x
```

```python
import functools
import math

import jax
import jax.numpy as jnp
import numpy as np
from jax import lax
from jax.experimental import pallas as pl
from jax.experimental.pallas import tpu as pltpu
from jax.experimental.pallas import tpu_sc as plsc

BF = jnp.bfloat16
F32 = jnp.float32

LANES = 128
VMEM_LIMIT = 56 * 1024 * 1024

GRID_W = 64
EPS = 1e-6
ROPE_THETA = 10000.0
N_MOD = 6

MLA_HEADS = 8
MLA_NOPE = 64
MLA_ROPE = 32
MLA_V = 64
MLA_Q_RANK = 384
MLA_KV_RANK = 256
LOG2E = math.log2(math.e)
MLA_SCALE = (MLA_NOPE + MLA_ROPE) ** -0.5 * LOG2E

DIFF_HEADS = 4
DIFF_HD = 64
DIFF_SCALE = DIFF_HD ** -0.5 * LOG2E

NA_HEADS = 16
NA_HD = 64
NA_KH = 8
NA_KW = 16
NA_SCALE = NA_HD ** -0.5 * LOG2E
NA_SLOTS = NA_HEADS * NA_HD // LANES
MASK_VALUE = -1e30

N_EXPERTS = 8

_NT = (((1,), (1,)), ((), ()))


def _cparams(*sem):
    return pltpu.CompilerParams(dimension_semantics=sem, vmem_limit_bytes=VMEM_LIMIT)


def _dot(a, b):
    return jnp.dot(a, b, preferred_element_type=F32)


def _rms(x, g):
    return x * lax.rsqrt(jnp.mean(x * x, axis=-1, keepdims=True) + EPS) * g


def _norm_mod(x, g, shift, scale):
    return _rms(x, g) * (1 + scale) + shift


def _const_spec(shape):
    return pl.BlockSpec(shape, lambda *_: (0,) * len(shape))


def _mod_spec(mod):
    if mod.shape[0] == 1:
        return pl.BlockSpec((1,) + mod.shape[1:], lambda b, *_: (0, 0, 0))
    return pl.BlockSpec((1,) + mod.shape[1:], lambda b, *_: (b, 0, 0))


def _mod_kernel(c_ref, w_ref, b_ref, o_ref):
    c = c_ref[...]
    sc = c * jax.nn.sigmoid(c)
    o_ref[0] = _dot(sc.astype(BF), w_ref[0].astype(BF)) + b_ref[0]


def _modulation(cvec, w_mod, b_mod):
    depth, d, n = w_mod.shape
    rows = cvec.shape[0]
    return pl.pallas_call(
        _mod_kernel,
        grid=(depth, n // d),
        in_specs=[
            pl.BlockSpec((rows, d), lambda i, j: (0, 0)),
            pl.BlockSpec((1, d, d), lambda i, j: (i, 0, j)),
            pl.BlockSpec((1, 1, d), lambda i, j: (i, 0, j)),
        ],
        out_specs=pl.BlockSpec((1, rows, d), lambda i, j: (i, 0, j)),
        out_shape=jax.ShapeDtypeStruct((depth, rows, n), F32),
        compiler_params=_cparams("arbitrary", "arbitrary"),
        name="modulation",
    )(cvec, w_mod, b_mod.reshape(depth, 1, n))


_Z_CQ = 0
_Z_CKV = MLA_Q_RANK
_Z_KR = _Z_CKV + MLA_KV_RANK
_Z_KR_ROT = _Z_KR + LANES
_Z_DQ = _Z_KR_ROT + LANES
_DW = 2 * DIFF_HEADS * DIFF_HD
_Z_DQ_ROT = _Z_DQ + _DW
_Z_DK = _Z_DQ_ROT + _DW
_Z_DK_ROT = _Z_DK + _DW
_Z_DV = _Z_DK_ROT + _DW
_Z_END = _Z_DV + _DW


def _proj_ab_kernel(x_ref, mod_ref, g_ref, win_ref, qn_ref, kvn_ref, wuq_ref, wuk_ref, wuv_ref,
                    place_ref, tab_ref, q_ref, k_ref, v_ref, dq_ref, dk_ref, dv_ref):
    h = _norm_mod(x_ref[0], g_ref[0:1, :], mod_ref[0, 0:1, :], mod_ref[0, 1:2, :])
    z = _dot(h.astype(BF), win_ref[...])
    cqn = _rms(z[:, _Z_CQ:_Z_CKV], qn_ref[...]).astype(BF)
    ckvn = _rms(z[:, _Z_CKV:_Z_KR], kvn_ref[...]).astype(BF)
    q2 = _dot(cqn, wuq_ref[...])
    nq = MLA_HEADS * LANES
    cos_q, sin_q = tab_ref[0], tab_ref[1]
    for hd in range(MLA_HEADS):
        lo = hd * LANES
        q_ref[0, hd] = (q2[:, lo:lo + LANES] * cos_q + q2[:, nq + lo:nq + lo + LANES] * sin_q).astype(BF)
    kr = (z[:, _Z_KR:_Z_KR_ROT] * tab_ref[2] + z[:, _Z_KR_ROT:_Z_DQ] * tab_ref[3]).astype(BF)
    kk = _dot(ckvn, wuk_ref[...]) + _dot(kr, place_ref[...])
    vv = _dot(ckvn, wuv_ref[...])
    for hd in range(MLA_HEADS):
        lo = hd * LANES
        k_ref[0, hd] = kk[:, lo:lo + LANES].astype(BF)
        v_ref[0, hd] = vv[:, lo:lo + LANES].T.astype(BF)
    cos_d, sin_d = tab_ref[4], tab_ref[5]
    for hd in range(DIFF_HEADS):
        lo = hd * LANES
        dq = z[:, _Z_DQ + lo:_Z_DQ + lo + LANES] * cos_d + z[:, _Z_DQ_ROT + lo:_Z_DQ_ROT + lo + LANES] * sin_d
        dq_ref[0, hd] = (dq * DIFF_SCALE).astype(BF)
        dk = z[:, _Z_DK + lo:_Z_DK + lo + LANES] * cos_d + z[:, _Z_DK_ROT + lo:_Z_DK_ROT + lo + LANES] * sin_d
        dk_ref[0, hd] = dk.astype(BF)
        dv_ref[0, hd] = z[:, _Z_DV + lo:_Z_DV + lo + LANES].T.astype(BF)


def _proj_ab(x, mod, g, wts, tabs, tm):
    b, t, d = x.shape
    win, qn, kvn, wuq, wuk, wuv, place = wts
    tok = lambda hh: pl.BlockSpec((1, hh, tm, LANES), lambda bi, i: (bi, 0, i, 0))
    shp = lambda hh: jax.ShapeDtypeStruct((b, hh, t, LANES), BF)
    tok_t = lambda hh, dv: pl.BlockSpec((1, hh, dv, tm), lambda bi, i: (bi, 0, 0, i))
    shp_t = lambda hh, dv: jax.ShapeDtypeStruct((b, hh, dv, t), BF)
    return pl.pallas_call(
        _proj_ab_kernel,
        grid=(b, t // tm),
        in_specs=[
            pl.BlockSpec((1, tm, d), lambda bi, i: (bi, i, 0)),
            _mod_spec(mod),
            _const_spec(g.shape),
            _const_spec(win.shape), _const_spec(qn.shape), _const_spec(kvn.shape),
            _const_spec(wuq.shape), _const_spec(wuk.shape), _const_spec(wuv.shape),
            _const_spec(place.shape),
            pl.BlockSpec((6, tm, LANES), lambda bi, i: (0, i, 0)),
        ],
        out_specs=[tok(MLA_HEADS), tok(MLA_HEADS), tok_t(MLA_HEADS, LANES),
                   tok(DIFF_HEADS), tok(DIFF_HEADS), tok_t(DIFF_HEADS, LANES)],
        out_shape=[shp(MLA_HEADS), shp(MLA_HEADS), shp_t(MLA_HEADS, LANES),
                   shp(DIFF_HEADS), shp(DIFF_HEADS), shp_t(DIFF_HEADS, LANES)],
        compiler_params=_cparams("arbitrary", "arbitrary"),
        name="proj_ab",
    )(x, mod, g, win, qn, kvn, wuq, wuk, wuv, place, tabs)


KEY_CHUNK = 256


SUBLANES = 8
ATTN_QUERY_LANES = 1024


def _key_chunks(kv_refs):
    off = 0
    for k_ref, v_ref in kv_refs:
        lk = k_ref.shape[2]
        for c0 in range(0, lk, KEY_CHUNK):
            w = min(KEY_CHUNK, lk - c0)
            yield k_ref, v_ref, c0, w, off
            off += w


def _attn_kernel(*refs, nseg, mode, lam_init):
    q_ref = refs[0]
    kv_refs = [(refs[1 + 2 * s], refs[2 + 2 * s]) for s in range(nseg)]
    o_ref, s0_ref, s1_ref, mm0_ref, mm1_ref = refs[-5:]
    nmap, _, tq = s0_ref.shape
    maps = range(nmap)
    fold = lambda a: a.reshape(a.shape[0] // SUBLANES, SUBLANES, tq)
    t = pl.program_id(0)

    @pl.when(t == 0)
    def _():
        s1_ref[...] = jnp.zeros_like(s1_ref)
        mm1_ref[...] = jnp.zeros_like(mm1_ref)

    def step(s_cur, mm_cur, s_prv, mm_prv):
        q = q_ref[0, 0]
        if mode == "single":
            qms = [q]
        else:
            lane = lax.broadcasted_iota(jnp.int32, q.shape, 1)
            zero = jnp.zeros_like(q)
            qms = [jnp.where(lane < DIFF_HD, q, zero), jnp.where(lane >= DIFF_HD, q, zero)]
        m_prv = [jnp.max(mm_prv[j], axis=0, keepdims=True) for j in maps]
        mms = [jnp.full((SUBLANES, tq), -jnp.inf, F32)] * nmap
        lls = [jnp.zeros((SUBLANES, tq), F32)] * nmap
        dv = kv_refs[0][1].shape[2]
        o_ts = [jnp.zeros((dv, tq), F32)] * nmap
        for k_ref, v_ref, c0, w, off in _key_chunks(kv_refs):
            kc = k_ref[0, 0, c0:c0 + w, :]
            vc = v_ref[0, 0, :, c0:c0 + w]
            for j in maps:
                s = lax.dot_general(kc, qms[j], _NT, preferred_element_type=F32)
                s_cur[j, off:off + w, :] = s
                mms[j] = jnp.maximum(mms[j], jnp.max(fold(s), axis=0))
            for j in maps:
                p = jnp.exp2(s_prv[j, off:off + w, :] - m_prv[j])
                lls[j] = lls[j] + jnp.sum(fold(p), axis=0)
                o_ts[j] = o_ts[j] + _dot(vc, p.astype(BF))
        for j in maps:
            mm_cur[j] = mms[j]
        outs = [(o_ts[j] / jnp.sum(lls[j], axis=0, keepdims=True)).T for j in maps]
        if mode == "single":
            o = outs[0].astype(BF)
            if dv < LANES:
                o = jnp.concatenate([o, jnp.zeros((tq, LANES - dv), BF)], axis=-1)
            o_ref[0, 0] = o
        elif mode == "pair":
            o_ref[0, 0] = jnp.where(lane < DIFF_HD, outs[0], outs[1]).astype(BF)
        else:
            lam_ref, subln_ref = refs[1 + 2 * nseg], refs[2 + 2 * nseg]
            lv = lam_ref[...]
            lam = (jnp.exp(jnp.sum(lv[0:1] * lv[1:2], axis=-1, keepdims=True))
                   - jnp.exp(jnp.sum(lv[2:3] * lv[3:4], axis=-1, keepdims=True)) + lam_init)
            o_ref[0, 0] = (_rms(outs[0] - lam * outs[1], subln_ref[...]) * (1 - lam_init)).astype(BF)

    pl.when(t % 2 == 0)(lambda: step(s0_ref, mm0_ref, s1_ref, mm1_ref))
    pl.when(t % 2 == 1)(lambda: step(s1_ref, mm1_ref, s0_ref, mm0_ref))


def _attention(q, kvs, tq, mode="single", lam_vecs=None, subln=None, lam_init=0.0):
    b, nh, lq, _ = q.shape
    diff = mode == "diff"
    nmap = 1 if mode == "single" else 2
    nq = lq // tq
    ntile = b * nh * nq

    def tile(t):
        t = jnp.clip(t, 0, ntile - 1)
        return t // (nh * nq), (t // nq) % nh, t % nq

    cur_head = lambda t: tile(t)[:2] + (0, 0)
    prv_head = lambda t: tile(t - 1)[:2] + (0, 0)
    in_specs = [pl.BlockSpec((1, 1, tq, LANES), lambda t: tile(t) + (0,))]
    args = [q]
    for k, v in kvs:
        lk = k.shape[2]
        in_specs += [pl.BlockSpec((1, 1, lk, LANES), cur_head), pl.BlockSpec((1, 1, v.shape[2], lk), prv_head)]
        args += [k, v]
    if diff:
        in_specs += [_const_spec(lam_vecs.shape), _const_spec(subln.shape)]
        args += [lam_vecs, subln]
    nkeys = sum(k.shape[2] for k, _ in kvs)
    return pl.pallas_call(
        functools.partial(_attn_kernel, nseg=len(kvs), mode=mode, lam_init=lam_init),
        grid=(ntile + 1,),
        in_specs=in_specs,
        out_specs=pl.BlockSpec((1, 1, tq, LANES), lambda t: tile(t - 1) + (0,)),
        out_shape=jax.ShapeDtypeStruct((b, nh, lq, LANES), BF),
        scratch_shapes=([pltpu.VMEM((nmap, nkeys, tq), F32)] * 2 + [pltpu.VMEM((nmap, SUBLANES, tq), F32)] * 2),
        compiler_params=_cparams("arbitrary"),
        name="attn_diff" if diff else "attn_mla",
    )(*args)


ROUTER_PIECES = 2


def _outproj_kernel(*refs, n_in, has_bias, tail):
    o_refs = refs[:n_in]
    w_ref = refs[n_in]
    pos = n_in + 1
    b_ref = None
    if has_bias:
        b_ref = refs[pos]
        pos += 1
    x_ref, mod_ref, g_ref = refs[pos:pos + 3]
    rest = refs[pos + 3:]
    tm = x_ref.shape[1]

    def residual(rows):
        o = jnp.concatenate([r[0, h, rows, :] for r in o_refs for h in range(r.shape[1])], axis=-1)
        y = _dot(o, w_ref[...])
        if has_bias:
            y = y + b_ref[...]
        x1 = x_ref[0, rows, :] + mod_ref[0, 2:3, :] * _rms(y, g_ref[1:2, :])
        return x1, _norm_mod(x1, g_ref[2:3, :], mod_ref[0, 3:4, :], mod_ref[0, 4:5, :])

    if tail == "ffn":
        w1_ref, w3_ref, w2_ref, out_ref = rest
        x1, h = residual(slice(0, tm))
        out_ref[0] = x1 + mod_ref[0, 5:6, :] * _rms(_swiglu(h.astype(BF), w1_ref, w3_ref, w2_ref), g_ref[3:4, :])
    else:
        router_ref, tri_ref, out_ref = rest[:3]
        step = tm // ROUTER_PIECES
        halves = [slice(r0, r0 + step) for r0 in range(0, tm, step)]
        parts = [residual(rows) for rows in halves]
        for rows, (x1, _) in zip(halves, parts):
            out_ref[0, rows, :] = x1
        first = (pl.program_id(0) == 0) & (pl.program_id(1) == 0)
        _route([h for _, h in parts], halves, first, router_ref, tri_ref, *rest[3:])


def _outproj(os_, w, bias, x, mod, g, tm, tail, tail_args):
    b, t, d = x.shape
    n = b * t
    nt = t // tm
    in_specs = [pl.BlockSpec((1, o.shape[1], tm, LANES), lambda bi, i: (bi, 0, i, 0)) for o in os_]
    in_specs.append(_const_spec(w.shape))
    args = list(os_) + [w]
    if bias is not None:
        in_specs.append(_const_spec(bias.shape))
        args.append(bias)
    x_spec = pl.BlockSpec((1, tm, d), lambda bi, i: (bi, i, 0))
    in_specs += [x_spec, _mod_spec(mod), _const_spec(g.shape)]
    args += [x, mod, g]
    out_specs, out_shape, scratch = x_spec, jax.ShapeDtypeStruct((b, t, d), F32), []
    if tail == "ffn":
        in_specs += [_resident_spec(a.shape, lambda bi, i: (0, 0)) for a in tail_args]
        args += list(tail_args)
    else:
        router, = tail_args
        tri = jnp.asarray(np.tril(np.ones((tm, tm), np.float32), -1), BF)
        in_specs += [_const_spec(router.shape), _const_spec(tri.shape)]
        args += [router, tri]
        rows = lambda wd: pl.BlockSpec((tm, wd), lambda bi, i: (bi * nt + i, 0))
        out_specs = [x_spec, pl.BlockSpec((PACK_CHUNKS, tm, LANES), lambda bi, i: (0, bi * nt + i, 0)),
                     rows(8), rows(8), pl.BlockSpec((1, LANES), lambda bi, i: (0, 0))]
        out_shape = [out_shape, jax.ShapeDtypeStruct((PACK_CHUNKS, n, LANES), jnp.int32),
                     jax.ShapeDtypeStruct((n, 8), jnp.int32), jax.ShapeDtypeStruct((n, 8), F32),
                     jax.ShapeDtypeStruct((1, LANES), F32)]
        scratch = [pltpu.VMEM((1, LANES), F32)]
    return pl.pallas_call(
        functools.partial(_outproj_kernel, n_in=len(os_), has_bias=bias is not None, tail=tail),
        grid=(b, nt),
        in_specs=in_specs,
        out_specs=out_specs,
        out_shape=out_shape,
        scratch_shapes=scratch,
        compiler_params=_cparams("arbitrary", "arbitrary"),
        name="outproj_" + tail,
    )(*args)


MXU_TILE = 256


def _ff_chunks(f):
    tiles = f // MXU_TILE
    if f % MXU_TILE or tiles < 2:
        return [(0, f)]
    first = (tiles + 1) // 2 * MXU_TILE
    return [(0, first), (first, f - first)]


def _swiglu(hn, w1_ref, w3_ref, w2_ref):
    y = None
    for lo, n in _ff_chunks(w1_ref.shape[1]):
        a = _dot(hn, w1_ref[:, lo:lo + n])
        act = (a * jax.nn.sigmoid(a) * _dot(hn, w3_ref[:, lo:lo + n])).astype(BF)
        part = _dot(act, w2_ref[lo:lo + n, :])
        y = part if y is None else y + part
    return y


def _resident_spec(shape, index_map):
    return pl.BlockSpec(shape, index_map, pipeline_mode=pl.Buffered(1))


MOE_ROW_TILE = 512
PACK_CHUNKS = 4
SC_WINDOW = 128
_HI_MASK = -65536
_LO_MASK = 65535


def _pack_rows(v, out_ref, rows=slice(None)):
    half = v.shape[1] // 2
    vb = v.astype(BF).astype(F32)
    lo = (pltpu.bitcast(vb[:, :half], jnp.int32) >> 16) & _LO_MASK
    hi = pltpu.bitcast(vb[:, half:], jnp.int32) & _HI_MASK
    w = lo | hi
    for c in range(PACK_CHUNKS):
        out_ref[c, rows, :] = w[:, c * LANES:(c + 1) * LANES]


def _unpack_rows(ref):
    w = jnp.concatenate([ref[c] for c in range(PACK_CHUNKS)], axis=-1)
    lo = pltpu.bitcast(w << 16, F32)
    hi = pltpu.bitcast(w & _HI_MASK, F32)
    return jnp.concatenate([lo, hi], axis=-1)


def _route(hs, row_slices, first, router_ref, tri_ref, hp_ref, meta_ref, gate_ref, cnt_ref, carry_ref):
    @pl.when(first)
    def _():
        carry_ref[...] = jnp.zeros_like(carry_ref)

    pieces = range(len(hs))
    for h, rows in zip(hs, row_slices):
        _pack_rows(h, hp_ref, rows)
    his = [h.astype(BF) for h in hs]
    los = [(hs[p] - his[p].astype(F32)).astype(BF) for p in pieces]
    raw = [_dot(his[p], router_ref[0]) + (_dot(his[p], router_ref[1]) + _dot(los[p], router_ref[0])) for p in pieces]
    lanes = [lax.broadcasted_iota(jnp.int32, r.shape, 1) for r in raw]
    lgs = [jnp.where(lanes[p] < N_EXPERTS, raw[p], -jnp.inf) for p in pieces]
    m1s = [jnp.max(lg, axis=-1, keepdims=True) for lg in lgs]
    i1s = [jnp.min(jnp.where(lgs[p] == m1s[p], lanes[p], LANES), axis=-1, keepdims=True) for p in pieces]
    rests = [jnp.where(lanes[p] == i1s[p], -jnp.inf, lgs[p]) for p in pieces]
    m2s = [jnp.max(r, axis=-1, keepdims=True) for r in rests]
    i2s = [jnp.min(jnp.where(rests[p] == m2s[p], lanes[p], LANES), axis=-1, keepdims=True) for p in pieces]
    rows_cat = lambda parts: jnp.concatenate(parts, axis=0)
    m1, m2, i1, i2, lane = rows_cat(m1s), rows_cat(m2s), rows_cat(i1s), rows_cat(i2s), rows_cat(lanes)
    e2 = jnp.exp(m2 - m1)
    denom = 1.0 + e2
    assigned = jnp.where((lane == i1) | (lane == i2), 1.0, 0.0)
    ranks = _dot(tri_ref[...], assigned.astype(BF)) + carry_ref[...]
    r1 = jnp.sum(jnp.where(lane == i1, ranks, 0.0), axis=-1, keepdims=True).astype(jnp.int32)
    r2 = jnp.sum(jnp.where(lane == i2, ranks, 0.0), axis=-1, keepdims=True).astype(jnp.int32)
    carry_ref[...] += jnp.sum(assigned, axis=0, keepdims=True)
    cnt_ref[...] = carry_ref[...]
    col = lax.broadcasted_iota(jnp.int32, meta_ref.shape, 1)
    meta_ref[...] = jnp.where(col == 0, i1, jnp.where(col == 1, i2, jnp.where(col == 2, r1, r2)))
    gate_ref[...] = jnp.where(col == 0, 1.0 / denom, e2 / denom)


def _sc_mesh():
    return plsc.VectorSubcoreMesh(core_axis_name="core", subcore_axis_name="subcore")


def _sc_scatter_rows(rows, idx_a, idx_b, n_out):
    nrows = rows.shape[0]

    @pl.kernel(out_type=jax.ShapeDtypeStruct((n_out, LANES), rows.dtype), mesh=_sc_mesh(), scratch_types=[])
    def scatter_kernel(x_hbm, ia_hbm, ib_hbm, o_hbm):
        def body(x_vmem, ia_vmem, ib_vmem):
            pltpu.sync_copy(x_vmem, o_hbm.at[ia_vmem.at[0]])
            pltpu.sync_copy(x_vmem, o_hbm.at[ib_vmem.at[0]])

        pltpu.emit_pipeline(
            body,
            grid=(nrows // SC_WINDOW,),
            in_specs=[pl.BlockSpec((SC_WINDOW, LANES), lambda i: (i, 0)),
                      pl.BlockSpec((1, SC_WINDOW), lambda i: (0, i)),
                      pl.BlockSpec((1, SC_WINDOW), lambda i: (0, i))],
            out_specs=[],
            core_axis_name=("core", "subcore"),
            dimension_semantics=(pltpu.PARALLEL,),
        )(x_hbm, ia_hbm, ib_hbm)

    return scatter_kernel(rows, idx_a, idx_b)


def _sc_gather_rows(table, idx_a, idx_b):
    nrows = idx_a.shape[1]
    out = jax.ShapeDtypeStruct((nrows, LANES), table.dtype)

    @pl.kernel(out_type=(out, out), mesh=_sc_mesh(), scratch_types=[])
    def gather_kernel(t_hbm, ia_hbm, ib_hbm, oa_hbm, ob_hbm):
        def body(ia_vmem, ib_vmem, oa_vmem, ob_vmem):
            pltpu.sync_copy(t_hbm.at[ia_vmem.at[0]], oa_vmem)
            pltpu.sync_copy(t_hbm.at[ib_vmem.at[0]], ob_vmem)

        pltpu.emit_pipeline(
            body,
            grid=(nrows // SC_WINDOW,),
            in_specs=[pl.BlockSpec((1, SC_WINDOW), lambda i: (0, i)),
                      pl.BlockSpec((1, SC_WINDOW), lambda i: (0, i))],
            out_specs=[pl.BlockSpec((SC_WINDOW, LANES), lambda i: (i, 0)),
                       pl.BlockSpec((SC_WINDOW, LANES), lambda i: (i, 0))],
            core_axis_name=("core", "subcore"),
            dimension_semantics=(pltpu.PARALLEL,),
        )(ia_hbm, ib_hbm, oa_hbm, ob_hbm)

    return gather_kernel(table, idx_a, idx_b)


def _experts_kernel(te_ref, tv_ref, xs_ref, w1_ref, w3_ref, w2_ref, ys_ref):
    del te_ref
    valid = tv_ref[pl.program_id(0)]

    @pl.when(valid > 0)
    def _():
        row = lax.broadcasted_iota(jnp.int32, (xs_ref.shape[1], 1), 0)
        hn = jnp.where(row < valid, _unpack_rows(xs_ref), 0.0).astype(BF)
        _pack_rows(_swiglu(hn, w1_ref.at[0], w3_ref.at[0], w2_ref.at[0]), ys_ref)

    @pl.when(valid == 0)
    def _():
        ys_ref[...] = jnp.zeros_like(ys_ref)


def _experts(xs, tile_expert, tile_valid, w1, w3, w2):
    _, p, _ = xs.shape
    tr = MOE_ROW_TILE
    rows = pl.BlockSpec((PACK_CHUNKS, tr, LANES), lambda j, te, tv: (0, j, 0))
    weights = lambda w: pl.BlockSpec((1,) + w.shape[1:], lambda j, te, tv: (te[j], 0, 0))
    return pl.pallas_call(
        _experts_kernel,
        grid_spec=pltpu.PrefetchScalarGridSpec(
            num_scalar_prefetch=2,
            grid=(p // tr,),
            in_specs=[rows, weights(w1), weights(w3), weights(w2)],
            out_specs=rows,
        ),
        out_shape=jax.ShapeDtypeStruct(xs.shape, jnp.int32),
        compiler_params=_cparams("arbitrary"),
        name="moe_experts",
    )(tile_expert, tile_valid, xs, w1, w3, w2)


def _combine_kernel(x_ref, ya_ref, yb_ref, gate_ref, mod_ref, g_ref, out_ref):
    gates = gate_ref[...]
    fx = gates[:, 0:1] * _unpack_rows(ya_ref) + gates[:, 1:2] * _unpack_rows(yb_ref)
    out_ref[0] = x_ref[0] + mod_ref[0, 5:6, :] * _rms(fx, g_ref[3:4, :])


def _combine(x, ya, yb, gates, mod, g, tm):
    b, t, d = x.shape
    nt = t // tm
    packed = pl.BlockSpec((PACK_CHUNKS, tm, LANES), lambda bi, i: (0, bi * nt + i, 0))
    return pl.pallas_call(
        _combine_kernel,
        grid=(b, nt),
        in_specs=[
            pl.BlockSpec((1, tm, d), lambda bi, i: (bi, i, 0)),
            packed, packed,
            pl.BlockSpec((tm, gates.shape[1]), lambda bi, i: (bi * nt + i, 0)),
            _mod_spec(mod),
            _const_spec(g.shape),
        ],
        out_specs=pl.BlockSpec((1, tm, d), lambda bi, i: (bi, i, 0)),
        out_shape=jax.ShapeDtypeStruct((b, t, d), F32),
        compiler_params=_cparams("arbitrary", "arbitrary"),
        name="moe_combine",
    )(x, ya, yb, gates, mod, g)


def _moe(x, routed, mod, g, w1, w3, w2, tm):
    b, t, d = x.shape
    n = b * t
    ne = w1.shape[0]
    tr = MOE_ROW_TILE
    assert d == 2 * PACK_CHUNKS * LANES and (PACK_CHUNKS * n) % (SC_WINDOW * 32) == 0
    hp, meta, gates, counts = routed

    ntile = 2 * n // tr + ne
    p = ntile * tr
    cnt = counts[0, :ne].astype(jnp.int32)
    tiles = (cnt + tr - 1) // tr
    tile_end = jnp.cumsum(tiles)
    tile_start = tile_end - tiles
    base = tile_start * tr
    eids = jnp.arange(ne, dtype=jnp.int32)
    base_of = lambda e: jnp.sum(jnp.where(e[:, None] == eids[None, :], base[None, :], 0), axis=-1)
    pos_a = base_of(meta[:, 0]) + meta[:, 2]
    pos_b = base_of(meta[:, 1]) + meta[:, 3]
    chunk = jnp.arange(PACK_CHUNKS, dtype=jnp.int32)[:, None] * p
    idx_a = (chunk + pos_a[None, :]).reshape(1, PACK_CHUNKS * n)
    idx_b = (chunk + pos_b[None, :]).reshape(1, PACK_CHUNKS * n)
    tj = jnp.arange(ntile, dtype=jnp.int32)
    tile_expert = jnp.minimum(jnp.sum(tj[:, None] >= tile_end[None, :], axis=-1), ne - 1).astype(jnp.int32)
    done = jnp.sum(jnp.where(tile_expert[:, None] == eids[None, :], tile_start[None, :], 0), axis=-1)
    left = jnp.sum(jnp.where(tile_expert[:, None] == eids[None, :], cnt[None, :], 0), axis=-1) - (tj - done) * tr
    tile_valid = jnp.where(tj < tile_end[-1], jnp.clip(left, 0, tr), 0).astype(jnp.int32)

    xs = _sc_scatter_rows(hp.reshape(PACK_CHUNKS * n, LANES), idx_a, idx_b, PACK_CHUNKS * p)
    ys = _experts(xs.reshape(PACK_CHUNKS, p, LANES), tile_expert, tile_valid, w1, w3, w2)
    ya, yb = _sc_gather_rows(ys.reshape(PACK_CHUNKS * p, LANES), idx_a, idx_b)
    shp = (PACK_CHUNKS, n, LANES)
    return _combine(x, ya.reshape(shp), yb.reshape(shp), gates, mod, g, tm)


def _proj_na_kernel(x_ref, mod_ref, g_ref, w_ref, b_ref, q_ref, k_ref, v_ref):
    h = _norm_mod(x_ref[0], g_ref[0:1, :], mod_ref[0, 0:1, :], mod_ref[0, 1:2, :])
    z = _dot(h.astype(BF), w_ref[...]) + b_ref[...]
    n = NA_SLOTS * LANES
    for s in range(NA_SLOTS):
        lo = s * LANES
        q_ref[0, s] = (z[:, lo:lo + LANES] * NA_SCALE).astype(BF)
        k_ref[0, s] = z[:, n + lo:n + lo + LANES].astype(BF)
        v_ref[0, s] = z[:, 2 * n + lo:2 * n + lo + LANES].astype(BF)


def _proj_na(x, mod, g, w, bias, tm):
    b, t, d = x.shape
    tok = pl.BlockSpec((1, NA_SLOTS, tm, LANES), lambda bi, i: (bi, 0, i, 0))
    shp = jax.ShapeDtypeStruct((b, NA_SLOTS, t, LANES), BF)
    return pl.pallas_call(
        _proj_na_kernel,
        grid=(b, t // tm),
        in_specs=[
            pl.BlockSpec((1, tm, d), lambda bi, i: (bi, i, 0)),
            _mod_spec(mod),
            _const_spec(g.shape),
            _const_spec(w.shape),
            _const_spec(bias.shape),
        ],
        out_specs=[tok, tok, tok],
        out_shape=[shp, shp, shp],
        compiler_params=_cparams("arbitrary", "arbitrary"),
        name="proj_na",
    )(x, mod, g, w, bias)


NA_ROWS_PER_STEP = 2
NA_GROUP = 4


def _na_window_start(r, rows):
    return jnp.clip(r - NA_KH // 2, 0, rows - NA_KH)


def _na_kernel(q_ref, kx_ref, vx_ref, kc_ref, vc_ref, *rest, rows):
    bias_refs, o_ref = rest[:NA_ROWS_PER_STEP], rest[NA_ROWS_PER_STEP]
    nwin = NA_KH * GRID_W
    lane = lax.broadcasted_iota(jnp.int32, (GRID_W, LANES), 1)
    problems = [(a, s) for a in range(NA_ROWS_PER_STEP) for s in range(NA_SLOTS)]
    starts = [pl.multiple_of(_na_window_start(pl.program_id(1) * NA_ROWS_PER_STEP + a, rows) * GRID_W, GRID_W)
              for a in range(NA_ROWS_PER_STEP)]
    for g0 in range(0, len(problems), NA_GROUP):
        _na_group(problems[g0:g0 + NA_GROUP], starts, lane, q_ref, kx_ref, vx_ref, kc_ref, vc_ref, bias_refs, o_ref)


def _na_group(work, starts, lane, q_ref, kx_ref, vx_ref, kc_ref, vc_ref, bias_refs, o_ref):
    nwin = NA_KH * GRID_W
    idx = range(len(work))
    sws, scs = [], []
    for a, s in work:
        q = q_ref[0, s, a * GRID_W:(a + 1) * GRID_W, :]
        zero = jnp.zeros_like(q)
        q2 = jnp.concatenate([jnp.where(lane < NA_HD, q, zero), jnp.where(lane >= NA_HD, q, zero)], axis=0)
        kw = kx_ref[0, s, pl.ds(starts[a], nwin), :]
        sws.append(lax.dot_general(q2, kw, _NT, preferred_element_type=F32) + bias_refs[a][0, s])
        scs.append(lax.dot_general(q2, kc_ref[0, s], _NT, preferred_element_type=F32))
    ms = [jnp.maximum(jnp.max(sws[i], axis=-1, keepdims=True), jnp.max(scs[i], axis=-1, keepdims=True))
          for i in idx]
    pws = [jnp.exp2(sws[i] - ms[i]) for i in idx]
    pcs = [jnp.exp2(scs[i] - ms[i]) for i in idx]
    ls = [jnp.sum(pws[i], axis=-1, keepdims=True) + jnp.sum(pcs[i], axis=-1, keepdims=True) for i in idx]
    for i, (a, s) in enumerate(work):
        vw = vx_ref[0, s, pl.ds(starts[a], nwin), :]
        o2 = (_dot(pws[i].astype(BF), vw) + _dot(pcs[i].astype(BF), vc_ref[0, s])) / ls[i]
        o_ref[0, s, a * GRID_W:(a + 1) * GRID_W, :] = jnp.where(lane < NA_HD, o2[:GRID_W], o2[GRID_W:]).astype(BF)


def _na_attention(q, kx, vx, kc, vc, bias):
    b, ns, l, _ = q.shape
    rows = l // GRID_W
    lc = kc.shape[2]
    nr = NA_ROWS_PER_STEP
    assert rows % nr == 0

    def bias_spec(a):
        def index_map(bi, i):
            r = i * nr + a
            return (r - _na_window_start(r, rows), 0, 0, 0)
        return pl.BlockSpec((1,) + bias.shape[1:], index_map)

    full = lambda n: pl.BlockSpec((1, ns, n, LANES), lambda bi, i: (bi, 0, 0, 0))
    row = pl.BlockSpec((1, ns, nr * GRID_W, LANES), lambda bi, i: (bi, 0, i, 0))
    return pl.pallas_call(
        functools.partial(_na_kernel, rows=rows),
        grid=(b, rows // nr),
        in_specs=[row, full(l), full(l), full(lc), full(lc)] + [bias_spec(a) for a in range(nr)],
        out_specs=row,
        out_shape=jax.ShapeDtypeStruct((b, ns, l, LANES), BF),
        compiler_params=_cparams("arbitrary", "arbitrary"),
        name="na_attn",
    )(q, kx, vx, kc, vc, *([bias] * nr))


def _axis_tables(pos, dim):
    inv = ROPE_THETA ** (-jnp.arange(0, dim, 2, dtype=F32) / dim)
    ang = pos.astype(F32)[:, None] * inv[None, :]
    ang = jnp.concatenate([ang, ang], axis=-1)
    return jnp.cos(ang), jnp.sin(ang)


def _rope_tables(n, rope):
    if rope:
        t = jnp.arange(n, dtype=jnp.int32)
        row, col = t // GRID_W, t % GRID_W

        def cs(d):
            cr, sr = _axis_tables(row, d // 2)
            cc, sc = _axis_tables(col, d // 2)
            return jnp.concatenate([cr, cc], axis=-1), jnp.concatenate([sr, sc], axis=-1)

        c32, s32 = cs(MLA_ROPE)
        c64, s64 = cs(DIFF_HD)
    else:
        c32, s32 = jnp.ones((n, MLA_ROPE), F32), jnp.zeros((n, MLA_ROPE), F32)
        c64, s64 = jnp.ones((n, DIFF_HD), F32), jnp.zeros((n, DIFF_HD), F32)
    ones = jnp.ones((n, MLA_NOPE), F32)
    pad = lambda a: jnp.pad(a, ((0, 0), (0, LANES - a.shape[1])))
    cos_q = MLA_SCALE * pad(jnp.concatenate([ones, c32], axis=-1))
    sin_q = MLA_SCALE * pad(jnp.concatenate([jnp.zeros_like(ones), s32], axis=-1))
    return jnp.stack([cos_q, sin_q, pad(c32), pad(s32),
                      jnp.concatenate([c64, c64], axis=-1), jnp.concatenate([s64, s64], axis=-1)])


def _rot_cols(w, d):
    shp = w.shape
    w5 = w.reshape(shp[:-1] + (-1, 2, 2, d // 4))
    r = jnp.concatenate([-w5[..., 1:2, :], w5[..., 0:1, :]], axis=-2)
    return r.reshape(shp)


def _pad_cols(w, n):
    return jnp.pad(w, ((0, 0), (0, n - w.shape[1])))


def _prep_ab_weights(w_in, q_norm, kv_norm, w_uq, w_ukv, w_out):
    o_cq, o_ckv, o_kr = MLA_Q_RANK, MLA_Q_RANK + MLA_KV_RANK, MLA_Q_RANK + MLA_KV_RANK + MLA_ROPE
    o_dq, o_dk = o_kr + _DW, o_kr + 2 * _DW
    w_kr, w_dq, w_dk, w_dv = w_in[:, o_ckv:o_kr], w_in[:, o_kr:o_dq], w_in[:, o_dq:o_dk], w_in[:, o_dk:]
    win = jnp.concatenate([
        w_in[:, :o_ckv], _pad_cols(w_kr, LANES), _pad_cols(_rot_cols(w_kr, MLA_ROPE), LANES),
        w_dq, _rot_cols(w_dq, DIFF_HD), w_dk, _rot_cols(w_dk, DIFF_HD), w_dv], axis=1).astype(BF)
    rq = w_uq.shape[0]
    uq = w_uq.reshape(rq, MLA_HEADS, MLA_NOPE + MLA_ROPE)
    nope, rope = uq[..., :MLA_NOPE], uq[..., MLA_NOPE:]
    zpad = jnp.zeros((rq, MLA_HEADS, LANES - MLA_NOPE - MLA_ROPE), F32)
    main = jnp.concatenate([nope, rope, zpad], axis=-1).reshape(rq, MLA_HEADS * LANES)
    rot = jnp.concatenate([jnp.zeros_like(nope), _rot_cols(rope, MLA_ROPE), zpad], axis=-1)
    wuq = jnp.concatenate([main, rot.reshape(rq, MLA_HEADS * LANES)], axis=1).astype(BF)
    rkv = w_ukv.shape[0]
    ukv = w_ukv.reshape(rkv, MLA_HEADS, MLA_NOPE + MLA_V)
    slot = lambda a: jnp.pad(a, ((0, 0), (0, 0), (0, LANES - a.shape[-1]))).reshape(rkv, MLA_HEADS * LANES)
    wuk = slot(ukv[..., :MLA_NOPE]).astype(BF)
    wuv = slot(ukv[..., MLA_NOPE:]).astype(BF)
    place = np.zeros((LANES, MLA_HEADS * LANES), np.float32)
    for hd in range(MLA_HEADS):
        place[np.arange(MLA_ROPE), hd * LANES + MLA_NOPE + np.arange(MLA_ROPE)] = 1.0
    d = w_out.shape[1]
    wo_a = w_out[:MLA_HEADS * MLA_V].reshape(MLA_HEADS, MLA_V, d)
    wo_a = jnp.pad(wo_a, ((0, 0), (0, LANES - MLA_V), (0, 0))).reshape(MLA_HEADS * LANES, d)
    wo = jnp.concatenate([wo_a, w_out[MLA_HEADS * MLA_V:]], axis=0).astype(BF)
    proj = (win, q_norm.reshape(1, -1), kv_norm.reshape(1, -1), wuq, wuk, wuv, jnp.asarray(place, BF))
    return proj, wo


def _na_bias_table(rpb):
    nh = rpb.shape[0]
    cols = np.arange(GRID_W)
    col_start = np.clip(cols - NA_KW // 2, 0, GRID_W - NA_KW)
    kcol = np.arange(GRID_W)
    valid = (kcol[None, :] >= col_start[:, None]) & (kcol[None, :] < col_start[:, None] + NA_KW)
    dc = kcol[None, :] - cols[:, None] + (NA_KW - 1)
    onehot = (dc[None] == np.arange(2 * NA_KW - 1)[:, None, None]) & valid[None]
    tz = jnp.einsum('hrd,dck->hrck', rpb * LOG2E, jnp.asarray(onehot, F32), precision=lax.Precision.HIGHEST)
    tz = jnp.where(valid[None, None], tz, MASK_VALUE)
    cases = [tz[:, NA_KH - 1 - c:2 * NA_KH - 1 - c] for c in range(NA_KH)]
    tbl = jnp.transpose(jnp.stack(cases), (0, 1, 3, 2, 4))
    return tbl.reshape(NA_KH, nh // 2, 2 * GRID_W, NA_KH * GRID_W)


def _tile(n, pref):
    return pref if n % pref == 0 else n


def kernel(x, c, ctx, c_ctx, w_mod, b_mod, norm_g, a_w_in, a_q_norm, a_kv_norm, a_w_uq, a_w_ukv, b_lambda, b_subln,
           ab_w_out, f_w1, f_w3, f_w2, c_w_qkv, c_b_qkv, c_rpb, c_w_out, c_b_out, m_router, m_w1, m_w3, m_w2):
    b, l, d = x.shape
    lc = ctx.shape[1]
    depth = w_mod.shape[0]
    assert l % GRID_W == 0 and l // GRID_W >= NA_KH

    mod_rows = 16
    cvec = jnp.zeros((mod_rows, d), F32).at[:b].set(c).at[b].set(c_ctx)
    mod = _modulation(cvec, w_mod, b_mod)

    tm_x, tm_c = _tile(l, 512), _tile(lc, 256)
    tq_x, tq_c = _tile(l, ATTN_QUERY_LANES), _tile(lc, ATTN_QUERY_LANES // 2)
    tq2_x = _tile(l, ATTN_QUERY_LANES // 2)
    cs = ctx
    for i in range(depth):
        last = i == depth - 1
        j = i // 2
        mx = mod[i, :b].reshape(b, N_MOD, d)
        mc = mod[i, b].reshape(1, N_MOD, d)
        g = norm_g[i]
        if i % 2 == 0:
            lam_init = 0.8 - 0.6 * math.exp(-0.3 * i)
            proj_w, wo = _prep_ab_weights(a_w_in[j], a_q_norm[j], a_kv_norm[j], a_w_uq[j], a_w_ukv[j], ab_w_out[j])
            qx, kx, vx, dqx, dkx, dvx = _proj_ab(x, mx, g, proj_w, _rope_tables(l, True), tm_x)
            qc, kc, vc, dqc, dkc, dvc = _proj_ab(cs, mc, g, proj_w, _rope_tables(lc, False), tm_c)
            subln = b_subln[j].reshape(1, -1)
            oa = _attention(qx, [(kx, vx), (kc, vc)], tq_x)
            ob = _attention(dqx, [(dkx, dvx), (dkc, dvc)], tq2_x, "diff", b_lambda[j], subln, lam_init)
            ffn_w = (f_w1[j].astype(BF), f_w3[j].astype(BF), f_w2[j].astype(BF))
            x = _outproj([oa, ob], wo, None, x, mx, g, tm_x, "ffn", ffn_w)
            if not last:
                oa = _attention(qc, [(kc, vc)], tq_c)
                ob = _attention(dqc, [(dkc, dvc)], tq_c, "diff", b_lambda[j], subln, lam_init)
                cs = _outproj([oa, ob], wo, None, cs, mc, g, tm_c, "ffn", ffn_w)
        else:
            wqkv = c_w_qkv[j].astype(BF)
            bqkv = c_b_qkv[j].reshape(1, -1)
            wo = c_w_out[j].astype(BF)
            bo = c_b_out[j].reshape(1, -1)
            qx, kx, vx = _proj_na(x, mx, g, wqkv, bqkv, tm_x)
            qc, kc, vc = _proj_na(cs, mc, g, wqkv, bqkv, tm_c)
            o = _na_attention(qx, kx, vx, kc, vc, _na_bias_table(c_rpb[j]))
            r_f32 = _pad_cols(m_router[j], LANES)
            r_hi = r_f32.astype(BF)
            router = (jnp.stack([r_hi, (r_f32 - r_hi.astype(F32)).astype(BF)]),)
            w1, w3, w2 = m_w1[j].astype(BF), m_w3[j].astype(BF), m_w2[j].astype(BF)
            x, *routed = _outproj([o], wo, bo, x, mx, g, tm_x, "router", router)
            x = _moe(x, routed, mx, g, w1, w3, w2, tm_x)
            if not last:
                oc = _attention(qc, [(kc, jnp.swapaxes(vc, 2, 3))], tq_c, "pair")
                cs, *routed = _outproj([oc], wo, bo, cs, mc, g, tm_c, "router", router)
                cs = _moe(cs, routed, mc, g, w1, w3, w2, tm_c)
    return x
```

```python
import functools
import math

import jax
import jax.numpy as jnp
import numpy as np
from jax import lax
from jax.experimental import pallas as pl
from jax.experimental.pallas import tpu as pltpu
from jax.experimental.pallas import tpu_sc as plsc

BF = jnp.bfloat16
F32 = jnp.float32

LANES = 128
VMEM_LIMIT = 56 * 1024 * 1024

GRID_W = 64
EPS = 1e-6
ROPE_THETA = 10000.0
N_MOD = 6

MLA_HEADS = 8
MLA_NOPE = 64
MLA_ROPE = 32
MLA_V = 64
MLA_Q_RANK = 384
MLA_KV_RANK = 256
LOG2E = math.log2(math.e)
MLA_SCALE = (MLA_NOPE + MLA_ROPE) ** -0.5 * LOG2E

DIFF_HEADS = 4
DIFF_HD = 64
DIFF_SCALE = DIFF_HD ** -0.5 * LOG2E

NA_HEADS = 16
NA_HD = 64
NA_KH = 8
NA_KW = 16
NA_SCALE = NA_HD ** -0.5 * LOG2E
NA_SLOTS = NA_HEADS * NA_HD // LANES
MASK_VALUE = -1e30

N_EXPERTS = 8

_NT = (((1,), (1,)), ((), ()))


def _cparams(*sem):
    return pltpu.CompilerParams(dimension_semantics=sem, vmem_limit_bytes=VMEM_LIMIT)


def _dot(a, b):
    return jnp.dot(a, b, preferred_element_type=F32)


def _rms(x, g):
    return x * lax.rsqrt(jnp.mean(x * x, axis=-1, keepdims=True) + EPS) * g


def _norm_mod(x, g, shift, scale):
    return _rms(x, g) * (1 + scale) + shift


def _const_spec(shape):
    return pl.BlockSpec(shape, lambda *_: (0,) * len(shape))


def _mod_spec(mod):
    if mod.shape[0] == 1:
        return pl.BlockSpec((1,) + mod.shape[1:], lambda b, *_: (0, 0, 0))
    return pl.BlockSpec((1,) + mod.shape[1:], lambda b, *_: (b, 0, 0))


def _mod_kernel(c_ref, w_ref, b_ref, o_ref):
    c = c_ref[...]
    sc = c * jax.nn.sigmoid(c)
    o_ref[0] = _dot(sc.astype(BF), w_ref[0].astype(BF)) + b_ref[0]


def _modulation(cvec, w_mod, b_mod):
    depth, d, n = w_mod.shape
    rows = cvec.shape[0]
    return pl.pallas_call(
        _mod_kernel,
        grid=(depth, n // d),
        in_specs=[
            pl.BlockSpec((rows, d), lambda i, j: (0, 0)),
            pl.BlockSpec((1, d, d), lambda i, j: (i, 0, j)),
            pl.BlockSpec((1, 1, d), lambda i, j: (i, 0, j)),
        ],
        out_specs=pl.BlockSpec((1, rows, d), lambda i, j: (i, 0, j)),
        out_shape=jax.ShapeDtypeStruct((depth, rows, n), F32),
        compiler_params=_cparams("arbitrary", "arbitrary"),
        name="modulation",
    )(cvec, w_mod, b_mod.reshape(depth, 1, n))


_Z_CQ = 0
_Z_CKV = MLA_Q_RANK
_Z_KR = _Z_CKV + MLA_KV_RANK
_Z_DQ = _Z_KR + LANES
_DW = 2 * DIFF_HEADS * DIFF_HD
_Z_DK = _Z_DQ + _DW
_Z_DV = _Z_DK + _DW


def _rope(x, cos, sin_signed, quarter):
    lane = lax.broadcasted_iota(jnp.int32, x.shape, 1)
    first = (lane % (2 * quarter)) < quarter
    rx = jnp.where(first, pltpu.roll(x, LANES - quarter, 1), pltpu.roll(x, quarter, 1))
    return x * cos + rx * sin_signed


def _proj_ab_kernel(x_ref, mod_ref, g_ref, win_ref, qn_ref, kvn_ref, wuq_ref, wuk_ref, wuv_ref,
                    place_ref, tab_ref, q_ref, k_ref, v_ref, dq_ref, dk_ref, dv_ref):
    h = _norm_mod(x_ref[0], g_ref[0:1, :], mod_ref[0, 0:1, :], mod_ref[0, 1:2, :])
    z = _dot(h.astype(BF), win_ref[...])
    cqn = _rms(z[:, _Z_CQ:_Z_CKV], qn_ref[...]).astype(BF)
    ckvn = _rms(z[:, _Z_CKV:_Z_KR], kvn_ref[...]).astype(BF)
    q2 = _dot(cqn, wuq_ref[...])
    cos_q, sin_q = tab_ref[0], tab_ref[1]
    for hd in range(MLA_HEADS):
        lo = hd * LANES
        q_ref[0, hd] = _rope(q2[:, lo:lo + LANES], cos_q, sin_q, MLA_ROPE // 4).astype(BF)
    kr = _rope(z[:, _Z_KR:_Z_DQ], tab_ref[2], tab_ref[3], MLA_ROPE // 4).astype(BF)
    kk = _dot(ckvn, wuk_ref[...]) + _dot(kr, place_ref[...])
    vv = _dot(ckvn, wuv_ref[...])
    for hd in range(MLA_HEADS):
        lo = hd * LANES
        k_ref[0, hd] = kk[:, lo:lo + LANES].astype(BF)
        v_ref[0, hd] = vv[:, lo:lo + LANES].T.astype(BF)
    cos_d, sin_d = tab_ref[4], tab_ref[5]
    for hd in range(DIFF_HEADS):
        lo = hd * LANES
        dq = _rope(z[:, _Z_DQ + lo:_Z_DQ + lo + LANES], cos_d, sin_d, DIFF_HD // 4)
        dq_ref[0, hd] = (dq * DIFF_SCALE).astype(BF)
        dk_ref[0, hd] = _rope(z[:, _Z_DK + lo:_Z_DK + lo + LANES], cos_d, sin_d, DIFF_HD // 4).astype(BF)
        dv_ref[0, hd] = z[:, _Z_DV + lo:_Z_DV + lo + LANES].T.astype(BF)


def _proj_ab(x, mod, g, wts, tabs, tm):
    b, t, d = x.shape
    win, qn, kvn, wuq, wuk, wuv, place = wts
    tok = lambda hh: pl.BlockSpec((1, hh, tm, LANES), lambda bi, i: (bi, 0, i, 0))
    shp = lambda hh: jax.ShapeDtypeStruct((b, hh, t, LANES), BF)
    tok_t = lambda hh, dv: pl.BlockSpec((1, hh, dv, tm), lambda bi, i: (bi, 0, 0, i))
    shp_t = lambda hh, dv: jax.ShapeDtypeStruct((b, hh, dv, t), BF)
    return pl.pallas_call(
        _proj_ab_kernel,
        grid=(b, t // tm),
        in_specs=[
            pl.BlockSpec((1, tm, d), lambda bi, i: (bi, i, 0)),
            _mod_spec(mod),
            _const_spec(g.shape),
            _const_spec(win.shape), _const_spec(qn.shape), _const_spec(kvn.shape),
            _const_spec(wuq.shape), _const_spec(wuk.shape), _const_spec(wuv.shape),
            _const_spec(place.shape),
            pl.BlockSpec((6, tm, LANES), lambda bi, i: (0, i, 0)),
        ],
        out_specs=[tok(MLA_HEADS), tok(MLA_HEADS), tok_t(MLA_HEADS, LANES),
                   tok(DIFF_HEADS), tok(DIFF_HEADS), tok_t(DIFF_HEADS, LANES)],
        out_shape=[shp(MLA_HEADS), shp(MLA_HEADS), shp_t(MLA_HEADS, LANES),
                   shp(DIFF_HEADS), shp(DIFF_HEADS), shp_t(DIFF_HEADS, LANES)],
        compiler_params=_cparams("arbitrary", "arbitrary"),
        name="proj_ab",
    )(x, mod, g, win, qn, kvn, wuq, wuk, wuv, place, tabs)


KEY_CHUNK = 256


SUBLANES = 8
ATTN_QUERY_LANES = 1024


def _key_chunks(kv_refs):
    off = 0
    for k_ref, v_ref in kv_refs:
        lk = k_ref.shape[2]
        for c0 in range(0, lk, KEY_CHUNK):
            w = min(KEY_CHUNK, lk - c0)
            yield k_ref, v_ref, c0, w, off
            off += w


def _attn_kernel(*refs, nseg, mode, lam_init):
    q_ref = refs[0]
    kv_refs = [(refs[1 + 2 * s], refs[2 + 2 * s]) for s in range(nseg)]
    o_ref, s0_ref, s1_ref, mm0_ref, mm1_ref = refs[-5:]
    nmap, _, tq = s0_ref.shape
    maps = range(nmap)
    fold = lambda a: a.reshape(a.shape[0] // SUBLANES, SUBLANES, tq)
    t = pl.program_id(0)

    @pl.when(t == 0)
    def _():
        s1_ref[...] = jnp.zeros_like(s1_ref)
        mm1_ref[...] = jnp.zeros_like(mm1_ref)

    def step(s_cur, mm_cur, s_prv, mm_prv):
        q = q_ref[0, 0]
        if mode == "single":
            qms = [q]
        else:
            lane = lax.broadcasted_iota(jnp.int32, q.shape, 1)
            zero = jnp.zeros_like(q)
            qms = [jnp.where(lane < DIFF_HD, q, zero), jnp.where(lane >= DIFF_HD, q, zero)]
        m_prv = [jnp.max(mm_prv[j], axis=0, keepdims=True) for j in maps]
        mms = [jnp.full((SUBLANES, tq), -jnp.inf, F32)] * nmap
        lls = [jnp.zeros((SUBLANES, tq), F32)] * nmap
        o_ts = [jnp.zeros((LANES, tq), F32)] * nmap
        for k_ref, v_ref, c0, w, off in _key_chunks(kv_refs):
            kc = k_ref[0, 0, c0:c0 + w, :]
            vc = v_ref[0, 0, :, c0:c0 + w]
            for j in maps:
                s = lax.dot_general(kc, qms[j], _NT, preferred_element_type=F32)
                s_cur[j, off:off + w, :] = s
                mms[j] = jnp.maximum(mms[j], jnp.max(fold(s), axis=0))
            for j in maps:
                p = jnp.exp2(s_prv[j, off:off + w, :] - m_prv[j])
                lls[j] = lls[j] + jnp.sum(fold(p), axis=0)
                o_ts[j] = o_ts[j] + _dot(vc, p.astype(BF))
        for j in maps:
            mm_cur[j] = mms[j]
        outs = [(o_ts[j] / jnp.sum(lls[j], axis=0, keepdims=True)).T for j in maps]
        if mode == "single":
            o_ref[0, 0] = outs[0].astype(BF)
        elif mode == "pair":
            o_ref[0, 0] = jnp.where(lane < DIFF_HD, outs[0], outs[1]).astype(BF)
        else:
            lam_ref, subln_ref = refs[1 + 2 * nseg], refs[2 + 2 * nseg]
            lv = lam_ref[...]
            lam = (jnp.exp(jnp.sum(lv[0:1] * lv[1:2], axis=-1, keepdims=True))
                   - jnp.exp(jnp.sum(lv[2:3] * lv[3:4], axis=-1, keepdims=True)) + lam_init)
            o_ref[0, 0] = (_rms(outs[0] - lam * outs[1], subln_ref[...]) * (1 - lam_init)).astype(BF)

    pl.when(t % 2 == 0)(lambda: step(s0_ref, mm0_ref, s1_ref, mm1_ref))
    pl.when(t % 2 == 1)(lambda: step(s1_ref, mm1_ref, s0_ref, mm0_ref))


def _attention(q, kvs, tq, mode="single", lam_vecs=None, subln=None, lam_init=0.0):
    b, nh, lq, _ = q.shape
    diff = mode == "diff"
    nmap = 1 if mode == "single" else 2
    nq = lq // tq
    ntile = b * nh * nq

    def tile(t):
        t = jnp.clip(t, 0, ntile - 1)
        return t // (nh * nq), (t // nq) % nh, t % nq

    cur_head = lambda t: tile(t)[:2] + (0, 0)
    prv_head = lambda t: tile(t - 1)[:2] + (0, 0)
    in_specs = [pl.BlockSpec((1, 1, tq, LANES), lambda t: tile(t) + (0,))]
    args = [q]
    for k, v in kvs:
        lk = k.shape[2]
        in_specs += [pl.BlockSpec((1, 1, lk, LANES), cur_head), pl.BlockSpec((1, 1, v.shape[2], lk), prv_head)]
        args += [k, v]
    if diff:
        in_specs += [_const_spec(lam_vecs.shape), _const_spec(subln.shape)]
        args += [lam_vecs, subln]
    nkeys = sum(k.shape[2] for k, _ in kvs)
    return pl.pallas_call(
        functools.partial(_attn_kernel, nseg=len(kvs), mode=mode, lam_init=lam_init),
        grid=(ntile + 1,),
        in_specs=in_specs,
        out_specs=pl.BlockSpec((1, 1, tq, LANES), lambda t: tile(t - 1) + (0,)),
        out_shape=jax.ShapeDtypeStruct((b, nh, lq, LANES), BF),
        scratch_shapes=([pltpu.VMEM((nmap, nkeys, tq), F32)] * 2 + [pltpu.VMEM((nmap, SUBLANES, tq), F32)] * 2),
        compiler_params=_cparams("arbitrary"),
        name="attn_diff" if diff else "attn_mla",
    )(*args)


ROUTER_PIECES = 2


def _outproj_kernel(*refs, n_in, has_bias, tail):
    o_refs = refs[:n_in]
    w_ref = refs[n_in]
    pos = n_in + 1
    b_ref = None
    if has_bias:
        b_ref = refs[pos]
        pos += 1
    x_ref, mod_ref, g_ref = refs[pos:pos + 3]
    rest = refs[pos + 3:]
    tm = x_ref.shape[1]

    def residual(rows):
        o = jnp.concatenate([r[0, h, rows, :] for r in o_refs for h in range(r.shape[1])], axis=-1)
        y = _dot(o, w_ref[...])
        if has_bias:
            y = y + b_ref[...]
        x1 = x_ref[0, rows, :] + mod_ref[0, 2:3, :] * _rms(y, g_ref[1:2, :])
        return x1, _norm_mod(x1, g_ref[2:3, :], mod_ref[0, 3:4, :], mod_ref[0, 4:5, :])

    if tail == "ffn":
        w1_ref, w3_ref, w2_ref, out_ref = rest
        x1, h = residual(slice(0, tm))
        out_ref[0] = x1 + mod_ref[0, 5:6, :] * _rms(_swiglu(h.astype(BF), w1_ref, w3_ref, w2_ref), g_ref[3:4, :])
    else:
        router_ref, tri_ref, out_ref = rest[:3]
        step = tm // ROUTER_PIECES
        halves = [slice(r0, r0 + step) for r0 in range(0, tm, step)]
        parts = [residual(rows) for rows in halves]
        for rows, (x1, _) in zip(halves, parts):
            out_ref[0, rows, :] = x1
        first = (pl.program_id(0) == 0) & (pl.program_id(1) == 0)
        _route([h for _, h in parts], halves, first, router_ref, tri_ref, *rest[3:])


def _outproj(os_, w, bias, x, mod, g, tm, tail, tail_args):
    b, t, d = x.shape
    n = b * t
    nt = t // tm
    in_specs = [pl.BlockSpec((1, o.shape[1], tm, LANES), lambda bi, i: (bi, 0, i, 0)) for o in os_]
    in_specs.append(_const_spec(w.shape))
    args = list(os_) + [w]
    if bias is not None:
        in_specs.append(_const_spec(bias.shape))
        args.append(bias)
    x_spec = pl.BlockSpec((1, tm, d), lambda bi, i: (bi, i, 0))
    in_specs += [x_spec, _mod_spec(mod), _const_spec(g.shape)]
    args += [x, mod, g]
    out_specs, out_shape, scratch = x_spec, jax.ShapeDtypeStruct((b, t, d), F32), []
    if tail == "ffn":
        in_specs += [_resident_spec(a.shape, lambda bi, i: (0, 0)) for a in tail_args]
        args += list(tail_args)
    else:
        router, = tail_args
        tri = jnp.asarray(np.tril(np.ones((tm, tm), np.float32), -1), BF)
        in_specs += [_const_spec(router.shape), _const_spec(tri.shape)]
        args += [router, tri]
        rows = lambda wd: pl.BlockSpec((tm, wd), lambda bi, i: (bi * nt + i, 0))
        out_specs = [x_spec, pl.BlockSpec((PACK_CHUNKS, tm, LANES), lambda bi, i: (0, bi * nt + i, 0)),
                     rows(8), rows(8), pl.BlockSpec((1, LANES), lambda bi, i: (0, 0))]
        out_shape = [out_shape, jax.ShapeDtypeStruct((PACK_CHUNKS, n, LANES), jnp.int32),
                     jax.ShapeDtypeStruct((n, 8), jnp.int32), jax.ShapeDtypeStruct((n, 8), F32),
                     jax.ShapeDtypeStruct((1, LANES), F32)]
        scratch = [pltpu.VMEM((1, LANES), F32)]
    return pl.pallas_call(
        functools.partial(_outproj_kernel, n_in=len(os_), has_bias=bias is not None, tail=tail),
        grid=(b, nt),
        in_specs=in_specs,
        out_specs=out_specs,
        out_shape=out_shape,
        scratch_shapes=scratch,
        compiler_params=_cparams("arbitrary", "arbitrary"),
        name="outproj_" + tail,
    )(*args)


MXU_TILE = 256


def _ff_chunks(f):
    tiles = f // MXU_TILE
    if f % MXU_TILE or tiles < 2:
        return [(0, f)]
    first = (tiles + 1) // 2 * MXU_TILE
    return [(0, first), (first, f - first)]


def _swiglu(hn, w1_ref, w3_ref, w2_ref):
    y = None
    for lo, n in _ff_chunks(w1_ref.shape[1]):
        a = _dot(hn, w1_ref[:, lo:lo + n])
        act = (a * jax.nn.sigmoid(a) * _dot(hn, w3_ref[:, lo:lo + n])).astype(BF)
        part = _dot(act, w2_ref[lo:lo + n, :])
        y = part if y is None else y + part
    return y


def _resident_spec(shape, index_map):
    return pl.BlockSpec(shape, index_map, pipeline_mode=pl.Buffered(1))


MOE_ROW_TILE = 512
PACK_CHUNKS = 4
SC_WINDOW = 128
_HI_MASK = -65536
_LO_MASK = 65535


def _pack_rows(v, out_ref, rows=slice(None)):
    half = v.shape[1] // 2
    vb = v.astype(BF).astype(F32)
    lo = (pltpu.bitcast(vb[:, :half], jnp.int32) >> 16) & _LO_MASK
    hi = pltpu.bitcast(vb[:, half:], jnp.int32) & _HI_MASK
    w = lo | hi
    for c in range(PACK_CHUNKS):
        out_ref[c, rows, :] = w[:, c * LANES:(c + 1) * LANES]


def _unpack_rows(ref):
    w = jnp.concatenate([ref[c] for c in range(PACK_CHUNKS)], axis=-1)
    lo = pltpu.bitcast(w << 16, F32)
    hi = pltpu.bitcast(w & _HI_MASK, F32)
    return jnp.concatenate([lo, hi], axis=-1)


def _route(hs, row_slices, first, router_ref, tri_ref, hp_ref, meta_ref, gate_ref, cnt_ref, carry_ref):
    @pl.when(first)
    def _():
        carry_ref[...] = jnp.zeros_like(carry_ref)

    pieces = range(len(hs))
    for h, rows in zip(hs, row_slices):
        _pack_rows(h, hp_ref, rows)
    his = [h.astype(BF) for h in hs]
    los = [(hs[p] - his[p].astype(F32)).astype(BF) for p in pieces]
    raw = [_dot(his[p], router_ref[0]) + (_dot(his[p], router_ref[1]) + _dot(los[p], router_ref[0])) for p in pieces]
    lanes = [lax.broadcasted_iota(jnp.int32, r.shape, 1) for r in raw]
    lgs = [jnp.where(lanes[p] < N_EXPERTS, raw[p], -jnp.inf) for p in pieces]
    m1s = [jnp.max(lg, axis=-1, keepdims=True) for lg in lgs]
    i1s = [jnp.min(jnp.where(lgs[p] == m1s[p], lanes[p], LANES), axis=-1, keepdims=True) for p in pieces]
    rests = [jnp.where(lanes[p] == i1s[p], -jnp.inf, lgs[p]) for p in pieces]
    m2s = [jnp.max(r, axis=-1, keepdims=True) for r in rests]
    i2s = [jnp.min(jnp.where(rests[p] == m2s[p], lanes[p], LANES), axis=-1, keepdims=True) for p in pieces]
    rows_cat = lambda parts: jnp.concatenate(parts, axis=0)
    m1, m2, i1, i2, lane = rows_cat(m1s), rows_cat(m2s), rows_cat(i1s), rows_cat(i2s), rows_cat(lanes)
    e2 = jnp.exp(m2 - m1)
    denom = 1.0 + e2
    assigned = jnp.where((lane == i1) | (lane == i2), 1.0, 0.0)
    ranks = _dot(tri_ref[...], assigned.astype(BF)) + carry_ref[...]
    r1 = jnp.sum(jnp.where(lane == i1, ranks, 0.0), axis=-1, keepdims=True).astype(jnp.int32)
    r2 = jnp.sum(jnp.where(lane == i2, ranks, 0.0), axis=-1, keepdims=True).astype(jnp.int32)
    carry_ref[...] += jnp.sum(assigned, axis=0, keepdims=True)
    cnt_ref[...] = carry_ref[...]
    col = lax.broadcasted_iota(jnp.int32, meta_ref.shape, 1)
    meta_ref[...] = jnp.where(col == 0, i1, jnp.where(col == 1, i2, jnp.where(col == 2, r1, r2)))
    gate_ref[...] = jnp.where(col == 0, 1.0 / denom, e2 / denom)


def _sc_mesh():
    return plsc.VectorSubcoreMesh(core_axis_name="core", subcore_axis_name="subcore")


def _sc_scatter_rows(rows, idx_a, idx_b, n_out):
    nrows = rows.shape[0]

    @pl.kernel(out_type=jax.ShapeDtypeStruct((n_out, LANES), rows.dtype), mesh=_sc_mesh(), scratch_types=[])
    def scatter_kernel(x_hbm, ia_hbm, ib_hbm, o_hbm):
        def body(x_vmem, ia_vmem, ib_vmem):
            pltpu.sync_copy(x_vmem, o_hbm.at[ia_vmem.at[0]])
            pltpu.sync_copy(x_vmem, o_hbm.at[ib_vmem.at[0]])

        pltpu.emit_pipeline(
            body,
            grid=(nrows // SC_WINDOW,),
            in_specs=[pl.BlockSpec((SC_WINDOW, LANES), lambda i: (i, 0)),
                      pl.BlockSpec((1, SC_WINDOW), lambda i: (0, i)),
                      pl.BlockSpec((1, SC_WINDOW), lambda i: (0, i))],
            out_specs=[],
            core_axis_name=("core", "subcore"),
            dimension_semantics=(pltpu.PARALLEL,),
        )(x_hbm, ia_hbm, ib_hbm)

    return scatter_kernel(rows, idx_a, idx_b)


def _sc_gather_rows(table, idx_a, idx_b):
    nrows = idx_a.shape[1]
    out = jax.ShapeDtypeStruct((nrows, LANES), table.dtype)

    @pl.kernel(out_type=(out, out), mesh=_sc_mesh(), scratch_types=[])
    def gather_kernel(t_hbm, ia_hbm, ib_hbm, oa_hbm, ob_hbm):
        def body(ia_vmem, ib_vmem, oa_vmem, ob_vmem):
            pltpu.sync_copy(t_hbm.at[ia_vmem.at[0]], oa_vmem)
            pltpu.sync_copy(t_hbm.at[ib_vmem.at[0]], ob_vmem)

        pltpu.emit_pipeline(
            body,
            grid=(nrows // SC_WINDOW,),
            in_specs=[pl.BlockSpec((1, SC_WINDOW), lambda i: (0, i)),
                      pl.BlockSpec((1, SC_WINDOW), lambda i: (0, i))],
            out_specs=[pl.BlockSpec((SC_WINDOW, LANES), lambda i: (i, 0)),
                       pl.BlockSpec((SC_WINDOW, LANES), lambda i: (i, 0))],
            core_axis_name=("core", "subcore"),
            dimension_semantics=(pltpu.PARALLEL,),
        )(ia_hbm, ib_hbm, oa_hbm, ob_hbm)

    return gather_kernel(table, idx_a, idx_b)


def _experts_kernel(te_ref, tv_ref, xs_ref, w1_ref, w3_ref, w2_ref, ys_ref):
    del te_ref
    valid = tv_ref[pl.program_id(0)]

    @pl.when(valid > 0)
    def _():
        row = lax.broadcasted_iota(jnp.int32, (xs_ref.shape[1], 1), 0)
        hn = jnp.where(row < valid, _unpack_rows(xs_ref), 0.0).astype(BF)
        _pack_rows(_swiglu(hn, w1_ref.at[0], w3_ref.at[0], w2_ref.at[0]), ys_ref)

    @pl.when(valid == 0)
    def _():
        ys_ref[...] = jnp.zeros_like(ys_ref)


def _experts(xs, tile_expert, tile_valid, w1, w3, w2):
    _, p, _ = xs.shape
    tr = MOE_ROW_TILE
    rows = pl.BlockSpec((PACK_CHUNKS, tr, LANES), lambda j, te, tv: (0, j, 0))
    weights = lambda w: pl.BlockSpec((1,) + w.shape[1:], lambda j, te, tv: (te[j], 0, 0))
    return pl.pallas_call(
        _experts_kernel,
        grid_spec=pltpu.PrefetchScalarGridSpec(
            num_scalar_prefetch=2,
            grid=(p // tr,),
            in_specs=[rows, weights(w1), weights(w3), weights(w2)],
            out_specs=rows,
        ),
        out_shape=jax.ShapeDtypeStruct(xs.shape, jnp.int32),
        compiler_params=_cparams("arbitrary"),
        name="moe_experts",
    )(tile_expert, tile_valid, xs, w1, w3, w2)


def _combine_kernel(x_ref, ya_ref, yb_ref, gate_ref, mod_ref, g_ref, out_ref):
    gates = gate_ref[...]
    fx = gates[:, 0:1] * _unpack_rows(ya_ref) + gates[:, 1:2] * _unpack_rows(yb_ref)
    out_ref[0] = x_ref[0] + mod_ref[0, 5:6, :] * _rms(fx, g_ref[3:4, :])


def _combine(x, ya, yb, gates, mod, g, tm):
    b, t, d = x.shape
    nt = t // tm
    packed = pl.BlockSpec((PACK_CHUNKS, tm, LANES), lambda bi, i: (0, bi * nt + i, 0))
    return pl.pallas_call(
        _combine_kernel,
        grid=(b, nt),
        in_specs=[
            pl.BlockSpec((1, tm, d), lambda bi, i: (bi, i, 0)),
            packed, packed,
            pl.BlockSpec((tm, gates.shape[1]), lambda bi, i: (bi * nt + i, 0)),
            _mod_spec(mod),
            _const_spec(g.shape),
        ],
        out_specs=pl.BlockSpec((1, tm, d), lambda bi, i: (bi, i, 0)),
        out_shape=jax.ShapeDtypeStruct((b, t, d), F32),
        compiler_params=_cparams("arbitrary", "arbitrary"),
        name="moe_combine",
    )(x, ya, yb, gates, mod, g)


def _moe(x, routed, mod, g, w1, w3, w2, tm):
    b, t, d = x.shape
    n = b * t
    ne = w1.shape[0]
    tr = MOE_ROW_TILE
    assert d == 2 * PACK_CHUNKS * LANES and (PACK_CHUNKS * n) % (SC_WINDOW * 32) == 0
    hp, meta, gates, counts = routed

    ntile = 2 * n // tr + ne
    p = ntile * tr
    cnt = counts[0, :ne].astype(jnp.int32)
    tiles = (cnt + tr - 1) // tr
    tile_end = jnp.cumsum(tiles)
    tile_start = tile_end - tiles
    base = tile_start * tr
    eids = jnp.arange(ne, dtype=jnp.int32)
    base_of = lambda e: jnp.sum(jnp.where(e[:, None] == eids[None, :], base[None, :], 0), axis=-1)
    pos_a = base_of(meta[:, 0]) + meta[:, 2]
    pos_b = base_of(meta[:, 1]) + meta[:, 3]
    chunk = jnp.arange(PACK_CHUNKS, dtype=jnp.int32)[:, None] * p
    idx_a = (chunk + pos_a[None, :]).reshape(1, PACK_CHUNKS * n)
    idx_b = (chunk + pos_b[None, :]).reshape(1, PACK_CHUNKS * n)
    tj = jnp.arange(ntile, dtype=jnp.int32)
    tile_expert = jnp.minimum(jnp.sum(tj[:, None] >= tile_end[None, :], axis=-1), ne - 1).astype(jnp.int32)
    done = jnp.sum(jnp.where(tile_expert[:, None] == eids[None, :], tile_start[None, :], 0), axis=-1)
    left = jnp.sum(jnp.where(tile_expert[:, None] == eids[None, :], cnt[None, :], 0), axis=-1) - (tj - done) * tr
    tile_valid = jnp.where(tj < tile_end[-1], jnp.clip(left, 0, tr), 0).astype(jnp.int32)

    xs = _sc_scatter_rows(hp.reshape(PACK_CHUNKS * n, LANES), idx_a, idx_b, PACK_CHUNKS * p)
    ys = _experts(xs.reshape(PACK_CHUNKS, p, LANES), tile_expert, tile_valid, w1, w3, w2)
    ya, yb = _sc_gather_rows(ys.reshape(PACK_CHUNKS * p, LANES), idx_a, idx_b)
    shp = (PACK_CHUNKS, n, LANES)
    return _combine(x, ya.reshape(shp), yb.reshape(shp), gates, mod, g, tm)


def _proj_na_kernel(x_ref, mod_ref, g_ref, w_ref, b_ref, q_ref, k_ref, v_ref):
    h = _norm_mod(x_ref[0], g_ref[0:1, :], mod_ref[0, 0:1, :], mod_ref[0, 1:2, :])
    z = _dot(h.astype(BF), w_ref[...]) + b_ref[...]
    n = NA_SLOTS * LANES
    for s in range(NA_SLOTS):
        lo = s * LANES
        q_ref[0, s] = (z[:, lo:lo + LANES] * NA_SCALE).astype(BF)
        k_ref[0, s] = z[:, n + lo:n + lo + LANES].astype(BF)
        v_ref[0, s] = z[:, 2 * n + lo:2 * n + lo + LANES].astype(BF)


def _proj_na(x, mod, g, w, bias, tm):
    b, t, d = x.shape
    tok = pl.BlockSpec((1, NA_SLOTS, tm, LANES), lambda bi, i: (bi, 0, i, 0))
    shp = jax.ShapeDtypeStruct((b, NA_SLOTS, t, LANES), BF)
    return pl.pallas_call(
        _proj_na_kernel,
        grid=(b, t // tm),
        in_specs=[
            pl.BlockSpec((1, tm, d), lambda bi, i: (bi, i, 0)),
            _mod_spec(mod),
            _const_spec(g.shape),
            _const_spec(w.shape),
            _const_spec(bias.shape),
        ],
        out_specs=[tok, tok, tok],
        out_shape=[shp, shp, shp],
        compiler_params=_cparams("arbitrary", "arbitrary"),
        name="proj_na",
    )(x, mod, g, w, bias)


NA_ROWS_PER_STEP = 2
NA_GROUP = 4


def _na_window_start(r, rows):
    return jnp.clip(r - NA_KH // 2, 0, rows - NA_KH)


def _na_kernel(q_ref, kx_ref, vx_ref, kc_ref, vc_ref, *rest, rows):
    bias_refs, o_ref = rest[:NA_ROWS_PER_STEP], rest[NA_ROWS_PER_STEP]
    lane = lax.broadcasted_iota(jnp.int32, (GRID_W, LANES), 1)
    problems = [(a, s) for a in range(NA_ROWS_PER_STEP) for s in range(NA_SLOTS)]
    starts = [pl.multiple_of(_na_window_start(pl.program_id(1) * NA_ROWS_PER_STEP + a, rows) * GRID_W, GRID_W)
              for a in range(NA_ROWS_PER_STEP)]
    for g0 in range(0, len(problems), NA_GROUP):
        _na_group(problems[g0:g0 + NA_GROUP], starts, lane, q_ref, kx_ref, vx_ref, kc_ref, vc_ref, bias_refs, o_ref)


def _na_group(work, starts, lane, q_ref, kx_ref, vx_ref, kc_ref, vc_ref, bias_refs, o_ref):
    nwin = NA_KH * GRID_W
    idx = range(len(work))
    sws, scs = [], []
    for a, s in work:
        q = q_ref[0, s, a * GRID_W:(a + 1) * GRID_W, :]
        zero = jnp.zeros_like(q)
        q2 = jnp.concatenate([jnp.where(lane < NA_HD, q, zero), jnp.where(lane >= NA_HD, q, zero)], axis=0)
        kw = kx_ref[0, s, pl.ds(starts[a], nwin), :]
        sws.append(lax.dot_general(q2, kw, _NT, preferred_element_type=F32) + bias_refs[a][0, s])
        scs.append(lax.dot_general(q2, kc_ref[0, s], _NT, preferred_element_type=F32))
    ms = [jnp.maximum(jnp.max(sws[i], axis=-1, keepdims=True), jnp.max(scs[i], axis=-1, keepdims=True))
          for i in idx]
    pws = [jnp.exp2(sws[i] - ms[i]) for i in idx]
    pcs = [jnp.exp2(scs[i] - ms[i]) for i in idx]
    ls = [jnp.sum(pws[i], axis=-1, keepdims=True) + jnp.sum(pcs[i], axis=-1, keepdims=True) for i in idx]
    for i, (a, s) in enumerate(work):
        vw = vx_ref[0, s, pl.ds(starts[a], nwin), :]
        o2 = (_dot(pws[i].astype(BF), vw) + _dot(pcs[i].astype(BF), vc_ref[0, s])) / ls[i]
        o_ref[0, s, a * GRID_W:(a + 1) * GRID_W, :] = jnp.where(lane < NA_HD, o2[:GRID_W], o2[GRID_W:]).astype(BF)


def _na_attention(q, kx, vx, kc, vc, bias):
    b, ns, l, _ = q.shape
    rows = l // GRID_W
    lc = kc.shape[2]
    nr = NA_ROWS_PER_STEP
    assert rows % nr == 0

    def bias_spec(a):
        def index_map(bi, i):
            r = i * nr + a
            return (r - _na_window_start(r, rows), 0, 0, 0)
        return pl.BlockSpec((1,) + bias.shape[1:], index_map)

    full = lambda n: pl.BlockSpec((1, ns, n, LANES), lambda bi, i: (bi, 0, 0, 0))
    row = pl.BlockSpec((1, ns, nr * GRID_W, LANES), lambda bi, i: (bi, 0, i, 0))
    return pl.pallas_call(
        functools.partial(_na_kernel, rows=rows),
        grid=(b, rows // nr),
        in_specs=[row, full(l), full(l), full(lc), full(lc)] + [bias_spec(a) for a in range(nr)],
        out_specs=row,
        out_shape=jax.ShapeDtypeStruct((b, ns, l, LANES), BF),
        compiler_params=_cparams("arbitrary", "arbitrary"),
        name="na_attn",
    )(q, kx, vx, kc, vc, *([bias] * nr))


def _axis_tables(pos, dim):
    inv = ROPE_THETA ** (-jnp.arange(0, dim, 2, dtype=F32) / dim)
    ang = pos.astype(F32)[:, None] * inv[None, :]
    ang = jnp.concatenate([ang, ang], axis=-1)
    return jnp.cos(ang), jnp.sin(ang)


def _rope_tables(n, rope):
    if rope:
        t = jnp.arange(n, dtype=jnp.int32)
        row, col = t // GRID_W, t % GRID_W

        def cs(d):
            cr, sr = _axis_tables(row, d // 2)
            cc, sc = _axis_tables(col, d // 2)
            sign = np.where(np.arange(d) % (d // 2) < d // 4, -1.0, 1.0).astype(np.float32)
            return jnp.concatenate([cr, cc], axis=-1), jnp.concatenate([sr, sc], axis=-1) * sign

        c32, s32 = cs(MLA_ROPE)
        c64, s64 = cs(DIFF_HD)
    else:
        c32, s32 = jnp.ones((n, MLA_ROPE), F32), jnp.zeros((n, MLA_ROPE), F32)
        c64, s64 = jnp.ones((n, DIFF_HD), F32), jnp.zeros((n, DIFF_HD), F32)
    ones = jnp.ones((n, MLA_NOPE), F32)
    pad = lambda a: jnp.pad(a, ((0, 0), (0, LANES - a.shape[1])))
    cos_q = MLA_SCALE * pad(jnp.concatenate([ones, c32], axis=-1))
    sin_q = MLA_SCALE * pad(jnp.concatenate([jnp.zeros_like(ones), s32], axis=-1))
    return jnp.stack([cos_q, sin_q, pad(c32), pad(s32),
                      jnp.concatenate([c64, c64], axis=-1), jnp.concatenate([s64, s64], axis=-1)])


def _pad_cols(w, n):
    return jnp.pad(w, ((0, 0), (0, n - w.shape[1])))


def _prep_ab_weights(w_in, q_norm, kv_norm, w_uq, w_ukv, w_out):
    o_cq, o_ckv, o_kr = MLA_Q_RANK, MLA_Q_RANK + MLA_KV_RANK, MLA_Q_RANK + MLA_KV_RANK + MLA_ROPE
    o_dq, o_dk = o_kr + _DW, o_kr + 2 * _DW
    w_kr, w_dq, w_dk, w_dv = w_in[:, o_ckv:o_kr], w_in[:, o_kr:o_dq], w_in[:, o_dq:o_dk], w_in[:, o_dk:]
    win = jnp.concatenate([w_in[:, :o_ckv], _pad_cols(w_kr, LANES), w_dq, w_dk, w_dv], axis=1).astype(BF)
    rq = w_uq.shape[0]
    uq = w_uq.reshape(rq, MLA_HEADS, MLA_NOPE + MLA_ROPE)
    uq = jnp.pad(uq, ((0, 0), (0, 0), (0, LANES - MLA_NOPE - MLA_ROPE)))
    wuq = uq.reshape(rq, MLA_HEADS * LANES).astype(BF)
    rkv = w_ukv.shape[0]
    ukv = w_ukv.reshape(rkv, MLA_HEADS, MLA_NOPE + MLA_V)
    slot = lambda a: jnp.pad(a, ((0, 0), (0, 0), (0, LANES - a.shape[-1]))).reshape(rkv, MLA_HEADS * LANES)
    wuk = slot(ukv[..., :MLA_NOPE]).astype(BF)
    wuv = slot(ukv[..., MLA_NOPE:]).astype(BF)
    place = np.zeros((LANES, MLA_HEADS * LANES), np.float32)
    for hd in range(MLA_HEADS):
        place[np.arange(MLA_ROPE), hd * LANES + MLA_NOPE + np.arange(MLA_ROPE)] = 1.0
    d = w_out.shape[1]
    wo_a = w_out[:MLA_HEADS * MLA_V].reshape(MLA_HEADS, MLA_V, d)
    wo_a = jnp.pad(wo_a, ((0, 0), (0, LANES - MLA_V), (0, 0))).reshape(MLA_HEADS * LANES, d)
    wo = jnp.concatenate([wo_a, w_out[MLA_HEADS * MLA_V:]], axis=0).astype(BF)
    proj = (win, q_norm.reshape(1, -1), kv_norm.reshape(1, -1), wuq, wuk, wuv, jnp.asarray(place, BF))
    return proj, wo


def _na_bias_table(rpb):
    nh = rpb.shape[0]
    cols = np.arange(GRID_W)
    col_start = np.clip(cols - NA_KW // 2, 0, GRID_W - NA_KW)
    kcol = np.arange(GRID_W)
    valid = (kcol[None, :] >= col_start[:, None]) & (kcol[None, :] < col_start[:, None] + NA_KW)
    dc = kcol[None, :] - cols[:, None] + (NA_KW - 1)
    onehot = (dc[None] == np.arange(2 * NA_KW - 1)[:, None, None]) & valid[None]
    tz = jnp.einsum('hrd,dck->hrck', rpb * LOG2E, jnp.asarray(onehot, F32), precision=lax.Precision.HIGHEST)
    tz = jnp.where(valid[None, None], tz, MASK_VALUE)
    cases = [tz[:, NA_KH - 1 - c:2 * NA_KH - 1 - c] for c in range(NA_KH)]
    tbl = jnp.transpose(jnp.stack(cases), (0, 1, 3, 2, 4))
    return tbl.reshape(NA_KH, nh // 2, 2 * GRID_W, NA_KH * GRID_W)


def _tile(n, pref):
    return pref if n % pref == 0 else n


def kernel(x, c, ctx, c_ctx, w_mod, b_mod, norm_g, a_w_in, a_q_norm, a_kv_norm, a_w_uq, a_w_ukv, b_lambda, b_subln,
           ab_w_out, f_w1, f_w3, f_w2, c_w_qkv, c_b_qkv, c_rpb, c_w_out, c_b_out, m_router, m_w1, m_w3, m_w2):
    b, l, d = x.shape
    lc = ctx.shape[1]
    depth = w_mod.shape[0]
    assert l % GRID_W == 0 and l // GRID_W >= NA_KH

    mod_rows = 16
    cvec = jnp.zeros((mod_rows, d), F32).at[:b].set(c).at[b].set(c_ctx)
    mod = _modulation(cvec, w_mod, b_mod)

    tm_x, tm_c = _tile(l, 512), _tile(lc, 256)
    tq_x, tq_c = _tile(l, ATTN_QUERY_LANES), _tile(lc, ATTN_QUERY_LANES // 2)
    tq2_x = _tile(l, ATTN_QUERY_LANES // 2)
    cs = ctx
    for i in range(depth):
        last = i == depth - 1
        j = i // 2
        mx = mod[i, :b].reshape(b, N_MOD, d)
        mc = mod[i, b].reshape(1, N_MOD, d)
        g = norm_g[i]
        if i % 2 == 0:
            lam_init = 0.8 - 0.6 * math.exp(-0.3 * i)
            proj_w, wo = _prep_ab_weights(a_w_in[j], a_q_norm[j], a_kv_norm[j], a_w_uq[j], a_w_ukv[j], ab_w_out[j])
            qx, kx, vx, dqx, dkx, dvx = _proj_ab(x, mx, g, proj_w, _rope_tables(l, True), tm_x)
            qc, kc, vc, dqc, dkc, dvc = _proj_ab(cs, mc, g, proj_w, _rope_tables(lc, False), tm_c)
            subln = b_subln[j].reshape(1, -1)
            oa = _attention(qx, [(kx, vx), (kc, vc)], tq_x)
            ob = _attention(dqx, [(dkx, dvx), (dkc, dvc)], tq2_x, "diff", b_lambda[j], subln, lam_init)
            ffn_w = (f_w1[j].astype(BF), f_w3[j].astype(BF), f_w2[j].astype(BF))
            x = _outproj([oa, ob], wo, None, x, mx, g, tm_x, "ffn", ffn_w)
            if not last:
                oa = _attention(qc, [(kc, vc)], tq_c)
                ob = _attention(dqc, [(dkc, dvc)], tq_c, "diff", b_lambda[j], subln, lam_init)
                cs = _outproj([oa, ob], wo, None, cs, mc, g, tm_c, "ffn", ffn_w)
        else:
            wqkv = c_w_qkv[j].astype(BF)
            bqkv = c_b_qkv[j].reshape(1, -1)
            wo = c_w_out[j].astype(BF)
            bo = c_b_out[j].reshape(1, -1)
            qx, kx, vx = _proj_na(x, mx, g, wqkv, bqkv, tm_x)
            qc, kc, vc = _proj_na(cs, mc, g, wqkv, bqkv, tm_c)
            o = _na_attention(qx, kx, vx, kc, vc, _na_bias_table(c_rpb[j]))
            r_f32 = _pad_cols(m_router[j], LANES)
            r_hi = r_f32.astype(BF)
            router = (jnp.stack([r_hi, (r_f32 - r_hi.astype(F32)).astype(BF)]),)
            w1, w3, w2 = m_w1[j].astype(BF), m_w3[j].astype(BF), m_w2[j].astype(BF)
            x, *routed = _outproj([o], wo, bo, x, mx, g, tm_x, "router", router)
            x = _moe(x, routed, mx, g, w1, w3, w2, tm_x)
            if not last:
                oc = _attention(qc, [(kc, jnp.swapaxes(vc, 2, 3))], tq_c, "pair")
                cs, *routed = _outproj([oc], wo, bo, cs, mc, g, tm_c, "router", router)
                cs = _moe(cs, routed, mc, g, w1, w3, w2, tm_c)
    return x
```

```python
import functools
import math

import jax
import jax.numpy as jnp
import numpy as np
from jax import lax
from jax.experimental import pallas as pl
from jax.experimental.pallas import tpu as pltpu
from jax.experimental.pallas import tpu_sc as plsc

BF = jnp.bfloat16
F32 = jnp.float32

LANES = 128
VMEM_LIMIT = 56 * 1024 * 1024

GRID_W = 64
EPS = 1e-6
ROPE_THETA = 10000.0
N_MOD = 6

MLA_HEADS = 8
MLA_NOPE = 64
MLA_ROPE = 32
MLA_V = 64
MLA_Q_RANK = 384
MLA_KV_RANK = 256
LOG2E = math.log2(math.e)
MLA_SCALE = (MLA_NOPE + MLA_ROPE) ** -0.5 * LOG2E

DIFF_HEADS = 4
DIFF_HD = 64
DIFF_SCALE = DIFF_HD ** -0.5 * LOG2E

NA_HEADS = 16
NA_HD = 64
NA_KH = 8
NA_KW = 16
NA_SCALE = NA_HD ** -0.5 * LOG2E
NA_SLOTS = NA_HEADS * NA_HD // LANES
MASK_VALUE = -1e30

N_EXPERTS = 8

_NT = (((1,), (1,)), ((), ()))


def _cparams(*sem):
    return pltpu.CompilerParams(dimension_semantics=sem, vmem_limit_bytes=VMEM_LIMIT)


def _dot(a, b):
    return jnp.dot(a, b, preferred_element_type=F32)


def _rms(x, g):
    return x * lax.rsqrt(jnp.mean(x * x, axis=-1, keepdims=True) + EPS) * g


def _norm_mod(x, g, shift, scale):
    return _rms(x, g) * (1 + scale) + shift


def _const_spec(shape):
    return pl.BlockSpec(shape, lambda *_: (0,) * len(shape))


def _mod_spec(mod):
    if mod.shape[0] == 1:
        return pl.BlockSpec((1,) + mod.shape[1:], lambda b, *_: (0, 0, 0))
    return pl.BlockSpec((1,) + mod.shape[1:], lambda b, *_: (b, 0, 0))


def _mod_kernel(c_ref, w_ref, b_ref, o_ref):
    c = c_ref[...]
    sc = c * jax.nn.sigmoid(c)
    o_ref[0] = _dot(sc.astype(BF), w_ref[0].astype(BF)) + b_ref[0]


def _modulation(cvec, w_mod, b_mod):
    depth, d, n = w_mod.shape
    rows = cvec.shape[0]
    return pl.pallas_call(
        _mod_kernel,
        grid=(depth, n // d),
        in_specs=[
            pl.BlockSpec((rows, d), lambda i, j: (0, 0)),
            pl.BlockSpec((1, d, d), lambda i, j: (i, 0, j)),
            pl.BlockSpec((1, 1, d), lambda i, j: (i, 0, j)),
        ],
        out_specs=pl.BlockSpec((1, rows, d), lambda i, j: (i, 0, j)),
        out_shape=jax.ShapeDtypeStruct((depth, rows, n), F32),
        compiler_params=_cparams("arbitrary", "arbitrary"),
        name="modulation",
    )(cvec, w_mod, b_mod.reshape(depth, 1, n))


_Z_CQ = 0
_Z_CKV = MLA_Q_RANK
_Z_KR = _Z_CKV + MLA_KV_RANK
_Z_DQ = _Z_KR + LANES
_DW = 2 * DIFF_HEADS * DIFF_HD
_Z_DK = _Z_DQ + _DW
_Z_DV = _Z_DK + _DW


def _rope(x, cos, sin_signed, quarter):
    lane = lax.broadcasted_iota(jnp.int32, x.shape, 1)
    first = (lane % (2 * quarter)) < quarter
    rx = jnp.where(first, pltpu.roll(x, LANES - quarter, 1), pltpu.roll(x, quarter, 1))
    return x * cos + rx * sin_signed


def _proj_ab_kernel(x_ref, mod_ref, g_ref, win_ref, qn_ref, kvn_ref, wuq_ref, wuk_ref, wuv_ref,
                    tab_ref, q_ref, k_ref, v_ref, dq_ref, dk_ref, dv_ref):
    h = _norm_mod(x_ref[0], g_ref[0:1, :], mod_ref[0, 0:1, :], mod_ref[0, 1:2, :])
    z = _dot(h.astype(BF), win_ref[...])
    cqn = _rms(z[:, _Z_CQ:_Z_CKV], qn_ref[...]).astype(BF)
    ckvn = _rms(z[:, _Z_CKV:_Z_KR], kvn_ref[...]).astype(BF)
    q2 = _dot(cqn, wuq_ref[...])
    cos_q, sin_q = tab_ref[0], tab_ref[1]
    for hd in range(MLA_HEADS):
        lo = hd * LANES
        q_ref[0, hd] = _rope(q2[:, lo:lo + LANES], cos_q, sin_q, MLA_ROPE // 4).astype(BF)
    kr = _rope(z[:, _Z_KR:_Z_DQ], tab_ref[2], tab_ref[3], MLA_ROPE // 4)
    kk = _dot(ckvn, wuk_ref[...])
    vv = _dot(ckvn, wuv_ref[...])
    for hd in range(MLA_HEADS):
        lo = hd * LANES
        k_ref[0, hd] = (kk[:, lo:lo + LANES] + kr).astype(BF)
        v_ref[0, hd] = vv[:, lo:lo + LANES].T.astype(BF)
    cos_d, sin_d = tab_ref[4], tab_ref[5]
    for hd in range(DIFF_HEADS):
        lo = hd * LANES
        dq = _rope(z[:, _Z_DQ + lo:_Z_DQ + lo + LANES], cos_d, sin_d, DIFF_HD // 4)
        dq_ref[0, hd] = (dq * DIFF_SCALE).astype(BF)
        dk_ref[0, hd] = _rope(z[:, _Z_DK + lo:_Z_DK + lo + LANES], cos_d, sin_d, DIFF_HD // 4).astype(BF)
        dv_ref[0, hd] = z[:, _Z_DV + lo:_Z_DV + lo + LANES].T.astype(BF)


def _proj_ab(x, mod, g, wts, tabs, tm):
    b, t, d = x.shape
    win, qn, kvn, wuq, wuk, wuv = wts
    tok = lambda hh: pl.BlockSpec((1, hh, tm, LANES), lambda bi, i: (bi, 0, i, 0))
    shp = lambda hh: jax.ShapeDtypeStruct((b, hh, t, LANES), BF)
    tok_t = lambda hh, dv: pl.BlockSpec((1, hh, dv, tm), lambda bi, i: (bi, 0, 0, i))
    shp_t = lambda hh, dv: jax.ShapeDtypeStruct((b, hh, dv, t), BF)
    return pl.pallas_call(
        _proj_ab_kernel,
        grid=(b, t // tm),
        in_specs=[
            pl.BlockSpec((1, tm, d), lambda bi, i: (bi, i, 0)),
            _mod_spec(mod),
            _const_spec(g.shape),
            _const_spec(win.shape), _const_spec(qn.shape), _const_spec(kvn.shape),
            _const_spec(wuq.shape), _const_spec(wuk.shape), _const_spec(wuv.shape),
            pl.BlockSpec((6, tm, LANES), lambda bi, i: (0, i, 0)),
        ],
        out_specs=[tok(MLA_HEADS), tok(MLA_HEADS), tok_t(MLA_HEADS, LANES),
                   tok(DIFF_HEADS), tok(DIFF_HEADS), tok_t(DIFF_HEADS, LANES)],
        out_shape=[shp(MLA_HEADS), shp(MLA_HEADS), shp_t(MLA_HEADS, LANES),
                   shp(DIFF_HEADS), shp(DIFF_HEADS), shp_t(DIFF_HEADS, LANES)],
        compiler_params=_cparams("arbitrary", "arbitrary"),
        name="proj_ab",
    )(x, mod, g, win, qn, kvn, wuq, wuk, wuv, tabs)


KEY_CHUNK = 256


SUBLANES = 8
ATTN_QUERY_LANES = 1024


def _key_chunks(kv_refs):
    off = 0
    for k_ref, v_ref in kv_refs:
        lk = k_ref.shape[2]
        for c0 in range(0, lk, KEY_CHUNK):
            w = min(KEY_CHUNK, lk - c0)
            yield k_ref, v_ref, c0, w, off
            off += w


def _attn_kernel(*refs, nseg, mode, lam_init):
    q_ref = refs[0]
    kv_refs = [(refs[1 + 2 * s], refs[2 + 2 * s]) for s in range(nseg)]
    o_ref, s0_ref, s1_ref, mm0_ref, mm1_ref = refs[-5:]
    nmap, _, tq = s0_ref.shape
    maps = range(nmap)
    fold = lambda a: a.reshape(a.shape[0] // SUBLANES, SUBLANES, tq)
    t = pl.program_id(0)

    @pl.when(t == 0)
    def _():
        s1_ref[...] = jnp.zeros_like(s1_ref)
        mm1_ref[...] = jnp.zeros_like(mm1_ref)

    def step(s_cur, mm_cur, s_prv, mm_prv):
        q = q_ref[0, 0]
        if mode == "single":
            qms = [q]
        else:
            lane = lax.broadcasted_iota(jnp.int32, q.shape, 1)
            zero = jnp.zeros_like(q)
            qms = [jnp.where(lane < DIFF_HD, q, zero), jnp.where(lane >= DIFF_HD, q, zero)]
        m_prv = [jnp.max(mm_prv[j], axis=0, keepdims=True) for j in maps]
        mms = [jnp.full((SUBLANES, tq), -jnp.inf, F32)] * nmap
        lls = [jnp.zeros((SUBLANES, tq), F32)] * nmap
        o_ts = [jnp.zeros((LANES, tq), F32)] * nmap
        for k_ref, v_ref, c0, w, off in _key_chunks(kv_refs):
            kc = k_ref[0, 0, c0:c0 + w, :]
            vc = v_ref[0, 0, :, c0:c0 + w]
            for j in maps:
                s = lax.dot_general(kc, qms[j], _NT, preferred_element_type=F32)
                s_cur[j, off:off + w, :] = s
                mms[j] = jnp.maximum(mms[j], jnp.max(fold(s), axis=0))
            for j in maps:
                p = jnp.exp2(s_prv[j, off:off + w, :] - m_prv[j])
                lls[j] = lls[j] + jnp.sum(fold(p), axis=0)
                o_ts[j] = o_ts[j] + _dot(vc, p.astype(BF))
        for j in maps:
            mm_cur[j] = mms[j]
        outs = [(o_ts[j] / jnp.sum(lls[j], axis=0, keepdims=True)).T for j in maps]
        if mode == "single":
            o_ref[0, 0] = outs[0].astype(BF)
        elif mode == "pair":
            o_ref[0, 0] = jnp.where(lane < DIFF_HD, outs[0], outs[1]).astype(BF)
        else:
            lam_ref, subln_ref = refs[1 + 2 * nseg], refs[2 + 2 * nseg]
            lv = lam_ref[...]
            lam = (jnp.exp(jnp.sum(lv[0:1] * lv[1:2], axis=-1, keepdims=True))
                   - jnp.exp(jnp.sum(lv[2:3] * lv[3:4], axis=-1, keepdims=True)) + lam_init)
            o_ref[0, 0] = (_rms(outs[0] - lam * outs[1], subln_ref[...]) * (1 - lam_init)).astype(BF)

    pl.when(t % 2 == 0)(lambda: step(s0_ref, mm0_ref, s1_ref, mm1_ref))
    pl.when(t % 2 == 1)(lambda: step(s1_ref, mm1_ref, s0_ref, mm0_ref))


def _attention(q, kvs, tq, mode="single", lam_vecs=None, subln=None, lam_init=0.0):
    b, nh, lq, _ = q.shape
    diff = mode == "diff"
    nmap = 1 if mode == "single" else 2
    nq = lq // tq
    ntile = b * nh * nq

    def tile(t):
        t = jnp.clip(t, 0, ntile - 1)
        return t // (nh * nq), (t // nq) % nh, t % nq

    cur_head = lambda t: tile(t)[:2] + (0, 0)
    prv_head = lambda t: tile(t - 1)[:2] + (0, 0)
    in_specs = [pl.BlockSpec((1, 1, tq, LANES), lambda t: tile(t) + (0,))]
    args = [q]
    for k, v in kvs:
        lk = k.shape[2]
        in_specs += [pl.BlockSpec((1, 1, lk, LANES), cur_head), pl.BlockSpec((1, 1, v.shape[2], lk), prv_head)]
        args += [k, v]
    if diff:
        in_specs += [_const_spec(lam_vecs.shape), _const_spec(subln.shape)]
        args += [lam_vecs, subln]
    nkeys = sum(k.shape[2] for k, _ in kvs)
    return pl.pallas_call(
        functools.partial(_attn_kernel, nseg=len(kvs), mode=mode, lam_init=lam_init),
        grid=(ntile + 1,),
        in_specs=in_specs,
        out_specs=pl.BlockSpec((1, 1, tq, LANES), lambda t: tile(t - 1) + (0,)),
        out_shape=jax.ShapeDtypeStruct((b, nh, lq, LANES), BF),
        scratch_shapes=([pltpu.VMEM((nmap, nkeys, tq), F32)] * 2 + [pltpu.VMEM((nmap, SUBLANES, tq), F32)] * 2),
        compiler_params=_cparams("arbitrary"),
        name="attn_diff" if diff else "attn_mla",
    )(*args)


ROUTER_PIECES = 2


def _outproj_kernel(*refs, n_in, has_bias, tail):
    o_refs = refs[:n_in]
    w_ref = refs[n_in]
    pos = n_in + 1
    b_ref = None
    if has_bias:
        b_ref = refs[pos]
        pos += 1
    x_ref, mod_ref, g_ref = refs[pos:pos + 3]
    rest = refs[pos + 3:]
    tm = x_ref.shape[1]

    def residual(rows):
        o = jnp.concatenate([r[0, h, rows, :] for r in o_refs for h in range(r.shape[1])], axis=-1)
        y = _dot(o, w_ref[...])
        if has_bias:
            y = y + b_ref[...]
        x1 = x_ref[0, rows, :] + mod_ref[0, 2:3, :] * _rms(y, g_ref[1:2, :])
        return x1, _norm_mod(x1, g_ref[2:3, :], mod_ref[0, 3:4, :], mod_ref[0, 4:5, :])

    if tail == "ffn":
        w1_ref, w3_ref, w2_ref, out_ref = rest
        x1, h = residual(slice(0, tm))
        out_ref[0] = x1 + mod_ref[0, 5:6, :] * _rms(_swiglu(h.astype(BF), w1_ref, w3_ref, w2_ref), g_ref[3:4, :])
    else:
        router_ref, tri_ref, out_ref = rest[:3]
        step = tm // ROUTER_PIECES
        halves = [slice(r0, r0 + step) for r0 in range(0, tm, step)]
        parts = [residual(rows) for rows in halves]
        for rows, (x1, _) in zip(halves, parts):
            out_ref[0, rows, :] = x1
        first = (pl.program_id(0) == 0) & (pl.program_id(1) == 0)
        _route([h for _, h in parts], halves, first, router_ref, tri_ref, *rest[3:])


def _outproj(os_, w, bias, x, mod, g, tm, tail, tail_args):
    b, t, d = x.shape
    n = b * t
    nt = t // tm
    in_specs = [pl.BlockSpec((1, o.shape[1], tm, LANES), lambda bi, i: (bi, 0, i, 0)) for o in os_]
    in_specs.append(_const_spec(w.shape))
    args = list(os_) + [w]
    if bias is not None:
        in_specs.append(_const_spec(bias.shape))
        args.append(bias)
    x_spec = pl.BlockSpec((1, tm, d), lambda bi, i: (bi, i, 0))
    in_specs += [x_spec, _mod_spec(mod), _const_spec(g.shape)]
    args += [x, mod, g]
    out_specs, out_shape, scratch = x_spec, jax.ShapeDtypeStruct((b, t, d), F32), []
    if tail == "ffn":
        in_specs += [_resident_spec(a.shape, lambda bi, i: (0, 0)) for a in tail_args]
        args += list(tail_args)
    else:
        router, = tail_args
        tri = jnp.asarray(np.tril(np.ones((tm, tm), np.float32), -1), BF)
        in_specs += [_const_spec(router.shape), _const_spec(tri.shape)]
        args += [router, tri]
        rows = lambda wd: pl.BlockSpec((tm, wd), lambda bi, i: (bi * nt + i, 0))
        out_specs = [x_spec, pl.BlockSpec((PACK_CHUNKS, tm, LANES), lambda bi, i: (0, bi * nt + i, 0)),
                     rows(8), rows(8), pl.BlockSpec((1, LANES), lambda bi, i: (0, 0))]
        out_shape = [out_shape, jax.ShapeDtypeStruct((PACK_CHUNKS, n, LANES), jnp.int32),
                     jax.ShapeDtypeStruct((n, 8), jnp.int32), jax.ShapeDtypeStruct((n, 8), F32),
                     jax.ShapeDtypeStruct((1, LANES), F32)]
        scratch = [pltpu.VMEM((1, LANES), F32)]
    return pl.pallas_call(
        functools.partial(_outproj_kernel, n_in=len(os_), has_bias=bias is not None, tail=tail),
        grid=(b, nt),
        in_specs=in_specs,
        out_specs=out_specs,
        out_shape=out_shape,
        scratch_shapes=scratch,
        compiler_params=_cparams("arbitrary", "arbitrary"),
        name="outproj_" + tail,
    )(*args)


MXU_TILE = 256


def _ff_chunks(f):
    tiles = f // MXU_TILE
    if f % MXU_TILE or tiles < 2:
        return [(0, f)]
    first = (tiles + 1) // 2 * MXU_TILE
    return [(0, first), (first, f - first)]


def _swiglu(hn, w1_ref, w3_ref, w2_ref):
    y = None
    for lo, n in _ff_chunks(w1_ref.shape[1]):
        a = _dot(hn, w1_ref[:, lo:lo + n])
        act = (a * jax.nn.sigmoid(a) * _dot(hn, w3_ref[:, lo:lo + n])).astype(BF)
        part = _dot(act, w2_ref[lo:lo + n, :])
        y = part if y is None else y + part
    return y


def _resident_spec(shape, index_map):
    return pl.BlockSpec(shape, index_map, pipeline_mode=pl.Buffered(1))


MOE_ROW_TILE = 512
PACK_CHUNKS = 4
SC_WINDOW = 128
_HI_MASK = -65536
_LO_MASK = 65535


def _pack_rows(v, out_ref, rows=slice(None)):
    half = v.shape[1] // 2
    vb = v.astype(BF).astype(F32)
    lo = (pltpu.bitcast(vb[:, :half], jnp.int32) >> 16) & _LO_MASK
    hi = pltpu.bitcast(vb[:, half:], jnp.int32) & _HI_MASK
    w = lo | hi
    for c in range(PACK_CHUNKS):
        out_ref[c, rows, :] = w[:, c * LANES:(c + 1) * LANES]


def _unpack_rows(ref):
    w = jnp.concatenate([ref[c] for c in range(PACK_CHUNKS)], axis=-1)
    lo = pltpu.bitcast(w << 16, F32)
    hi = pltpu.bitcast(w & _HI_MASK, F32)
    return jnp.concatenate([lo, hi], axis=-1)


def _route(hs, row_slices, first, router_ref, tri_ref, hp_ref, meta_ref, gate_ref, cnt_ref, carry_ref):
    @pl.when(first)
    def _():
        carry_ref[...] = jnp.zeros_like(carry_ref)

    pieces = range(len(hs))
    for h, rows in zip(hs, row_slices):
        _pack_rows(h, hp_ref, rows)
    his = [h.astype(BF) for h in hs]
    los = [(hs[p] - his[p].astype(F32)).astype(BF) for p in pieces]
    raw = [_dot(his[p], router_ref[0]) + (_dot(his[p], router_ref[1]) + _dot(los[p], router_ref[0])) for p in pieces]
    lanes = [lax.broadcasted_iota(jnp.int32, r.shape, 1) for r in raw]
    lgs = [jnp.where(lanes[p] < N_EXPERTS, raw[p], -jnp.inf) for p in pieces]
    m1s = [jnp.max(lg, axis=-1, keepdims=True) for lg in lgs]
    i1s = [jnp.min(jnp.where(lgs[p] == m1s[p], lanes[p], LANES), axis=-1, keepdims=True) for p in pieces]
    rests = [jnp.where(lanes[p] == i1s[p], -jnp.inf, lgs[p]) for p in pieces]
    m2s = [jnp.max(r, axis=-1, keepdims=True) for r in rests]
    i2s = [jnp.min(jnp.where(rests[p] == m2s[p], lanes[p], LANES), axis=-1, keepdims=True) for p in pieces]
    rows_cat = lambda parts: jnp.concatenate(parts, axis=0)
    m1, m2, i1, i2, lane = rows_cat(m1s), rows_cat(m2s), rows_cat(i1s), rows_cat(i2s), rows_cat(lanes)
    e2 = jnp.exp(m2 - m1)
    denom = 1.0 + e2
    assigned = jnp.where((lane == i1) | (lane == i2), 1.0, 0.0)
    ranks = _dot(tri_ref[...], assigned.astype(BF)) + carry_ref[...]
    r1 = jnp.sum(jnp.where(lane == i1, ranks, 0.0), axis=-1, keepdims=True).astype(jnp.int32)
    r2 = jnp.sum(jnp.where(lane == i2, ranks, 0.0), axis=-1, keepdims=True).astype(jnp.int32)
    carry_ref[...] += jnp.sum(assigned, axis=0, keepdims=True)
    cnt_ref[...] = carry_ref[...]
    col = lax.broadcasted_iota(jnp.int32, meta_ref.shape, 1)
    meta_ref[...] = jnp.where(col == 0, i1, jnp.where(col == 1, i2, jnp.where(col == 2, r1, r2)))
    gate_ref[...] = jnp.where(col == 0, 1.0 / denom, e2 / denom)


def _sc_mesh():
    return plsc.VectorSubcoreMesh(core_axis_name="core", subcore_axis_name="subcore")


def _sc_scatter_rows(rows, idx_a, idx_b, n_out):
    nrows = rows.shape[0]

    @pl.kernel(out_type=jax.ShapeDtypeStruct((n_out, LANES), rows.dtype), mesh=_sc_mesh(), scratch_types=[])
    def scatter_kernel(x_hbm, ia_hbm, ib_hbm, o_hbm):
        def body(x_vmem, ia_vmem, ib_vmem):
            pltpu.sync_copy(x_vmem, o_hbm.at[ia_vmem.at[0]])
            pltpu.sync_copy(x_vmem, o_hbm.at[ib_vmem.at[0]])

        pltpu.emit_pipeline(
            body,
            grid=(nrows // SC_WINDOW,),
            in_specs=[pl.BlockSpec((SC_WINDOW, LANES), lambda i: (i, 0)),
                      pl.BlockSpec((1, SC_WINDOW), lambda i: (0, i)),
                      pl.BlockSpec((1, SC_WINDOW), lambda i: (0, i))],
            out_specs=[],
            core_axis_name=("core", "subcore"),
            dimension_semantics=(pltpu.PARALLEL,),
        )(x_hbm, ia_hbm, ib_hbm)

    return scatter_kernel(rows, idx_a, idx_b)


def _sc_gather_rows(table, idx_a, idx_b):
    nrows = idx_a.shape[1]
    out = jax.ShapeDtypeStruct((nrows, LANES), table.dtype)

    @pl.kernel(out_type=(out, out), mesh=_sc_mesh(), scratch_types=[])
    def gather_kernel(t_hbm, ia_hbm, ib_hbm, oa_hbm, ob_hbm):
        def body(ia_vmem, ib_vmem, oa_vmem, ob_vmem):
            pltpu.sync_copy(t_hbm.at[ia_vmem.at[0]], oa_vmem)
            pltpu.sync_copy(t_hbm.at[ib_vmem.at[0]], ob_vmem)

        pltpu.emit_pipeline(
            body,
            grid=(nrows // SC_WINDOW,),
            in_specs=[pl.BlockSpec((1, SC_WINDOW), lambda i: (0, i)),
                      pl.BlockSpec((1, SC_WINDOW), lambda i: (0, i))],
            out_specs=[pl.BlockSpec((SC_WINDOW, LANES), lambda i: (i, 0)),
                       pl.BlockSpec((SC_WINDOW, LANES), lambda i: (i, 0))],
            core_axis_name=("core", "subcore"),
            dimension_semantics=(pltpu.PARALLEL,),
        )(ia_hbm, ib_hbm, oa_hbm, ob_hbm)

    return gather_kernel(table, idx_a, idx_b)


def _experts_kernel(te_ref, tv_ref, xs_ref, w1_ref, w3_ref, w2_ref, ys_ref):
    del te_ref
    valid = tv_ref[pl.program_id(0)]

    @pl.when(valid > 0)
    def _():
        row = lax.broadcasted_iota(jnp.int32, (xs_ref.shape[1], 1), 0)
        hn = jnp.where(row < valid, _unpack_rows(xs_ref), 0.0).astype(BF)
        _pack_rows(_swiglu(hn, w1_ref.at[0], w3_ref.at[0], w2_ref.at[0]), ys_ref)

    @pl.when(valid == 0)
    def _():
        ys_ref[...] = jnp.zeros_like(ys_ref)


def _experts(xs, tile_expert, tile_valid, w1, w3, w2):
    _, p, _ = xs.shape
    tr = MOE_ROW_TILE
    rows = pl.BlockSpec((PACK_CHUNKS, tr, LANES), lambda j, te, tv: (0, j, 0))
    weights = lambda w: pl.BlockSpec((1,) + w.shape[1:], lambda j, te, tv: (te[j], 0, 0))
    return pl.pallas_call(
        _experts_kernel,
        grid_spec=pltpu.PrefetchScalarGridSpec(
            num_scalar_prefetch=2,
            grid=(p // tr,),
            in_specs=[rows, weights(w1), weights(w3), weights(w2)],
            out_specs=rows,
        ),
        out_shape=jax.ShapeDtypeStruct(xs.shape, jnp.int32),
        compiler_params=_cparams("arbitrary"),
        name="moe_experts",
    )(tile_expert, tile_valid, xs, w1, w3, w2)


def _combine_kernel(x_ref, ya_ref, yb_ref, gate_ref, mod_ref, g_ref, out_ref):
    gates = gate_ref[...]
    fx = gates[:, 0:1] * _unpack_rows(ya_ref) + gates[:, 1:2] * _unpack_rows(yb_ref)
    out_ref[0] = x_ref[0] + mod_ref[0, 5:6, :] * _rms(fx, g_ref[3:4, :])


def _combine(x, ya, yb, gates, mod, g, tm):
    b, t, d = x.shape
    nt = t // tm
    packed = pl.BlockSpec((PACK_CHUNKS, tm, LANES), lambda bi, i: (0, bi * nt + i, 0))
    return pl.pallas_call(
        _combine_kernel,
        grid=(b, nt),
        in_specs=[
            pl.BlockSpec((1, tm, d), lambda bi, i: (bi, i, 0)),
            packed, packed,
            pl.BlockSpec((tm, gates.shape[1]), lambda bi, i: (bi * nt + i, 0)),
            _mod_spec(mod),
            _const_spec(g.shape),
        ],
        out_specs=pl.BlockSpec((1, tm, d), lambda bi, i: (bi, i, 0)),
        out_shape=jax.ShapeDtypeStruct((b, t, d), F32),
        compiler_params=_cparams("arbitrary", "arbitrary"),
        name="moe_combine",
    )(x, ya, yb, gates, mod, g)


def _moe(x, routed, mod, g, w1, w3, w2, tm):
    b, t, d = x.shape
    n = b * t
    ne = w1.shape[0]
    tr = MOE_ROW_TILE
    assert d == 2 * PACK_CHUNKS * LANES and (PACK_CHUNKS * n) % (SC_WINDOW * 32) == 0
    hp, meta, gates, counts = routed

    ntile = 2 * n // tr + ne
    p = ntile * tr
    cnt = counts[0, :ne].astype(jnp.int32)
    tiles = (cnt + tr - 1) // tr
    tile_end = jnp.cumsum(tiles)
    tile_start = tile_end - tiles
    base = tile_start * tr
    eids = jnp.arange(ne, dtype=jnp.int32)
    base_of = lambda e: jnp.sum(jnp.where(e[:, None] == eids[None, :], base[None, :], 0), axis=-1)
    pos_a = base_of(meta[:, 0]) + meta[:, 2]
    pos_b = base_of(meta[:, 1]) + meta[:, 3]
    chunk = jnp.arange(PACK_CHUNKS, dtype=jnp.int32)[:, None] * p
    idx_a = (chunk + pos_a[None, :]).reshape(1, PACK_CHUNKS * n)
    idx_b = (chunk + pos_b[None, :]).reshape(1, PACK_CHUNKS * n)
    tj = jnp.arange(ntile, dtype=jnp.int32)
    tile_expert = jnp.minimum(jnp.sum(tj[:, None] >= tile_end[None, :], axis=-1), ne - 1).astype(jnp.int32)
    done = jnp.sum(jnp.where(tile_expert[:, None] == eids[None, :], tile_start[None, :], 0), axis=-1)
    left = jnp.sum(jnp.where(tile_expert[:, None] == eids[None, :], cnt[None, :], 0), axis=-1) - (tj - done) * tr
    tile_valid = jnp.where(tj < tile_end[-1], jnp.clip(left, 0, tr), 0).astype(jnp.int32)

    xs = _sc_scatter_rows(hp.reshape(PACK_CHUNKS * n, LANES), idx_a, idx_b, PACK_CHUNKS * p)
    ys = _experts(xs.reshape(PACK_CHUNKS, p, LANES), tile_expert, tile_valid, w1, w3, w2)
    ya, yb = _sc_gather_rows(ys.reshape(PACK_CHUNKS * p, LANES), idx_a, idx_b)
    shp = (PACK_CHUNKS, n, LANES)
    return _combine(x, ya.reshape(shp), yb.reshape(shp), gates, mod, g, tm)


def _proj_na_kernel(x_ref, mod_ref, g_ref, w_ref, b_ref, q_ref, k_ref, v_ref):
    h = _norm_mod(x_ref[0], g_ref[0:1, :], mod_ref[0, 0:1, :], mod_ref[0, 1:2, :])
    z = _dot(h.astype(BF), w_ref[...]) + b_ref[...]
    n = NA_SLOTS * LANES
    for s in range(NA_SLOTS):
        lo = s * LANES
        q_ref[0, s] = (z[:, lo:lo + LANES] * NA_SCALE).astype(BF)
        k_ref[0, s] = z[:, n + lo:n + lo + LANES].astype(BF)
        v_ref[0, s] = z[:, 2 * n + lo:2 * n + lo + LANES].astype(BF)


def _proj_na(x, mod, g, w, bias, tm):
    b, t, d = x.shape
    tok = pl.BlockSpec((1, NA_SLOTS, tm, LANES), lambda bi, i: (bi, 0, i, 0))
    shp = jax.ShapeDtypeStruct((b, NA_SLOTS, t, LANES), BF)
    return pl.pallas_call(
        _proj_na_kernel,
        grid=(b, t // tm),
        in_specs=[
            pl.BlockSpec((1, tm, d), lambda bi, i: (bi, i, 0)),
            _mod_spec(mod),
            _const_spec(g.shape),
            _const_spec(w.shape),
            _const_spec(bias.shape),
        ],
        out_specs=[tok, tok, tok],
        out_shape=[shp, shp, shp],
        compiler_params=_cparams("arbitrary", "arbitrary"),
        name="proj_na",
    )(x, mod, g, w, bias)


NA_ROWS_PER_STEP = 2
NA_GROUP = 4


def _na_window_start(r, rows):
    return jnp.clip(r - NA_KH // 2, 0, rows - NA_KH)


def _na_kernel(q_ref, kx_ref, vx_ref, kc_ref, vc_ref, *rest, rows):
    bias_refs, o_ref = rest[:NA_ROWS_PER_STEP], rest[NA_ROWS_PER_STEP]
    lane = lax.broadcasted_iota(jnp.int32, (GRID_W, LANES), 1)
    problems = [(a, s) for a in range(NA_ROWS_PER_STEP) for s in range(NA_SLOTS)]
    starts = [pl.multiple_of(_na_window_start(pl.program_id(1) * NA_ROWS_PER_STEP + a, rows) * GRID_W, GRID_W)
              for a in range(NA_ROWS_PER_STEP)]
    for g0 in range(0, len(problems), NA_GROUP):
        _na_group(problems[g0:g0 + NA_GROUP], starts, lane, q_ref, kx_ref, vx_ref, kc_ref, vc_ref, bias_refs, o_ref)


def _na_group(work, starts, lane, q_ref, kx_ref, vx_ref, kc_ref, vc_ref, bias_refs, o_ref):
    nwin = NA_KH * GRID_W
    idx = range(len(work))
    sws, scs = [], []
    for a, s in work:
        q = q_ref[0, s, a * GRID_W:(a + 1) * GRID_W, :]
        zero = jnp.zeros_like(q)
        q2 = jnp.concatenate([jnp.where(lane < NA_HD, q, zero), jnp.where(lane >= NA_HD, q, zero)], axis=0)
        kw = kx_ref[0, s, pl.ds(starts[a], nwin), :]
        sws.append(lax.dot_general(q2, kw, _NT, preferred_element_type=F32) + bias_refs[a][0, s])
        scs.append(lax.dot_general(q2, kc_ref[0, s], _NT, preferred_element_type=F32))
    ms = [jnp.maximum(jnp.max(sws[i], axis=-1, keepdims=True), jnp.max(scs[i], axis=-1, keepdims=True))
          for i in idx]
    pws = [jnp.exp2(sws[i] - ms[i]) for i in idx]
    pcs = [jnp.exp2(scs[i] - ms[i]) for i in idx]
    ls = [jnp.sum(pws[i], axis=-1, keepdims=True) + jnp.sum(pcs[i], axis=-1, keepdims=True) for i in idx]
    for i, (a, s) in enumerate(work):
        vw = vx_ref[0, s, pl.ds(starts[a], nwin), :]
        o2 = (_dot(pws[i].astype(BF), vw) + _dot(pcs[i].astype(BF), vc_ref[0, s])) / ls[i]
        o_ref[0, s, a * GRID_W:(a + 1) * GRID_W, :] = jnp.where(lane < NA_HD, o2[:GRID_W], o2[GRID_W:]).astype(BF)


def _na_attention(q, kx, vx, kc, vc, bias):
    b, ns, l, _ = q.shape
    rows = l // GRID_W
    lc = kc.shape[2]
    nr = NA_ROWS_PER_STEP
    assert rows % nr == 0

    def bias_spec(a):
        def index_map(bi, i):
            r = i * nr + a
            return (r - _na_window_start(r, rows), 0, 0, 0)
        return pl.BlockSpec((1,) + bias.shape[1:], index_map)

    full = lambda n: pl.BlockSpec((1, ns, n, LANES), lambda bi, i: (bi, 0, 0, 0))
    row = pl.BlockSpec((1, ns, nr * GRID_W, LANES), lambda bi, i: (bi, 0, i, 0))
    return pl.pallas_call(
        functools.partial(_na_kernel, rows=rows),
        grid=(b, rows // nr),
        in_specs=[row, full(l), full(l), full(lc), full(lc)] + [bias_spec(a) for a in range(nr)],
        out_specs=row,
        out_shape=jax.ShapeDtypeStruct((b, ns, l, LANES), BF),
        compiler_params=_cparams("arbitrary", "arbitrary"),
        name="na_attn",
    )(q, kx, vx, kc, vc, *([bias] * nr))


def _axis_tables(pos, dim):
    inv = ROPE_THETA ** (-jnp.arange(0, dim, 2, dtype=F32) / dim)
    ang = pos.astype(F32)[:, None] * inv[None, :]
    ang = jnp.concatenate([ang, ang], axis=-1)
    return jnp.cos(ang), jnp.sin(ang)


def _rope_tables(n, rope):
    if rope:
        t = jnp.arange(n, dtype=jnp.int32)
        row, col = t // GRID_W, t % GRID_W

        def cs(d):
            cr, sr = _axis_tables(row, d // 2)
            cc, sc = _axis_tables(col, d // 2)
            sign = np.where(np.arange(d) % (d // 2) < d // 4, -1.0, 1.0).astype(np.float32)
            return jnp.concatenate([cr, cc], axis=-1), jnp.concatenate([sr, sc], axis=-1) * sign

        c32, s32 = cs(MLA_ROPE)
        c64, s64 = cs(DIFF_HD)
    else:
        c32, s32 = jnp.ones((n, MLA_ROPE), F32), jnp.zeros((n, MLA_ROPE), F32)
        c64, s64 = jnp.ones((n, DIFF_HD), F32), jnp.zeros((n, DIFF_HD), F32)
    ones = jnp.ones((n, MLA_NOPE), F32)
    pad = lambda a: jnp.pad(a, ((0, 0), (0, LANES - a.shape[1])))
    cos_q = MLA_SCALE * pad(jnp.concatenate([ones, c32], axis=-1))
    sin_q = MLA_SCALE * pad(jnp.concatenate([jnp.zeros_like(ones), s32], axis=-1))
    at_rope_lanes = lambda a: jnp.pad(a, ((0, 0), (MLA_NOPE, LANES - MLA_NOPE - MLA_ROPE)))
    return jnp.stack([cos_q, sin_q, at_rope_lanes(c32), at_rope_lanes(s32),
                      jnp.concatenate([c64, c64], axis=-1), jnp.concatenate([s64, s64], axis=-1)])


def _pad_cols(w, n):
    return jnp.pad(w, ((0, 0), (0, n - w.shape[1])))


def _prep_ab_weights(w_in, q_norm, kv_norm, w_uq, w_ukv, w_out):
    o_cq, o_ckv, o_kr = MLA_Q_RANK, MLA_Q_RANK + MLA_KV_RANK, MLA_Q_RANK + MLA_KV_RANK + MLA_ROPE
    o_dq, o_dk = o_kr + _DW, o_kr + 2 * _DW
    w_kr, w_dq, w_dk, w_dv = w_in[:, o_ckv:o_kr], w_in[:, o_kr:o_dq], w_in[:, o_dq:o_dk], w_in[:, o_dk:]
    w_kr = jnp.pad(w_kr, ((0, 0), (MLA_NOPE, LANES - MLA_NOPE - MLA_ROPE)))
    win = jnp.concatenate([w_in[:, :o_ckv], w_kr, w_dq, w_dk, w_dv], axis=1).astype(BF)
    rq = w_uq.shape[0]
    uq = w_uq.reshape(rq, MLA_HEADS, MLA_NOPE + MLA_ROPE)
    uq = jnp.pad(uq, ((0, 0), (0, 0), (0, LANES - MLA_NOPE - MLA_ROPE)))
    wuq = uq.reshape(rq, MLA_HEADS * LANES).astype(BF)
    rkv = w_ukv.shape[0]
    ukv = w_ukv.reshape(rkv, MLA_HEADS, MLA_NOPE + MLA_V)
    slot = lambda a: jnp.pad(a, ((0, 0), (0, 0), (0, LANES - a.shape[-1]))).reshape(rkv, MLA_HEADS * LANES)
    wuk = slot(ukv[..., :MLA_NOPE]).astype(BF)
    wuv = slot(ukv[..., MLA_NOPE:]).astype(BF)
    d = w_out.shape[1]
    wo_a = w_out[:MLA_HEADS * MLA_V].reshape(MLA_HEADS, MLA_V, d)
    wo_a = jnp.pad(wo_a, ((0, 0), (0, LANES - MLA_V), (0, 0))).reshape(MLA_HEADS * LANES, d)
    wo = jnp.concatenate([wo_a, w_out[MLA_HEADS * MLA_V:]], axis=0).astype(BF)
    proj = (win, q_norm.reshape(1, -1), kv_norm.reshape(1, -1), wuq, wuk, wuv)
    return proj, wo


def _na_bias_table(rpb):
    nh = rpb.shape[0]
    cols = np.arange(GRID_W)
    col_start = np.clip(cols - NA_KW // 2, 0, GRID_W - NA_KW)
    kcol = np.arange(GRID_W)
    valid = (kcol[None, :] >= col_start[:, None]) & (kcol[None, :] < col_start[:, None] + NA_KW)
    dc = kcol[None, :] - cols[:, None] + (NA_KW - 1)
    onehot = (dc[None] == np.arange(2 * NA_KW - 1)[:, None, None]) & valid[None]
    tz = jnp.einsum('hrd,dck->hrck', rpb * LOG2E, jnp.asarray(onehot, F32), precision=lax.Precision.HIGHEST)
    tz = jnp.where(valid[None, None], tz, MASK_VALUE)
    cases = [tz[:, NA_KH - 1 - c:2 * NA_KH - 1 - c] for c in range(NA_KH)]
    tbl = jnp.transpose(jnp.stack(cases), (0, 1, 3, 2, 4))
    return tbl.reshape(NA_KH, nh // 2, 2 * GRID_W, NA_KH * GRID_W)


def _tile(n, pref):
    return pref if n % pref == 0 else n


def kernel(x, c, ctx, c_ctx, w_mod, b_mod, norm_g, a_w_in, a_q_norm, a_kv_norm, a_w_uq, a_w_ukv, b_lambda, b_subln,
           ab_w_out, f_w1, f_w3, f_w2, c_w_qkv, c_b_qkv, c_rpb, c_w_out, c_b_out, m_router, m_w1, m_w3, m_w2):
    b, l, d = x.shape
    lc = ctx.shape[1]
    depth = w_mod.shape[0]
    assert l % GRID_W == 0 and l // GRID_W >= NA_KH

    mod_rows = 16
    cvec = jnp.zeros((mod_rows, d), F32).at[:b].set(c).at[b].set(c_ctx)
    mod = _modulation(cvec, w_mod, b_mod)

    tm_x, tm_c = _tile(l, 512), _tile(lc, 256)
    tq_x, tq_c = _tile(l, ATTN_QUERY_LANES), _tile(lc, ATTN_QUERY_LANES // 2)
    tq2_x = _tile(l, ATTN_QUERY_LANES // 2)
    cs = ctx
    for i in range(depth):
        last = i == depth - 1
        j = i // 2
        mx = mod[i, :b].reshape(b, N_MOD, d)
        mc = mod[i, b].reshape(1, N_MOD, d)
        g = norm_g[i]
        if i % 2 == 0:
            lam_init = 0.8 - 0.6 * math.exp(-0.3 * i)
            proj_w, wo = _prep_ab_weights(a_w_in[j], a_q_norm[j], a_kv_norm[j], a_w_uq[j], a_w_ukv[j], ab_w_out[j])
            qx, kx, vx, dqx, dkx, dvx = _proj_ab(x, mx, g, proj_w, _rope_tables(l, True), tm_x)
            qc, kc, vc, dqc, dkc, dvc = _proj_ab(cs, mc, g, proj_w, _rope_tables(lc, False), tm_c)
            subln = b_subln[j].reshape(1, -1)
            oa = _attention(qx, [(kx, vx), (kc, vc)], tq_x)
            ob = _attention(dqx, [(dkx, dvx), (dkc, dvc)], tq2_x, "diff", b_lambda[j], subln, lam_init)
            ffn_w = (f_w1[j].astype(BF), f_w3[j].astype(BF), f_w2[j].astype(BF))
            x = _outproj([oa, ob], wo, None, x, mx, g, tm_x, "ffn", ffn_w)
            if not last:
                oa = _attention(qc, [(kc, vc)], tq_c)
                ob = _attention(dqc, [(dkc, dvc)], tq_c, "diff", b_lambda[j], subln, lam_init)
                cs = _outproj([oa, ob], wo, None, cs, mc, g, tm_c, "ffn", ffn_w)
        else:
            wqkv = c_w_qkv[j].astype(BF)
            bqkv = c_b_qkv[j].reshape(1, -1)
            wo = c_w_out[j].astype(BF)
            bo = c_b_out[j].reshape(1, -1)
            qx, kx, vx = _proj_na(x, mx, g, wqkv, bqkv, tm_x)
            qc, kc, vc = _proj_na(cs, mc, g, wqkv, bqkv, tm_c)
            o = _na_attention(qx, kx, vx, kc, vc, _na_bias_table(c_rpb[j]))
            r_f32 = _pad_cols(m_router[j], LANES)
            r_hi = r_f32.astype(BF)
            router = (jnp.stack([r_hi, (r_f32 - r_hi.astype(F32)).astype(BF)]),)
            w1, w3, w2 = m_w1[j].astype(BF), m_w3[j].astype(BF), m_w2[j].astype(BF)
            x, *routed = _outproj([o], wo, bo, x, mx, g, tm_x, "router", router)
            x = _moe(x, routed, mx, g, w1, w3, w2, tm_x)
            if not last:
                oc = _attention(qc, [(kc, jnp.swapaxes(vc, 2, 3))], tq_c, "pair")
                cs, *routed = _outproj([oc], wo, bo, cs, mc, g, tm_c, "router", router)
                cs = _moe(cs, routed, mc, g, w1, w3, w2, tm_c)
    return x
```

```python
import functools
import math

import jax
import jax.numpy as jnp
import numpy as np
from jax import lax
from jax.experimental import pallas as pl
from jax.experimental.pallas import tpu as pltpu
from jax.experimental.pallas import tpu_sc as plsc

BF = jnp.bfloat16
F32 = jnp.float32

LANES = 128
VMEM_LIMIT = 56 * 1024 * 1024

GRID_W = 64
EPS = 1e-6
ROPE_THETA = 10000.0
N_MOD = 6

MLA_HEADS = 8
MLA_NOPE = 64
MLA_ROPE = 32
MLA_V = 64
MLA_Q_RANK = 384
MLA_KV_RANK = 256
LOG2E = math.log2(math.e)
MLA_SCALE = (MLA_NOPE + MLA_ROPE) ** -0.5 * LOG2E

DIFF_HEADS = 4
DIFF_HD = 64
DIFF_SCALE = DIFF_HD ** -0.5 * LOG2E

NA_HEADS = 16
NA_HD = 64
NA_KH = 8
NA_KW = 16
NA_SCALE = NA_HD ** -0.5 * LOG2E
NA_SLOTS = NA_HEADS * NA_HD // LANES
MASK_VALUE = -1e30

N_EXPERTS = 8

_NT = (((1,), (1,)), ((), ()))


def _cparams(*sem):
    return pltpu.CompilerParams(dimension_semantics=sem, vmem_limit_bytes=VMEM_LIMIT)


def _dot(a, b):
    return jnp.dot(a, b, preferred_element_type=F32)


def _rms(x, g):
    return x * lax.rsqrt(jnp.mean(x * x, axis=-1, keepdims=True) + EPS) * g


def _norm_mod(x, g, shift, scale):
    return _rms(x, g) * (1 + scale) + shift


def _const_spec(shape):
    return pl.BlockSpec(shape, lambda *_: (0,) * len(shape))


def _mod_spec(mod):
    if mod.shape[0] == 1:
        return pl.BlockSpec((1,) + mod.shape[1:], lambda b, *_: (0, 0, 0))
    return pl.BlockSpec((1,) + mod.shape[1:], lambda b, *_: (b, 0, 0))


def _mod_kernel(c_ref, w_ref, b_ref, o_ref):
    c = c_ref[...]
    sc = c * jax.nn.sigmoid(c)
    o_ref[0] = _dot(sc.astype(BF), w_ref[0].astype(BF)) + b_ref[0]


def _modulation(cvec, w_mod, b_mod):
    depth, d, n = w_mod.shape
    rows = cvec.shape[0]
    return pl.pallas_call(
        _mod_kernel,
        grid=(depth, n // d),
        in_specs=[
            pl.BlockSpec((rows, d), lambda i, j: (0, 0)),
            pl.BlockSpec((1, d, d), lambda i, j: (i, 0, j)),
            pl.BlockSpec((1, 1, d), lambda i, j: (i, 0, j)),
        ],
        out_specs=pl.BlockSpec((1, rows, d), lambda i, j: (i, 0, j)),
        out_shape=jax.ShapeDtypeStruct((depth, rows, n), F32),
        compiler_params=_cparams("arbitrary", "arbitrary"),
        name="modulation",
    )(cvec, w_mod, b_mod.reshape(depth, 1, n))


_Z_CQ = 0
_Z_CKV = MLA_Q_RANK
_Z_KR = _Z_CKV + MLA_KV_RANK
_Z_DQ = _Z_KR + LANES
_DW = 2 * DIFF_HEADS * DIFF_HD
_Z_DK = _Z_DQ + _DW
_Z_DV = _Z_DK + _DW


def _rope(x, cos, sin_signed, quarter):
    lane = lax.broadcasted_iota(jnp.int32, x.shape, 1)
    first = (lane % (2 * quarter)) < quarter
    rx = jnp.where(first, pltpu.roll(x, LANES - quarter, 1), pltpu.roll(x, quarter, 1))
    return x * cos + rx * sin_signed


def _proj_ab_kernel(x_ref, mod_ref, g_ref, win_ref, qn_ref, kvn_ref, wuq_ref, wuk_ref, wuv_ref,
                    tab_ref, q_ref, k_ref, v_ref, dq_ref, dk_ref, dv_ref):
    h = _norm_mod(x_ref[0], g_ref[0:1, :], mod_ref[0, 0:1, :], mod_ref[0, 1:2, :])
    z = _dot(h.astype(BF), win_ref[...])
    cqn = _rms(z[:, _Z_CQ:_Z_CKV], qn_ref[...]).astype(BF)
    ckvn = _rms(z[:, _Z_CKV:_Z_KR], kvn_ref[...]).astype(BF)
    q2 = _dot(cqn, wuq_ref[...])
    cos_q, sin_q = tab_ref[0], tab_ref[1]
    for hd in range(MLA_HEADS):
        lo = hd * LANES
        q_ref[0, hd] = _rope(q2[:, lo:lo + LANES], cos_q, sin_q, MLA_ROPE // 4).astype(BF)
    kr = _rope(z[:, _Z_KR:_Z_DQ], tab_ref[2], tab_ref[3], MLA_ROPE // 4)
    kk = _dot(ckvn, wuk_ref[...])
    vv = _dot(ckvn, wuv_ref[...])
    for hd in range(MLA_HEADS):
        lo = hd * LANES
        k_ref[0, hd] = (kk[:, lo:lo + LANES] + kr).astype(BF)
        v_ref[0, hd] = vv[:, lo:lo + LANES].T.astype(BF)
    cos_d, sin_d = tab_ref[4], tab_ref[5]
    for hd in range(DIFF_HEADS):
        lo = hd * LANES
        dq = _rope(z[:, _Z_DQ + lo:_Z_DQ + lo + LANES], cos_d, sin_d, DIFF_HD // 4)
        dq_ref[0, hd] = (dq * DIFF_SCALE).astype(BF)
        dk_ref[0, hd] = _rope(z[:, _Z_DK + lo:_Z_DK + lo + LANES], cos_d, sin_d, DIFF_HD // 4).astype(BF)
        dv_ref[0, hd] = z[:, _Z_DV + lo:_Z_DV + lo + LANES].T.astype(BF)


def _proj_ab(x, mod, g, wts, tabs, tm):
    b, t, d = x.shape
    win, qn, kvn, wuq, wuk, wuv = wts
    tok = lambda hh: pl.BlockSpec((1, hh, tm, LANES), lambda bi, i: (bi, 0, i, 0))
    shp = lambda hh: jax.ShapeDtypeStruct((b, hh, t, LANES), BF)
    tok_t = lambda hh, dv: pl.BlockSpec((1, hh, dv, tm), lambda bi, i: (bi, 0, 0, i))
    shp_t = lambda hh, dv: jax.ShapeDtypeStruct((b, hh, dv, t), BF)
    return pl.pallas_call(
        _proj_ab_kernel,
        grid=(b, t // tm),
        in_specs=[
            pl.BlockSpec((1, tm, d), lambda bi, i: (bi, i, 0)),
            _mod_spec(mod),
            _const_spec(g.shape),
            _const_spec(win.shape), _const_spec(qn.shape), _const_spec(kvn.shape),
            _const_spec(wuq.shape), _const_spec(wuk.shape), _const_spec(wuv.shape),
            pl.BlockSpec((6, tm, LANES), lambda bi, i: (0, i, 0)),
        ],
        out_specs=[tok(MLA_HEADS), tok(MLA_HEADS), tok_t(MLA_HEADS, LANES),
                   tok(DIFF_HEADS), tok(DIFF_HEADS), tok_t(DIFF_HEADS, LANES)],
        out_shape=[shp(MLA_HEADS), shp(MLA_HEADS), shp_t(MLA_HEADS, LANES),
                   shp(DIFF_HEADS), shp(DIFF_HEADS), shp_t(DIFF_HEADS, LANES)],
        compiler_params=_cparams("arbitrary", "arbitrary"),
        name="proj_ab",
    )(x, mod, g, win, qn, kvn, wuq, wuk, wuv, tabs)


KEY_CHUNK = 256


SUBLANES = 8
ATTN_QUERY_LANES = 1024


def _key_chunks(kv_refs):
    off = 0
    for k_ref, v_ref in kv_refs:
        lk = k_ref.shape[2]
        for c0 in range(0, lk, KEY_CHUNK):
            w = min(KEY_CHUNK, lk - c0)
            yield k_ref, v_ref, c0, w, off
            off += w


def _attn_kernel(*refs, nseg, mode, lam_init):
    q_ref = refs[0]
    kv_refs = [(refs[1 + 2 * s], refs[2 + 2 * s]) for s in range(nseg)]
    o_ref, s0_ref, s1_ref, mm0_ref, mm1_ref = refs[-5:]
    tq = q_ref.shape[2]
    width = s0_ref.shape[1]
    fold = lambda a: a.reshape(a.shape[0] // SUBLANES, SUBLANES, width)
    t = pl.program_id(0)

    @pl.when(t == 0)
    def _():
        s1_ref[...] = jnp.zeros_like(s1_ref)
        mm1_ref[...] = jnp.zeros_like(mm1_ref)

    def step(s_cur, mm_cur, s_prv, mm_prv):
        q = q_ref[0, 0]
        if mode != "single":
            lane = lax.broadcasted_iota(jnp.int32, q.shape, 1)
            zero = jnp.zeros_like(q)
            q = jnp.concatenate([jnp.where(lane < DIFF_HD, q, zero), jnp.where(lane >= DIFF_HD, q, zero)], axis=0)
        m_prv = jnp.max(mm_prv[...], axis=0, keepdims=True)
        mm = jnp.full((SUBLANES, width), -jnp.inf, F32)
        ll = jnp.zeros((SUBLANES, width), F32)
        o_t = jnp.zeros((LANES, width), F32)
        for k_ref, v_ref, c0, w, off in _key_chunks(kv_refs):
            s = lax.dot_general(k_ref[0, 0, c0:c0 + w, :], q, _NT, preferred_element_type=F32)
            s_cur[off:off + w, :] = s
            mm = jnp.maximum(mm, jnp.max(fold(s), axis=0))
            p = jnp.exp2(s_prv[off:off + w, :] - m_prv)
            ll = ll + jnp.sum(fold(p), axis=0)
            o_t = o_t + _dot(v_ref[0, 0, :, c0:c0 + w], p.astype(BF))
        mm_cur[...] = mm
        o_t = o_t / jnp.sum(ll, axis=0, keepdims=True)
        if mode == "single":
            o_ref[0, 0] = o_t.T.astype(BF)
        elif mode == "pair":
            o_ref[0, 0] = jnp.where(lane < DIFF_HD, o_t[:, :tq].T, o_t[:, tq:].T).astype(BF)
        else:
            lam_ref, subln_ref = refs[1 + 2 * nseg], refs[2 + 2 * nseg]
            lv = lam_ref[...]
            lam = (jnp.exp(jnp.sum(lv[0:1] * lv[1:2], axis=-1, keepdims=True))
                   - jnp.exp(jnp.sum(lv[2:3] * lv[3:4], axis=-1, keepdims=True)) + lam_init)
            d = o_t[:, :tq] - lam * o_t[:, tq:]
            dn = d * lax.rsqrt(jnp.mean(d * d, axis=0, keepdims=True) + EPS)
            o_ref[0, 0] = (dn.T * subln_ref[...] * (1 - lam_init)).astype(BF)

    pl.when(t % 2 == 0)(lambda: step(s0_ref, mm0_ref, s1_ref, mm1_ref))
    pl.when(t % 2 == 1)(lambda: step(s1_ref, mm1_ref, s0_ref, mm0_ref))


def _attention(q, kvs, tq, mode="single", lam_vecs=None, subln=None, lam_init=0.0):
    b, nh, lq, _ = q.shape
    diff = mode == "diff"
    nmap = 1 if mode == "single" else 2
    nq = lq // tq
    ntile = b * nh * nq

    def tile(t):
        t = jnp.clip(t, 0, ntile - 1)
        return t // (nh * nq), (t // nq) % nh, t % nq

    cur_head = lambda t: tile(t)[:2] + (0, 0)
    prv_head = lambda t: tile(t - 1)[:2] + (0, 0)
    in_specs = [pl.BlockSpec((1, 1, tq, LANES), lambda t: tile(t) + (0,))]
    args = [q]
    for k, v in kvs:
        lk = k.shape[2]
        in_specs += [pl.BlockSpec((1, 1, lk, LANES), cur_head), pl.BlockSpec((1, 1, v.shape[2], lk), prv_head)]
        args += [k, v]
    if diff:
        in_specs += [_const_spec(lam_vecs.shape), _const_spec(subln.shape)]
        args += [lam_vecs, subln]
    nkeys = sum(k.shape[2] for k, _ in kvs)
    return pl.pallas_call(
        functools.partial(_attn_kernel, nseg=len(kvs), mode=mode, lam_init=lam_init),
        grid=(ntile + 1,),
        in_specs=in_specs,
        out_specs=pl.BlockSpec((1, 1, tq, LANES), lambda t: tile(t - 1) + (0,)),
        out_shape=jax.ShapeDtypeStruct((b, nh, lq, LANES), BF),
        scratch_shapes=([pltpu.VMEM((nkeys, nmap * tq), F32)] * 2 + [pltpu.VMEM((SUBLANES, nmap * tq), F32)] * 2),
        compiler_params=_cparams("arbitrary"),
        name="attn_diff" if diff else "attn_mla",
    )(*args)


ROUTER_PIECES = 2


def _outproj_kernel(*refs, n_in, has_bias, tail):
    o_refs = refs[:n_in]
    w_ref = refs[n_in]
    pos = n_in + 1
    b_ref = None
    if has_bias:
        b_ref = refs[pos]
        pos += 1
    x_ref, mod_ref, g_ref = refs[pos:pos + 3]
    rest = refs[pos + 3:]
    tm = x_ref.shape[1]

    def residual(rows):
        o = jnp.concatenate([r[0, h, rows, :] for r in o_refs for h in range(r.shape[1])], axis=-1)
        y = _dot(o, w_ref[...])
        if has_bias:
            y = y + b_ref[...]
        x1 = x_ref[0, rows, :] + mod_ref[0, 2:3, :] * _rms(y, g_ref[1:2, :])
        return x1, _norm_mod(x1, g_ref[2:3, :], mod_ref[0, 3:4, :], mod_ref[0, 4:5, :])

    if tail == "ffn":
        w1_ref, w3_ref, w2_ref, out_ref = rest
        x1, h = residual(slice(0, tm))
        out_ref[0] = x1 + mod_ref[0, 5:6, :] * _rms(_swiglu(h.astype(BF), w1_ref, w3_ref, w2_ref), g_ref[3:4, :])
    else:
        router_ref, tri_ref, out_ref = rest[:3]
        step = tm // ROUTER_PIECES
        halves = [slice(r0, r0 + step) for r0 in range(0, tm, step)]
        parts = [residual(rows) for rows in halves]
        for rows, (x1, _) in zip(halves, parts):
            out_ref[0, rows, :] = x1
        first = (pl.program_id(0) == 0) & (pl.program_id(1) == 0)
        _route([h for _, h in parts], halves, first, router_ref, tri_ref, *rest[3:])


def _outproj(os_, w, bias, x, mod, g, tm, tail, tail_args):
    b, t, d = x.shape
    n = b * t
    nt = t // tm
    in_specs = [pl.BlockSpec((1, o.shape[1], tm, LANES), lambda bi, i: (bi, 0, i, 0)) for o in os_]
    in_specs.append(_const_spec(w.shape))
    args = list(os_) + [w]
    if bias is not None:
        in_specs.append(_const_spec(bias.shape))
        args.append(bias)
    x_spec = pl.BlockSpec((1, tm, d), lambda bi, i: (bi, i, 0))
    in_specs += [x_spec, _mod_spec(mod), _const_spec(g.shape)]
    args += [x, mod, g]
    out_specs, out_shape, scratch = x_spec, jax.ShapeDtypeStruct((b, t, d), F32), []
    if tail == "ffn":
        in_specs += [_resident_spec(a.shape, lambda bi, i: (0, 0)) for a in tail_args]
        args += list(tail_args)
    else:
        router, = tail_args
        tri = jnp.asarray(np.tril(np.ones((tm, tm), np.float32), -1), BF)
        in_specs += [_const_spec(router.shape), _const_spec(tri.shape)]
        args += [router, tri]
        rows = lambda wd: pl.BlockSpec((tm, wd), lambda bi, i: (bi * nt + i, 0))
        out_specs = [x_spec, pl.BlockSpec((PACK_CHUNKS, tm, LANES), lambda bi, i: (0, bi * nt + i, 0)),
                     rows(ROUTE_COLS), rows(ROUTE_COLS), pl.BlockSpec((1, LANES), lambda bi, i: (0, 0))]
        out_shape = [out_shape, jax.ShapeDtypeStruct((PACK_CHUNKS, n, LANES), jnp.int32),
                     jax.ShapeDtypeStruct((n, ROUTE_COLS), jnp.int32), jax.ShapeDtypeStruct((n, ROUTE_COLS), F32),
                     jax.ShapeDtypeStruct((1, LANES), F32)]
        scratch = [pltpu.VMEM((1, LANES), F32)]
    return pl.pallas_call(
        functools.partial(_outproj_kernel, n_in=len(os_), has_bias=bias is not None, tail=tail),
        grid=(b, nt),
        in_specs=in_specs,
        out_specs=out_specs,
        out_shape=out_shape,
        scratch_shapes=scratch,
        compiler_params=_cparams("arbitrary", "arbitrary"),
        name="outproj_" + tail,
    )(*args)


MXU_TILE = 256


def _ff_chunks(f):
    tiles = f // MXU_TILE
    if f % MXU_TILE or tiles < 2:
        return [(0, f)]
    first = (tiles + 1) // 2 * MXU_TILE
    return [(0, first), (first, f - first)]


def _swiglu(hn, w1_ref, w3_ref, w2_ref):
    y = None
    for lo, n in _ff_chunks(w1_ref.shape[1]):
        a = _dot(hn, w1_ref[:, lo:lo + n])
        act = (a * jax.nn.sigmoid(a) * _dot(hn, w3_ref[:, lo:lo + n])).astype(BF)
        part = _dot(act, w2_ref[lo:lo + n, :])
        y = part if y is None else y + part
    return y


def _resident_spec(shape, index_map):
    return pl.BlockSpec(shape, index_map, pipeline_mode=pl.Buffered(1))


MOE_ROW_TILE = 512
PACK_CHUNKS = 4
SC_WINDOW = 128
SC_SUBCORES = 32
ROUTE_COLS = 8
_HI_MASK = -65536
_LO_MASK = 65535


def _pack_rows(v, out_ref, rows=slice(None)):
    half = v.shape[1] // 2
    vb = v.astype(BF).astype(F32)
    lo = (pltpu.bitcast(vb[:, :half], jnp.int32) >> 16) & _LO_MASK
    hi = pltpu.bitcast(vb[:, half:], jnp.int32) & _HI_MASK
    w = lo | hi
    for c in range(PACK_CHUNKS):
        out_ref[c, rows, :] = w[:, c * LANES:(c + 1) * LANES]


def _unpack_rows(ref):
    w = jnp.concatenate([ref[c] for c in range(PACK_CHUNKS)], axis=-1)
    lo = pltpu.bitcast(w << 16, F32)
    hi = pltpu.bitcast(w & _HI_MASK, F32)
    return jnp.concatenate([lo, hi], axis=-1)


def _route(hs, row_slices, first, router_ref, tri_ref, hp_ref, meta_ref, gate_ref, cnt_ref, carry_ref):
    @pl.when(first)
    def _():
        carry_ref[...] = jnp.zeros_like(carry_ref)

    pieces = range(len(hs))
    for h, rows in zip(hs, row_slices):
        _pack_rows(h, hp_ref, rows)
    his = [h.astype(BF) for h in hs]
    los = [(hs[p] - his[p].astype(F32)).astype(BF) for p in pieces]
    raw = [_dot(his[p], router_ref[0]) + (_dot(his[p], router_ref[1]) + _dot(los[p], router_ref[0])) for p in pieces]
    lanes = [lax.broadcasted_iota(jnp.int32, r.shape, 1) for r in raw]
    lgs = [jnp.where(lanes[p] < N_EXPERTS, raw[p], -jnp.inf) for p in pieces]
    m1s = [jnp.max(lg, axis=-1, keepdims=True) for lg in lgs]
    i1s = [jnp.min(jnp.where(lgs[p] == m1s[p], lanes[p], LANES), axis=-1, keepdims=True) for p in pieces]
    rests = [jnp.where(lanes[p] == i1s[p], -jnp.inf, lgs[p]) for p in pieces]
    m2s = [jnp.max(r, axis=-1, keepdims=True) for r in rests]
    i2s = [jnp.min(jnp.where(rests[p] == m2s[p], lanes[p], LANES), axis=-1, keepdims=True) for p in pieces]
    rows_cat = lambda parts: jnp.concatenate(parts, axis=0)
    m1, m2, i1, i2, lane = rows_cat(m1s), rows_cat(m2s), rows_cat(i1s), rows_cat(i2s), rows_cat(lanes)
    e2 = jnp.exp(m2 - m1)
    denom = 1.0 + e2
    assigned = jnp.where((lane == i1) | (lane == i2), 1.0, 0.0)
    ranks = _dot(tri_ref[...], assigned.astype(BF)) + carry_ref[...]
    r1 = jnp.sum(jnp.where(lane == i1, ranks, 0.0), axis=-1, keepdims=True).astype(jnp.int32)
    r2 = jnp.sum(jnp.where(lane == i2, ranks, 0.0), axis=-1, keepdims=True).astype(jnp.int32)
    carry_ref[...] += jnp.sum(assigned, axis=0, keepdims=True)
    cnt_ref[...] = carry_ref[...]
    col = lax.broadcasted_iota(jnp.int32, meta_ref.shape, 1)
    meta_ref[...] = jnp.where(col == 0, i1, jnp.where(col == 1, i2, jnp.where(col == 2, r1, r2)))
    gate_ref[...] = jnp.where(col == 0, 1.0 / denom, e2 / denom)


def _sc_mesh():
    return plsc.VectorSubcoreMesh(core_axis_name="core", subcore_axis_name="subcore")


def _sc_scatter_rows(rows, idx_a, idx_b, n_out):
    nrows = rows.shape[0]

    @pl.kernel(out_type=jax.ShapeDtypeStruct((n_out, LANES), rows.dtype), mesh=_sc_mesh(), scratch_types=[])
    def scatter_kernel(x_hbm, ia_hbm, ib_hbm, o_hbm):
        def body(x_vmem, ia_vmem, ib_vmem):
            pltpu.sync_copy(x_vmem, o_hbm.at[ia_vmem.at[0]])
            pltpu.sync_copy(x_vmem, o_hbm.at[ib_vmem.at[0]])

        pltpu.emit_pipeline(
            body,
            grid=(nrows // SC_WINDOW,),
            in_specs=[pl.BlockSpec((SC_WINDOW, LANES), lambda i: (i, 0)),
                      pl.BlockSpec((1, SC_WINDOW), lambda i: (0, i)),
                      pl.BlockSpec((1, SC_WINDOW), lambda i: (0, i))],
            out_specs=[],
            core_axis_name=("core", "subcore"),
            dimension_semantics=(pltpu.PARALLEL,),
        )(x_hbm, ia_hbm, ib_hbm)

    return scatter_kernel(rows, idx_a, idx_b)


def _sc_gather_rows(table, idx_a, idx_b):
    nrows = idx_a.shape[1]
    out = jax.ShapeDtypeStruct((nrows, LANES), table.dtype)

    @pl.kernel(out_type=(out, out), mesh=_sc_mesh(), scratch_types=[])
    def gather_kernel(t_hbm, ia_hbm, ib_hbm, oa_hbm, ob_hbm):
        def body(ia_vmem, ib_vmem, oa_vmem, ob_vmem):
            pltpu.sync_copy(t_hbm.at[ia_vmem.at[0]], oa_vmem)
            pltpu.sync_copy(t_hbm.at[ib_vmem.at[0]], ob_vmem)

        pltpu.emit_pipeline(
            body,
            grid=(nrows // SC_WINDOW,),
            in_specs=[pl.BlockSpec((1, SC_WINDOW), lambda i: (0, i)),
                      pl.BlockSpec((1, SC_WINDOW), lambda i: (0, i))],
            out_specs=[pl.BlockSpec((SC_WINDOW, LANES), lambda i: (i, 0)),
                       pl.BlockSpec((SC_WINDOW, LANES), lambda i: (i, 0))],
            core_axis_name=("core", "subcore"),
            dimension_semantics=(pltpu.PARALLEL,),
        )(ia_hbm, ib_hbm, oa_hbm, ob_hbm)

    return gather_kernel(table, idx_a, idx_b)


def _experts_kernel(te_ref, tv_ref, xs_ref, w1_ref, w3_ref, w2_ref, ys_ref):
    del te_ref
    valid = tv_ref[pl.program_id(0)]

    @pl.when(valid > 0)
    def _():
        row = lax.broadcasted_iota(jnp.int32, (xs_ref.shape[1], 1), 0)
        hn = jnp.where(row < valid, _unpack_rows(xs_ref), 0.0).astype(BF)
        _pack_rows(_swiglu(hn, w1_ref.at[0], w3_ref.at[0], w2_ref.at[0]), ys_ref)

    @pl.when(valid == 0)
    def _():
        ys_ref[...] = jnp.zeros_like(ys_ref)


def _experts(xs, tile_expert, tile_valid, w1, w3, w2):
    _, p, _ = xs.shape
    tr = MOE_ROW_TILE
    rows = pl.BlockSpec((PACK_CHUNKS, tr, LANES), lambda j, te, tv: (0, j, 0))
    weights = lambda w: pl.BlockSpec((1,) + w.shape[1:], lambda j, te, tv: (te[j], 0, 0))
    return pl.pallas_call(
        _experts_kernel,
        grid_spec=pltpu.PrefetchScalarGridSpec(
            num_scalar_prefetch=2,
            grid=(p // tr,),
            in_specs=[rows, weights(w1), weights(w3), weights(w2)],
            out_specs=rows,
        ),
        out_shape=jax.ShapeDtypeStruct(xs.shape, jnp.int32),
        compiler_params=_cparams("arbitrary"),
        name="moe_experts",
    )(tile_expert, tile_valid, xs, w1, w3, w2)


def _combine_kernel(x_ref, ya_ref, yb_ref, gate_ref, mod_ref, g_ref, out_ref):
    gates = gate_ref[...]
    fx = gates[:, 0:1] * _unpack_rows(ya_ref) + gates[:, 1:2] * _unpack_rows(yb_ref)
    out_ref[0] = x_ref[0] + mod_ref[0, 5:6, :] * _rms(fx, g_ref[3:4, :])


def _combine(x, ya, yb, gates, mod, g, tm):
    b, t, d = x.shape
    nt = t // tm
    packed = pl.BlockSpec((PACK_CHUNKS, tm, LANES), lambda bi, i: (0, bi * nt + i, 0))
    return pl.pallas_call(
        _combine_kernel,
        grid=(b, nt),
        in_specs=[
            pl.BlockSpec((1, tm, d), lambda bi, i: (bi, i, 0)),
            packed, packed,
            pl.BlockSpec((tm, gates.shape[1]), lambda bi, i: (bi * nt + i, 0)),
            _mod_spec(mod),
            _const_spec(g.shape),
        ],
        out_specs=pl.BlockSpec((1, tm, d), lambda bi, i: (bi, i, 0)),
        out_shape=jax.ShapeDtypeStruct((b, t, d), F32),
        compiler_params=_cparams("arbitrary", "arbitrary"),
        name="moe_combine",
    )(x, ya, yb, gates, mod, g)


def _moe(x, routed, mod, g, w1, w3, w2, tm):
    b, t, d = x.shape
    n = b * t
    ne = w1.shape[0]
    tr = MOE_ROW_TILE
    assert d == 2 * PACK_CHUNKS * LANES and (PACK_CHUNKS * n) % (SC_WINDOW * SC_SUBCORES) == 0
    hp, meta, gates, counts = routed

    ntile = 2 * n // tr + ne
    p = ntile * tr
    cnt = counts[0, :ne].astype(jnp.int32)
    tiles = (cnt + tr - 1) // tr
    tile_end = jnp.cumsum(tiles)
    tile_start = tile_end - tiles
    base = tile_start * tr
    eids = jnp.arange(ne, dtype=jnp.int32)
    base_of = lambda e: jnp.sum(jnp.where(e[:, None] == eids[None, :], base[None, :], 0), axis=-1)
    pos_a = base_of(meta[:, 0]) + meta[:, 2]
    pos_b = base_of(meta[:, 1]) + meta[:, 3]
    chunk = jnp.arange(PACK_CHUNKS, dtype=jnp.int32)[:, None] * p
    idx_a = (chunk + pos_a[None, :]).reshape(1, PACK_CHUNKS * n)
    idx_b = (chunk + pos_b[None, :]).reshape(1, PACK_CHUNKS * n)
    tj = jnp.arange(ntile, dtype=jnp.int32)
    tile_expert = jnp.minimum(jnp.sum(tj[:, None] >= tile_end[None, :], axis=-1), ne - 1).astype(jnp.int32)
    done = jnp.sum(jnp.where(tile_expert[:, None] == eids[None, :], tile_start[None, :], 0), axis=-1)
    left = jnp.sum(jnp.where(tile_expert[:, None] == eids[None, :], cnt[None, :], 0), axis=-1) - (tj - done) * tr
    tile_valid = jnp.where(tj < tile_end[-1], jnp.clip(left, 0, tr), 0).astype(jnp.int32)

    xs = _sc_scatter_rows(hp.reshape(PACK_CHUNKS * n, LANES), idx_a, idx_b, PACK_CHUNKS * p)
    ys = _experts(xs.reshape(PACK_CHUNKS, p, LANES), tile_expert, tile_valid, w1, w3, w2)
    ya, yb = _sc_gather_rows(ys.reshape(PACK_CHUNKS * p, LANES), idx_a, idx_b)
    shp = (PACK_CHUNKS, n, LANES)
    return _combine(x, ya.reshape(shp), yb.reshape(shp), gates, mod, g, tm)


def _proj_na_kernel(x_ref, mod_ref, g_ref, w_ref, b_ref, q_ref, k_ref, v_ref):
    h = _norm_mod(x_ref[0], g_ref[0:1, :], mod_ref[0, 0:1, :], mod_ref[0, 1:2, :])
    z = _dot(h.astype(BF), w_ref[...]) + b_ref[...]
    n = NA_SLOTS * LANES
    for s in range(NA_SLOTS):
        lo = s * LANES
        q_ref[0, s] = (z[:, lo:lo + LANES] * NA_SCALE).astype(BF)
        k_ref[0, s] = z[:, n + lo:n + lo + LANES].astype(BF)
        v_ref[0, s] = z[:, 2 * n + lo:2 * n + lo + LANES].astype(BF)


def _proj_na(x, mod, g, w, bias, tm):
    b, t, d = x.shape
    tok = pl.BlockSpec((1, NA_SLOTS, tm, LANES), lambda bi, i: (bi, 0, i, 0))
    shp = jax.ShapeDtypeStruct((b, NA_SLOTS, t, LANES), BF)
    return pl.pallas_call(
        _proj_na_kernel,
        grid=(b, t // tm),
        in_specs=[
            pl.BlockSpec((1, tm, d), lambda bi, i: (bi, i, 0)),
            _mod_spec(mod),
            _const_spec(g.shape),
            _const_spec(w.shape),
            _const_spec(bias.shape),
        ],
        out_specs=[tok, tok, tok],
        out_shape=[shp, shp, shp],
        compiler_params=_cparams("arbitrary", "arbitrary"),
        name="proj_na",
    )(x, mod, g, w, bias)


NA_ROWS_PER_STEP = 2
NA_GROUP = 4


def _na_window_start(r, rows):
    return jnp.clip(r - NA_KH // 2, 0, rows - NA_KH)


def _na_kernel(q_ref, kx_ref, vx_ref, kc_ref, vc_ref, *rest, rows):
    bias_refs, o_ref = rest[:NA_ROWS_PER_STEP], rest[NA_ROWS_PER_STEP]
    lane = lax.broadcasted_iota(jnp.int32, (GRID_W, LANES), 1)
    problems = [(a, s) for a in range(NA_ROWS_PER_STEP) for s in range(NA_SLOTS)]
    starts = [pl.multiple_of(_na_window_start(pl.program_id(1) * NA_ROWS_PER_STEP + a, rows) * GRID_W, GRID_W)
              for a in range(NA_ROWS_PER_STEP)]
    for g0 in range(0, len(problems), NA_GROUP):
        _na_group(problems[g0:g0 + NA_GROUP], starts, lane, q_ref, kx_ref, vx_ref, kc_ref, vc_ref, bias_refs, o_ref)


def _na_group(work, starts, lane, q_ref, kx_ref, vx_ref, kc_ref, vc_ref, bias_refs, o_ref):
    nwin = NA_KH * GRID_W
    idx = range(len(work))
    sws, scs = [], []
    for a, s in work:
        q = q_ref[0, s, a * GRID_W:(a + 1) * GRID_W, :]
        zero = jnp.zeros_like(q)
        q2 = jnp.concatenate([jnp.where(lane < NA_HD, q, zero), jnp.where(lane >= NA_HD, q, zero)], axis=0)
        kw = kx_ref[0, s, pl.ds(starts[a], nwin), :]
        sws.append(lax.dot_general(q2, kw, _NT, preferred_element_type=F32) + bias_refs[a][0, s])
        scs.append(lax.dot_general(q2, kc_ref[0, s], _NT, preferred_element_type=F32))
    ms = [jnp.maximum(jnp.max(sws[i], axis=-1, keepdims=True), jnp.max(scs[i], axis=-1, keepdims=True))
          for i in idx]
    pws = [jnp.exp2(sws[i] - ms[i]) for i in idx]
    pcs = [jnp.exp2(scs[i] - ms[i]) for i in idx]
    ls = [jnp.sum(pws[i], axis=-1, keepdims=True) + jnp.sum(pcs[i], axis=-1, keepdims=True) for i in idx]
    for i, (a, s) in enumerate(work):
        vw = vx_ref[0, s, pl.ds(starts[a], nwin), :]
        o2 = (_dot(pws[i].astype(BF), vw) + _dot(pcs[i].astype(BF), vc_ref[0, s])) / ls[i]
        o_ref[0, s, a * GRID_W:(a + 1) * GRID_W, :] = jnp.where(lane < NA_HD, o2[:GRID_W], o2[GRID_W:]).astype(BF)


def _na_attention(q, kx, vx, kc, vc, bias):
    b, ns, l, _ = q.shape
    rows = l // GRID_W
    lc = kc.shape[2]
    nr = NA_ROWS_PER_STEP
    assert rows % nr == 0

    def bias_spec(a):
        def index_map(bi, i):
            r = i * nr + a
            return (r - _na_window_start(r, rows), 0, 0, 0)
        return pl.BlockSpec((1,) + bias.shape[1:], index_map)

    full = lambda n: pl.BlockSpec((1, ns, n, LANES), lambda bi, i: (bi, 0, 0, 0))
    row = pl.BlockSpec((1, ns, nr * GRID_W, LANES), lambda bi, i: (bi, 0, i, 0))
    return pl.pallas_call(
        functools.partial(_na_kernel, rows=rows),
        grid=(b, rows // nr),
        in_specs=[row, full(l), full(l), full(lc), full(lc)] + [bias_spec(a) for a in range(nr)],
        out_specs=row,
        out_shape=jax.ShapeDtypeStruct((b, ns, l, LANES), BF),
        compiler_params=_cparams("arbitrary", "arbitrary"),
        name="na_attn",
    )(q, kx, vx, kc, vc, *([bias] * nr))


def _axis_tables(pos, dim):
    inv = ROPE_THETA ** (-jnp.arange(0, dim, 2, dtype=F32) / dim)
    ang = pos.astype(F32)[:, None] * inv[None, :]
    ang = jnp.concatenate([ang, ang], axis=-1)
    return jnp.cos(ang), jnp.sin(ang)


def _rope_tables(n, rope):
    if rope:
        t = jnp.arange(n, dtype=jnp.int32)
        row, col = t // GRID_W, t % GRID_W

        def cs(d):
            cr, sr = _axis_tables(row, d // 2)
            cc, sc = _axis_tables(col, d // 2)
            sign = np.where(np.arange(d) % (d // 2) < d // 4, -1.0, 1.0).astype(np.float32)
            return jnp.concatenate([cr, cc], axis=-1), jnp.concatenate([sr, sc], axis=-1) * sign

        c32, s32 = cs(MLA_ROPE)
        c64, s64 = cs(DIFF_HD)
    else:
        c32, s32 = jnp.ones((n, MLA_ROPE), F32), jnp.zeros((n, MLA_ROPE), F32)
        c64, s64 = jnp.ones((n, DIFF_HD), F32), jnp.zeros((n, DIFF_HD), F32)
    ones = jnp.ones((n, MLA_NOPE), F32)
    pad = lambda a: jnp.pad(a, ((0, 0), (0, LANES - a.shape[1])))
    cos_q = MLA_SCALE * pad(jnp.concatenate([ones, c32], axis=-1))
    sin_q = MLA_SCALE * pad(jnp.concatenate([jnp.zeros_like(ones), s32], axis=-1))
    at_rope_lanes = lambda a: jnp.pad(a, ((0, 0), (MLA_NOPE, LANES - MLA_NOPE - MLA_ROPE)))
    return jnp.stack([cos_q, sin_q, at_rope_lanes(c32), at_rope_lanes(s32),
                      jnp.concatenate([c64, c64], axis=-1), jnp.concatenate([s64, s64], axis=-1)])


def _pad_cols(w, n):
    return jnp.pad(w, ((0, 0), (0, n - w.shape[1])))


def _prep_ab_weights(w_in, q_norm, kv_norm, w_uq, w_ukv, w_out):
    o_cq, o_ckv, o_kr = MLA_Q_RANK, MLA_Q_RANK + MLA_KV_RANK, MLA_Q_RANK + MLA_KV_RANK + MLA_ROPE
    o_dq, o_dk = o_kr + _DW, o_kr + 2 * _DW
    w_kr, w_dq, w_dk, w_dv = w_in[:, o_ckv:o_kr], w_in[:, o_kr:o_dq], w_in[:, o_dq:o_dk], w_in[:, o_dk:]
    w_kr = jnp.pad(w_kr, ((0, 0), (MLA_NOPE, LANES - MLA_NOPE - MLA_ROPE)))
    win = jnp.concatenate([w_in[:, :o_ckv], w_kr, w_dq, w_dk, w_dv], axis=1).astype(BF)
    rq = w_uq.shape[0]
    uq = w_uq.reshape(rq, MLA_HEADS, MLA_NOPE + MLA_ROPE)
    uq = jnp.pad(uq, ((0, 0), (0, 0), (0, LANES - MLA_NOPE - MLA_ROPE)))
    wuq = uq.reshape(rq, MLA_HEADS * LANES).astype(BF)
    rkv = w_ukv.shape[0]
    ukv = w_ukv.reshape(rkv, MLA_HEADS, MLA_NOPE + MLA_V)
    slot = lambda a: jnp.pad(a, ((0, 0), (0, 0), (0, LANES - a.shape[-1]))).reshape(rkv, MLA_HEADS * LANES)
    wuk = slot(ukv[..., :MLA_NOPE]).astype(BF)
    wuv = slot(ukv[..., MLA_NOPE:]).astype(BF)
    d = w_out.shape[1]
    wo_a = w_out[:MLA_HEADS * MLA_V].reshape(MLA_HEADS, MLA_V, d)
    wo_a = jnp.pad(wo_a, ((0, 0), (0, LANES - MLA_V), (0, 0))).reshape(MLA_HEADS * LANES, d)
    wo = jnp.concatenate([wo_a, w_out[MLA_HEADS * MLA_V:]], axis=0).astype(BF)
    proj = (win, q_norm.reshape(1, -1), kv_norm.reshape(1, -1), wuq, wuk, wuv)
    return proj, wo


def _na_bias_table(rpb):
    nh = rpb.shape[0]
    cols = np.arange(GRID_W)
    col_start = np.clip(cols - NA_KW // 2, 0, GRID_W - NA_KW)
    kcol = np.arange(GRID_W)
    valid = (kcol[None, :] >= col_start[:, None]) & (kcol[None, :] < col_start[:, None] + NA_KW)
    dc = kcol[None, :] - cols[:, None] + (NA_KW - 1)
    onehot = (dc[None] == np.arange(2 * NA_KW - 1)[:, None, None]) & valid[None]
    tz = jnp.einsum('hrd,dck->hrck', rpb * LOG2E, jnp.asarray(onehot, F32), precision=lax.Precision.HIGHEST)
    tz = jnp.where(valid[None, None], tz, MASK_VALUE)
    cases = [tz[:, NA_KH - 1 - c:2 * NA_KH - 1 - c] for c in range(NA_KH)]
    tbl = jnp.transpose(jnp.stack(cases), (0, 1, 3, 2, 4))
    return tbl.reshape(NA_KH, nh // 2, 2 * GRID_W, NA_KH * GRID_W)


def _tile(n, pref):
    return pref if n % pref == 0 else n


def kernel(x, c, ctx, c_ctx, w_mod, b_mod, norm_g, a_w_in, a_q_norm, a_kv_norm, a_w_uq, a_w_ukv, b_lambda, b_subln,
           ab_w_out, f_w1, f_w3, f_w2, c_w_qkv, c_b_qkv, c_rpb, c_w_out, c_b_out, m_router, m_w1, m_w3, m_w2):
    b, l, d = x.shape
    lc = ctx.shape[1]
    depth = w_mod.shape[0]
    assert l % GRID_W == 0 and l // GRID_W >= NA_KH

    mod_rows = 16
    cvec = jnp.zeros((mod_rows, d), F32).at[:b].set(c).at[b].set(c_ctx)
    mod = _modulation(cvec, w_mod, b_mod)

    tm_x, tm_c = _tile(l, 512), _tile(lc, 256)
    tq_x, tq_c = _tile(l, ATTN_QUERY_LANES), _tile(lc, ATTN_QUERY_LANES // 2)
    tq2_x = _tile(l, ATTN_QUERY_LANES // 2)
    cs = ctx
    for i in range(depth):
        last = i == depth - 1
        j = i // 2
        mx = mod[i, :b].reshape(b, N_MOD, d)
        mc = mod[i, b].reshape(1, N_MOD, d)
        g = norm_g[i]
        if i % 2 == 0:
            lam_init = 0.8 - 0.6 * math.exp(-0.3 * i)
            proj_w, wo = _prep_ab_weights(a_w_in[j], a_q_norm[j], a_kv_norm[j], a_w_uq[j], a_w_ukv[j], ab_w_out[j])
            qx, kx, vx, dqx, dkx, dvx = _proj_ab(x, mx, g, proj_w, _rope_tables(l, True), tm_x)
            qc, kc, vc, dqc, dkc, dvc = _proj_ab(cs, mc, g, proj_w, _rope_tables(lc, False), tm_c)
            subln = b_subln[j].reshape(1, -1)
            oa = _attention(qx, [(kx, vx), (kc, vc)], tq_x)
            ob = _attention(dqx, [(dkx, dvx), (dkc, dvc)], tq2_x, "diff", b_lambda[j], subln, lam_init)
            ffn_w = (f_w1[j].astype(BF), f_w3[j].astype(BF), f_w2[j].astype(BF))
            x = _outproj([oa, ob], wo, None, x, mx, g, tm_x, "ffn", ffn_w)
            if not last:
                oa = _attention(qc, [(kc, vc)], tq_c)
                ob = _attention(dqc, [(dkc, dvc)], tq_c, "diff", b_lambda[j], subln, lam_init)
                cs = _outproj([oa, ob], wo, None, cs, mc, g, tm_c, "ffn", ffn_w)
        else:
            wqkv = c_w_qkv[j].astype(BF)
            bqkv = c_b_qkv[j].reshape(1, -1)
            wo = c_w_out[j].astype(BF)
            bo = c_b_out[j].reshape(1, -1)
            qx, kx, vx = _proj_na(x, mx, g, wqkv, bqkv, tm_x)
            qc, kc, vc = _proj_na(cs, mc, g, wqkv, bqkv, tm_c)
            o = _na_attention(qx, kx, vx, kc, vc, _na_bias_table(c_rpb[j]))
            r_f32 = _pad_cols(m_router[j], LANES)
            r_hi = r_f32.astype(BF)
            router = (jnp.stack([r_hi, (r_f32 - r_hi.astype(F32)).astype(BF)]),)
            w1, w3, w2 = m_w1[j].astype(BF), m_w3[j].astype(BF), m_w2[j].astype(BF)
            x, *routed = _outproj([o], wo, bo, x, mx, g, tm_x, "router", router)
            x = _moe(x, routed, mx, g, w1, w3, w2, tm_x)
            if not last:
                oc = _attention(qc, [(kc, jnp.swapaxes(vc, 2, 3))], tq_c, "pair")
                cs, *routed = _outproj([oc], wo, bo, cs, mc, g, tm_c, "router", router)
                cs = _moe(cs, routed, mc, g, w1, w3, w2, tm_c)
    return x
```

```python
import functools
import math

import jax
import jax.numpy as jnp
import numpy as np
from jax import lax
from jax.experimental import pallas as pl
from jax.experimental.pallas import tpu as pltpu
from jax.experimental.pallas import tpu_sc as plsc

BF = jnp.bfloat16
F32 = jnp.float32

LANES = 128
VMEM_LIMIT = 56 * 1024 * 1024

GRID_W = 64
EPS = 1e-6
ROPE_THETA = 10000.0
N_MOD = 6

MLA_HEADS = 8
MLA_NOPE = 64
MLA_ROPE = 32
MLA_V = 64
MLA_Q_RANK = 384
MLA_KV_RANK = 256
LOG2E = math.log2(math.e)
MLA_SCALE = (MLA_NOPE + MLA_ROPE) ** -0.5 * LOG2E

DIFF_HEADS = 4
DIFF_HD = 64
DIFF_SCALE = DIFF_HD ** -0.5 * LOG2E

NA_HEADS = 16
NA_HD = 64
NA_KH = 8
NA_KW = 16
NA_SCALE = NA_HD ** -0.5 * LOG2E
NA_SLOTS = NA_HEADS * NA_HD // LANES
MASK_VALUE = -1e30

N_EXPERTS = 8

_NT = (((1,), (1,)), ((), ()))


def _cparams(*sem):
    return pltpu.CompilerParams(dimension_semantics=sem, vmem_limit_bytes=VMEM_LIMIT)


def _dot(a, b):
    return jnp.dot(a, b, preferred_element_type=F32)


def _rms(x, g):
    return x * lax.rsqrt(jnp.mean(x * x, axis=-1, keepdims=True) + EPS) * g


def _norm_mod(x, g, shift, scale):
    return _rms(x, g) * (1 + scale) + shift


def _const_spec(shape):
    return pl.BlockSpec(shape, lambda *_: (0,) * len(shape))


def _mod_spec(mod):
    if mod.shape[0] == 1:
        return pl.BlockSpec((1,) + mod.shape[1:], lambda b, *_: (0, 0, 0))
    return pl.BlockSpec((1,) + mod.shape[1:], lambda b, *_: (b, 0, 0))


def _mod_kernel(c_ref, w_ref, b_ref, o_ref):
    c = c_ref[...]
    sc = c * jax.nn.sigmoid(c)
    o_ref[0] = _dot(sc.astype(BF), w_ref[0].astype(BF)) + b_ref[0]


def _modulation(cvec, w_mod, b_mod):
    depth, d, n = w_mod.shape
    rows = cvec.shape[0]
    return pl.pallas_call(
        _mod_kernel,
        grid=(depth, n // d),
        in_specs=[
            pl.BlockSpec((rows, d), lambda i, j: (0, 0)),
            pl.BlockSpec((1, d, d), lambda i, j: (i, 0, j)),
            pl.BlockSpec((1, 1, d), lambda i, j: (i, 0, j)),
        ],
        out_specs=pl.BlockSpec((1, rows, d), lambda i, j: (i, 0, j)),
        out_shape=jax.ShapeDtypeStruct((depth, rows, n), F32),
        compiler_params=_cparams("arbitrary", "arbitrary"),
        name="modulation",
    )(cvec, w_mod, b_mod.reshape(depth, 1, n))


_Z_CQ = 0
_Z_CKV = MLA_Q_RANK
_Z_KR = _Z_CKV + MLA_KV_RANK
_Z_DQ = _Z_KR + LANES
_DW = 2 * DIFF_HEADS * DIFF_HD
_Z_DK = _Z_DQ + _DW
_Z_DV = _Z_DK + _DW


def _rope(x, cos, sin_signed, quarter):
    lane = lax.broadcasted_iota(jnp.int32, x.shape, 1)
    first = (lane % (2 * quarter)) < quarter
    rx = jnp.where(first, pltpu.roll(x, LANES - quarter, 1), pltpu.roll(x, quarter, 1))
    return x * cos + rx * sin_signed


def _proj_ab_kernel(x_ref, mod_ref, g_ref, win_ref, qn_ref, kvn_ref, wuq_ref, wuk_ref, wuv_ref,
                    tab_ref, q_ref, k_ref, v_ref, dq_ref, dk_ref, dv_ref):
    h = _norm_mod(x_ref[0], g_ref[0:1, :], mod_ref[0, 0:1, :], mod_ref[0, 1:2, :])
    z = _dot(h.astype(BF), win_ref[...])
    cqn = _rms(z[:, _Z_CQ:_Z_CKV], qn_ref[...]).astype(BF)
    ckvn = _rms(z[:, _Z_CKV:_Z_KR], kvn_ref[...]).astype(BF)
    q2 = _dot(cqn, wuq_ref[...])
    cos_q, sin_q = tab_ref[0], tab_ref[1]
    for hd in range(MLA_HEADS):
        lo = hd * LANES
        q_ref[0, hd] = _rope(q2[:, lo:lo + LANES], cos_q, sin_q, MLA_ROPE // 4).astype(BF)
    kr = _rope(z[:, _Z_KR:_Z_DQ], tab_ref[2], tab_ref[3], MLA_ROPE // 4)
    kk = _dot(ckvn, wuk_ref[...])
    vv = _dot(ckvn, wuv_ref[...])
    for hd in range(MLA_HEADS):
        lo = hd * LANES
        k_ref[0, hd] = (kk[:, lo:lo + LANES] + kr).astype(BF)
        v_ref[0, hd] = vv[:, lo:lo + LANES].T.astype(BF)
    cos_d, sin_d = tab_ref[4], tab_ref[5]
    for hd in range(DIFF_HEADS):
        lo = hd * LANES
        dq = _rope(z[:, _Z_DQ + lo:_Z_DQ + lo + LANES], cos_d, sin_d, DIFF_HD // 4)
        dq_ref[0, hd] = (dq * DIFF_SCALE).astype(BF)
        dk_ref[0, hd] = _rope(z[:, _Z_DK + lo:_Z_DK + lo + LANES], cos_d, sin_d, DIFF_HD // 4).astype(BF)
        dv_ref[0, hd] = z[:, _Z_DV + lo:_Z_DV + lo + LANES].T.astype(BF)


def _proj_ab(x, mod, g, wts, tabs, tm):
    b, t, d = x.shape
    win, qn, kvn, wuq, wuk, wuv = wts
    tok = lambda hh: pl.BlockSpec((1, hh, tm, LANES), lambda bi, i: (bi, 0, i, 0))
    shp = lambda hh: jax.ShapeDtypeStruct((b, hh, t, LANES), BF)
    tok_t = lambda hh, dv: pl.BlockSpec((1, hh, dv, tm), lambda bi, i: (bi, 0, 0, i))
    shp_t = lambda hh, dv: jax.ShapeDtypeStruct((b, hh, dv, t), BF)
    return pl.pallas_call(
        _proj_ab_kernel,
        grid=(b, t // tm),
        in_specs=[
            pl.BlockSpec((1, tm, d), lambda bi, i: (bi, i, 0)),
            _mod_spec(mod),
            _const_spec(g.shape),
            _const_spec(win.shape), _const_spec(qn.shape), _const_spec(kvn.shape),
            _const_spec(wuq.shape), _const_spec(wuk.shape), _const_spec(wuv.shape),
            pl.BlockSpec((6, tm, LANES), lambda bi, i: (0, i, 0)),
        ],
        out_specs=[tok(MLA_HEADS), tok(MLA_HEADS), tok_t(MLA_HEADS, LANES),
                   tok(DIFF_HEADS), tok(DIFF_HEADS), tok_t(DIFF_HEADS, LANES)],
        out_shape=[shp(MLA_HEADS), shp(MLA_HEADS), shp_t(MLA_HEADS, LANES),
                   shp(DIFF_HEADS), shp(DIFF_HEADS), shp_t(DIFF_HEADS, LANES)],
        compiler_params=_cparams("arbitrary", "arbitrary"),
        name="proj_ab",
    )(x, mod, g, win, qn, kvn, wuq, wuk, wuv, tabs)


KEY_CHUNK = 256


SUBLANES = 8
ATTN_QUERY_LANES = 1024


def _key_chunks(kv_refs):
    off = 0
    for k_ref, v_ref in kv_refs:
        lk = k_ref.shape[2]
        for c0 in range(0, lk, KEY_CHUNK):
            w = min(KEY_CHUNK, lk - c0)
            yield k_ref, v_ref, c0, w, off
            off += w


def _attn_kernel(*refs, nseg, mode, lam_init):
    q_ref = refs[0]
    kv_refs = [(refs[1 + 2 * s], refs[2 + 2 * s]) for s in range(nseg)]
    o_ref, s0_ref, s1_ref, mm0_ref, mm1_ref = refs[-5:]
    tq = q_ref.shape[2]
    width = s0_ref.shape[1]
    fold = lambda a: a.reshape(a.shape[0] // SUBLANES, SUBLANES, width)
    t = pl.program_id(0)

    @pl.when(t == 0)
    def _():
        s1_ref[...] = jnp.zeros_like(s1_ref)
        mm1_ref[...] = jnp.zeros_like(mm1_ref)

    def step(s_cur, mm_cur, s_prv, mm_prv):
        q = q_ref[0, 0]
        if mode != "single":
            lane = lax.broadcasted_iota(jnp.int32, q.shape, 1)
            zero = jnp.zeros_like(q)
            q = jnp.concatenate([jnp.where(lane < DIFF_HD, q, zero), jnp.where(lane >= DIFF_HD, q, zero)], axis=0)
        m_prv = jnp.max(mm_prv[...], axis=0, keepdims=True)
        mm = jnp.full((SUBLANES, width), -jnp.inf, F32)
        ll = jnp.zeros((SUBLANES, width), F32)
        o_t = jnp.zeros((LANES, width), F32)
        for k_ref, v_ref, c0, w, off in _key_chunks(kv_refs):
            s = lax.dot_general(k_ref[0, 0, c0:c0 + w, :], q, _NT, preferred_element_type=F32)
            s_cur[off:off + w, :] = s
            mm = jnp.maximum(mm, jnp.max(fold(s), axis=0))
            p = jnp.exp2(s_prv[off:off + w, :] - m_prv)
            ll = ll + jnp.sum(fold(p), axis=0)
            o_t = o_t + _dot(v_ref[0, 0, :, c0:c0 + w], p.astype(BF))
        mm_cur[...] = mm
        o_t = o_t / jnp.sum(ll, axis=0, keepdims=True)
        if mode == "single":
            o_ref[0, 0] = o_t.T.astype(BF)
        elif mode == "pair":
            o_ref[0, 0] = jnp.where(lane < DIFF_HD, o_t[:, :tq].T, o_t[:, tq:].T).astype(BF)
        else:
            lam_ref, subln_ref = refs[1 + 2 * nseg], refs[2 + 2 * nseg]
            lv = lam_ref[...]
            lam = (jnp.exp(jnp.sum(lv[0:1] * lv[1:2], axis=-1, keepdims=True))
                   - jnp.exp(jnp.sum(lv[2:3] * lv[3:4], axis=-1, keepdims=True)) + lam_init)
            d = o_t[:, :tq] - lam * o_t[:, tq:]
            dn = d * lax.rsqrt(jnp.mean(d * d, axis=0, keepdims=True) + EPS)
            o_ref[0, 0] = (dn.T * subln_ref[...] * (1 - lam_init)).astype(BF)

    pl.when(t % 2 == 0)(lambda: step(s0_ref, mm0_ref, s1_ref, mm1_ref))
    pl.when(t % 2 == 1)(lambda: step(s1_ref, mm1_ref, s0_ref, mm0_ref))


def _attention(q, kvs, tq, mode="single", lam_vecs=None, subln=None, lam_init=0.0):
    b, nh, lq, _ = q.shape
    diff = mode == "diff"
    nmap = 1 if mode == "single" else 2
    nq = lq // tq
    ntile = b * nh * nq

    def tile(t):
        t = jnp.clip(t, 0, ntile - 1)
        return t // (nh * nq), (t // nq) % nh, t % nq

    cur_head = lambda t: tile(t)[:2] + (0, 0)
    prv_head = lambda t: tile(t - 1)[:2] + (0, 0)
    in_specs = [pl.BlockSpec((1, 1, tq, LANES), lambda t: tile(t) + (0,))]
    args = [q]
    for k, v in kvs:
        lk = k.shape[2]
        in_specs += [pl.BlockSpec((1, 1, lk, LANES), cur_head), pl.BlockSpec((1, 1, v.shape[2], lk), prv_head)]
        args += [k, v]
    if diff:
        in_specs += [_const_spec(lam_vecs.shape), _const_spec(subln.shape)]
        args += [lam_vecs, subln]
    nkeys = sum(k.shape[2] for k, _ in kvs)
    return pl.pallas_call(
        functools.partial(_attn_kernel, nseg=len(kvs), mode=mode, lam_init=lam_init),
        grid=(ntile + 1,),
        in_specs=in_specs,
        out_specs=pl.BlockSpec((1, 1, tq, LANES), lambda t: tile(t - 1) + (0,)),
        out_shape=jax.ShapeDtypeStruct((b, nh, lq, LANES), BF),
        scratch_shapes=([pltpu.VMEM((nkeys, nmap * tq), F32)] * 2 + [pltpu.VMEM((SUBLANES, nmap * tq), F32)] * 2),
        compiler_params=_cparams("arbitrary"),
        name="attn_diff" if diff else "attn_mla",
    )(*args)


ROUTER_PIECES = 2


def _outproj_kernel(*refs, n_in, has_bias, tail):
    o_refs = refs[:n_in]
    w_ref = refs[n_in]
    pos = n_in + 1
    b_ref = None
    if has_bias:
        b_ref = refs[pos]
        pos += 1
    x_ref, mod_ref, g_ref = refs[pos:pos + 3]
    rest = refs[pos + 3:]
    tm = x_ref.shape[1]

    def residual(rows):
        o = jnp.concatenate([r[0, h, rows, :] for r in o_refs for h in range(r.shape[1])], axis=-1)
        y = _dot(o, w_ref[...])
        if has_bias:
            y = y + b_ref[...]
        x1 = x_ref[0, rows, :] + mod_ref[0, 2:3, :] * _rms(y, g_ref[1:2, :])
        return x1, _norm_mod(x1, g_ref[2:3, :], mod_ref[0, 3:4, :], mod_ref[0, 4:5, :])

    if tail == "ffn":
        w1_ref, w3_ref, w2_ref, out_ref = rest
        x1, h = residual(slice(0, tm))
        out_ref[0] = x1 + mod_ref[0, 5:6, :] * _rms(_swiglu(h.astype(BF), w1_ref, w3_ref, w2_ref), g_ref[3:4, :])
    else:
        router_ref, tri_ref, out_ref = rest[:3]
        step = tm // ROUTER_PIECES
        halves = [slice(r0, r0 + step) for r0 in range(0, tm, step)]
        parts = [residual(rows) for rows in halves]
        for rows, (x1, _) in zip(halves, parts):
            out_ref[0, rows, :] = x1
        first = (pl.program_id(0) == 0) & (pl.program_id(1) == 0)
        _route([h for _, h in parts], halves, first, router_ref, tri_ref, *rest[3:])


def _outproj(os_, w, bias, x, mod, g, tm, tail, tail_args):
    b, t, d = x.shape
    n = b * t
    nt = t // tm
    in_specs = [pl.BlockSpec((1, o.shape[1], tm, LANES), lambda bi, i: (bi, 0, i, 0)) for o in os_]
    in_specs.append(_const_spec(w.shape))
    args = list(os_) + [w]
    if bias is not None:
        in_specs.append(_const_spec(bias.shape))
        args.append(bias)
    x_spec = pl.BlockSpec((1, tm, d), lambda bi, i: (bi, i, 0))
    in_specs += [x_spec, _mod_spec(mod), _const_spec(g.shape)]
    args += [x, mod, g]
    out_specs, out_shape, scratch = x_spec, jax.ShapeDtypeStruct((b, t, d), F32), []
    if tail == "ffn":
        in_specs += [_resident_spec(a.shape, lambda bi, i: (0, 0)) for a in tail_args]
        args += list(tail_args)
    else:
        router, = tail_args
        tri = jnp.asarray(np.tril(np.ones((tm, tm), np.float32), -1), BF)
        in_specs += [_const_spec(router.shape), _const_spec(tri.shape)]
        args += [router, tri]
        rows = lambda wd: pl.BlockSpec((tm, wd), lambda bi, i: (bi * nt + i, 0))
        out_specs = [x_spec, pl.BlockSpec((PACK_CHUNKS, tm, LANES), lambda bi, i: (0, bi * nt + i, 0)),
                     rows(ROUTE_COLS), rows(ROUTE_COLS), pl.BlockSpec((1, LANES), lambda bi, i: (0, 0))]
        out_shape = [out_shape, jax.ShapeDtypeStruct((PACK_CHUNKS, n, LANES), jnp.int32),
                     jax.ShapeDtypeStruct((n, ROUTE_COLS), jnp.int32), jax.ShapeDtypeStruct((n, ROUTE_COLS), F32),
                     jax.ShapeDtypeStruct((1, LANES), F32)]
        scratch = [pltpu.VMEM((1, LANES), F32)]
    return pl.pallas_call(
        functools.partial(_outproj_kernel, n_in=len(os_), has_bias=bias is not None, tail=tail),
        grid=(b, nt),
        in_specs=in_specs,
        out_specs=out_specs,
        out_shape=out_shape,
        scratch_shapes=scratch,
        compiler_params=_cparams("arbitrary", "arbitrary"),
        name="outproj_" + tail,
    )(*args)


MXU_TILE = 256


def _ff_chunks(f):
    tiles = f // MXU_TILE
    if f % MXU_TILE or tiles < 2:
        return [(0, f)]
    first = (tiles + 1) // 2 * MXU_TILE
    return [(0, first), (first, f - first)]


def _swiglu(hn, w1_ref, w3_ref, w2_ref):
    y = None
    for lo, n in _ff_chunks(w1_ref.shape[1]):
        a = _dot(hn, w1_ref[:, lo:lo + n])
        act = (a * jax.nn.sigmoid(a) * _dot(hn, w3_ref[:, lo:lo + n])).astype(BF)
        part = _dot(act, w2_ref[lo:lo + n, :])
        y = part if y is None else y + part
    return y


def _resident_spec(shape, index_map):
    return pl.BlockSpec(shape, index_map, pipeline_mode=pl.Buffered(1))


MOE_ROW_TILE = 512
MOE_RETURN_PARTS = 2
PACK_CHUNKS = 4
SC_WINDOW = 128
SC_SUBCORES = 32
ROUTE_COLS = 8
_HI_MASK = -65536
_LO_MASK = 65535


def _pack_rows(v, out_ref, rows=slice(None)):
    half = v.shape[1] // 2
    vb = v.astype(BF).astype(F32)
    lo = (pltpu.bitcast(vb[:, :half], jnp.int32) >> 16) & _LO_MASK
    hi = pltpu.bitcast(vb[:, half:], jnp.int32) & _HI_MASK
    w = lo | hi
    for c in range(PACK_CHUNKS):
        out_ref[c, rows, :] = w[:, c * LANES:(c + 1) * LANES]


def _unpack_rows(ref):
    w = jnp.concatenate([ref[c] for c in range(PACK_CHUNKS)], axis=-1)
    lo = pltpu.bitcast(w << 16, F32)
    hi = pltpu.bitcast(w & _HI_MASK, F32)
    return jnp.concatenate([lo, hi], axis=-1)


def _route(hs, row_slices, first, router_ref, tri_ref, hp_ref, meta_ref, gate_ref, cnt_ref, carry_ref):
    @pl.when(first)
    def _():
        carry_ref[...] = jnp.zeros_like(carry_ref)

    pieces = range(len(hs))
    for h, rows in zip(hs, row_slices):
        _pack_rows(h, hp_ref, rows)
    his = [h.astype(BF) for h in hs]
    los = [(hs[p] - his[p].astype(F32)).astype(BF) for p in pieces]
    raw = [_dot(his[p], router_ref[0]) + (_dot(his[p], router_ref[1]) + _dot(los[p], router_ref[0])) for p in pieces]
    lanes = [lax.broadcasted_iota(jnp.int32, r.shape, 1) for r in raw]
    lgs = [jnp.where(lanes[p] < N_EXPERTS, raw[p], -jnp.inf) for p in pieces]
    m1s = [jnp.max(lg, axis=-1, keepdims=True) for lg in lgs]
    i1s = [jnp.min(jnp.where(lgs[p] == m1s[p], lanes[p], LANES), axis=-1, keepdims=True) for p in pieces]
    rests = [jnp.where(lanes[p] == i1s[p], -jnp.inf, lgs[p]) for p in pieces]
    m2s = [jnp.max(r, axis=-1, keepdims=True) for r in rests]
    i2s = [jnp.min(jnp.where(rests[p] == m2s[p], lanes[p], LANES), axis=-1, keepdims=True) for p in pieces]
    rows_cat = lambda parts: jnp.concatenate(parts, axis=0)
    m1, m2, i1, i2, lane = rows_cat(m1s), rows_cat(m2s), rows_cat(i1s), rows_cat(i2s), rows_cat(lanes)
    e2 = jnp.exp(m2 - m1)
    denom = 1.0 + e2
    assigned = jnp.where((lane == i1) | (lane == i2), 1.0, 0.0)
    ranks = _dot(tri_ref[...], assigned.astype(BF)) + carry_ref[...]
    r1 = jnp.sum(jnp.where(lane == i1, ranks, 0.0), axis=-1, keepdims=True).astype(jnp.int32)
    r2 = jnp.sum(jnp.where(lane == i2, ranks, 0.0), axis=-1, keepdims=True).astype(jnp.int32)
    carry_ref[...] += jnp.sum(assigned, axis=0, keepdims=True)
    cnt_ref[...] = carry_ref[...]
    col = lax.broadcasted_iota(jnp.int32, meta_ref.shape, 1)
    meta_ref[...] = jnp.where(col == 0, i1, jnp.where(col == 1, i2, jnp.where(col == 2, r1, r2)))
    gate_ref[...] = jnp.where(col == 0, 1.0 / denom, e2 / denom)


def _sc_mesh():
    return plsc.VectorSubcoreMesh(core_axis_name="core", subcore_axis_name="subcore")


def _sc_scatter_rows(rows, idx_a, idx_b, n_out):
    nrows = rows.shape[0]

    @pl.kernel(out_type=jax.ShapeDtypeStruct((n_out, LANES), rows.dtype), mesh=_sc_mesh(), scratch_types=[])
    def scatter_kernel(x_hbm, ia_hbm, ib_hbm, o_hbm):
        def body(x_vmem, ia_vmem, ib_vmem):
            pltpu.sync_copy(x_vmem, o_hbm.at[ia_vmem.at[0]])
            pltpu.sync_copy(x_vmem, o_hbm.at[ib_vmem.at[0]])

        pltpu.emit_pipeline(
            body,
            grid=(nrows // SC_WINDOW,),
            in_specs=[pl.BlockSpec((SC_WINDOW, LANES), lambda i: (i, 0)),
                      pl.BlockSpec((1, SC_WINDOW), lambda i: (0, i)),
                      pl.BlockSpec((1, SC_WINDOW), lambda i: (0, i))],
            out_specs=[],
            core_axis_name=("core", "subcore"),
            dimension_semantics=(pltpu.PARALLEL,),
        )(x_hbm, ia_hbm, ib_hbm)

    return scatter_kernel(rows, idx_a, idx_b)


def _sc_gather_rows(table, idx_a, idx_b):
    nrows = idx_a.shape[1]
    out = jax.ShapeDtypeStruct((nrows, LANES), table.dtype)

    @pl.kernel(out_type=(out, out), mesh=_sc_mesh(), scratch_types=[])
    def gather_kernel(t_hbm, ia_hbm, ib_hbm, oa_hbm, ob_hbm):
        def body(ia_vmem, ib_vmem, oa_vmem, ob_vmem):
            pltpu.sync_copy(t_hbm.at[ia_vmem.at[0]], oa_vmem)
            pltpu.sync_copy(t_hbm.at[ib_vmem.at[0]], ob_vmem)

        pltpu.emit_pipeline(
            body,
            grid=(nrows // SC_WINDOW,),
            in_specs=[pl.BlockSpec((1, SC_WINDOW), lambda i: (0, i)),
                      pl.BlockSpec((1, SC_WINDOW), lambda i: (0, i))],
            out_specs=[pl.BlockSpec((SC_WINDOW, LANES), lambda i: (i, 0)),
                       pl.BlockSpec((SC_WINDOW, LANES), lambda i: (i, 0))],
            core_axis_name=("core", "subcore"),
            dimension_semantics=(pltpu.PARALLEL,),
        )(ia_hbm, ib_hbm, oa_hbm, ob_hbm)

    return gather_kernel(table, idx_a, idx_b)


def _experts_kernel(te_ref, tv_ref, xs_ref, w1_ref, w3_ref, w2_ref, ys_ref):
    del te_ref
    valid = tv_ref[pl.program_id(0)]

    @pl.when(valid > 0)
    def _():
        row = lax.broadcasted_iota(jnp.int32, (xs_ref.shape[1], 1), 0)
        hn = jnp.where(row < valid, _unpack_rows(xs_ref), 0.0).astype(BF)
        _pack_rows(_swiglu(hn, w1_ref.at[0], w3_ref.at[0], w2_ref.at[0]), ys_ref)

    @pl.when(valid == 0)
    def _():
        ys_ref[...] = jnp.zeros_like(ys_ref)


def _experts(xs, tile_expert, tile_valid, w1, w3, w2):
    _, p, _ = xs.shape
    tr = MOE_ROW_TILE
    rows = pl.BlockSpec((PACK_CHUNKS, tr, LANES), lambda j, te, tv: (0, j, 0))
    weights = lambda w: pl.BlockSpec((1,) + w.shape[1:], lambda j, te, tv: (te[j], 0, 0))
    return pl.pallas_call(
        _experts_kernel,
        grid_spec=pltpu.PrefetchScalarGridSpec(
            num_scalar_prefetch=2,
            grid=(p // tr,),
            in_specs=[rows, weights(w1), weights(w3), weights(w2)],
            out_specs=rows,
        ),
        out_shape=jax.ShapeDtypeStruct(xs.shape, jnp.int32),
        compiler_params=_cparams("arbitrary"),
        name="moe_experts",
    )(tile_expert, tile_valid, xs, w1, w3, w2)


def _combine_kernel(x_ref, ya_ref, yb_ref, gate_ref, mod_ref, g_ref, *rest):
    out_ref = rest[-1]
    gates = gate_ref[...]
    fx = gates[:, 0:1] * _unpack_rows(ya_ref) + gates[:, 1:2] * _unpack_rows(yb_ref)
    out_ref[0] = x_ref[0] + mod_ref[0, 5:6, :] * _rms(fx, g_ref[3:4, :])


def _combine(x, ya, yb, gates, mod, g, tm, batch0, nbatch, prev):
    b, t, d = x.shape
    nt = t // tm
    tok = lambda bi, i: (batch0 + bi, i, 0)
    packed = pl.BlockSpec((PACK_CHUNKS, tm, LANES), lambda bi, i: (0, bi * nt + i, 0))
    mod_map = (lambda bi, i: (0, 0, 0)) if mod.shape[0] == 1 else (lambda bi, i: (batch0 + bi, 0, 0))
    in_specs = [
        pl.BlockSpec((1, tm, d), tok),
        packed, packed,
        pl.BlockSpec((tm, gates.shape[1]), lambda bi, i: ((batch0 + bi) * nt + i, 0)),
        pl.BlockSpec((1,) + mod.shape[1:], mod_map),
        _const_spec(g.shape),
    ]
    args = [x, ya, yb, gates, mod, g]
    aliases = {}
    if prev is not None:
        in_specs.append(pl.BlockSpec(memory_space=pl.ANY))
        args.append(prev)
        aliases = {len(args) - 1: 0}
    return pl.pallas_call(
        _combine_kernel,
        grid=(nbatch, nt),
        in_specs=in_specs,
        out_specs=pl.BlockSpec((1, tm, d), tok),
        out_shape=jax.ShapeDtypeStruct((b, t, d), F32),
        input_output_aliases=aliases,
        compiler_params=_cparams("arbitrary", "arbitrary"),
        name="moe_combine",
    )(*args)


def _moe(x, routed, mod, g, w1, w3, w2, tm):
    b, t, d = x.shape
    n = b * t
    ne = w1.shape[0]
    tr = MOE_ROW_TILE
    assert d == 2 * PACK_CHUNKS * LANES and (PACK_CHUNKS * n) % (SC_WINDOW * SC_SUBCORES) == 0
    hp, meta, gates, counts = routed

    ntile = 2 * n // tr + ne
    p = ntile * tr
    cnt = counts[0, :ne].astype(jnp.int32)
    tiles = (cnt + tr - 1) // tr
    tile_end = jnp.cumsum(tiles)
    tile_start = tile_end - tiles
    base = tile_start * tr
    eids = jnp.arange(ne, dtype=jnp.int32)
    base_of = lambda e: jnp.sum(jnp.where(e[:, None] == eids[None, :], base[None, :], 0), axis=-1)
    pos_a = base_of(meta[:, 0]) + meta[:, 2]
    pos_b = base_of(meta[:, 1]) + meta[:, 3]
    chunk = jnp.arange(PACK_CHUNKS, dtype=jnp.int32)[:, None] * p
    idx_a = (chunk + pos_a[None, :]).reshape(1, PACK_CHUNKS * n)
    idx_b = (chunk + pos_b[None, :]).reshape(1, PACK_CHUNKS * n)
    tj = jnp.arange(ntile, dtype=jnp.int32)
    tile_expert = jnp.minimum(jnp.sum(tj[:, None] >= tile_end[None, :], axis=-1), ne - 1).astype(jnp.int32)
    done = jnp.sum(jnp.where(tile_expert[:, None] == eids[None, :], tile_start[None, :], 0), axis=-1)
    left = jnp.sum(jnp.where(tile_expert[:, None] == eids[None, :], cnt[None, :], 0), axis=-1) - (tj - done) * tr
    tile_valid = jnp.where(tj < tile_end[-1], jnp.clip(left, 0, tr), 0).astype(jnp.int32)

    xs = _sc_scatter_rows(hp.reshape(PACK_CHUNKS * n, LANES), idx_a, idx_b, PACK_CHUNKS * p)
    ys = _experts(xs.reshape(PACK_CHUNKS, p, LANES), tile_expert, tile_valid, w1, w3, w2)
    ys = ys.reshape(PACK_CHUNKS * p, LANES)
    parts = MOE_RETURN_PARTS if b % MOE_RETURN_PARTS == 0 else 1
    nb, npart = b // parts, n // parts
    out = None
    for part in range(parts):
        take = lambda idx: idx.reshape(PACK_CHUNKS, n)[:, part * npart:(part + 1) * npart].reshape(1, -1)
        ya, yb = _sc_gather_rows(ys, take(idx_a), take(idx_b))
        shp = (PACK_CHUNKS, npart, LANES)
        out = _combine(x, ya.reshape(shp), yb.reshape(shp), gates, mod, g, tm, part * nb, nb, out)
    return out


def _proj_na_kernel(x_ref, mod_ref, g_ref, w_ref, b_ref, q_ref, k_ref, v_ref):
    h = _norm_mod(x_ref[0], g_ref[0:1, :], mod_ref[0, 0:1, :], mod_ref[0, 1:2, :])
    z = _dot(h.astype(BF), w_ref[...]) + b_ref[...]
    n = NA_SLOTS * LANES
    for s in range(NA_SLOTS):
        lo = s * LANES
        q_ref[0, s] = (z[:, lo:lo + LANES] * NA_SCALE).astype(BF)
        k_ref[0, s] = z[:, n + lo:n + lo + LANES].astype(BF)
        v_ref[0, s] = z[:, 2 * n + lo:2 * n + lo + LANES].astype(BF)


def _proj_na(x, mod, g, w, bias, tm):
    b, t, d = x.shape
    tok = pl.BlockSpec((1, NA_SLOTS, tm, LANES), lambda bi, i: (bi, 0, i, 0))
    shp = jax.ShapeDtypeStruct((b, NA_SLOTS, t, LANES), BF)
    return pl.pallas_call(
        _proj_na_kernel,
        grid=(b, t // tm),
        in_specs=[
            pl.BlockSpec((1, tm, d), lambda bi, i: (bi, i, 0)),
            _mod_spec(mod),
            _const_spec(g.shape),
            _const_spec(w.shape),
            _const_spec(bias.shape),
        ],
        out_specs=[tok, tok, tok],
        out_shape=[shp, shp, shp],
        compiler_params=_cparams("arbitrary", "arbitrary"),
        name="proj_na",
    )(x, mod, g, w, bias)


NA_ROWS_PER_STEP = 2
NA_GROUP = 4


def _na_window_start(r, rows):
    return jnp.clip(r - NA_KH // 2, 0, rows - NA_KH)


def _na_kernel(q_ref, kx_ref, vx_ref, kc_ref, vc_ref, *rest, rows):
    bias_refs, o_ref = rest[:NA_ROWS_PER_STEP], rest[NA_ROWS_PER_STEP]
    lane = lax.broadcasted_iota(jnp.int32, (GRID_W, LANES), 1)
    problems = [(a, s) for a in range(NA_ROWS_PER_STEP) for s in range(NA_SLOTS)]
    starts = [pl.multiple_of(_na_window_start(pl.program_id(1) * NA_ROWS_PER_STEP + a, rows) * GRID_W, GRID_W)
              for a in range(NA_ROWS_PER_STEP)]
    for g0 in range(0, len(problems), NA_GROUP):
        _na_group(problems[g0:g0 + NA_GROUP], starts, lane, q_ref, kx_ref, vx_ref, kc_ref, vc_ref, bias_refs, o_ref)


def _na_group(work, starts, lane, q_ref, kx_ref, vx_ref, kc_ref, vc_ref, bias_refs, o_ref):
    nwin = NA_KH * GRID_W
    idx = range(len(work))
    sws, scs = [], []
    for a, s in work:
        q = q_ref[0, s, a * GRID_W:(a + 1) * GRID_W, :]
        zero = jnp.zeros_like(q)
        q2 = jnp.concatenate([jnp.where(lane < NA_HD, q, zero), jnp.where(lane >= NA_HD, q, zero)], axis=0)
        kw = kx_ref[0, s, pl.ds(starts[a], nwin), :]
        sws.append(lax.dot_general(q2, kw, _NT, preferred_element_type=F32) + bias_refs[a][0, s])
        scs.append(lax.dot_general(q2, kc_ref[0, s], _NT, preferred_element_type=F32))
    ms = [jnp.maximum(jnp.max(sws[i], axis=-1, keepdims=True), jnp.max(scs[i], axis=-1, keepdims=True))
          for i in idx]
    pws = [jnp.exp2(sws[i] - ms[i]) for i in idx]
    pcs = [jnp.exp2(scs[i] - ms[i]) for i in idx]
    ls = [jnp.sum(pws[i], axis=-1, keepdims=True) + jnp.sum(pcs[i], axis=-1, keepdims=True) for i in idx]
    for i, (a, s) in enumerate(work):
        vw = vx_ref[0, s, pl.ds(starts[a], nwin), :]
        o2 = (_dot(pws[i].astype(BF), vw) + _dot(pcs[i].astype(BF), vc_ref[0, s])) / ls[i]
        o_ref[0, s, a * GRID_W:(a + 1) * GRID_W, :] = jnp.where(lane < NA_HD, o2[:GRID_W], o2[GRID_W:]).astype(BF)


def _na_attention(q, kx, vx, kc, vc, bias):
    b, ns, l, _ = q.shape
    rows = l // GRID_W
    lc = kc.shape[2]
    nr = NA_ROWS_PER_STEP
    assert rows % nr == 0

    def bias_spec(a):
        def index_map(bi, i):
            r = i * nr + a
            return (r - _na_window_start(r, rows), 0, 0, 0)
        return pl.BlockSpec((1,) + bias.shape[1:], index_map)

    full = lambda n: pl.BlockSpec((1, ns, n, LANES), lambda bi, i: (bi, 0, 0, 0))
    row = pl.BlockSpec((1, ns, nr * GRID_W, LANES), lambda bi, i: (bi, 0, i, 0))
    return pl.pallas_call(
        functools.partial(_na_kernel, rows=rows),
        grid=(b, rows // nr),
        in_specs=[row, full(l), full(l), full(lc), full(lc)] + [bias_spec(a) for a in range(nr)],
        out_specs=row,
        out_shape=jax.ShapeDtypeStruct((b, ns, l, LANES), BF),
        compiler_params=_cparams("arbitrary", "arbitrary"),
        name="na_attn",
    )(q, kx, vx, kc, vc, *([bias] * nr))


def _axis_tables(pos, dim):
    inv = ROPE_THETA ** (-jnp.arange(0, dim, 2, dtype=F32) / dim)
    ang = pos.astype(F32)[:, None] * inv[None, :]
    ang = jnp.concatenate([ang, ang], axis=-1)
    return jnp.cos(ang), jnp.sin(ang)


def _rope_tables(n, rope):
    if rope:
        t = jnp.arange(n, dtype=jnp.int32)
        row, col = t // GRID_W, t % GRID_W

        def cs(d):
            cr, sr = _axis_tables(row, d // 2)
            cc, sc = _axis_tables(col, d // 2)
            sign = np.where(np.arange(d) % (d // 2) < d // 4, -1.0, 1.0).astype(np.float32)
            return jnp.concatenate([cr, cc], axis=-1), jnp.concatenate([sr, sc], axis=-1) * sign

        c32, s32 = cs(MLA_ROPE)
        c64, s64 = cs(DIFF_HD)
    else:
        c32, s32 = jnp.ones((n, MLA_ROPE), F32), jnp.zeros((n, MLA_ROPE), F32)
        c64, s64 = jnp.ones((n, DIFF_HD), F32), jnp.zeros((n, DIFF_HD), F32)
    ones = jnp.ones((n, MLA_NOPE), F32)
    pad = lambda a: jnp.pad(a, ((0, 0), (0, LANES - a.shape[1])))
    cos_q = MLA_SCALE * pad(jnp.concatenate([ones, c32], axis=-1))
    sin_q = MLA_SCALE * pad(jnp.concatenate([jnp.zeros_like(ones), s32], axis=-1))
    at_rope_lanes = lambda a: jnp.pad(a, ((0, 0), (MLA_NOPE, LANES - MLA_NOPE - MLA_ROPE)))
    return jnp.stack([cos_q, sin_q, at_rope_lanes(c32), at_rope_lanes(s32),
                      jnp.concatenate([c64, c64], axis=-1), jnp.concatenate([s64, s64], axis=-1)])


def _pad_cols(w, n):
    return jnp.pad(w, ((0, 0), (0, n - w.shape[1])))


def _prep_ab_weights(w_in, q_norm, kv_norm, w_uq, w_ukv, w_out):
    o_cq, o_ckv, o_kr = MLA_Q_RANK, MLA_Q_RANK + MLA_KV_RANK, MLA_Q_RANK + MLA_KV_RANK + MLA_ROPE
    o_dq, o_dk = o_kr + _DW, o_kr + 2 * _DW
    w_kr, w_dq, w_dk, w_dv = w_in[:, o_ckv:o_kr], w_in[:, o_kr:o_dq], w_in[:, o_dq:o_dk], w_in[:, o_dk:]
    w_kr = jnp.pad(w_kr, ((0, 0), (MLA_NOPE, LANES - MLA_NOPE - MLA_ROPE)))
    win = jnp.concatenate([w_in[:, :o_ckv], w_kr, w_dq, w_dk, w_dv], axis=1).astype(BF)
    rq = w_uq.shape[0]
    uq = w_uq.reshape(rq, MLA_HEADS, MLA_NOPE + MLA_ROPE)
    uq = jnp.pad(uq, ((0, 0), (0, 0), (0, LANES - MLA_NOPE - MLA_ROPE)))
    wuq = uq.reshape(rq, MLA_HEADS * LANES).astype(BF)
    rkv = w_ukv.shape[0]
    ukv = w_ukv.reshape(rkv, MLA_HEADS, MLA_NOPE + MLA_V)
    slot = lambda a: jnp.pad(a, ((0, 0), (0, 0), (0, LANES - a.shape[-1]))).reshape(rkv, MLA_HEADS * LANES)
    wuk = slot(ukv[..., :MLA_NOPE]).astype(BF)
    wuv = slot(ukv[..., MLA_NOPE:]).astype(BF)
    d = w_out.shape[1]
    wo_a = w_out[:MLA_HEADS * MLA_V].reshape(MLA_HEADS, MLA_V, d)
    wo_a = jnp.pad(wo_a, ((0, 0), (0, LANES - MLA_V), (0, 0))).reshape(MLA_HEADS * LANES, d)
    wo = jnp.concatenate([wo_a, w_out[MLA_HEADS * MLA_V:]], axis=0).astype(BF)
    proj = (win, q_norm.reshape(1, -1), kv_norm.reshape(1, -1), wuq, wuk, wuv)
    return proj, wo


def _na_bias_table(rpb):
    nh = rpb.shape[0]
    cols = np.arange(GRID_W)
    col_start = np.clip(cols - NA_KW // 2, 0, GRID_W - NA_KW)
    kcol = np.arange(GRID_W)
    valid = (kcol[None, :] >= col_start[:, None]) & (kcol[None, :] < col_start[:, None] + NA_KW)
    dc = kcol[None, :] - cols[:, None] + (NA_KW - 1)
    onehot = (dc[None] == np.arange(2 * NA_KW - 1)[:, None, None]) & valid[None]
    tz = jnp.einsum('hrd,dck->hrck', rpb * LOG2E, jnp.asarray(onehot, F32), precision=lax.Precision.HIGHEST)
    tz = jnp.where(valid[None, None], tz, MASK_VALUE)
    cases = [tz[:, NA_KH - 1 - c:2 * NA_KH - 1 - c] for c in range(NA_KH)]
    tbl = jnp.transpose(jnp.stack(cases), (0, 1, 3, 2, 4))
    return tbl.reshape(NA_KH, nh // 2, 2 * GRID_W, NA_KH * GRID_W)


def _tile(n, pref):
    return pref if n % pref == 0 else n


def kernel(x, c, ctx, c_ctx, w_mod, b_mod, norm_g, a_w_in, a_q_norm, a_kv_norm, a_w_uq, a_w_ukv, b_lambda, b_subln,
           ab_w_out, f_w1, f_w3, f_w2, c_w_qkv, c_b_qkv, c_rpb, c_w_out, c_b_out, m_router, m_w1, m_w3, m_w2):
    b, l, d = x.shape
    lc = ctx.shape[1]
    depth = w_mod.shape[0]
    assert l % GRID_W == 0 and l // GRID_W >= NA_KH

    mod_rows = 16
    cvec = jnp.zeros((mod_rows, d), F32).at[:b].set(c).at[b].set(c_ctx)
    mod = _modulation(cvec, w_mod, b_mod)

    tm_x, tm_c = _tile(l, 512), _tile(lc, 256)
    tq_x, tq_c = _tile(l, ATTN_QUERY_LANES), _tile(lc, ATTN_QUERY_LANES // 2)
    tq2_x = _tile(l, ATTN_QUERY_LANES // 2)
    cs = ctx
    for i in range(depth):
        last = i == depth - 1
        j = i // 2
        mx = mod[i, :b].reshape(b, N_MOD, d)
        mc = mod[i, b].reshape(1, N_MOD, d)
        g = norm_g[i]
        if i % 2 == 0:
            lam_init = 0.8 - 0.6 * math.exp(-0.3 * i)
            proj_w, wo = _prep_ab_weights(a_w_in[j], a_q_norm[j], a_kv_norm[j], a_w_uq[j], a_w_ukv[j], ab_w_out[j])
            qx, kx, vx, dqx, dkx, dvx = _proj_ab(x, mx, g, proj_w, _rope_tables(l, True), tm_x)
            qc, kc, vc, dqc, dkc, dvc = _proj_ab(cs, mc, g, proj_w, _rope_tables(lc, False), tm_c)
            subln = b_subln[j].reshape(1, -1)
            oa = _attention(qx, [(kx, vx), (kc, vc)], tq_x)
            ob = _attention(dqx, [(dkx, dvx), (dkc, dvc)], tq2_x, "diff", b_lambda[j], subln, lam_init)
            ffn_w = (f_w1[j].astype(BF), f_w3[j].astype(BF), f_w2[j].astype(BF))
            x = _outproj([oa, ob], wo, None, x, mx, g, tm_x, "ffn", ffn_w)
            if not last:
                oa = _attention(qc, [(kc, vc)], tq_c)
                ob = _attention(dqc, [(dkc, dvc)], tq_c, "diff", b_lambda[j], subln, lam_init)
                cs = _outproj([oa, ob], wo, None, cs, mc, g, tm_c, "ffn", ffn_w)
        else:
            wqkv = c_w_qkv[j].astype(BF)
            bqkv = c_b_qkv[j].reshape(1, -1)
            wo = c_w_out[j].astype(BF)
            bo = c_b_out[j].reshape(1, -1)
            qx, kx, vx = _proj_na(x, mx, g, wqkv, bqkv, tm_x)
            qc, kc, vc = _proj_na(cs, mc, g, wqkv, bqkv, tm_c)
            o = _na_attention(qx, kx, vx, kc, vc, _na_bias_table(c_rpb[j]))
            r_f32 = _pad_cols(m_router[j], LANES)
            r_hi = r_f32.astype(BF)
            router = (jnp.stack([r_hi, (r_f32 - r_hi.astype(F32)).astype(BF)]),)
            w1, w3, w2 = m_w1[j].astype(BF), m_w3[j].astype(BF), m_w2[j].astype(BF)
            x, *routed = _outproj([o], wo, bo, x, mx, g, tm_x, "router", router)
            x = _moe(x, routed, mx, g, w1, w3, w2, tm_x)
            if not last:
                oc = _attention(qc, [(kc, jnp.swapaxes(vc, 2, 3))], tq_c, "pair")
                cs, *routed = _outproj([oc], wo, bo, cs, mc, g, tm_c, "router", router)
                cs = _moe(cs, routed, mc, g, w1, w3, w2, tm_c)
    return x
```

```python
import functools
import math

import jax
import jax.numpy as jnp
import numpy as np
from jax import lax
from jax.experimental import pallas as pl
from jax.experimental.pallas import tpu as pltpu
from jax.experimental.pallas import tpu_sc as plsc

BF = jnp.bfloat16
F32 = jnp.float32

LANES = 128
VMEM_LIMIT = 56 * 1024 * 1024

GRID_W = 64
EPS = 1e-6
ROPE_THETA = 10000.0
N_MOD = 6

MLA_HEADS = 8
MLA_NOPE = 64
MLA_ROPE = 32
MLA_V = 64
MLA_Q_RANK = 384
MLA_KV_RANK = 256
LOG2E = math.log2(math.e)
MLA_SCALE = (MLA_NOPE + MLA_ROPE) ** -0.5 * LOG2E

DIFF_HEADS = 4
DIFF_HD = 64
DIFF_SCALE = DIFF_HD ** -0.5 * LOG2E

NA_HEADS = 16
NA_HD = 64
NA_KH = 8
NA_KW = 16
NA_SCALE = NA_HD ** -0.5 * LOG2E
NA_SLOTS = NA_HEADS * NA_HD // LANES
MASK_VALUE = -1e30

N_EXPERTS = 8

_NT = (((1,), (1,)), ((), ()))


def _cparams(*sem):
    return pltpu.CompilerParams(dimension_semantics=sem, vmem_limit_bytes=VMEM_LIMIT)


def _dot(a, b):
    return jnp.dot(a, b, preferred_element_type=F32)


def _rms(x, g):
    return x * lax.rsqrt(jnp.mean(x * x, axis=-1, keepdims=True) + EPS) * g


def _norm_mod(x, g, shift, scale):
    return _rms(x, g) * (1 + scale) + shift


def _const_spec(shape):
    return pl.BlockSpec(shape, lambda *_: (0,) * len(shape))


def _mod_spec(mod):
    if mod.shape[0] == 1:
        return pl.BlockSpec((1,) + mod.shape[1:], lambda b, *_: (0, 0, 0))
    return pl.BlockSpec((1,) + mod.shape[1:], lambda b, *_: (b, 0, 0))


def _mod_kernel(c_ref, w_ref, b_ref, o_ref):
    c = c_ref[...]
    sc = c * jax.nn.sigmoid(c)
    o_ref[0] = _dot(sc.astype(BF), w_ref[0].astype(BF)) + b_ref[0]


def _modulation(cvec, w_mod, b_mod):
    depth, d, n = w_mod.shape
    rows = cvec.shape[0]
    return pl.pallas_call(
        _mod_kernel,
        grid=(depth, n // d),
        in_specs=[
            pl.BlockSpec((rows, d), lambda i, j: (0, 0)),
            pl.BlockSpec((1, d, d), lambda i, j: (i, 0, j)),
            pl.BlockSpec((1, 1, d), lambda i, j: (i, 0, j)),
        ],
        out_specs=pl.BlockSpec((1, rows, d), lambda i, j: (i, 0, j)),
        out_shape=jax.ShapeDtypeStruct((depth, rows, n), F32),
        compiler_params=_cparams("arbitrary", "arbitrary"),
        name="modulation",
    )(cvec, w_mod, b_mod.reshape(depth, 1, n))


_Z_CQ = 0
_Z_CKV = MLA_Q_RANK
_Z_KR = _Z_CKV + MLA_KV_RANK
_Z_DQ = _Z_KR + LANES
_DW = 2 * DIFF_HEADS * DIFF_HD
_Z_DK = _Z_DQ + _DW
_Z_DV = _Z_DK + _DW


def _rope(x, cos, sin_signed, quarter):
    lane = lax.broadcasted_iota(jnp.int32, x.shape, 1)
    first = (lane % (2 * quarter)) < quarter
    rx = jnp.where(first, pltpu.roll(x, LANES - quarter, 1), pltpu.roll(x, quarter, 1))
    return x * cos + rx * sin_signed


def _proj_ab_kernel(x_ref, mod_ref, g_ref, win_ref, qn_ref, kvn_ref, wuq_ref, wuk_ref, wuv_ref,
                    tab_ref, q_ref, k_ref, v_ref, dq_ref, dk_ref, dv_ref):
    h = _norm_mod(x_ref[0], g_ref[0:1, :], mod_ref[0, 0:1, :], mod_ref[0, 1:2, :])
    z = _dot(h.astype(BF), win_ref[...])
    cqn = _rms(z[:, _Z_CQ:_Z_CKV], qn_ref[...]).astype(BF)
    ckvn = _rms(z[:, _Z_CKV:_Z_KR], kvn_ref[...]).astype(BF)
    q2 = _dot(cqn, wuq_ref[...])
    cos_q, sin_q = tab_ref[0], tab_ref[1]
    nq = MLA_HEADS * LANES
    for hd in range(MLA_HEADS):
        lo = hd * LANES
        q_ref[0, hd] = (q2[:, lo:lo + LANES] * cos_q + q2[:, nq + lo:nq + lo + LANES] * sin_q).astype(BF)
    kr = _rope(z[:, _Z_KR:_Z_DQ], tab_ref[2], tab_ref[3], MLA_ROPE // 4)
    kk = _dot(ckvn, wuk_ref[...])
    vv = _dot(ckvn, wuv_ref[...])
    for hd in range(MLA_HEADS):
        lo = hd * LANES
        k_ref[0, hd] = (kk[:, lo:lo + LANES] + kr).astype(BF)
        v_ref[0, hd] = vv[:, lo:lo + LANES].T.astype(BF)
    cos_d, sin_d = tab_ref[4], tab_ref[5]
    for hd in range(DIFF_HEADS):
        lo = hd * LANES
        dq = _rope(z[:, _Z_DQ + lo:_Z_DQ + lo + LANES], cos_d, sin_d, DIFF_HD // 4)
        dq_ref[0, hd] = (dq * DIFF_SCALE).astype(BF)
        dk_ref[0, hd] = _rope(z[:, _Z_DK + lo:_Z_DK + lo + LANES], cos_d, sin_d, DIFF_HD // 4).astype(BF)
        dv_ref[0, hd] = z[:, _Z_DV + lo:_Z_DV + lo + LANES].T.astype(BF)


def _proj_ab(x, mod, g, wts, tabs, tm):
    b, t, d = x.shape
    win, qn, kvn, wuq, wuk, wuv = wts
    tok = lambda hh: pl.BlockSpec((1, hh, tm, LANES), lambda bi, i: (bi, 0, i, 0))
    shp = lambda hh: jax.ShapeDtypeStruct((b, hh, t, LANES), BF)
    tok_t = lambda hh, dv: pl.BlockSpec((1, hh, dv, tm), lambda bi, i: (bi, 0, 0, i))
    shp_t = lambda hh, dv: jax.ShapeDtypeStruct((b, hh, dv, t), BF)
    return pl.pallas_call(
        _proj_ab_kernel,
        grid=(b, t // tm),
        in_specs=[
            pl.BlockSpec((1, tm, d), lambda bi, i: (bi, i, 0)),
            _mod_spec(mod),
            _const_spec(g.shape),
            _const_spec(win.shape), _const_spec(qn.shape), _const_spec(kvn.shape),
            _const_spec(wuq.shape), _const_spec(wuk.shape), _const_spec(wuv.shape),
            pl.BlockSpec((6, tm, LANES), lambda bi, i: (0, i, 0)),
        ],
        out_specs=[tok(MLA_HEADS), tok(MLA_HEADS), tok_t(MLA_HEADS, LANES),
                   tok(DIFF_HEADS), tok(DIFF_HEADS), tok_t(DIFF_HEADS, LANES)],
        out_shape=[shp(MLA_HEADS), shp(MLA_HEADS), shp_t(MLA_HEADS, LANES),
                   shp(DIFF_HEADS), shp(DIFF_HEADS), shp_t(DIFF_HEADS, LANES)],
        compiler_params=_cparams("arbitrary", "arbitrary"),
        name="proj_ab",
    )(x, mod, g, win, qn, kvn, wuq, wuk, wuv, tabs)


KEY_CHUNK = 256


SUBLANES = 8
ATTN_QUERY_LANES = 1024


def _key_chunks(kv_refs):
    off = 0
    for k_ref, v_ref in kv_refs:
        lk = k_ref.shape[2]
        for c0 in range(0, lk, KEY_CHUNK):
            w = min(KEY_CHUNK, lk - c0)
            yield k_ref, v_ref, c0, w, off
            off += w


def _attn_kernel(*refs, nseg, mode, lam_init):
    q_ref = refs[0]
    kv_refs = [(refs[1 + 2 * s], refs[2 + 2 * s]) for s in range(nseg)]
    o_ref, s0_ref, s1_ref, mm0_ref, mm1_ref = refs[-5:]
    tq = q_ref.shape[2]
    width = s0_ref.shape[1]
    fold = lambda a: a.reshape(a.shape[0] // SUBLANES, SUBLANES, width)
    t = pl.program_id(0)

    @pl.when(t == 0)
    def _():
        s1_ref[...] = jnp.zeros_like(s1_ref)
        mm1_ref[...] = jnp.zeros_like(mm1_ref)

    def step(s_cur, mm_cur, s_prv, mm_prv):
        q = q_ref[0, 0]
        if mode != "single":
            lane = lax.broadcasted_iota(jnp.int32, q.shape, 1)
            zero = jnp.zeros_like(q)
            q = jnp.concatenate([jnp.where(lane < DIFF_HD, q, zero), jnp.where(lane >= DIFF_HD, q, zero)], axis=0)
        m_prv = jnp.max(mm_prv[...], axis=0, keepdims=True)
        mm = jnp.full((SUBLANES, width), -jnp.inf, F32)
        ll = jnp.zeros((SUBLANES, width), F32)
        o_t = jnp.zeros((LANES, width), F32)
        for k_ref, v_ref, c0, w, off in _key_chunks(kv_refs):
            s = lax.dot_general(k_ref[0, 0, c0:c0 + w, :], q, _NT, preferred_element_type=F32)
            s_cur[off:off + w, :] = s
            mm = jnp.maximum(mm, jnp.max(fold(s), axis=0))
            p = jnp.exp2(s_prv[off:off + w, :] - m_prv)
            ll = ll + jnp.sum(fold(p), axis=0)
            o_t = o_t + _dot(v_ref[0, 0, :, c0:c0 + w], p.astype(BF))
        mm_cur[...] = mm
        o_t = o_t / jnp.sum(ll, axis=0, keepdims=True)
        if mode == "single":
            o_ref[0, 0] = o_t.T.astype(BF)
        elif mode == "pair":
            o_ref[0, 0] = jnp.where(lane < DIFF_HD, o_t[:, :tq].T, o_t[:, tq:].T).astype(BF)
        else:
            lam_ref, subln_ref = refs[1 + 2 * nseg], refs[2 + 2 * nseg]
            lv = lam_ref[...]
            lam = (jnp.exp(jnp.sum(lv[0:1] * lv[1:2], axis=-1, keepdims=True))
                   - jnp.exp(jnp.sum(lv[2:3] * lv[3:4], axis=-1, keepdims=True)) + lam_init)
            d = o_t[:, :tq] - lam * o_t[:, tq:]
            dn = d * lax.rsqrt(jnp.mean(d * d, axis=0, keepdims=True) + EPS)
            o_ref[0, 0] = (dn.T * subln_ref[...] * (1 - lam_init)).astype(BF)

    pl.when(t % 2 == 0)(lambda: step(s0_ref, mm0_ref, s1_ref, mm1_ref))
    pl.when(t % 2 == 1)(lambda: step(s1_ref, mm1_ref, s0_ref, mm0_ref))


def _attention(q, kvs, tq, mode="single", lam_vecs=None, subln=None, lam_init=0.0):
    b, nh, lq, _ = q.shape
    diff = mode == "diff"
    nmap = 1 if mode == "single" else 2
    nq = lq // tq
    ntile = b * nh * nq

    def tile(t):
        t = jnp.clip(t, 0, ntile - 1)
        return t // (nh * nq), (t // nq) % nh, t % nq

    cur_head = lambda t: tile(t)[:2] + (0, 0)
    prv_head = lambda t: tile(t - 1)[:2] + (0, 0)
    in_specs = [pl.BlockSpec((1, 1, tq, LANES), lambda t: tile(t) + (0,))]
    args = [q]
    for k, v in kvs:
        lk = k.shape[2]
        in_specs += [pl.BlockSpec((1, 1, lk, LANES), cur_head), pl.BlockSpec((1, 1, v.shape[2], lk), prv_head)]
        args += [k, v]
    if diff:
        in_specs += [_const_spec(lam_vecs.shape), _const_spec(subln.shape)]
        args += [lam_vecs, subln]
    nkeys = sum(k.shape[2] for k, _ in kvs)
    return pl.pallas_call(
        functools.partial(_attn_kernel, nseg=len(kvs), mode=mode, lam_init=lam_init),
        grid=(ntile + 1,),
        in_specs=in_specs,
        out_specs=pl.BlockSpec((1, 1, tq, LANES), lambda t: tile(t - 1) + (0,)),
        out_shape=jax.ShapeDtypeStruct((b, nh, lq, LANES), BF),
        scratch_shapes=([pltpu.VMEM((nkeys, nmap * tq), F32)] * 2 + [pltpu.VMEM((SUBLANES, nmap * tq), F32)] * 2),
        compiler_params=_cparams("arbitrary"),
        name="attn_diff" if diff else "attn_mla",
    )(*args)


ROUTER_PIECES = 2


def _outproj_kernel(*refs, n_in, has_bias, tail):
    o_refs = refs[:n_in]
    w_ref = refs[n_in]
    pos = n_in + 1
    b_ref = None
    if has_bias:
        b_ref = refs[pos]
        pos += 1
    x_ref, mod_ref, g_ref = refs[pos:pos + 3]
    rest = refs[pos + 3:]
    tm = x_ref.shape[1]

    def residual(rows):
        o = jnp.concatenate([r[0, h, rows, :] for r in o_refs for h in range(r.shape[1])], axis=-1)
        y = _dot(o, w_ref[...])
        if has_bias:
            y = y + b_ref[...]
        x1 = x_ref[0, rows, :] + mod_ref[0, 2:3, :] * _rms(y, g_ref[1:2, :])
        return x1, _norm_mod(x1, g_ref[2:3, :], mod_ref[0, 3:4, :], mod_ref[0, 4:5, :])

    if tail == "ffn":
        w1_ref, w3_ref, w2_ref, out_ref = rest
        x1, h = residual(slice(0, tm))
        out_ref[0] = x1 + mod_ref[0, 5:6, :] * _rms(_swiglu(h.astype(BF), w1_ref, w3_ref, w2_ref), g_ref[3:4, :])
    else:
        router_ref, tri_ref, out_ref = rest[:3]
        step = tm // ROUTER_PIECES
        halves = [slice(r0, r0 + step) for r0 in range(0, tm, step)]
        parts = [residual(rows) for rows in halves]
        for rows, (x1, _) in zip(halves, parts):
            out_ref[0, rows, :] = x1
        first = (pl.program_id(0) == 0) & (pl.program_id(1) == 0)
        _route([h for _, h in parts], halves, first, router_ref, tri_ref, *rest[3:])


def _outproj(os_, w, bias, x, mod, g, tm, tail, tail_args):
    b, t, d = x.shape
    n = b * t
    nt = t // tm
    in_specs = [pl.BlockSpec((1, o.shape[1], tm, LANES), lambda bi, i: (bi, 0, i, 0)) for o in os_]
    in_specs.append(_const_spec(w.shape))
    args = list(os_) + [w]
    if bias is not None:
        in_specs.append(_const_spec(bias.shape))
        args.append(bias)
    x_spec = pl.BlockSpec((1, tm, d), lambda bi, i: (bi, i, 0))
    in_specs += [x_spec, _mod_spec(mod), _const_spec(g.shape)]
    args += [x, mod, g]
    out_specs, out_shape, scratch = x_spec, jax.ShapeDtypeStruct((b, t, d), F32), []
    if tail == "ffn":
        in_specs += [_resident_spec(a.shape, lambda bi, i: (0, 0)) for a in tail_args]
        args += list(tail_args)
    else:
        router, = tail_args
        tri = jnp.asarray(np.tril(np.ones((tm, tm), np.float32), -1), BF)
        in_specs += [_const_spec(router.shape), _const_spec(tri.shape)]
        args += [router, tri]
        rows = lambda wd: pl.BlockSpec((tm, wd), lambda bi, i: (bi * nt + i, 0))
        out_specs = [x_spec, pl.BlockSpec((PACK_CHUNKS, tm, LANES), lambda bi, i: (0, bi * nt + i, 0)),
                     rows(ROUTE_COLS), rows(ROUTE_COLS), pl.BlockSpec((1, LANES), lambda bi, i: (0, 0))]
        out_shape = [out_shape, jax.ShapeDtypeStruct((PACK_CHUNKS, n, LANES), jnp.int32),
                     jax.ShapeDtypeStruct((n, ROUTE_COLS), jnp.int32), jax.ShapeDtypeStruct((n, ROUTE_COLS), F32),
                     jax.ShapeDtypeStruct((1, LANES), F32)]
        scratch = [pltpu.VMEM((1, LANES), F32)]
    return pl.pallas_call(
        functools.partial(_outproj_kernel, n_in=len(os_), has_bias=bias is not None, tail=tail),
        grid=(b, nt),
        in_specs=in_specs,
        out_specs=out_specs,
        out_shape=out_shape,
        scratch_shapes=scratch,
        compiler_params=_cparams("arbitrary", "arbitrary"),
        name="outproj_" + tail,
    )(*args)


MXU_TILE = 256


def _ff_chunks(f):
    tiles = f // MXU_TILE
    if f % MXU_TILE or tiles < 2:
        return [(0, f)]
    first = (tiles + 1) // 2 * MXU_TILE
    return [(0, first), (first, f - first)]


def _swiglu(hn, w1_ref, w3_ref, w2_ref):
    y = None
    for lo, n in _ff_chunks(w1_ref.shape[1]):
        a = _dot(hn, w1_ref[:, lo:lo + n])
        act = (a * jax.nn.sigmoid(a) * _dot(hn, w3_ref[:, lo:lo + n])).astype(BF)
        part = _dot(act, w2_ref[lo:lo + n, :])
        y = part if y is None else y + part
    return y


def _resident_spec(shape, index_map):
    return pl.BlockSpec(shape, index_map, pipeline_mode=pl.Buffered(1))


MOE_ROW_TILE = 512
PACK_CHUNKS = 4
SC_WINDOW = 128
SC_SUBCORES = 32
ROUTE_COLS = 8
_HI_MASK = -65536
_LO_MASK = 65535


def _pack_rows(v, out_ref, rows=slice(None)):
    half = v.shape[1] // 2
    vb = v.astype(BF).astype(F32)
    lo = (pltpu.bitcast(vb[:, :half], jnp.int32) >> 16) & _LO_MASK
    hi = pltpu.bitcast(vb[:, half:], jnp.int32) & _HI_MASK
    w = lo | hi
    for c in range(PACK_CHUNKS):
        out_ref[c, rows, :] = w[:, c * LANES:(c + 1) * LANES]


def _unpack_rows(ref):
    w = jnp.concatenate([ref[c] for c in range(PACK_CHUNKS)], axis=-1)
    lo = pltpu.bitcast(w << 16, F32)
    hi = pltpu.bitcast(w & _HI_MASK, F32)
    return jnp.concatenate([lo, hi], axis=-1)


def _route(hs, row_slices, first, router_ref, tri_ref, hp_ref, meta_ref, gate_ref, cnt_ref, carry_ref):
    @pl.when(first)
    def _():
        carry_ref[...] = jnp.zeros_like(carry_ref)

    pieces = range(len(hs))
    for h, rows in zip(hs, row_slices):
        _pack_rows(h, hp_ref, rows)
    his = [h.astype(BF) for h in hs]
    los = [(hs[p] - his[p].astype(F32)).astype(BF) for p in pieces]
    raw = [_dot(his[p], router_ref[0]) + (_dot(his[p], router_ref[1]) + _dot(los[p], router_ref[0])) for p in pieces]
    lanes = [lax.broadcasted_iota(jnp.int32, r.shape, 1) for r in raw]
    lgs = [jnp.where(lanes[p] < N_EXPERTS, raw[p], -jnp.inf) for p in pieces]
    m1s = [jnp.max(lg, axis=-1, keepdims=True) for lg in lgs]
    i1s = [jnp.min(jnp.where(lgs[p] == m1s[p], lanes[p], LANES), axis=-1, keepdims=True) for p in pieces]
    rests = [jnp.where(lanes[p] == i1s[p], -jnp.inf, lgs[p]) for p in pieces]
    m2s = [jnp.max(r, axis=-1, keepdims=True) for r in rests]
    i2s = [jnp.min(jnp.where(rests[p] == m2s[p], lanes[p], LANES), axis=-1, keepdims=True) for p in pieces]
    rows_cat = lambda parts: jnp.concatenate(parts, axis=0)
    m1, m2, i1, i2, lane = rows_cat(m1s), rows_cat(m2s), rows_cat(i1s), rows_cat(i2s), rows_cat(lanes)
    e2 = jnp.exp(m2 - m1)
    denom = 1.0 + e2
    assigned = jnp.where((lane == i1) | (lane == i2), 1.0, 0.0)
    ranks = _dot(tri_ref[...], assigned.astype(BF)) + carry_ref[...]
    r1 = jnp.sum(jnp.where(lane == i1, ranks, 0.0), axis=-1, keepdims=True).astype(jnp.int32)
    r2 = jnp.sum(jnp.where(lane == i2, ranks, 0.0), axis=-1, keepdims=True).astype(jnp.int32)
    carry_ref[...] += jnp.sum(assigned, axis=0, keepdims=True)
    cnt_ref[...] = carry_ref[...]
    col = lax.broadcasted_iota(jnp.int32, meta_ref.shape, 1)
    meta_ref[...] = jnp.where(col == 0, i1, jnp.where(col == 1, i2, jnp.where(col == 2, r1, r2)))
    gate_ref[...] = jnp.where(col == 0, 1.0 / denom, e2 / denom)


def _sc_mesh():
    return plsc.VectorSubcoreMesh(core_axis_name="core", subcore_axis_name="subcore")


def _sc_scatter_rows(rows, idx_a, idx_b, n_out):
    nrows = rows.shape[0]

    @pl.kernel(out_type=jax.ShapeDtypeStruct((n_out, LANES), rows.dtype), mesh=_sc_mesh(), scratch_types=[])
    def scatter_kernel(x_hbm, ia_hbm, ib_hbm, o_hbm):
        def body(x_vmem, ia_vmem, ib_vmem):
            pltpu.sync_copy(x_vmem, o_hbm.at[ia_vmem.at[0]])
            pltpu.sync_copy(x_vmem, o_hbm.at[ib_vmem.at[0]])

        pltpu.emit_pipeline(
            body,
            grid=(nrows // SC_WINDOW,),
            in_specs=[pl.BlockSpec((SC_WINDOW, LANES), lambda i: (i, 0)),
                      pl.BlockSpec((1, SC_WINDOW), lambda i: (0, i)),
                      pl.BlockSpec((1, SC_WINDOW), lambda i: (0, i))],
            out_specs=[],
            core_axis_name=("core", "subcore"),
            dimension_semantics=(pltpu.PARALLEL,),
        )(x_hbm, ia_hbm, ib_hbm)

    return scatter_kernel(rows, idx_a, idx_b)


def _sc_gather_rows(table, idx_a, idx_b):
    nrows = idx_a.shape[1]
    out = jax.ShapeDtypeStruct((nrows, LANES), table.dtype)

    @pl.kernel(out_type=(out, out), mesh=_sc_mesh(), scratch_types=[])
    def gather_kernel(t_hbm, ia_hbm, ib_hbm, oa_hbm, ob_hbm):
        def body(ia_vmem, ib_vmem, oa_vmem, ob_vmem):
            pltpu.sync_copy(t_hbm.at[ia_vmem.at[0]], oa_vmem)
            pltpu.sync_copy(t_hbm.at[ib_vmem.at[0]], ob_vmem)

        pltpu.emit_pipeline(
            body,
            grid=(nrows // SC_WINDOW,),
            in_specs=[pl.BlockSpec((1, SC_WINDOW), lambda i: (0, i)),
                      pl.BlockSpec((1, SC_WINDOW), lambda i: (0, i))],
            out_specs=[pl.BlockSpec((SC_WINDOW, LANES), lambda i: (i, 0)),
                       pl.BlockSpec((SC_WINDOW, LANES), lambda i: (i, 0))],
            core_axis_name=("core", "subcore"),
            dimension_semantics=(pltpu.PARALLEL,),
        )(ia_hbm, ib_hbm, oa_hbm, ob_hbm)

    return gather_kernel(table, idx_a, idx_b)


def _experts_kernel(te_ref, tv_ref, xs_ref, w1_ref, w3_ref, w2_ref, ys_ref):
    del te_ref
    valid = tv_ref[pl.program_id(0)]

    @pl.when(valid > 0)
    def _():
        row = lax.broadcasted_iota(jnp.int32, (xs_ref.shape[1], 1), 0)
        hn = jnp.where(row < valid, _unpack_rows(xs_ref), 0.0).astype(BF)
        _pack_rows(_swiglu(hn, w1_ref.at[0], w3_ref.at[0], w2_ref.at[0]), ys_ref)

    @pl.when(valid == 0)
    def _():
        ys_ref[...] = jnp.zeros_like(ys_ref)


def _experts(xs, tile_expert, tile_valid, w1, w3, w2):
    _, p, _ = xs.shape
    tr = MOE_ROW_TILE
    rows = pl.BlockSpec((PACK_CHUNKS, tr, LANES), lambda j, te, tv: (0, j, 0))
    weights = lambda w: pl.BlockSpec((1,) + w.shape[1:], lambda j, te, tv: (te[j], 0, 0))
    return pl.pallas_call(
        _experts_kernel,
        grid_spec=pltpu.PrefetchScalarGridSpec(
            num_scalar_prefetch=2,
            grid=(p // tr,),
            in_specs=[rows, weights(w1), weights(w3), weights(w2)],
            out_specs=rows,
        ),
        out_shape=jax.ShapeDtypeStruct(xs.shape, jnp.int32),
        compiler_params=_cparams("arbitrary"),
        name="moe_experts",
    )(tile_expert, tile_valid, xs, w1, w3, w2)


def _combine_kernel(x_ref, ya_ref, yb_ref, gate_ref, mod_ref, g_ref, out_ref):
    gates = gate_ref[...]
    fx = gates[:, 0:1] * _unpack_rows(ya_ref) + gates[:, 1:2] * _unpack_rows(yb_ref)
    out_ref[0] = x_ref[0] + mod_ref[0, 5:6, :] * _rms(fx, g_ref[3:4, :])


def _combine(x, ya, yb, gates, mod, g, tm):
    b, t, d = x.shape
    nt = t // tm
    packed = pl.BlockSpec((PACK_CHUNKS, tm, LANES), lambda bi, i: (0, bi * nt + i, 0))
    return pl.pallas_call(
        _combine_kernel,
        grid=(b, nt),
        in_specs=[
            pl.BlockSpec((1, tm, d), lambda bi, i: (bi, i, 0)),
            packed, packed,
            pl.BlockSpec((tm, gates.shape[1]), lambda bi, i: (bi * nt + i, 0)),
            _mod_spec(mod),
            _const_spec(g.shape),
        ],
        out_specs=pl.BlockSpec((1, tm, d), lambda bi, i: (bi, i, 0)),
        out_shape=jax.ShapeDtypeStruct((b, t, d), F32),
        compiler_params=_cparams("arbitrary", "arbitrary"),
        name="moe_combine",
    )(x, ya, yb, gates, mod, g)


def _moe(x, routed, mod, g, w1, w3, w2, tm):
    b, t, d = x.shape
    n = b * t
    ne = w1.shape[0]
    tr = MOE_ROW_TILE
    assert d == 2 * PACK_CHUNKS * LANES and (PACK_CHUNKS * n) % (SC_WINDOW * SC_SUBCORES) == 0
    hp, meta, gates, counts = routed

    ntile = 2 * n // tr + ne
    p = ntile * tr
    cnt = counts[0, :ne].astype(jnp.int32)
    tiles = (cnt + tr - 1) // tr
    tile_end = jnp.cumsum(tiles)
    tile_start = tile_end - tiles
    base = tile_start * tr
    eids = jnp.arange(ne, dtype=jnp.int32)
    base_of = lambda e: jnp.sum(jnp.where(e[:, None] == eids[None, :], base[None, :], 0), axis=-1)
    pos_a = base_of(meta[:, 0]) + meta[:, 2]
    pos_b = base_of(meta[:, 1]) + meta[:, 3]
    chunk = jnp.arange(PACK_CHUNKS, dtype=jnp.int32)[:, None] * p
    idx_a = (chunk + pos_a[None, :]).reshape(1, PACK_CHUNKS * n)
    idx_b = (chunk + pos_b[None, :]).reshape(1, PACK_CHUNKS * n)
    tj = jnp.arange(ntile, dtype=jnp.int32)
    tile_expert = jnp.minimum(jnp.sum(tj[:, None] >= tile_end[None, :], axis=-1), ne - 1).astype(jnp.int32)
    done = jnp.sum(jnp.where(tile_expert[:, None] == eids[None, :], tile_start[None, :], 0), axis=-1)
    left = jnp.sum(jnp.where(tile_expert[:, None] == eids[None, :], cnt[None, :], 0), axis=-1) - (tj - done) * tr
    tile_valid = jnp.where(tj < tile_end[-1], jnp.clip(left, 0, tr), 0).astype(jnp.int32)

    xs = _sc_scatter_rows(hp.reshape(PACK_CHUNKS * n, LANES), idx_a, idx_b, PACK_CHUNKS * p)
    ys = _experts(xs.reshape(PACK_CHUNKS, p, LANES), tile_expert, tile_valid, w1, w3, w2)
    ya, yb = _sc_gather_rows(ys.reshape(PACK_CHUNKS * p, LANES), idx_a, idx_b)
    shp = (PACK_CHUNKS, n, LANES)
    return _combine(x, ya.reshape(shp), yb.reshape(shp), gates, mod, g, tm)


def _proj_na_kernel(x_ref, mod_ref, g_ref, w_ref, b_ref, q_ref, k_ref, v_ref):
    h = _norm_mod(x_ref[0], g_ref[0:1, :], mod_ref[0, 0:1, :], mod_ref[0, 1:2, :])
    z = _dot(h.astype(BF), w_ref[...]) + b_ref[...]
    n = NA_SLOTS * LANES
    for s in range(NA_SLOTS):
        lo = s * LANES
        q_ref[0, s] = (z[:, lo:lo + LANES] * NA_SCALE).astype(BF)
        k_ref[0, s] = z[:, n + lo:n + lo + LANES].astype(BF)
        v_ref[0, s] = z[:, 2 * n + lo:2 * n + lo + LANES].astype(BF)


def _proj_na(x, mod, g, w, bias, tm):
    b, t, d = x.shape
    tok = pl.BlockSpec((1, NA_SLOTS, tm, LANES), lambda bi, i: (bi, 0, i, 0))
    shp = jax.ShapeDtypeStruct((b, NA_SLOTS, t, LANES), BF)
    return pl.pallas_call(
        _proj_na_kernel,
        grid=(b, t // tm),
        in_specs=[
            pl.BlockSpec((1, tm, d), lambda bi, i: (bi, i, 0)),
            _mod_spec(mod),
            _const_spec(g.shape),
            _const_spec(w.shape),
            _const_spec(bias.shape),
        ],
        out_specs=[tok, tok, tok],
        out_shape=[shp, shp, shp],
        compiler_params=_cparams("arbitrary", "arbitrary"),
        name="proj_na",
    )(x, mod, g, w, bias)


NA_ROWS_PER_STEP = 2
NA_GROUP = 4


def _na_window_start(r, rows):
    return jnp.clip(r - NA_KH // 2, 0, rows - NA_KH)


def _na_kernel(q_ref, kx_ref, vx_ref, kc_ref, vc_ref, *rest, rows):
    bias_refs, o_ref = rest[:NA_ROWS_PER_STEP], rest[NA_ROWS_PER_STEP]
    lane = lax.broadcasted_iota(jnp.int32, (GRID_W, LANES), 1)
    problems = [(a, s) for a in range(NA_ROWS_PER_STEP) for s in range(NA_SLOTS)]
    starts = [pl.multiple_of(_na_window_start(pl.program_id(1) * NA_ROWS_PER_STEP + a, rows) * GRID_W, GRID_W)
              for a in range(NA_ROWS_PER_STEP)]
    for g0 in range(0, len(problems), NA_GROUP):
        _na_group(problems[g0:g0 + NA_GROUP], starts, lane, q_ref, kx_ref, vx_ref, kc_ref, vc_ref, bias_refs, o_ref)


def _na_group(work, starts, lane, q_ref, kx_ref, vx_ref, kc_ref, vc_ref, bias_refs, o_ref):
    nwin = NA_KH * GRID_W
    idx = range(len(work))
    sws, scs = [], []
    for a, s in work:
        q = q_ref[0, s, a * GRID_W:(a + 1) * GRID_W, :]
        zero = jnp.zeros_like(q)
        q2 = jnp.concatenate([jnp.where(lane < NA_HD, q, zero), jnp.where(lane >= NA_HD, q, zero)], axis=0)
        kw = kx_ref[0, s, pl.ds(starts[a], nwin), :]
        sws.append(lax.dot_general(q2, kw, _NT, preferred_element_type=F32) + bias_refs[a][0, s])
        scs.append(lax.dot_general(q2, kc_ref[0, s], _NT, preferred_element_type=F32))
    ms = [jnp.maximum(jnp.max(sws[i], axis=-1, keepdims=True), jnp.max(scs[i], axis=-1, keepdims=True))
          for i in idx]
    pws = [jnp.exp2(sws[i] - ms[i]) for i in idx]
    pcs = [jnp.exp2(scs[i] - ms[i]) for i in idx]
    ls = [jnp.sum(pws[i], axis=-1, keepdims=True) + jnp.sum(pcs[i], axis=-1, keepdims=True) for i in idx]
    for i, (a, s) in enumerate(work):
        vw = vx_ref[0, s, pl.ds(starts[a], nwin), :]
        o2 = (_dot(pws[i].astype(BF), vw) + _dot(pcs[i].astype(BF), vc_ref[0, s])) / ls[i]
        o_ref[0, s, a * GRID_W:(a + 1) * GRID_W, :] = jnp.where(lane < NA_HD, o2[:GRID_W], o2[GRID_W:]).astype(BF)


def _na_attention(q, kx, vx, kc, vc, bias):
    b, ns, l, _ = q.shape
    rows = l // GRID_W
    lc = kc.shape[2]
    nr = NA_ROWS_PER_STEP
    assert rows % nr == 0

    def bias_spec(a):
        def index_map(bi, i):
            r = i * nr + a
            return (r - _na_window_start(r, rows), 0, 0, 0)
        return pl.BlockSpec((1,) + bias.shape[1:], index_map)

    full = lambda n: pl.BlockSpec((1, ns, n, LANES), lambda bi, i: (bi, 0, 0, 0))
    row = pl.BlockSpec((1, ns, nr * GRID_W, LANES), lambda bi, i: (bi, 0, i, 0))
    return pl.pallas_call(
        functools.partial(_na_kernel, rows=rows),
        grid=(b, rows // nr),
        in_specs=[row, full(l), full(l), full(lc), full(lc)] + [bias_spec(a) for a in range(nr)],
        out_specs=row,
        out_shape=jax.ShapeDtypeStruct((b, ns, l, LANES), BF),
        compiler_params=_cparams("arbitrary", "arbitrary"),
        name="na_attn",
    )(q, kx, vx, kc, vc, *([bias] * nr))


def _axis_tables(pos, dim):
    inv = ROPE_THETA ** (-jnp.arange(0, dim, 2, dtype=F32) / dim)
    ang = pos.astype(F32)[:, None] * inv[None, :]
    ang = jnp.concatenate([ang, ang], axis=-1)
    return jnp.cos(ang), jnp.sin(ang)


def _rope_tables(n, rope):
    if rope:
        t = jnp.arange(n, dtype=jnp.int32)
        row, col = t // GRID_W, t % GRID_W

        def cs(d):
            cr, sr = _axis_tables(row, d // 2)
            cc, sc = _axis_tables(col, d // 2)
            sign = np.where(np.arange(d) % (d // 2) < d // 4, -1.0, 1.0).astype(np.float32)
            return jnp.concatenate([cr, cc], axis=-1), jnp.concatenate([sr, sc], axis=-1) * sign

        c32, s32 = cs(MLA_ROPE)
        c64, s64 = cs(DIFF_HD)
    else:
        c32, s32 = jnp.ones((n, MLA_ROPE), F32), jnp.zeros((n, MLA_ROPE), F32)
        c64, s64 = jnp.ones((n, DIFF_HD), F32), jnp.zeros((n, DIFF_HD), F32)
    ones = jnp.ones((n, MLA_NOPE), F32)
    pad = lambda a: jnp.pad(a, ((0, 0), (0, LANES - a.shape[1])))
    cos_q = MLA_SCALE * pad(jnp.concatenate([ones, c32], axis=-1))
    sin_q = MLA_SCALE * pad(jnp.concatenate([jnp.zeros_like(ones), s32], axis=-1))
    at_rope_lanes = lambda a: jnp.pad(a, ((0, 0), (MLA_NOPE, LANES - MLA_NOPE - MLA_ROPE)))
    return jnp.stack([cos_q, sin_q, at_rope_lanes(c32), at_rope_lanes(s32),
                      jnp.concatenate([c64, c64], axis=-1), jnp.concatenate([s64, s64], axis=-1)])


def _pad_cols(w, n):
    return jnp.pad(w, ((0, 0), (0, n - w.shape[1])))


def _prep_ab_weights(w_in, q_norm, kv_norm, w_uq, w_ukv, w_out):
    o_cq, o_ckv, o_kr = MLA_Q_RANK, MLA_Q_RANK + MLA_KV_RANK, MLA_Q_RANK + MLA_KV_RANK + MLA_ROPE
    o_dq, o_dk = o_kr + _DW, o_kr + 2 * _DW
    w_kr, w_dq, w_dk, w_dv = w_in[:, o_ckv:o_kr], w_in[:, o_kr:o_dq], w_in[:, o_dq:o_dk], w_in[:, o_dk:]
    w_kr = jnp.pad(w_kr, ((0, 0), (MLA_NOPE, LANES - MLA_NOPE - MLA_ROPE)))
    win = jnp.concatenate([w_in[:, :o_ckv], w_kr, w_dq, w_dk, w_dv], axis=1).astype(BF)
    rq = w_uq.shape[0]
    uq = w_uq.reshape(rq, MLA_HEADS, MLA_NOPE + MLA_ROPE)
    rope = uq[..., MLA_NOPE:].reshape(rq, MLA_HEADS, 2, 2, MLA_ROPE // 4)
    swapped = jnp.concatenate([rope[..., 1:2, :], rope[..., 0:1, :]], axis=-2).reshape(rq, MLA_HEADS, MLA_ROPE)
    partner = jnp.pad(swapped, ((0, 0), (0, 0), (MLA_NOPE, LANES - MLA_NOPE - MLA_ROPE)))
    uq = jnp.pad(uq, ((0, 0), (0, 0), (0, LANES - MLA_NOPE - MLA_ROPE)))
    wuq = jnp.concatenate([uq.reshape(rq, -1), partner.reshape(rq, -1)], axis=1).astype(BF)
    rkv = w_ukv.shape[0]
    ukv = w_ukv.reshape(rkv, MLA_HEADS, MLA_NOPE + MLA_V)
    slot = lambda a: jnp.pad(a, ((0, 0), (0, 0), (0, LANES - a.shape[-1]))).reshape(rkv, MLA_HEADS * LANES)
    wuk = slot(ukv[..., :MLA_NOPE]).astype(BF)
    wuv = slot(ukv[..., MLA_NOPE:]).astype(BF)
    d = w_out.shape[1]
    wo_a = w_out[:MLA_HEADS * MLA_V].reshape(MLA_HEADS, MLA_V, d)
    wo_a = jnp.pad(wo_a, ((0, 0), (0, LANES - MLA_V), (0, 0))).reshape(MLA_HEADS * LANES, d)
    wo = jnp.concatenate([wo_a, w_out[MLA_HEADS * MLA_V:]], axis=0).astype(BF)
    proj = (win, q_norm.reshape(1, -1), kv_norm.reshape(1, -1), wuq, wuk, wuv)
    return proj, wo


def _na_bias_table(rpb):
    nh = rpb.shape[0]
    cols = np.arange(GRID_W)
    col_start = np.clip(cols - NA_KW // 2, 0, GRID_W - NA_KW)
    kcol = np.arange(GRID_W)
    valid = (kcol[None, :] >= col_start[:, None]) & (kcol[None, :] < col_start[:, None] + NA_KW)
    dc = kcol[None, :] - cols[:, None] + (NA_KW - 1)
    onehot = (dc[None] == np.arange(2 * NA_KW - 1)[:, None, None]) & valid[None]
    tz = jnp.einsum('hrd,dck->hrck', rpb * LOG2E, jnp.asarray(onehot, F32), precision=lax.Precision.HIGHEST)
    tz = jnp.where(valid[None, None], tz, MASK_VALUE)
    cases = [tz[:, NA_KH - 1 - c:2 * NA_KH - 1 - c] for c in range(NA_KH)]
    tbl = jnp.transpose(jnp.stack(cases), (0, 1, 3, 2, 4))
    return tbl.reshape(NA_KH, nh // 2, 2 * GRID_W, NA_KH * GRID_W)


def _tile(n, pref):
    return pref if n % pref == 0 else n


def kernel(x, c, ctx, c_ctx, w_mod, b_mod, norm_g, a_w_in, a_q_norm, a_kv_norm, a_w_uq, a_w_ukv, b_lambda, b_subln,
           ab_w_out, f_w1, f_w3, f_w2, c_w_qkv, c_b_qkv, c_rpb, c_w_out, c_b_out, m_router, m_w1, m_w3, m_w2):
    b, l, d = x.shape
    lc = ctx.shape[1]
    depth = w_mod.shape[0]
    assert l % GRID_W == 0 and l // GRID_W >= NA_KH

    mod_rows = 16
    cvec = jnp.zeros((mod_rows, d), F32).at[:b].set(c).at[b].set(c_ctx)
    mod = _modulation(cvec, w_mod, b_mod)

    tm_x, tm_c = _tile(l, 512), _tile(lc, 256)
    tq_x, tq_c = _tile(l, ATTN_QUERY_LANES), _tile(lc, ATTN_QUERY_LANES // 2)
    tq2_x = _tile(l, ATTN_QUERY_LANES // 2)
    cs = ctx
    for i in range(depth):
        last = i == depth - 1
        j = i // 2
        mx = mod[i, :b].reshape(b, N_MOD, d)
        mc = mod[i, b].reshape(1, N_MOD, d)
        g = norm_g[i]
        if i % 2 == 0:
            lam_init = 0.8 - 0.6 * math.exp(-0.3 * i)
            proj_w, wo = _prep_ab_weights(a_w_in[j], a_q_norm[j], a_kv_norm[j], a_w_uq[j], a_w_ukv[j], ab_w_out[j])
            qx, kx, vx, dqx, dkx, dvx = _proj_ab(x, mx, g, proj_w, _rope_tables(l, True), tm_x)
            qc, kc, vc, dqc, dkc, dvc = _proj_ab(cs, mc, g, proj_w, _rope_tables(lc, False), tm_c)
            subln = b_subln[j].reshape(1, -1)
            oa = _attention(qx, [(kx, vx), (kc, vc)], tq_x)
            ob = _attention(dqx, [(dkx, dvx), (dkc, dvc)], tq2_x, "diff", b_lambda[j], subln, lam_init)
            ffn_w = (f_w1[j].astype(BF), f_w3[j].astype(BF), f_w2[j].astype(BF))
            x = _outproj([oa, ob], wo, None, x, mx, g, tm_x, "ffn", ffn_w)
            if not last:
                oa = _attention(qc, [(kc, vc)], tq_c)
                ob = _attention(dqc, [(dkc, dvc)], tq_c, "diff", b_lambda[j], subln, lam_init)
                cs = _outproj([oa, ob], wo, None, cs, mc, g, tm_c, "ffn", ffn_w)
        else:
            wqkv = c_w_qkv[j].astype(BF)
            bqkv = c_b_qkv[j].reshape(1, -1)
            wo = c_w_out[j].astype(BF)
            bo = c_b_out[j].reshape(1, -1)
            qx, kx, vx = _proj_na(x, mx, g, wqkv, bqkv, tm_x)
            qc, kc, vc = _proj_na(cs, mc, g, wqkv, bqkv, tm_c)
            o = _na_attention(qx, kx, vx, kc, vc, _na_bias_table(c_rpb[j]))
            r_f32 = _pad_cols(m_router[j], LANES)
            r_hi = r_f32.astype(BF)
            router = (jnp.stack([r_hi, (r_f32 - r_hi.astype(F32)).astype(BF)]),)
            w1, w3, w2 = m_w1[j].astype(BF), m_w3[j].astype(BF), m_w2[j].astype(BF)
            x, *routed = _outproj([o], wo, bo, x, mx, g, tm_x, "router", router)
            x = _moe(x, routed, mx, g, w1, w3, w2, tm_x)
            if not last:
                oc = _attention(qc, [(kc, jnp.swapaxes(vc, 2, 3))], tq_c, "pair")
                cs, *routed = _outproj([oc], wo, bo, cs, mc, g, tm_c, "router", router)
                cs = _moe(cs, routed, mc, g, w1, w3, w2, tm_c)
    return x
```

```python
import functools
import math

import jax
import jax.numpy as jnp
import numpy as np
from jax import lax
from jax.experimental import pallas as pl
from jax.experimental.pallas import tpu as pltpu
from jax.experimental.pallas import tpu_sc as plsc

BF = jnp.bfloat16
F32 = jnp.float32

LANES = 128
VMEM_LIMIT = 56 * 1024 * 1024

GRID_W = 64
EPS = 1e-6
ROPE_THETA = 10000.0
N_MOD = 6

MLA_HEADS = 8
MLA_NOPE = 64
MLA_ROPE = 32
MLA_V = 64
MLA_Q_RANK = 384
MLA_KV_RANK = 256
LOG2E = math.log2(math.e)
MLA_SCALE = (MLA_NOPE + MLA_ROPE) ** -0.5 * LOG2E

DIFF_HEADS = 4
DIFF_HD = 64
DIFF_SCALE = DIFF_HD ** -0.5 * LOG2E

NA_HEADS = 16
NA_HD = 64
NA_KH = 8
NA_KW = 16
NA_SCALE = NA_HD ** -0.5 * LOG2E
NA_SLOTS = NA_HEADS * NA_HD // LANES
MASK_VALUE = -1e30

N_EXPERTS = 8

_NT = (((1,), (1,)), ((), ()))


def _cparams(*sem):
    return pltpu.CompilerParams(dimension_semantics=sem, vmem_limit_bytes=VMEM_LIMIT)


def _dot(a, b):
    return jnp.dot(a, b, preferred_element_type=F32)


def _rms(x, g):
    return x * lax.rsqrt(jnp.mean(x * x, axis=-1, keepdims=True) + EPS) * g


def _norm_mod(x, g, shift, scale):
    return _rms(x, g) * (1 + scale) + shift


def _const_spec(shape):
    return pl.BlockSpec(shape, lambda *_: (0,) * len(shape))


def _mod_spec(mod):
    if mod.shape[0] == 1:
        return pl.BlockSpec((1,) + mod.shape[1:], lambda b, *_: (0, 0, 0))
    return pl.BlockSpec((1,) + mod.shape[1:], lambda b, *_: (b, 0, 0))


def _mod_kernel(c_ref, w_ref, b_ref, o_ref):
    c = c_ref[...]
    sc = c * jax.nn.sigmoid(c)
    o_ref[0] = _dot(sc.astype(BF), w_ref[0].astype(BF)) + b_ref[0]


def _modulation(cvec, w_mod, b_mod):
    depth, d, n = w_mod.shape
    rows = cvec.shape[0]
    return pl.pallas_call(
        _mod_kernel,
        grid=(depth, n // d),
        in_specs=[
            pl.BlockSpec((rows, d), lambda i, j: (0, 0)),
            pl.BlockSpec((1, d, d), lambda i, j: (i, 0, j)),
            pl.BlockSpec((1, 1, d), lambda i, j: (i, 0, j)),
        ],
        out_specs=pl.BlockSpec((1, rows, d), lambda i, j: (i, 0, j)),
        out_shape=jax.ShapeDtypeStruct((depth, rows, n), F32),
        compiler_params=_cparams("arbitrary", "arbitrary"),
        name="modulation",
    )(cvec, w_mod, b_mod.reshape(depth, 1, n))


_Z_CQ = 0
_Z_CKV = MLA_Q_RANK
_Z_KR = _Z_CKV + MLA_KV_RANK
_Z_DQ = _Z_KR + LANES
_DW = 2 * DIFF_HEADS * DIFF_HD
_Z_DK = _Z_DQ + _DW
_Z_DV = _Z_DK + _DW


def _rope(x, cos, sin_signed, quarter):
    lane = lax.broadcasted_iota(jnp.int32, x.shape, 1)
    first = (lane % (2 * quarter)) < quarter
    rx = jnp.where(first, pltpu.roll(x, LANES - quarter, 1), pltpu.roll(x, quarter, 1))
    return x * cos + rx * sin_signed


def _proj_ab_kernel(x_ref, mod_ref, g_ref, win_ref, qn_ref, kvn_ref, wuq_ref, wuk_ref, wuv_ref,
                    tab_ref, q_ref, k_ref, v_ref, dq_ref, dk_ref, dv_ref):
    h = _norm_mod(x_ref[0], g_ref[0:1, :], mod_ref[0, 0:1, :], mod_ref[0, 1:2, :])
    z = _dot(h.astype(BF), win_ref[...])
    cqn = _rms(z[:, _Z_CQ:_Z_CKV], qn_ref[...]).astype(BF)
    ckvn = _rms(z[:, _Z_CKV:_Z_KR], kvn_ref[...]).astype(BF)
    q2 = _dot(cqn, wuq_ref[...])
    cos_q, sin_q = tab_ref[0], tab_ref[1]
    nq = MLA_HEADS * LANES
    for hd in range(MLA_HEADS):
        lo = hd * LANES
        q_ref[0, hd] = (q2[:, lo:lo + LANES] * cos_q + q2[:, nq + lo:nq + lo + LANES] * sin_q).astype(BF)
    kr = _rope(z[:, _Z_KR:_Z_DQ], tab_ref[2], tab_ref[3], MLA_ROPE // 4)
    kk = _dot(ckvn, wuk_ref[...])
    vv = _dot(ckvn, wuv_ref[...])
    for hd in range(MLA_HEADS):
        lo = hd * LANES
        k_ref[0, hd] = (kk[:, lo:lo + LANES] + kr).astype(BF)
        v_ref[0, hd] = vv[:, lo:lo + LANES].T.astype(BF)
    cos_d, sin_d = tab_ref[4], tab_ref[5]
    for hd in range(DIFF_HEADS):
        lo = hd * LANES
        dq = _rope(z[:, _Z_DQ + lo:_Z_DQ + lo + LANES], cos_d, sin_d, DIFF_HD // 4)
        dq_ref[0, hd] = (dq * DIFF_SCALE).astype(BF)
        dk_ref[0, hd] = _rope(z[:, _Z_DK + lo:_Z_DK + lo + LANES], cos_d, sin_d, DIFF_HD // 4).astype(BF)
        dv_ref[0, hd] = z[:, _Z_DV + lo:_Z_DV + lo + LANES].T.astype(BF)


def _proj_ab(x, mod, g, wts, tabs, tm):
    b, t, d = x.shape
    win, qn, kvn, wuq, wuk, wuv = wts
    tok = lambda hh: pl.BlockSpec((1, hh, tm, LANES), lambda bi, i: (bi, 0, i, 0))
    shp = lambda hh: jax.ShapeDtypeStruct((b, hh, t, LANES), BF)
    tok_t = lambda hh, dv: pl.BlockSpec((1, hh, dv, tm), lambda bi, i: (bi, 0, 0, i))
    shp_t = lambda hh, dv: jax.ShapeDtypeStruct((b, hh, dv, t), BF)
    return pl.pallas_call(
        _proj_ab_kernel,
        grid=(b, t // tm),
        in_specs=[
            pl.BlockSpec((1, tm, d), lambda bi, i: (bi, i, 0)),
            _mod_spec(mod),
            _const_spec(g.shape),
            _const_spec(win.shape), _const_spec(qn.shape), _const_spec(kvn.shape),
            _const_spec(wuq.shape), _const_spec(wuk.shape), _const_spec(wuv.shape),
            pl.BlockSpec((6, tm, LANES), lambda bi, i: (0, i, 0)),
        ],
        out_specs=[tok(MLA_HEADS), tok(MLA_HEADS), tok_t(MLA_HEADS, LANES),
                   tok(DIFF_HEADS), tok(DIFF_HEADS), tok_t(DIFF_HEADS, LANES)],
        out_shape=[shp(MLA_HEADS), shp(MLA_HEADS), shp_t(MLA_HEADS, LANES),
                   shp(DIFF_HEADS), shp(DIFF_HEADS), shp_t(DIFF_HEADS, LANES)],
        compiler_params=_cparams("arbitrary", "arbitrary"),
        name="proj_ab",
    )(x, mod, g, win, qn, kvn, wuq, wuk, wuv, tabs)


KEY_CHUNK = 256


SUBLANES = 8
ATTN_QUERY_LANES = 1024


def _key_chunks(kv_refs):
    off = 0
    for k_ref, v_ref in kv_refs:
        lk = k_ref.shape[2]
        for c0 in range(0, lk, KEY_CHUNK):
            w = min(KEY_CHUNK, lk - c0)
            yield k_ref, v_ref, c0, w, off
            off += w


def _attn_kernel(*refs, nseg, mode, lam_init):
    q_ref = refs[0]
    kv_refs = [(refs[1 + 2 * s], refs[2 + 2 * s]) for s in range(nseg)]
    o_ref, s0_ref, s1_ref, mm0_ref, mm1_ref = refs[-5:]
    tq = q_ref.shape[2]
    width = s0_ref.shape[1]
    fold = lambda a: a.reshape(a.shape[0] // SUBLANES, SUBLANES, width)
    t = pl.program_id(0)

    @pl.when(t == 0)
    def _():
        s1_ref[...] = jnp.zeros_like(s1_ref)
        mm1_ref[...] = jnp.zeros_like(mm1_ref)

    def step(s_cur, mm_cur, s_prv, mm_prv):
        q = q_ref[0, 0]
        if mode != "single":
            lane = lax.broadcasted_iota(jnp.int32, q.shape, 1)
            zero = jnp.zeros_like(q)
            q = jnp.concatenate([jnp.where(lane < DIFF_HD, q, zero), jnp.where(lane >= DIFF_HD, q, zero)], axis=0)
        m_prv = jnp.max(mm_prv[...], axis=0, keepdims=True)
        mm = jnp.full((SUBLANES, width), -jnp.inf, F32)
        ll = jnp.zeros((SUBLANES, width), F32)
        o_t = jnp.zeros((LANES, width), F32)
        for k_ref, v_ref, c0, w, off in _key_chunks(kv_refs):
            s = lax.dot_general(k_ref[0, 0, c0:c0 + w, :], q, _NT, preferred_element_type=F32)
            s_cur[off:off + w, :] = s
            mm = jnp.maximum(mm, jnp.max(fold(s), axis=0))
            p = jnp.exp2(s_prv[off:off + w, :] - m_prv)
            ll = ll + jnp.sum(fold(p), axis=0)
            o_t = o_t + _dot(v_ref[0, 0, :, c0:c0 + w], p.astype(BF))
        mm_cur[...] = mm
        o_t = o_t / jnp.sum(ll, axis=0, keepdims=True)
        if mode == "single":
            o_ref[0, 0] = o_t.T.astype(BF)
        elif mode == "pair":
            o_ref[0, 0] = jnp.where(lane < DIFF_HD, o_t[:, :tq].T, o_t[:, tq:].T).astype(BF)
        else:
            lam_ref, subln_ref = refs[1 + 2 * nseg], refs[2 + 2 * nseg]
            lv = lam_ref[...]
            lam = (jnp.exp(jnp.sum(lv[0:1] * lv[1:2], axis=-1, keepdims=True))
                   - jnp.exp(jnp.sum(lv[2:3] * lv[3:4], axis=-1, keepdims=True)) + lam_init)
            d = o_t[:, :tq] - lam * o_t[:, tq:]
            dn = d * lax.rsqrt(jnp.mean(d * d, axis=0, keepdims=True) + EPS)
            o_ref[0, 0] = (dn.T * subln_ref[...] * (1 - lam_init)).astype(BF)

    pl.when(t % 2 == 0)(lambda: step(s0_ref, mm0_ref, s1_ref, mm1_ref))
    pl.when(t % 2 == 1)(lambda: step(s1_ref, mm1_ref, s0_ref, mm0_ref))


def _attention(q, kvs, tq, mode="single", lam_vecs=None, subln=None, lam_init=0.0):
    b, nh, lq, _ = q.shape
    diff = mode == "diff"
    nmap = 1 if mode == "single" else 2
    nq = lq // tq
    ntile = b * nh * nq

    def tile(t):
        t = jnp.clip(t, 0, ntile - 1)
        return t // (nh * nq), (t // nq) % nh, t % nq

    cur_head = lambda t: tile(t)[:2] + (0, 0)
    prv_head = lambda t: tile(t - 1)[:2] + (0, 0)
    in_specs = [pl.BlockSpec((1, 1, tq, LANES), lambda t: tile(t) + (0,))]
    args = [q]
    for k, v in kvs:
        lk = k.shape[2]
        in_specs += [pl.BlockSpec((1, 1, lk, LANES), cur_head), pl.BlockSpec((1, 1, v.shape[2], lk), prv_head)]
        args += [k, v]
    if diff:
        in_specs += [_const_spec(lam_vecs.shape), _const_spec(subln.shape)]
        args += [lam_vecs, subln]
    nkeys = sum(k.shape[2] for k, _ in kvs)
    return pl.pallas_call(
        functools.partial(_attn_kernel, nseg=len(kvs), mode=mode, lam_init=lam_init),
        grid=(ntile + 1,),
        in_specs=in_specs,
        out_specs=pl.BlockSpec((1, 1, tq, LANES), lambda t: tile(t - 1) + (0,)),
        out_shape=jax.ShapeDtypeStruct((b, nh, lq, LANES), BF),
        scratch_shapes=([pltpu.VMEM((nkeys, nmap * tq), F32)] * 2 + [pltpu.VMEM((SUBLANES, nmap * tq), F32)] * 2),
        compiler_params=_cparams("arbitrary"),
        name="attn_diff" if diff else "attn_mla",
    )(*args)


ROUTER_PIECES = 4


def _outproj_kernel(*refs, n_in, has_bias, tail):
    o_refs = refs[:n_in]
    w_ref = refs[n_in]
    pos = n_in + 1
    b_ref = None
    if has_bias:
        b_ref = refs[pos]
        pos += 1
    x_ref, mod_ref, g_ref = refs[pos:pos + 3]
    rest = refs[pos + 3:]
    tm = x_ref.shape[1]

    def residual(rows):
        o = jnp.concatenate([r[0, h, rows, :] for r in o_refs for h in range(r.shape[1])], axis=-1)
        y = _dot(o, w_ref[...])
        if has_bias:
            y = y + b_ref[...]
        x1 = x_ref[0, rows, :] + mod_ref[0, 2:3, :] * _rms(y, g_ref[1:2, :])
        return x1, _norm_mod(x1, g_ref[2:3, :], mod_ref[0, 3:4, :], mod_ref[0, 4:5, :])

    if tail == "ffn":
        w1_ref, w3_ref, w2_ref, out_ref = rest
        x1, h = residual(slice(0, tm))
        out_ref[0] = x1 + mod_ref[0, 5:6, :] * _rms(_swiglu(h.astype(BF), w1_ref, w3_ref, w2_ref), g_ref[3:4, :])
    else:
        router_ref, tri_ref, out_ref = rest[:3]
        step = tm // ROUTER_PIECES
        halves = [slice(r0, r0 + step) for r0 in range(0, tm, step)]
        parts = [residual(rows) for rows in halves]
        for rows, (x1, _) in zip(halves, parts):
            out_ref[0, rows, :] = x1
        first = (pl.program_id(0) == 0) & (pl.program_id(1) == 0)
        _route([h for _, h in parts], halves, first, router_ref, tri_ref, *rest[3:])


def _outproj(os_, w, bias, x, mod, g, tm, tail, tail_args):
    b, t, d = x.shape
    n = b * t
    nt = t // tm
    in_specs = [pl.BlockSpec((1, o.shape[1], tm, LANES), lambda bi, i: (bi, 0, i, 0)) for o in os_]
    in_specs.append(_const_spec(w.shape))
    args = list(os_) + [w]
    if bias is not None:
        in_specs.append(_const_spec(bias.shape))
        args.append(bias)
    x_spec = pl.BlockSpec((1, tm, d), lambda bi, i: (bi, i, 0))
    in_specs += [x_spec, _mod_spec(mod), _const_spec(g.shape)]
    args += [x, mod, g]
    out_specs, out_shape, scratch = x_spec, jax.ShapeDtypeStruct((b, t, d), F32), []
    if tail == "ffn":
        in_specs += [_resident_spec(a.shape, lambda bi, i: (0, 0)) for a in tail_args]
        args += list(tail_args)
    else:
        router, = tail_args
        tri = jnp.asarray(np.tril(np.ones((tm, tm), np.float32), -1), BF)
        in_specs += [_const_spec(router.shape), _const_spec(tri.shape)]
        args += [router, tri]
        rows = lambda wd: pl.BlockSpec((tm, wd), lambda bi, i: (bi * nt + i, 0))
        out_specs = [x_spec, pl.BlockSpec((PACK_CHUNKS, tm, LANES), lambda bi, i: (0, bi * nt + i, 0)),
                     rows(ROUTE_COLS), rows(ROUTE_COLS), pl.BlockSpec((1, LANES), lambda bi, i: (0, 0))]
        out_shape = [out_shape, jax.ShapeDtypeStruct((PACK_CHUNKS, n, LANES), jnp.int32),
                     jax.ShapeDtypeStruct((n, ROUTE_COLS), jnp.int32), jax.ShapeDtypeStruct((n, ROUTE_COLS), F32),
                     jax.ShapeDtypeStruct((1, LANES), F32)]
        scratch = [pltpu.VMEM((1, LANES), F32)]
    return pl.pallas_call(
        functools.partial(_outproj_kernel, n_in=len(os_), has_bias=bias is not None, tail=tail),
        grid=(b, nt),
        in_specs=in_specs,
        out_specs=out_specs,
        out_shape=out_shape,
        scratch_shapes=scratch,
        compiler_params=_cparams("arbitrary", "arbitrary"),
        name="outproj_" + tail,
    )(*args)


MXU_TILE = 256


def _ff_chunks(f):
    tiles = f // MXU_TILE
    if f % MXU_TILE or tiles < 2:
        return [(0, f)]
    first = (tiles + 1) // 2 * MXU_TILE
    return [(0, first), (first, f - first)]


def _swiglu(hn, w1_ref, w3_ref, w2_ref):
    y = None
    for lo, n in _ff_chunks(w1_ref.shape[1]):
        a = _dot(hn, w1_ref[:, lo:lo + n])
        act = (a * jax.nn.sigmoid(a) * _dot(hn, w3_ref[:, lo:lo + n])).astype(BF)
        part = _dot(act, w2_ref[lo:lo + n, :])
        y = part if y is None else y + part
    return y


def _resident_spec(shape, index_map):
    return pl.BlockSpec(shape, index_map, pipeline_mode=pl.Buffered(1))


MOE_ROW_TILE = 512
PACK_CHUNKS = 4
SC_WINDOW = 128
SC_SUBCORES = 32
ROUTE_COLS = 8
_HI_MASK = -65536
_LO_MASK = 65535


def _pack_rows(v, out_ref, rows=slice(None)):
    half = v.shape[1] // 2
    vb = v.astype(BF).astype(F32)
    lo = (pltpu.bitcast(vb[:, :half], jnp.int32) >> 16) & _LO_MASK
    hi = pltpu.bitcast(vb[:, half:], jnp.int32) & _HI_MASK
    w = lo | hi
    for c in range(PACK_CHUNKS):
        out_ref[c, rows, :] = w[:, c * LANES:(c + 1) * LANES]


def _unpack_rows(ref):
    w = jnp.concatenate([ref[c] for c in range(PACK_CHUNKS)], axis=-1)
    lo = pltpu.bitcast(w << 16, F32)
    hi = pltpu.bitcast(w & _HI_MASK, F32)
    return jnp.concatenate([lo, hi], axis=-1)


def _route(hs, row_slices, first, router_ref, tri_ref, hp_ref, meta_ref, gate_ref, cnt_ref, carry_ref):
    @pl.when(first)
    def _():
        carry_ref[...] = jnp.zeros_like(carry_ref)

    pieces = range(len(hs))
    for h, rows in zip(hs, row_slices):
        _pack_rows(h, hp_ref, rows)
    his = [h.astype(BF) for h in hs]
    los = [(hs[p] - his[p].astype(F32)).astype(BF) for p in pieces]
    raw = [_dot(his[p], router_ref[0]) + (_dot(his[p], router_ref[1]) + _dot(los[p], router_ref[0])) for p in pieces]
    lanes = [lax.broadcasted_iota(jnp.int32, r.shape, 1) for r in raw]
    lgs = [jnp.where(lanes[p] < N_EXPERTS, raw[p], -jnp.inf) for p in pieces]
    m1s = [jnp.max(lg, axis=-1, keepdims=True) for lg in lgs]
    i1s = [jnp.min(jnp.where(lgs[p] == m1s[p], lanes[p], LANES), axis=-1, keepdims=True) for p in pieces]
    rests = [jnp.where(lanes[p] == i1s[p], -jnp.inf, lgs[p]) for p in pieces]
    m2s = [jnp.max(r, axis=-1, keepdims=True) for r in rests]
    i2s = [jnp.min(jnp.where(rests[p] == m2s[p], lanes[p], LANES), axis=-1, keepdims=True) for p in pieces]
    rows_cat = lambda parts: jnp.concatenate(parts, axis=0)
    m1, m2, i1, i2, lane = rows_cat(m1s), rows_cat(m2s), rows_cat(i1s), rows_cat(i2s), rows_cat(lanes)
    e2 = jnp.exp(m2 - m1)
    denom = 1.0 + e2
    assigned = jnp.where((lane == i1) | (lane == i2), 1.0, 0.0)
    ranks = _dot(tri_ref[...], assigned.astype(BF)) + carry_ref[...]
    r1 = jnp.sum(jnp.where(lane == i1, ranks, 0.0), axis=-1, keepdims=True).astype(jnp.int32)
    r2 = jnp.sum(jnp.where(lane == i2, ranks, 0.0), axis=-1, keepdims=True).astype(jnp.int32)
    carry_ref[...] += jnp.sum(assigned, axis=0, keepdims=True)
    cnt_ref[...] = carry_ref[...]
    col = lax.broadcasted_iota(jnp.int32, meta_ref.shape, 1)
    meta_ref[...] = jnp.where(col == 0, i1, jnp.where(col == 1, i2, jnp.where(col == 2, r1, r2)))
    gate_ref[...] = jnp.where(col == 0, 1.0 / denom, e2 / denom)


def _sc_mesh():
    return plsc.VectorSubcoreMesh(core_axis_name="core", subcore_axis_name="subcore")


def _sc_scatter_rows(rows, idx_a, idx_b, n_out):
    nrows = rows.shape[0]

    @pl.kernel(out_type=jax.ShapeDtypeStruct((n_out, LANES), rows.dtype), mesh=_sc_mesh(), scratch_types=[])
    def scatter_kernel(x_hbm, ia_hbm, ib_hbm, o_hbm):
        def body(x_vmem, ia_vmem, ib_vmem):
            pltpu.sync_copy(x_vmem, o_hbm.at[ia_vmem.at[0]])
            pltpu.sync_copy(x_vmem, o_hbm.at[ib_vmem.at[0]])

        pltpu.emit_pipeline(
            body,
            grid=(nrows // SC_WINDOW,),
            in_specs=[pl.BlockSpec((SC_WINDOW, LANES), lambda i: (i, 0)),
                      pl.BlockSpec((1, SC_WINDOW), lambda i: (0, i)),
                      pl.BlockSpec((1, SC_WINDOW), lambda i: (0, i))],
            out_specs=[],
            core_axis_name=("core", "subcore"),
            dimension_semantics=(pltpu.PARALLEL,),
        )(x_hbm, ia_hbm, ib_hbm)

    return scatter_kernel(rows, idx_a, idx_b)


def _sc_gather_rows(table, idx_a, idx_b):
    nrows = idx_a.shape[1]
    out = jax.ShapeDtypeStruct((nrows, LANES), table.dtype)

    @pl.kernel(out_type=(out, out), mesh=_sc_mesh(), scratch_types=[])
    def gather_kernel(t_hbm, ia_hbm, ib_hbm, oa_hbm, ob_hbm):
        def body(ia_vmem, ib_vmem, oa_vmem, ob_vmem):
            pltpu.sync_copy(t_hbm.at[ia_vmem.at[0]], oa_vmem)
            pltpu.sync_copy(t_hbm.at[ib_vmem.at[0]], ob_vmem)

        pltpu.emit_pipeline(
            body,
            grid=(nrows // SC_WINDOW,),
            in_specs=[pl.BlockSpec((1, SC_WINDOW), lambda i: (0, i)),
                      pl.BlockSpec((1, SC_WINDOW), lambda i: (0, i))],
            out_specs=[pl.BlockSpec((SC_WINDOW, LANES), lambda i: (i, 0)),
                       pl.BlockSpec((SC_WINDOW, LANES), lambda i: (i, 0))],
            core_axis_name=("core", "subcore"),
            dimension_semantics=(pltpu.PARALLEL,),
        )(ia_hbm, ib_hbm, oa_hbm, ob_hbm)

    return gather_kernel(table, idx_a, idx_b)


def _experts_kernel(te_ref, tv_ref, xs_ref, w1_ref, w3_ref, w2_ref, ys_ref):
    del te_ref
    valid = tv_ref[pl.program_id(0)]

    @pl.when(valid > 0)
    def _():
        row = lax.broadcasted_iota(jnp.int32, (xs_ref.shape[1], 1), 0)
        hn = jnp.where(row < valid, _unpack_rows(xs_ref), 0.0).astype(BF)
        _pack_rows(_swiglu(hn, w1_ref.at[0], w3_ref.at[0], w2_ref.at[0]), ys_ref)

    @pl.when(valid == 0)
    def _():
        ys_ref[...] = jnp.zeros_like(ys_ref)


def _experts(xs, tile_expert, tile_valid, w1, w3, w2):
    _, p, _ = xs.shape
    tr = MOE_ROW_TILE
    rows = pl.BlockSpec((PACK_CHUNKS, tr, LANES), lambda j, te, tv: (0, j, 0))
    weights = lambda w: pl.BlockSpec((1,) + w.shape[1:], lambda j, te, tv: (te[j], 0, 0))
    return pl.pallas_call(
        _experts_kernel,
        grid_spec=pltpu.PrefetchScalarGridSpec(
            num_scalar_prefetch=2,
            grid=(p // tr,),
            in_specs=[rows, weights(w1), weights(w3), weights(w2)],
            out_specs=rows,
        ),
        out_shape=jax.ShapeDtypeStruct(xs.shape, jnp.int32),
        compiler_params=_cparams("arbitrary"),
        name="moe_experts",
    )(tile_expert, tile_valid, xs, w1, w3, w2)


def _combine_kernel(x_ref, ya_ref, yb_ref, gate_ref, mod_ref, g_ref, out_ref):
    gates = gate_ref[...]
    fx = gates[:, 0:1] * _unpack_rows(ya_ref) + gates[:, 1:2] * _unpack_rows(yb_ref)
    out_ref[0] = x_ref[0] + mod_ref[0, 5:6, :] * _rms(fx, g_ref[3:4, :])


def _combine(x, ya, yb, gates, mod, g, tm):
    b, t, d = x.shape
    nt = t // tm
    packed = pl.BlockSpec((PACK_CHUNKS, tm, LANES), lambda bi, i: (0, bi * nt + i, 0))
    return pl.pallas_call(
        _combine_kernel,
        grid=(b, nt),
        in_specs=[
            pl.BlockSpec((1, tm, d), lambda bi, i: (bi, i, 0)),
            packed, packed,
            pl.BlockSpec((tm, gates.shape[1]), lambda bi, i: (bi * nt + i, 0)),
            _mod_spec(mod),
            _const_spec(g.shape),
        ],
        out_specs=pl.BlockSpec((1, tm, d), lambda bi, i: (bi, i, 0)),
        out_shape=jax.ShapeDtypeStruct((b, t, d), F32),
        compiler_params=_cparams("arbitrary", "arbitrary"),
        name="moe_combine",
    )(x, ya, yb, gates, mod, g)


def _moe(x, routed, mod, g, w1, w3, w2, tm):
    b, t, d = x.shape
    n = b * t
    ne = w1.shape[0]
    tr = MOE_ROW_TILE
    assert d == 2 * PACK_CHUNKS * LANES and (PACK_CHUNKS * n) % (SC_WINDOW * SC_SUBCORES) == 0
    hp, meta, gates, counts = routed

    ntile = 2 * n // tr + ne
    p = ntile * tr
    cnt = counts[0, :ne].astype(jnp.int32)
    tiles = (cnt + tr - 1) // tr
    tile_end = jnp.cumsum(tiles)
    tile_start = tile_end - tiles
    base = tile_start * tr
    eids = jnp.arange(ne, dtype=jnp.int32)
    base_of = lambda e: jnp.sum(jnp.where(e[:, None] == eids[None, :], base[None, :], 0), axis=-1)
    pos_a = base_of(meta[:, 0]) + meta[:, 2]
    pos_b = base_of(meta[:, 1]) + meta[:, 3]
    chunk = jnp.arange(PACK_CHUNKS, dtype=jnp.int32)[:, None] * p
    idx_a = (chunk + pos_a[None, :]).reshape(1, PACK_CHUNKS * n)
    idx_b = (chunk + pos_b[None, :]).reshape(1, PACK_CHUNKS * n)
    tj = jnp.arange(ntile, dtype=jnp.int32)
    tile_expert = jnp.minimum(jnp.sum(tj[:, None] >= tile_end[None, :], axis=-1), ne - 1).astype(jnp.int32)
    done = jnp.sum(jnp.where(tile_expert[:, None] == eids[None, :], tile_start[None, :], 0), axis=-1)
    left = jnp.sum(jnp.where(tile_expert[:, None] == eids[None, :], cnt[None, :], 0), axis=-1) - (tj - done) * tr
    tile_valid = jnp.where(tj < tile_end[-1], jnp.clip(left, 0, tr), 0).astype(jnp.int32)

    xs = _sc_scatter_rows(hp.reshape(PACK_CHUNKS * n, LANES), idx_a, idx_b, PACK_CHUNKS * p)
    ys = _experts(xs.reshape(PACK_CHUNKS, p, LANES), tile_expert, tile_valid, w1, w3, w2)
    ya, yb = _sc_gather_rows(ys.reshape(PACK_CHUNKS * p, LANES), idx_a, idx_b)
    shp = (PACK_CHUNKS, n, LANES)
    return _combine(x, ya.reshape(shp), yb.reshape(shp), gates, mod, g, tm)


def _proj_na_kernel(x_ref, mod_ref, g_ref, w_ref, b_ref, q_ref, k_ref, v_ref):
    h = _norm_mod(x_ref[0], g_ref[0:1, :], mod_ref[0, 0:1, :], mod_ref[0, 1:2, :])
    z = _dot(h.astype(BF), w_ref[...]) + b_ref[...]
    n = NA_SLOTS * LANES
    for s in range(NA_SLOTS):
        lo = s * LANES
        q_ref[0, s] = (z[:, lo:lo + LANES] * NA_SCALE).astype(BF)
        k_ref[0, s] = z[:, n + lo:n + lo + LANES].astype(BF)
        v_ref[0, s] = z[:, 2 * n + lo:2 * n + lo + LANES].astype(BF)


def _proj_na(x, mod, g, w, bias, tm):
    b, t, d = x.shape
    tok = pl.BlockSpec((1, NA_SLOTS, tm, LANES), lambda bi, i: (bi, 0, i, 0))
    shp = jax.ShapeDtypeStruct((b, NA_SLOTS, t, LANES), BF)
    return pl.pallas_call(
        _proj_na_kernel,
        grid=(b, t // tm),
        in_specs=[
            pl.BlockSpec((1, tm, d), lambda bi, i: (bi, i, 0)),
            _mod_spec(mod),
            _const_spec(g.shape),
            _const_spec(w.shape),
            _const_spec(bias.shape),
        ],
        out_specs=[tok, tok, tok],
        out_shape=[shp, shp, shp],
        compiler_params=_cparams("arbitrary", "arbitrary"),
        name="proj_na",
    )(x, mod, g, w, bias)


NA_ROWS_PER_STEP = 2
NA_GROUP = 4


def _na_window_start(r, rows):
    return jnp.clip(r - NA_KH // 2, 0, rows - NA_KH)


def _na_kernel(q_ref, kx_ref, vx_ref, kc_ref, vc_ref, *rest, rows):
    bias_refs, o_ref = rest[:NA_ROWS_PER_STEP], rest[NA_ROWS_PER_STEP]
    lane = lax.broadcasted_iota(jnp.int32, (GRID_W, LANES), 1)
    problems = [(a, s) for a in range(NA_ROWS_PER_STEP) for s in range(NA_SLOTS)]
    starts = [pl.multiple_of(_na_window_start(pl.program_id(1) * NA_ROWS_PER_STEP + a, rows) * GRID_W, GRID_W)
              for a in range(NA_ROWS_PER_STEP)]
    for g0 in range(0, len(problems), NA_GROUP):
        _na_group(problems[g0:g0 + NA_GROUP], starts, lane, q_ref, kx_ref, vx_ref, kc_ref, vc_ref, bias_refs, o_ref)


def _na_group(work, starts, lane, q_ref, kx_ref, vx_ref, kc_ref, vc_ref, bias_refs, o_ref):
    nwin = NA_KH * GRID_W
    idx = range(len(work))
    sws, scs = [], []
    for a, s in work:
        q = q_ref[0, s, a * GRID_W:(a + 1) * GRID_W, :]
        zero = jnp.zeros_like(q)
        q2 = jnp.concatenate([jnp.where(lane < NA_HD, q, zero), jnp.where(lane >= NA_HD, q, zero)], axis=0)
        kw = kx_ref[0, s, pl.ds(starts[a], nwin), :]
        sws.append(lax.dot_general(q2, kw, _NT, preferred_element_type=F32) + bias_refs[a][0, s])
        scs.append(lax.dot_general(q2, kc_ref[0, s], _NT, preferred_element_type=F32))
    ms = [jnp.maximum(jnp.max(sws[i], axis=-1, keepdims=True), jnp.max(scs[i], axis=-1, keepdims=True))
          for i in idx]
    pws = [jnp.exp2(sws[i] - ms[i]) for i in idx]
    pcs = [jnp.exp2(scs[i] - ms[i]) for i in idx]
    ls = [jnp.sum(pws[i], axis=-1, keepdims=True) + jnp.sum(pcs[i], axis=-1, keepdims=True) for i in idx]
    for i, (a, s) in enumerate(work):
        vw = vx_ref[0, s, pl.ds(starts[a], nwin), :]
        o2 = (_dot(pws[i].astype(BF), vw) + _dot(pcs[i].astype(BF), vc_ref[0, s])) / ls[i]
        o_ref[0, s, a * GRID_W:(a + 1) * GRID_W, :] = jnp.where(lane < NA_HD, o2[:GRID_W], o2[GRID_W:]).astype(BF)


def _na_attention(q, kx, vx, kc, vc, bias):
    b, ns, l, _ = q.shape
    rows = l // GRID_W
    lc = kc.shape[2]
    nr = NA_ROWS_PER_STEP
    assert rows % nr == 0

    def bias_spec(a):
        def index_map(bi, i):
            r = i * nr + a
            return (r - _na_window_start(r, rows), 0, 0, 0)
        return pl.BlockSpec((1,) + bias.shape[1:], index_map)

    full = lambda n: pl.BlockSpec((1, ns, n, LANES), lambda bi, i: (bi, 0, 0, 0))
    row = pl.BlockSpec((1, ns, nr * GRID_W, LANES), lambda bi, i: (bi, 0, i, 0))
    return pl.pallas_call(
        functools.partial(_na_kernel, rows=rows),
        grid=(b, rows // nr),
        in_specs=[row, full(l), full(l), full(lc), full(lc)] + [bias_spec(a) for a in range(nr)],
        out_specs=row,
        out_shape=jax.ShapeDtypeStruct((b, ns, l, LANES), BF),
        compiler_params=_cparams("arbitrary", "arbitrary"),
        name="na_attn",
    )(q, kx, vx, kc, vc, *([bias] * nr))


def _axis_tables(pos, dim):
    inv = ROPE_THETA ** (-jnp.arange(0, dim, 2, dtype=F32) / dim)
    ang = pos.astype(F32)[:, None] * inv[None, :]
    ang = jnp.concatenate([ang, ang], axis=-1)
    return jnp.cos(ang), jnp.sin(ang)


def _rope_tables(n, rope):
    if rope:
        t = jnp.arange(n, dtype=jnp.int32)
        row, col = t // GRID_W, t % GRID_W

        def cs(d):
            cr, sr = _axis_tables(row, d // 2)
            cc, sc = _axis_tables(col, d // 2)
            sign = np.where(np.arange(d) % (d // 2) < d // 4, -1.0, 1.0).astype(np.float32)
            return jnp.concatenate([cr, cc], axis=-1), jnp.concatenate([sr, sc], axis=-1) * sign

        c32, s32 = cs(MLA_ROPE)
        c64, s64 = cs(DIFF_HD)
    else:
        c32, s32 = jnp.ones((n, MLA_ROPE), F32), jnp.zeros((n, MLA_ROPE), F32)
        c64, s64 = jnp.ones((n, DIFF_HD), F32), jnp.zeros((n, DIFF_HD), F32)
    ones = jnp.ones((n, MLA_NOPE), F32)
    pad = lambda a: jnp.pad(a, ((0, 0), (0, LANES - a.shape[1])))
    cos_q = MLA_SCALE * pad(jnp.concatenate([ones, c32], axis=-1))
    sin_q = MLA_SCALE * pad(jnp.concatenate([jnp.zeros_like(ones), s32], axis=-1))
    at_rope_lanes = lambda a: jnp.pad(a, ((0, 0), (MLA_NOPE, LANES - MLA_NOPE - MLA_ROPE)))
    return jnp.stack([cos_q, sin_q, at_rope_lanes(c32), at_rope_lanes(s32),
                      jnp.concatenate([c64, c64], axis=-1), jnp.concatenate([s64, s64], axis=-1)])


def _pad_cols(w, n):
    return jnp.pad(w, ((0, 0), (0, n - w.shape[1])))


def _prep_ab_weights(w_in, q_norm, kv_norm, w_uq, w_ukv, w_out):
    o_cq, o_ckv, o_kr = MLA_Q_RANK, MLA_Q_RANK + MLA_KV_RANK, MLA_Q_RANK + MLA_KV_RANK + MLA_ROPE
    o_dq, o_dk = o_kr + _DW, o_kr + 2 * _DW
    w_kr, w_dq, w_dk, w_dv = w_in[:, o_ckv:o_kr], w_in[:, o_kr:o_dq], w_in[:, o_dq:o_dk], w_in[:, o_dk:]
    w_kr = jnp.pad(w_kr, ((0, 0), (MLA_NOPE, LANES - MLA_NOPE - MLA_ROPE)))
    win = jnp.concatenate([w_in[:, :o_ckv], w_kr, w_dq, w_dk, w_dv], axis=1).astype(BF)
    rq = w_uq.shape[0]
    uq = w_uq.reshape(rq, MLA_HEADS, MLA_NOPE + MLA_ROPE)
    rope = uq[..., MLA_NOPE:].reshape(rq, MLA_HEADS, 2, 2, MLA_ROPE // 4)
    swapped = jnp.concatenate([rope[..., 1:2, :], rope[..., 0:1, :]], axis=-2).reshape(rq, MLA_HEADS, MLA_ROPE)
    partner = jnp.pad(swapped, ((0, 0), (0, 0), (MLA_NOPE, LANES - MLA_NOPE - MLA_ROPE)))
    uq = jnp.pad(uq, ((0, 0), (0, 0), (0, LANES - MLA_NOPE - MLA_ROPE)))
    wuq = jnp.concatenate([uq.reshape(rq, -1), partner.reshape(rq, -1)], axis=1).astype(BF)
    rkv = w_ukv.shape[0]
    ukv = w_ukv.reshape(rkv, MLA_HEADS, MLA_NOPE + MLA_V)
    slot = lambda a: jnp.pad(a, ((0, 0), (0, 0), (0, LANES - a.shape[-1]))).reshape(rkv, MLA_HEADS * LANES)
    wuk = slot(ukv[..., :MLA_NOPE]).astype(BF)
    wuv = slot(ukv[..., MLA_NOPE:]).astype(BF)
    d = w_out.shape[1]
    wo_a = w_out[:MLA_HEADS * MLA_V].reshape(MLA_HEADS, MLA_V, d)
    wo_a = jnp.pad(wo_a, ((0, 0), (0, LANES - MLA_V), (0, 0))).reshape(MLA_HEADS * LANES, d)
    wo = jnp.concatenate([wo_a, w_out[MLA_HEADS * MLA_V:]], axis=0).astype(BF)
    proj = (win, q_norm.reshape(1, -1), kv_norm.reshape(1, -1), wuq, wuk, wuv)
    return proj, wo


def _na_bias_table(rpb):
    nh = rpb.shape[0]
    cols = np.arange(GRID_W)
    col_start = np.clip(cols - NA_KW // 2, 0, GRID_W - NA_KW)
    kcol = np.arange(GRID_W)
    valid = (kcol[None, :] >= col_start[:, None]) & (kcol[None, :] < col_start[:, None] + NA_KW)
    dc = kcol[None, :] - cols[:, None] + (NA_KW - 1)
    onehot = (dc[None] == np.arange(2 * NA_KW - 1)[:, None, None]) & valid[None]
    tz = jnp.einsum('hrd,dck->hrck', rpb * LOG2E, jnp.asarray(onehot, F32), precision=lax.Precision.HIGHEST)
    tz = jnp.where(valid[None, None], tz, MASK_VALUE)
    cases = [tz[:, NA_KH - 1 - c:2 * NA_KH - 1 - c] for c in range(NA_KH)]
    tbl = jnp.transpose(jnp.stack(cases), (0, 1, 3, 2, 4))
    return tbl.reshape(NA_KH, nh // 2, 2 * GRID_W, NA_KH * GRID_W)


def _tile(n, pref):
    return pref if n % pref == 0 else n


def kernel(x, c, ctx, c_ctx, w_mod, b_mod, norm_g, a_w_in, a_q_norm, a_kv_norm, a_w_uq, a_w_ukv, b_lambda, b_subln,
           ab_w_out, f_w1, f_w3, f_w2, c_w_qkv, c_b_qkv, c_rpb, c_w_out, c_b_out, m_router, m_w1, m_w3, m_w2):
    b, l, d = x.shape
    lc = ctx.shape[1]
    depth = w_mod.shape[0]
    assert l % GRID_W == 0 and l // GRID_W >= NA_KH

    mod_rows = 16
    cvec = jnp.zeros((mod_rows, d), F32).at[:b].set(c).at[b].set(c_ctx)
    mod = _modulation(cvec, w_mod, b_mod)

    tm_x, tm_c = _tile(l, 512), _tile(lc, 256)
    tq_x, tq_c = _tile(l, ATTN_QUERY_LANES), _tile(lc, ATTN_QUERY_LANES // 2)
    tq2_x = _tile(l, ATTN_QUERY_LANES // 2)
    cs = ctx
    for i in range(depth):
        last = i == depth - 1
        j = i // 2
        mx = mod[i, :b].reshape(b, N_MOD, d)
        mc = mod[i, b].reshape(1, N_MOD, d)
        g = norm_g[i]
        if i % 2 == 0:
            lam_init = 0.8 - 0.6 * math.exp(-0.3 * i)
            proj_w, wo = _prep_ab_weights(a_w_in[j], a_q_norm[j], a_kv_norm[j], a_w_uq[j], a_w_ukv[j], ab_w_out[j])
            qx, kx, vx, dqx, dkx, dvx = _proj_ab(x, mx, g, proj_w, _rope_tables(l, True), tm_x)
            qc, kc, vc, dqc, dkc, dvc = _proj_ab(cs, mc, g, proj_w, _rope_tables(lc, False), tm_c)
            subln = b_subln[j].reshape(1, -1)
            oa = _attention(qx, [(kx, vx), (kc, vc)], tq_x)
            ob = _attention(dqx, [(dkx, dvx), (dkc, dvc)], tq2_x, "diff", b_lambda[j], subln, lam_init)
            ffn_w = (f_w1[j].astype(BF), f_w3[j].astype(BF), f_w2[j].astype(BF))
            x = _outproj([oa, ob], wo, None, x, mx, g, tm_x, "ffn", ffn_w)
            if not last:
                oa = _attention(qc, [(kc, vc)], tq_c)
                ob = _attention(dqc, [(dkc, dvc)], tq_c, "diff", b_lambda[j], subln, lam_init)
                cs = _outproj([oa, ob], wo, None, cs, mc, g, tm_c, "ffn", ffn_w)
        else:
            wqkv = c_w_qkv[j].astype(BF)
            bqkv = c_b_qkv[j].reshape(1, -1)
            wo = c_w_out[j].astype(BF)
            bo = c_b_out[j].reshape(1, -1)
            qx, kx, vx = _proj_na(x, mx, g, wqkv, bqkv, tm_x)
            qc, kc, vc = _proj_na(cs, mc, g, wqkv, bqkv, tm_c)
            o = _na_attention(qx, kx, vx, kc, vc, _na_bias_table(c_rpb[j]))
            r_f32 = _pad_cols(m_router[j], LANES)
            r_hi = r_f32.astype(BF)
            router = (jnp.stack([r_hi, (r_f32 - r_hi.astype(F32)).astype(BF)]),)
            w1, w3, w2 = m_w1[j].astype(BF), m_w3[j].astype(BF), m_w2[j].astype(BF)
            x, *routed = _outproj([o], wo, bo, x, mx, g, tm_x, "router", router)
            x = _moe(x, routed, mx, g, w1, w3, w2, tm_x)
            if not last:
                oc = _attention(qc, [(kc, jnp.swapaxes(vc, 2, 3))], tq_c, "pair")
                cs, *routed = _outproj([oc], wo, bo, cs, mc, g, tm_c, "router", router)
                cs = _moe(cs, routed, mc, g, w1, w3, w2, tm_c)
    return x
```
